```python
import math
import jax, jax.numpy as jnp
from jax import lax
import numpy as np

D_MODEL = 1024
BATCH = 8
SEQ = 2048
DEPTH = 1

D_MIX = D_MODEL
DN_WIDTH = D_MIX // 2
DN_HEADS = 4
DN_HEAD_DIM = DN_WIDTH // DN_HEADS
DN_CONV = 4
DN_CHUNK = 64
CF_WIDTH = D_MIX - DN_WIDTH
CF_KERNEL = 31
IN_COLS = 4 * DN_WIDTH + 2 * DN_HEADS + 2 * CF_WIDTH
N_GROUPS = 8
EXPERTS_PER_GROUP = 8
N_EXPERTS = N_GROUPS * EXPERTS_PER_GROUP
TOP_K = 2
D_EXPERT = D_MODEL // 2
MOE_BLOCK = 128
EPS = 1e-6
N_MOD = 6

kernel_name = "hybrid_deltanet_conformer_hmoe_block"

F32 = jnp.float32


def rms_norm(x, w):
    xf = x.astype(F32)
    y = xf * lax.rsqrt(jnp.mean(xf * xf, axis=-1, keepdims=True) + EPS)
    return (y * w.astype(F32)).astype(x.dtype)


def layer_norm(x, w, b):
    xf = x.astype(F32)
    mu = jnp.mean(xf, axis=-1, keepdims=True)
    xc = xf - mu
    var = jnp.mean(xc * xc, axis=-1, keepdims=True)
    return (xc * lax.rsqrt(var + EPS) * w.astype(F32) + b.astype(F32)).astype(x.dtype)


def l2norm(x):
    return x * lax.rsqrt(jnp.sum(x * x, axis=-1, keepdims=True) + EPS)


def causal_depthwise_conv(x, w):
    K, C = w.shape
    return lax.conv_general_dilated(
        x, w[:, None, :].astype(x.dtype), window_strides=(1,), padding=[(K - 1, 0)],
        dimension_numbers=("NWC", "WIO", "NWC"), feature_group_count=C)


def gated_delta_rule(q, k, v, g, beta):
    B, S, H, dk = q.shape
    dv = v.shape[-1]
    C = DN_CHUNK
    N = S // C

    def chunks(t):
        t = jnp.moveaxis(t, 2, 1)
        return t.reshape((B, H, N, C) + t.shape[3:])

    q, k, v, g, beta = chunks(q), chunks(k), chunks(v), chunks(g), chunks(beta)
    q = q * (dk ** -0.5)
    g = jnp.cumsum(g, axis=-1)
    causal = jnp.tril(jnp.ones((C, C), dtype=bool))
    strict = jnp.tril(jnp.ones((C, C), dtype=bool), -1)
    decay = jnp.exp(jnp.where(causal, g[..., :, None] - g[..., None, :], -jnp.inf))
    k_beta = k * beta[..., None]
    a = jnp.where(strict, jnp.einsum("bhncd,bhnsd->bhncs", k_beta, k) * decay, 0.0)
    rhs = jnp.concatenate([v * beta[..., None], k_beta * jnp.exp(g)[..., None]], axis=-1)
    sol = lax.linalg.triangular_solve(a, rhs, left_side=True, lower=True, unit_diagonal=True)
    u, w = sol[..., :dv], sol[..., dv:]
    attn = jnp.einsum("bhncd,bhnsd->bhncs", q, k) * decay
    q_g = q * jnp.exp(g)[..., None]
    g_last = g[..., -1]
    k_dec = k * jnp.exp(g_last[..., None] - g)[..., None]

    def step(state, inp):
        u_c, w_c, attn_c, qg_c, kd_c, gl_c = inp
        v_new = u_c - jnp.einsum("bhcd,bhde->bhce", w_c, state)
        o = jnp.einsum("bhcd,bhde->bhce", qg_c, state) + jnp.einsum("bhcs,bhse->bhce", attn_c, v_new)
        state = state * jnp.exp(gl_c)[..., None, None] + jnp.einsum("bhcd,bhce->bhde", kd_c, v_new)
        return state, o

    xs = tuple(jnp.moveaxis(t, 2, 0) for t in (u, w, attn, q_g, k_dec, g_last))
    state0 = jnp.zeros((B, H, dk, dv), F32)
    _, o = lax.scan(step, state0, xs)
    o = jnp.moveaxis(o, 0, 2).reshape(B, H, S, dv)
    return jnp.moveaxis(o, 1, 2)


def hybrid_mixer(h, w_in, dn_conv_w, dn_a_log, dn_dt_bias, dn_norm_w,
                 cf_pw1_b, cf_dw_w, cf_dw_b, cf_ln_w, cf_ln_b, w_out):
    B, S, _ = h.shape
    proj = jnp.einsum("bsd,de->bse", h, w_in)
    qkv, z, b_logit, a_logit, cf_in = jnp.split(
        proj, [3 * DN_WIDTH, 4 * DN_WIDTH, 4 * DN_WIDTH + DN_HEADS, 4 * DN_WIDTH + 2 * DN_HEADS], axis=-1)

    qkv = jax.nn.silu(causal_depthwise_conv(qkv, dn_conv_w))
    q, k, v = jnp.split(qkv.astype(F32), 3, axis=-1)
    q = l2norm(q.reshape(B, S, DN_HEADS, DN_HEAD_DIM))
    k = l2norm(k.reshape(B, S, DN_HEADS, DN_HEAD_DIM))
    v = v.reshape(B, S, DN_HEADS, DN_HEAD_DIM)
    beta = jax.nn.sigmoid(b_logit.astype(F32))
    g = -jnp.exp(dn_a_log.astype(F32)) * jax.nn.softplus(a_logit.astype(F32) + dn_dt_bias.astype(F32))
    o = gated_delta_rule(q, k, v, g, beta)
    zg = jax.nn.silu(z.astype(F32).reshape(B, S, DN_HEADS, DN_HEAD_DIM))
    o = (rms_norm(o, dn_norm_w) * zg).reshape(B, S, DN_WIDTH).astype(h.dtype)

    cf = jax.nn.glu(cf_in + cf_pw1_b, axis=-1)
    cf = causal_depthwise_conv(cf, cf_dw_w) + cf_dw_b
    cf = jax.nn.silu(layer_norm(cf, cf_ln_w, cf_ln_b))

    mixed = jnp.concatenate([o, cf.astype(h.dtype)], axis=-1)
    return jnp.einsum("bsd,de->bse", mixed, w_out)


def hierarchical_moe(h, w_router_group, b_router_group, w_router_expert, b_router_expert,
                     w_gate, w_up, w_down):
    B, S, D = h.shape
    T = B * S
    xt = h.reshape(T, D)
    grp_prob = jax.nn.softmax(jnp.dot(xt, w_router_group).astype(F32) + b_router_group.astype(F32), axis=-1)
    grp_p, grp_idx = lax.top_k(grp_prob, 1)
    exp_logits = (jnp.dot(xt, w_router_expert).astype(F32) + b_router_expert.astype(F32))
    exp_logits = exp_logits.reshape(T, N_GROUPS, EXPERTS_PER_GROUP)
    sel = jnp.take_along_axis(exp_logits, grp_idx[:, :, None], axis=1)[:, 0]
    top_p, top_local = lax.top_k(jax.nn.softmax(sel, axis=-1), TOP_K)
    top_p = top_p / jnp.sum(top_p, axis=-1, keepdims=True)
    weights = grp_p * top_p
    expert_id = grp_idx * EXPERTS_PER_GROUP + top_local

    n_assign = T * TOP_K
    flat_e = expert_id.reshape(-1)
    flat_w = weights.reshape(-1)
    flat_tok = jnp.arange(n_assign, dtype=jnp.int32) // TOP_K
    order = jnp.argsort(flat_e)
    sorted_e = flat_e[order]
    counts = jnp.bincount(flat_e, length=N_EXPERTS)
    padded = (counts + MOE_BLOCK - 1) // MOE_BLOCK * MOE_BLOCK
    start = jnp.cumsum(counts) - counts
    pend = jnp.cumsum(padded)
    pstart = pend - padded
    dest = pstart[sorted_e] + jnp.arange(n_assign, dtype=jnp.int32) - start[sorted_e]
    n_blocks = -(-n_assign // MOE_BLOCK) + N_EXPERTS
    rows = n_blocks * MOE_BLOCK
    row_tok = jnp.zeros((rows,), jnp.int32).at[dest].set(flat_tok[order])
    row_w = jnp.zeros((rows,), F32).at[dest].set(flat_w[order])
    block_start = jnp.arange(n_blocks, dtype=jnp.int32) * MOE_BLOCK
    block_e = jnp.minimum(jnp.sum(pend[None, :] <= block_start[:, None], axis=1), N_EXPERTS - 1)

    def run_block(args):
        tok, e = args
        xb = xt[tok]
        hid = jax.nn.silu(jnp.dot(xb, w_gate[e])) * jnp.dot(xb, w_up[e])
        return jnp.dot(hid, w_down[e])

    y = lax.map(run_block, (row_tok.reshape(n_blocks, MOE_BLOCK), block_e))
    y = y.reshape(rows, D) * row_w[:, None].astype(y.dtype)
    out = jnp.zeros((T, D), y.dtype).at[row_tok].add(y)
    return out.reshape(B, S, D)


def setup_inputs(seed: int = 0) -> dict:
    key = jax.random.key(seed)
    ks = jax.random.split(key, 32)
    L, D = DEPTH, D_MODEL
    nrm = lambda k, shape, s: jax.random.normal(k, shape, F32) * s
    gain = lambda k, shape: 1.0 + 0.02 * jax.random.normal(k, shape, F32)
    dt = jnp.exp(jax.random.uniform(ks[7], (L, DN_HEADS), F32) * (math.log(0.1) - math.log(1e-3)) + math.log(1e-3))
    return {
        "x": nrm(ks[0], (BATCH, SEQ, D), 1.0),
        "c": nrm(ks[1], (BATCH, D), 1.0),
        "w_ada": nrm(ks[2], (L, D, N_MOD * D), 0.5 * D ** -0.5),
        "b_ada": nrm(ks[3], (L, N_MOD * D), 0.02),
        "norm_pre_mix": gain(ks[4], (L, D)),
        "norm_post_mix": gain(ks[5], (L, D)),
        "w_in": nrm(ks[6], (L, D, IN_COLS), D ** -0.5),
        "dn_conv_w": nrm(ks[8], (L, DN_CONV, 3 * DN_WIDTH), DN_CONV ** -0.5),
        "dn_a_log": jnp.log(jax.random.uniform(ks[9], (L, DN_HEADS), F32, 1.0, 16.0)),
        "dn_dt_bias": dt + jnp.log(-jnp.expm1(-dt)),
        "dn_norm_w": gain(ks[10], (L, DN_HEAD_DIM)),
        "cf_pw1_b": nrm(ks[11], (L, 2 * CF_WIDTH), 0.02),
        "cf_dw_w": nrm(ks[12], (L, CF_KERNEL, CF_WIDTH), CF_KERNEL ** -0.5),
        "cf_dw_b": nrm(ks[13], (L, CF_WIDTH), 0.02),
        "cf_ln_w": gain(ks[14], (L, CF_WIDTH)),
        "cf_ln_b": nrm(ks[15], (L, CF_WIDTH), 0.02),
        "w_out": nrm(ks[16], (L, D_MIX, D), D_MIX ** -0.5),
        "norm_pre_ffn": gain(ks[17], (L, D)),
        "norm_post_ffn": gain(ks[18], (L, D)),
        "w_router_group": nrm(ks[19], (L, D, N_GROUPS), D ** -0.5),
        "b_router_group": nrm(ks[20], (L, N_GROUPS), 0.01),
        "w_router_expert": nrm(ks[21], (L, D, N_EXPERTS), D ** -0.5),
        "b_router_expert": nrm(ks[22], (L, N_EXPERTS), 0.01),
        "w_gate": nrm(ks[23], (L, N_EXPERTS, D, D_EXPERT), D ** -0.5),
        "w_up": nrm(ks[24], (L, N_EXPERTS, D, D_EXPERT), D ** -0.5),
        "w_down": nrm(ks[25], (L, N_EXPERTS, D_EXPERT, D), D_EXPERT ** -0.5),
    }


def reference(x, c, w_ada, b_ada, norm_pre_mix, norm_post_mix, w_in, dn_conv_w, dn_a_log, dn_dt_bias,
              dn_norm_w, cf_pw1_b, cf_dw_w, cf_dw_b, cf_ln_w, cf_ln_b, w_out, norm_pre_ffn, norm_post_ffn,
              w_router_group, b_router_group, w_router_expert, b_router_expert, w_gate, w_up, w_down):
    c_act = jax.nn.silu(c)
    for l in range(DEPTH):
        mod = jnp.einsum("bd,de->be", c_act, w_ada[l]) + b_ada[l]
        shift1, scale1, gate1, shift2, scale2, gate2 = jnp.split(mod[:, None, :], N_MOD, axis=-1)
        h = rms_norm(x, norm_pre_mix[l]) * (1.0 + scale1) + shift1
        h = hybrid_mixer(h, w_in[l], dn_conv_w[l], dn_a_log[l], dn_dt_bias[l], dn_norm_w[l],
                         cf_pw1_b[l], cf_dw_w[l], cf_dw_b[l], cf_ln_w[l], cf_ln_b[l], w_out[l])
        x = x + gate1 * rms_norm(h, norm_post_mix[l])
        h = rms_norm(x, norm_pre_ffn[l]) * (1.0 + scale2) + shift2
        h = hierarchical_moe(h, w_router_group[l], b_router_group[l], w_router_expert[l], b_router_expert[l],
                             w_gate[l], w_up[l], w_down[l])
        x = x + gate2 * rms_norm(h, norm_post_ffn[l])
    return x
```

```python
import functools

import jax
import jax.numpy as jnp
from jax import lax
from jax.experimental import pallas as pl
from jax.experimental.pallas import tpu as pltpu

F32 = jnp.float32
BF16 = jnp.bfloat16
EPS = 1e-6

DN_HEADS = 4
HEAD_DIM = 128
DN_WIDTH = DN_HEADS * HEAD_DIM
DN_CONV = 4
DN_CHUNK = 64
CF_KERNEL = 31
N_GROUPS = 8
EXPERTS_PER_GROUP = 8
N_EXPERTS = N_GROUPS * EXPERTS_PER_GROUP
TOP_K = 2

LANES = 128
QKV_HALO = 8
CF_HALO = 32
SEQ_TILE = 256
ROUTER_TILE = 512
EXPERT_BLOCK = 256
COMBINE_TILE = 256
VMEM_LIMIT = 56 * 1024 * 1024


def _dot(a, b):
    return jnp.dot(a, b, preferred_element_type=F32)


def _dot_nt(a, b):
    return lax.dot_general(a, b, (((1,), (1,)), ((), ())), preferred_element_type=F32)


def _dot_tn(a, b):
    return lax.dot_general(a, b, (((0,), (0,)), ((), ())), preferred_element_type=F32)


def _split3(x):
    hi = x.astype(BF16)
    r1 = x - hi.astype(F32)
    mid = r1.astype(BF16)
    lo = (r1 - mid.astype(F32)).astype(BF16)
    return hi, mid, lo


def _silu(x):
    return x * jax.nn.sigmoid(x)


def _softplus(x):
    return jnp.maximum(x, 0.0) + jnp.log1p(jnp.exp(-jnp.abs(x)))


def _ada_kernel(c_ref, w_ref, b_ref, o_ref):
    c = c_ref[...]
    ca = _silu(c)
    c_hi, c_mid, c_lo = _split3(ca)
    w_hi, w_mid, w_lo = _split3(w_ref[...])
    acc = _dot(c_hi, w_hi)
    acc += _dot(c_hi, w_mid) + _dot(c_mid, w_hi)
    acc += _dot(c_hi, w_lo) + _dot(c_mid, w_mid) + _dot(c_lo, w_hi)
    o_ref[...] = acc + b_ref[...]


def _ada_call(c, w, b):
    bsz, d = c.shape
    n = w.shape[1]
    tn = 512
    return pl.pallas_call(
        _ada_kernel,
        grid=(n // tn,),
        in_specs=[pl.BlockSpec((bsz, d), lambda i: (0, 0)),
                  pl.BlockSpec((d, tn), lambda i: (0, i)),
                  pl.BlockSpec((1, tn), lambda i: (0, i))],
        out_specs=pl.BlockSpec((bsz, tn), lambda i: (0, i)),
        out_shape=jax.ShapeDtypeStruct((bsz, n), F32),
        compiler_params=pltpu.CompilerParams(dimension_semantics=("arbitrary",),
                                             vmem_limit_bytes=VMEM_LIMIT),
        name="ada",
    )(c, w, b.reshape(1, n))


def _causal_conv(ext_ref, w_ref, n_taps, first, rows, row_blk, col_blk, bias=None):
    cols = ext_ref.shape[1]
    out_rows = []
    for r0 in range(0, rows, row_blk):
        out_cols = []
        for c0 in range(0, cols, col_blk):
            acc = jnp.zeros((row_blk, col_blk), F32)
            for k in range(n_taps):
                acc = acc + w_ref[k:k + 1, c0:c0 + col_blk] * ext_ref[first + k + r0:first + k + r0 + row_blk,
                                                                      c0:c0 + col_blk]
            if bias is not None:
                acc = acc + bias[:, c0:c0 + col_blk]
            out_cols.append(acc)
        out_rows.append(out_cols)
    return out_rows


def _mixer_kernel(x_ref, mod_ref, npre_ref, npost_ref, wqkv_ref, wz_ref, wba_ref, wcf_ref,
                  convw_ref, alog_ref, dtb_ref, dnw_ref, pw1b_ref, dww_ref, dwb_ref, lnw_ref, lnb_ref,
                  wout_ref, o_ref, qkv_ext, qkv_act, cf_ext, state, mixed):
    ts = x_ref.shape[1]
    n_chunks = ts // DN_CHUNK

    @pl.when(pl.program_id(1) == 0)
    def _():
        qkv_ext[0:QKV_HALO, :] = jnp.zeros((QKV_HALO, qkv_ext.shape[1]), F32)
        cf_ext[0:CF_HALO, :] = jnp.zeros((CF_HALO, cf_ext.shape[1]), F32)
        state[...] = jnp.zeros(state.shape, F32)

    x = x_ref[0]
    mod = mod_ref[0]
    shift1, scale1, gate1 = mod[0:1], mod[1:2], mod[2:3]
    h = x * lax.rsqrt(jnp.mean(x * x, axis=-1, keepdims=True) + EPS)
    h = h * npre_ref[...] * (1.0 + scale1) + shift1
    hb = h.astype(BF16)

    qkv_ext[QKV_HALO:QKV_HALO + ts, :] = _dot(hb, wqkv_ref[...])
    conv = _causal_conv(qkv_ext, convw_ref, DN_CONV, QKV_HALO - (DN_CONV - 1), ts, 32, 512)
    for ri, row in enumerate(conv):
        for ci, blk in enumerate(row):
            qkv_act[ri * 32:(ri + 1) * 32, ci * 512:(ci + 1) * 512] = _silu(blk)
    qkv_ext[0:QKV_HALO, :] = qkv_ext[ts:ts + QKV_HALO, :]

    z = _dot(hb, wz_ref[...])
    ba = _dot(hb, wba_ref[...])
    beta_all = jax.nn.sigmoid(ba)
    g_all = -jnp.exp(alog_ref[...]) * _softplus(ba + dtb_ref[...])

    ri = lax.broadcasted_iota(jnp.int32, (ts, ts), 0)
    ci = lax.broadcasted_iota(jnp.int32, (ts, ts), 1)
    tri = jnp.where((ri // DN_CHUNK == ci // DN_CHUNK) & (ci <= ri), 1.0, 0.0).astype(BF16)
    g_hi, g_mid, g_lo = _split3(g_all)
    gcum = _dot(tri, g_hi) + _dot(tri, g_mid) + _dot(tri, g_lo)
    gcum_t = gcum.T
    exp_g = jnp.exp(gcum)

    r64 = lax.broadcasted_iota(jnp.int32, (DN_CHUNK, DN_CHUNK), 0)
    c64 = lax.broadcasted_iota(jnp.int32, (DN_CHUNK, DN_CHUNK), 1)
    causal = c64 <= r64
    strict = c64 < r64
    eye = jnp.where(c64 == r64, 1.0, 0.0).astype(F32)
    dnw = dnw_ref[...]

    for hd in range(DN_HEADS):
        lo = hd * HEAD_DIM
        qh = qkv_act[:, lo:lo + HEAD_DIM]
        kh = qkv_act[:, DN_WIDTH + lo:DN_WIDTH + lo + HEAD_DIM]
        vh = qkv_act[:, 2 * DN_WIDTH + lo:2 * DN_WIDTH + lo + HEAD_DIM]
        qn = qh * lax.rsqrt(jnp.sum(qh * qh, axis=-1, keepdims=True) + EPS) * (HEAD_DIM ** -0.5)
        kn = kh * lax.rsqrt(jnp.sum(kh * kh, axis=-1, keepdims=True) + EPS)
        beta_h = beta_all[:, hd:hd + 1]
        gc_h = gcum[:, DN_HEADS + hd:DN_HEADS + hd + 1]
        eg_h = exp_g[:, DN_HEADS + hd:DN_HEADS + hd + 1]
        k_beta = kn * beta_h
        v_beta = vh * beta_h
        kbg = k_beta * eg_h
        qg = qn * eg_h
        zg = _silu(z[:, lo:lo + HEAD_DIM])
        s_h = state[hd]
        for ch in range(n_chunks):
            sl = slice(ch * DN_CHUNK, (ch + 1) * DN_CHUNK)
            gc_col = gc_h[sl]
            gc_row = gcum_t[DN_HEADS + hd:DN_HEADS + hd + 1, sl]
            decay = jnp.where(causal, jnp.exp(gc_col - gc_row), 0.0)
            kc = kn[sl].astype(BF16)
            a = jnp.where(strict, _dot_nt(k_beta[sl].astype(BF16), kc) * decay, 0.0)
            t_inv = eye - a
            pw = a
            for _ in range(5):
                pwb = pw.astype(BF16)
                pw = _dot(pwb, pwb)
                t_inv = t_inv + _dot(t_inv.astype(BF16), pw.astype(BF16))
            rhs = jnp.concatenate([v_beta[sl], kbg[sl]], axis=1).astype(BF16)
            sol = _dot(t_inv.astype(BF16), rhs)
            u, w = sol[:, :HEAD_DIM], sol[:, HEAD_DIM:]
            attn = _dot_nt(qn[sl].astype(BF16), kc) * decay
            g_last = gc_col[DN_CHUNK - 1:DN_CHUNK]
            k_dec = kn[sl] * jnp.exp(g_last - gc_col)
            sb = s_h.astype(BF16)
            v_new = u - _dot(w.astype(BF16), sb)
            o = _dot(qg[sl].astype(BF16), sb) + _dot(attn.astype(BF16), v_new.astype(BF16))
            s_h = s_h * jnp.exp(g_last) + _dot_tn(k_dec.astype(BF16), v_new.astype(BF16))
            on = o * lax.rsqrt(jnp.mean(o * o, axis=-1, keepdims=True) + EPS) * dnw * zg[sl]
            mixed[sl, lo:lo + HEAD_DIM] = on.astype(BF16)
        state[hd] = s_h

    cf_pre = _dot(hb, wcf_ref[...]) + pw1b_ref[...]
    cfw = cf_ext.shape[1]
    cf_ext[CF_HALO:CF_HALO + ts, :] = cf_pre[:, :cfw] * jax.nn.sigmoid(cf_pre[:, cfw:])
    conv = _causal_conv(cf_ext, dww_ref, CF_KERNEL, CF_HALO - (CF_KERNEL - 1), ts, 32, cfw, bias=dwb_ref[...])
    for ri_, row in enumerate(conv):
        cf = row[0]
        mu = jnp.mean(cf, axis=-1, keepdims=True)
        xc = cf - mu
        var = jnp.mean(xc * xc, axis=-1, keepdims=True)
        cfn = xc * lax.rsqrt(var + EPS) * lnw_ref[...] + lnb_ref[...]
        mixed[ri_ * 32:(ri_ + 1) * 32, DN_WIDTH:DN_WIDTH + cfw] = _silu(cfn).astype(BF16)
    cf_ext[0:CF_HALO, :] = cf_ext[ts:ts + CF_HALO, :]

    out = _dot(mixed[...], wout_ref[...])
    y = out * lax.rsqrt(jnp.mean(out * out, axis=-1, keepdims=True) + EPS) * npost_ref[...]
    o_ref[0] = x + gate1 * y


def _mixer_call(x, mod3, npre, npost, w_in, dn_conv_w, dn_a_log, dn_dt_bias, dn_norm_w,
                cf_pw1_b, cf_dw_w, cf_dw_b, cf_ln_w, cf_ln_b, w_out):
    bsz, seq, d = x.shape
    ts = SEQ_TILE
    cfw = cf_dw_w.shape[1]
    n_qkv = 3 * DN_WIDTH
    wqkv = w_in[:, :n_qkv].astype(BF16)
    wz = w_in[:, n_qkv:n_qkv + DN_WIDTH].astype(BF16)
    wba = jnp.pad(w_in[:, n_qkv + DN_WIDTH:n_qkv + DN_WIDTH + 2 * DN_HEADS],
                  ((0, 0), (0, LANES - 2 * DN_HEADS))).astype(BF16)
    wcf = w_in[:, n_qkv + DN_WIDTH + 2 * DN_HEADS:].astype(BF16)
    alog = jnp.pad(dn_a_log, (DN_HEADS, LANES - 2 * DN_HEADS)).reshape(1, LANES)
    dtb = jnp.pad(dn_dt_bias, (DN_HEADS, LANES - 2 * DN_HEADS)).reshape(1, LANES)
    dww = jnp.pad(cf_dw_w, ((0, 32 - CF_KERNEL), (0, 0)))
    convw = jnp.pad(dn_conv_w, ((0, 8 - DN_CONV), (0, 0)))

    def full(a):
        return pl.BlockSpec(a.shape, lambda b, j: (0,) * a.ndim)

    row = lambda a: a.reshape(1, -1)
    operands = [x, mod3, row(npre), row(npost), wqkv, wz, wba, wcf, convw, alog, dtb, row(dn_norm_w),
                row(cf_pw1_b), dww, row(cf_dw_b), row(cf_ln_w), row(cf_ln_b), w_out.astype(BF16)]
    in_specs = [pl.BlockSpec((1, ts, d), lambda b, j: (b, j, 0)),
                pl.BlockSpec((1,) + mod3.shape[1:], lambda b, j: (b, 0, 0))]
    in_specs += [full(a) for a in operands[2:]]
    return pl.pallas_call(
        _mixer_kernel,
        grid=(bsz, seq // ts),
        in_specs=in_specs,
        out_specs=pl.BlockSpec((1, ts, d), lambda b, j: (b, j, 0)),
        out_shape=jax.ShapeDtypeStruct((bsz, seq, d), F32),
        scratch_shapes=[pltpu.VMEM((QKV_HALO + ts, n_qkv), F32),
                        pltpu.VMEM((ts, n_qkv), F32),
                        pltpu.VMEM((CF_HALO + ts, cfw), F32),
                        pltpu.VMEM((DN_HEADS, HEAD_DIM, HEAD_DIM), F32),
                        pltpu.VMEM((ts, DN_WIDTH + cfw), BF16)],
        compiler_params=pltpu.CompilerParams(dimension_semantics=("arbitrary", "arbitrary"),
                                             vmem_limit_bytes=VMEM_LIMIT),
        name="mixer",
    )(*operands)


def _router_kernel(x_ref, mod_ref, nw_ref, wr_ref, br_ref, h_ref, info_ref, cnt_ref, carry):
    tt = x_ref.shape[1]

    @pl.when((pl.program_id(0) == 0) & (pl.program_id(1) == 0))
    def _():
        carry[...] = jnp.zeros(carry.shape, F32)

    x = x_ref[0]
    mod = mod_ref[0]
    shift2, scale2 = mod[3:4], mod[4:5]
    h = x * lax.rsqrt(jnp.mean(x * x, axis=-1, keepdims=True) + EPS)
    h = h * nw_ref[...] * (1.0 + scale2) + shift2
    h_ref[0] = h

    h_hi, h_mid, h_lo = _split3(h)
    w_hi, w_mid, w_lo = _split3(wr_ref[...])
    logits = _dot(h_hi, w_hi)
    logits += _dot(h_hi, w_mid) + _dot(h_mid, w_hi)
    logits += _dot(h_hi, w_lo) + _dot(h_mid, w_mid) + _dot(h_lo, w_hi)
    logits = logits + br_ref[...]

    lane = lax.broadcasted_iota(jnp.int32, (tt, LANES), 1)
    neg = -jnp.inf
    is_grp = (lane >= N_EXPERTS) & (lane < N_EXPERTS + N_GROUPS)
    gl = jnp.where(is_grp, logits, neg)
    gmax = jnp.max(gl, axis=-1, keepdims=True)
    gsum = jnp.sum(jnp.where(is_grp, jnp.exp(gl - gmax), 0.0), axis=-1, keepdims=True)
    grp_p = 1.0 / gsum
    grp_lane = jnp.min(jnp.where(is_grp & (gl == gmax), lane, LANES), axis=-1, keepdims=True)
    grp_idx = grp_lane - N_EXPERTS

    in_grp = (lane < N_EXPERTS) & (lane // EXPERTS_PER_GROUP == grp_idx)
    el = jnp.where(in_grp, logits, neg)
    m1 = jnp.max(el, axis=-1, keepdims=True)
    e1 = jnp.min(jnp.where(in_grp & (el == m1), lane, LANES), axis=-1, keepdims=True)
    el2 = jnp.where(lane == e1, neg, el)
    m2 = jnp.max(el2, axis=-1, keepdims=True)
    e2 = jnp.min(jnp.where(in_grp & (lane != e1) & (el2 == m2), lane, LANES), axis=-1, keepdims=True)
    r = jnp.exp(m2 - m1)
    w1 = grp_p / (1.0 + r)
    w2 = grp_p * r / (1.0 + r)

    hit1 = lane == e1
    hit2 = lane == e2
    onehot = jnp.where(hit1 | hit2, 1.0, 0.0)
    rr = lax.broadcasted_iota(jnp.int32, (tt, tt), 0)
    cc = lax.broadcasted_iota(jnp.int32, (tt, tt), 1)
    strict = jnp.where(cc < rr, 1.0, 0.0).astype(BF16)
    prefix = _dot(strict, onehot.astype(BF16)) + carry[0:1, :]
    rank1 = jnp.sum(jnp.where(hit1, prefix, 0.0), axis=-1, keepdims=True)
    rank2 = jnp.sum(jnp.where(hit2, prefix, 0.0), axis=-1, keepdims=True)
    total = carry[0:1, :] + jnp.sum(onehot, axis=0, keepdims=True)
    carry[...] = jnp.broadcast_to(total, carry.shape)
    cnt_ref[...] = jnp.broadcast_to(total, cnt_ref.shape)

    info = jnp.where(lane == 0, e1.astype(F32), 0.0)
    info = jnp.where(lane == 1, e2.astype(F32), info)
    info = jnp.where(lane == 2, w1, info)
    info = jnp.where(lane == 3, w2, info)
    info = jnp.where(lane == 4, rank1, info)
    info = jnp.where(lane == 5, rank2, info)
    info_ref[0] = info


def _router_call(x1, mod3, norm_w, w_router_group, b_router_group, w_router_expert, b_router_expert):
    bsz, seq, d = x1.shape
    tt = ROUTER_TILE
    pad = LANES - N_EXPERTS - N_GROUPS
    wr = jnp.pad(jnp.concatenate([w_router_expert, w_router_group], axis=1), ((0, 0), (0, pad)))
    br = jnp.pad(jnp.concatenate([b_router_expert, b_router_group]), (0, pad)).reshape(1, LANES)
    return pl.pallas_call(
        _router_kernel,
        grid=(bsz, seq // tt),
        in_specs=[pl.BlockSpec((1, tt, d), lambda b, j: (b, j, 0)),
                  pl.BlockSpec((1,) + mod3.shape[1:], lambda b, j: (b, 0, 0)),
                  pl.BlockSpec((1, d), lambda b, j: (0, 0)),
                  pl.BlockSpec((d, LANES), lambda b, j: (0, 0)),
                  pl.BlockSpec((1, LANES), lambda b, j: (0, 0))],
        out_specs=[pl.BlockSpec((1, tt, d), lambda b, j: (b, j, 0)),
                   pl.BlockSpec((1, tt, LANES), lambda b, j: (b, j, 0)),
                   pl.BlockSpec((8, LANES), lambda b, j: (0, 0))],
        out_shape=[jax.ShapeDtypeStruct((bsz, seq, d), F32),
                   jax.ShapeDtypeStruct((bsz, seq, LANES), F32),
                   jax.ShapeDtypeStruct((8, LANES), F32)],
        scratch_shapes=[pltpu.VMEM((8, LANES), F32)],
        compiler_params=pltpu.CompilerParams(dimension_semantics=("arbitrary", "arbitrary"),
                                             vmem_limit_bytes=VMEM_LIMIT),
        name="router",
    )(x1, mod3, norm_w.reshape(1, d), wr, br)


def _expert_kernel(nused_ref, be_ref, tok_cur_ref, tok_nxt_ref, h_hbm, wg_ref, wu_ref, wd_ref, y_ref,
                   xbuf, sems):
    i = pl.program_id(0)
    n_used = nused_ref[0]
    bm = xbuf.shape[1]

    def start_gather(tok_ref, slot):
        def body(r, carry):
            tok = tok_ref[0, 0, r]
            pltpu.make_async_copy(h_hbm.at[pl.ds(tok, 1), :], xbuf.at[slot, pl.ds(r, 1), :],
                                  sems.at[slot]).start()
            return carry
        lax.fori_loop(0, bm, body, 0)

    @pl.when((i == 0) & (n_used > 0))
    def _():
        start_gather(tok_cur_ref, 0)

    @pl.when(i + 1 < n_used)
    def _():
        start_gather(tok_nxt_ref, (i + 1) % 2)

    @pl.when(i < n_used)
    def _():
        slot = i % 2
        pltpu.make_async_copy(h_hbm.at[pl.ds(0, bm), :], xbuf.at[slot], sems.at[slot]).wait()
        xb = xbuf[slot].astype(BF16)
        gate = _dot(xb, wg_ref[0].astype(BF16))
        up = _dot(xb, wu_ref[0].astype(BF16))
        hid = (_silu(gate) * up).astype(BF16)
        y_ref[...] = _dot(hid, wd_ref[0].astype(BF16))

    @pl.when(i >= n_used)
    def _():
        y_ref[...] = jnp.zeros(y_ref.shape, F32)


def _expert_call(h2, row_tok, block_e, n_used, w_gate, w_up, w_down):
    t, d = h2.shape
    bm = EXPERT_BLOCK
    n_blocks = row_tok.shape[0] // bm
    de = w_gate.shape[2]
    tok3 = row_tok.reshape(n_blocks, 1, bm)
    smem_blk = lambda f: pl.BlockSpec((1, 1, bm), f, memory_space=pltpu.SMEM)
    grid_spec = pltpu.PrefetchScalarGridSpec(
        num_scalar_prefetch=2,
        grid=(n_blocks,),
        in_specs=[smem_blk(lambda i, nu, be: (i, 0, 0)),
                  smem_blk(lambda i, nu, be: (jnp.minimum(i + 1, n_blocks - 1), 0, 0)),
                  pl.BlockSpec(memory_space=pl.ANY),
                  pl.BlockSpec((1, d, de), lambda i, nu, be: (be[i], 0, 0)),
                  pl.BlockSpec((1, d, de), lambda i, nu, be: (be[i], 0, 0)),
                  pl.BlockSpec((1, de, d), lambda i, nu, be: (be[i], 0, 0))],
        out_specs=pl.BlockSpec((bm, d), lambda i, nu, be: (i, 0)),
        scratch_shapes=[pltpu.VMEM((2, bm, d), F32), pltpu.SemaphoreType.DMA((2,))],
    )
    return pl.pallas_call(
        _expert_kernel,
        grid_spec=grid_spec,
        out_shape=jax.ShapeDtypeStruct((n_blocks * bm, d), F32),
        compiler_params=pltpu.CompilerParams(dimension_semantics=("arbitrary",),
                                             vmem_limit_bytes=VMEM_LIMIT),
        name="experts",
    )(n_used, block_e, tok3, tok3, h2, w_gate, w_up, w_down)


def _combine_kernel(pos_ref, x_ref, info_ref, mod_ref, nw_ref, y_hbm, o_ref, ybuf, sem):
    tt = x_ref.shape[1]

    def body(r, carry):
        for k in range(TOP_K):
            p = pos_ref[0, 0, k * tt + r]
            pltpu.make_async_copy(y_hbm.at[pl.ds(p, 1), :], ybuf.at[k, pl.ds(r, 1), :], sem.at[0]).start()
        return carry
    lax.fori_loop(0, tt, body, 0)
    info = info_ref[0]
    w1 = info[:, 2:3]
    w2 = info[:, 3:4]
    x = x_ref[0]
    gate2 = mod_ref[0][5:6]
    for k in range(TOP_K):
        pltpu.make_async_copy(y_hbm.at[pl.ds(0, tt), :], ybuf.at[k], sem.at[0]).wait()
    moe = ybuf[0] * w1 + ybuf[1] * w2
    y = moe * lax.rsqrt(jnp.mean(moe * moe, axis=-1, keepdims=True) + EPS) * nw_ref[...]
    o_ref[0] = x + gate2 * y


def _combine_call(x1, info, mod3, norm_w, y, pos):
    bsz, seq, d = x1.shape
    tt = COMBINE_TILE
    nj = seq // tt
    pos3 = pos.reshape(bsz * nj, tt, TOP_K).transpose(0, 2, 1).reshape(bsz * nj, 1, TOP_K * tt)
    return pl.pallas_call(
        _combine_kernel,
        grid=(bsz, nj),
        in_specs=[pl.BlockSpec((1, 1, TOP_K * tt), lambda b, j: (b * nj + j, 0, 0), memory_space=pltpu.SMEM),
                  pl.BlockSpec((1, tt, d), lambda b, j: (b, j, 0)),
                  pl.BlockSpec((1, tt, LANES), lambda b, j: (b, j, 0)),
                  pl.BlockSpec((1,) + mod3.shape[1:], lambda b, j: (b, 0, 0)),
                  pl.BlockSpec((1, d), lambda b, j: (0, 0)),
                  pl.BlockSpec(memory_space=pl.ANY)],
        out_specs=pl.BlockSpec((1, tt, d), lambda b, j: (b, j, 0)),
        out_shape=jax.ShapeDtypeStruct((bsz, seq, d), F32),
        scratch_shapes=[pltpu.VMEM((TOP_K, tt, d), F32), pltpu.SemaphoreType.DMA((1,))],
        compiler_params=pltpu.CompilerParams(dimension_semantics=("arbitrary", "arbitrary"),
                                             vmem_limit_bytes=VMEM_LIMIT),
        name="combine",
    )(pos3, x1, info, mod3, norm_w.reshape(1, d), y)


def _layer(x, mod, norm_pre_mix, norm_post_mix, w_in, dn_conv_w, dn_a_log, dn_dt_bias, dn_norm_w,
           cf_pw1_b, cf_dw_w, cf_dw_b, cf_ln_w, cf_ln_b, w_out, norm_pre_ffn, norm_post_ffn,
           w_router_group, b_router_group, w_router_expert, b_router_expert, w_gate, w_up, w_down):
    bsz, seq, d = x.shape
    t = bsz * seq
    mod3 = mod.reshape(bsz, -1, d)
    x1 = _mixer_call(x, mod3, norm_pre_mix, norm_post_mix, w_in, dn_conv_w, dn_a_log, dn_dt_bias, dn_norm_w,
                     cf_pw1_b, cf_dw_w, cf_dw_b, cf_ln_w, cf_ln_b, w_out)
    h2, info, cnt = _router_call(x1, mod3, norm_pre_ffn, w_router_group, b_router_group,
                                 w_router_expert, b_router_expert)

    bm = EXPERT_BLOCK
    info2 = info.reshape(t, LANES)
    expert_id = info2[:, 0:TOP_K].astype(jnp.int32)
    rank = info2[:, 4:4 + TOP_K].astype(jnp.int32)
    counts = cnt[0, :N_EXPERTS].astype(jnp.int32)
    padded = (counts + bm - 1) // bm * bm
    pend = jnp.cumsum(padded)
    pstart = pend - padded
    pos = pstart[expert_id] + rank
    n_blocks = -(-(t * TOP_K) // bm) + N_EXPERTS
    tok = jnp.broadcast_to(jnp.arange(t, dtype=jnp.int32)[:, None], (t, TOP_K))
    row_tok = jnp.zeros((n_blocks * bm,), jnp.int32).at[pos.reshape(-1)].set(tok.reshape(-1))
    block_start = jnp.arange(n_blocks, dtype=jnp.int32) * bm
    block_e = jnp.minimum(jnp.sum(pend[None, :] <= block_start[:, None], axis=1), N_EXPERTS - 1).astype(jnp.int32)
    n_used = (pend[-1] // bm).astype(jnp.int32).reshape(1)

    y = _expert_call(h2.reshape(t, d), row_tok, block_e, n_used, w_gate, w_up, w_down)
    return _combine_call(x1, info, mod3, norm_post_ffn, y, pos)


def kernel(x, c, w_ada, b_ada, norm_pre_mix, norm_post_mix, w_in, dn_conv_w, dn_a_log, dn_dt_bias, dn_norm_w,
           cf_pw1_b, cf_dw_w, cf_dw_b, cf_ln_w, cf_ln_b, w_out, norm_pre_ffn, norm_post_ffn,
           w_router_group, b_router_group, w_router_expert, b_router_expert, w_gate, w_up, w_down):
    depth = w_ada.shape[0]
    for l in range(depth):
        mod = _ada_call(c, w_ada[l], b_ada[l])
        x = _layer(x, mod, norm_pre_mix[l], norm_post_mix[l], w_in[l], dn_conv_w[l], dn_a_log[l],
                   dn_dt_bias[l], dn_norm_w[l], cf_pw1_b[l], cf_dw_w[l], cf_dw_b[l], cf_ln_w[l], cf_ln_b[l],
                   w_out[l], norm_pre_ffn[l], norm_post_ffn[l], w_router_group[l], b_router_group[l],
                   w_router_expert[l], b_router_expert[l], w_gate[l], w_up[l], w_down[l])
    return x
```

```python
import functools

import jax
import jax.numpy as jnp
from jax import lax
from jax.experimental import pallas as pl
from jax.experimental.pallas import tpu as pltpu

F32 = jnp.float32
BF16 = jnp.bfloat16
EPS = 1e-6

DN_HEADS = 4
HEAD_DIM = 128
DN_WIDTH = DN_HEADS * HEAD_DIM
DN_CONV = 4
DN_CHUNK = 64
CF_KERNEL = 31
N_GROUPS = 8
EXPERTS_PER_GROUP = 8
N_EXPERTS = N_GROUPS * EXPERTS_PER_GROUP
TOP_K = 2

LANES = 128
QKV_HALO = 8
CF_HALO = 32
SEQ_TILE = 256
ROUTER_TILE = 512
EXPERT_BLOCK = 256
COMBINE_TILE = 256
VMEM_LIMIT = 56 * 1024 * 1024


def _dot(a, b):
    return jnp.dot(a, b, preferred_element_type=F32)


def _dot_nt(a, b):
    return lax.dot_general(a, b, (((1,), (1,)), ((), ())), preferred_element_type=F32)


def _dot_tn(a, b):
    return lax.dot_general(a, b, (((0,), (0,)), ((), ())), preferred_element_type=F32)


def _split3(x):
    hi = x.astype(BF16)
    r1 = x - hi.astype(F32)
    mid = r1.astype(BF16)
    lo = (r1 - mid.astype(F32)).astype(BF16)
    return hi, mid, lo


def _silu(x):
    return x * jax.nn.sigmoid(x)


def _softplus(x):
    return jnp.maximum(x, 0.0) + jnp.log1p(jnp.exp(-jnp.abs(x)))


def _ada_kernel(c_ref, w_ref, b_ref, o_ref):
    c = c_ref[...]
    ca = _silu(c)
    c_hi, c_mid, c_lo = _split3(ca)
    w_hi, w_mid, w_lo = _split3(w_ref[...])
    acc = _dot(c_hi, w_hi)
    acc += _dot(c_hi, w_mid) + _dot(c_mid, w_hi)
    acc += _dot(c_hi, w_lo) + _dot(c_mid, w_mid) + _dot(c_lo, w_hi)
    o_ref[...] = acc + b_ref[...]


def _ada_call(c, w, b):
    bsz, d = c.shape
    n = w.shape[1]
    tn = 512
    return pl.pallas_call(
        _ada_kernel,
        grid=(n // tn,),
        in_specs=[pl.BlockSpec((bsz, d), lambda i: (0, 0)),
                  pl.BlockSpec((d, tn), lambda i: (0, i)),
                  pl.BlockSpec((1, tn), lambda i: (0, i))],
        out_specs=pl.BlockSpec((bsz, tn), lambda i: (0, i)),
        out_shape=jax.ShapeDtypeStruct((bsz, n), F32),
        compiler_params=pltpu.CompilerParams(dimension_semantics=("arbitrary",),
                                             vmem_limit_bytes=VMEM_LIMIT),
        name="ada",
    )(c, w, b.reshape(1, n))


def _causal_conv(ext_ref, w_ref, n_taps, first, rows, row_blk, col_blk, bias=None):
    cols = ext_ref.shape[1]
    out_rows = []
    for r0 in range(0, rows, row_blk):
        out_cols = []
        for c0 in range(0, cols, col_blk):
            acc = jnp.zeros((row_blk, col_blk), F32)
            for k in range(n_taps):
                acc = acc + w_ref[k:k + 1, c0:c0 + col_blk] * ext_ref[first + k + r0:first + k + r0 + row_blk,
                                                                      c0:c0 + col_blk]
            if bias is not None:
                acc = acc + bias[:, c0:c0 + col_blk]
            out_cols.append(acc)
        out_rows.append(out_cols)
    return out_rows


def _mixer_kernel(x_ref, mod_ref, npre_ref, npost_ref, wqkv_ref, wz_ref, wba_ref, wcf_ref,
                  convw_ref, alog_ref, dtb_ref, dnw_ref, pw1b_ref, dww_ref, dwb_ref, lnw_ref, lnb_ref,
                  wout_ref, o_ref, qkv_ext, qkv_act, cf_ext, state, mixed):
    ts = x_ref.shape[1]
    n_chunks = ts // DN_CHUNK

    @pl.when(pl.program_id(1) == 0)
    def _():
        qkv_ext[0:QKV_HALO, :] = jnp.zeros((QKV_HALO, qkv_ext.shape[1]), F32)
        cf_ext[0:CF_HALO, :] = jnp.zeros((CF_HALO, cf_ext.shape[1]), F32)
        state[...] = jnp.zeros(state.shape, F32)

    x = x_ref[0]
    mod = mod_ref[0]
    shift1, scale1, gate1 = mod[0:1], mod[1:2], mod[2:3]
    h = x * lax.rsqrt(jnp.mean(x * x, axis=-1, keepdims=True) + EPS)
    h = h * npre_ref[...] * (1.0 + scale1) + shift1
    hb = h.astype(BF16)

    qkv_ext[QKV_HALO:QKV_HALO + ts, :] = _dot(hb, wqkv_ref[...])
    conv = _causal_conv(qkv_ext, convw_ref, DN_CONV, QKV_HALO - (DN_CONV - 1), ts, 32, 512)
    for ri, row in enumerate(conv):
        for ci, blk in enumerate(row):
            qkv_act[ri * 32:(ri + 1) * 32, ci * 512:(ci + 1) * 512] = _silu(blk)
    qkv_ext[0:QKV_HALO, :] = qkv_ext[ts:ts + QKV_HALO, :]

    z = _dot(hb, wz_ref[...])
    ba = _dot(hb, wba_ref[...])
    beta_all = jax.nn.sigmoid(ba)
    g_all = -jnp.exp(alog_ref[...]) * _softplus(ba + dtb_ref[...])

    ri = lax.broadcasted_iota(jnp.int32, (ts, ts), 0)
    ci = lax.broadcasted_iota(jnp.int32, (ts, ts), 1)
    tri = jnp.where((ri // DN_CHUNK == ci // DN_CHUNK) & (ci <= ri), 1.0, 0.0).astype(BF16)
    g_hi, g_mid, g_lo = _split3(g_all)
    gcum = _dot(tri, g_hi) + _dot(tri, g_mid) + _dot(tri, g_lo)
    gcum_t = gcum.T
    exp_g = jnp.exp(gcum)

    r64 = lax.broadcasted_iota(jnp.int32, (DN_CHUNK, DN_CHUNK), 0)
    c64 = lax.broadcasted_iota(jnp.int32, (DN_CHUNK, DN_CHUNK), 1)
    causal = c64 <= r64
    strict = c64 < r64
    eye = jnp.where(c64 == r64, 1.0, 0.0).astype(F32)
    dnw = dnw_ref[...]

    heads = []
    for hd in range(DN_HEADS):
        lo = hd * HEAD_DIM
        qh = qkv_act[:, lo:lo + HEAD_DIM]
        kh = qkv_act[:, DN_WIDTH + lo:DN_WIDTH + lo + HEAD_DIM]
        vh = qkv_act[:, 2 * DN_WIDTH + lo:2 * DN_WIDTH + lo + HEAD_DIM]
        qn = qh * lax.rsqrt(jnp.sum(qh * qh, axis=-1, keepdims=True) + EPS) * (HEAD_DIM ** -0.5)
        kn = kh * lax.rsqrt(jnp.sum(kh * kh, axis=-1, keepdims=True) + EPS)
        beta_h = beta_all[:, hd:hd + 1]
        gc_h = gcum[:, DN_HEADS + hd:DN_HEADS + hd + 1]
        eg_h = exp_g[:, DN_HEADS + hd:DN_HEADS + hd + 1]
        k_beta = kn * beta_h
        heads.append(dict(qn=qn, kn=kn, k_beta=k_beta, v_beta=vh * beta_h, kbg=k_beta * eg_h, qg=qn * eg_h,
                          gc=gc_h, zg=_silu(z[:, lo:lo + HEAD_DIM])))

    cells = [(hd, ch) for hd in range(DN_HEADS) for ch in range(n_chunks)]
    rows = lambda ch: slice(ch * DN_CHUNK, (ch + 1) * DN_CHUNK)

    decay, kq = {}, {}
    for hd, ch in cells:
        hv, sl = heads[hd], rows(ch)
        gc_row = gcum_t[DN_HEADS + hd:DN_HEADS + hd + 1, sl]
        decay[hd, ch] = jnp.where(causal, jnp.exp(hv["gc"][sl] - gc_row), 0.0)
        lhs = jnp.concatenate([hv["k_beta"][sl], hv["qn"][sl]], axis=0).astype(BF16)
        kq[hd, ch] = _dot_nt(lhs, hv["kn"][sl].astype(BF16))
    a = {c: jnp.where(strict, kq[c][:DN_CHUNK] * decay[c], 0.0) for c in cells}
    attn = {c: (kq[c][DN_CHUNK:] * decay[c]).astype(BF16) for c in cells}

    t_inv = {c: eye - a[c] for c in cells}
    pw = {c: a[c].astype(BF16) for c in cells}
    for _ in range(5):
        pw = {c: _dot(pw[c], pw[c]).astype(BF16) for c in cells}
        t_inv = {c: t_inv[c] + _dot(t_inv[c].astype(BF16), pw[c]) for c in cells}

    sol, aw, ks, glast = {}, {}, {}, {}
    for hd, ch in cells:
        hv, sl = heads[hd], rows(ch)
        rhs = jnp.concatenate([hv["v_beta"][sl], hv["kbg"][sl]], axis=1).astype(BF16)
        sol[hd, ch] = _dot(t_inv[hd, ch].astype(BF16), rhs).astype(BF16)
    for hd, ch in cells:
        hv, sl = heads[hd], rows(ch)
        gc_col = hv["gc"][sl]
        glast[hd, ch] = gc_col[DN_CHUNK - 1:DN_CHUNK]
        k_dec = (hv["kn"][sl] * jnp.exp(glast[hd, ch] - gc_col)).astype(BF16)
        aw[hd, ch] = _dot(attn[hd, ch], sol[hd, ch])
        ks[hd, ch] = _dot_tn(k_dec, sol[hd, ch])

    s_in = {}
    s_cur = [state[hd] for hd in range(DN_HEADS)]
    for ch in range(n_chunks):
        for hd in range(DN_HEADS):
            s_in[hd, ch] = s_cur[hd].astype(BF16)
            kd_u, kd_w = ks[hd, ch][:, :HEAD_DIM], ks[hd, ch][:, HEAD_DIM:]
            s_cur[hd] = s_cur[hd] * jnp.exp(glast[hd, ch]) + kd_u - _dot(kd_w.astype(BF16), s_in[hd, ch])
    for hd in range(DN_HEADS):
        state[hd] = s_cur[hd]

    for hd, ch in cells:
        hv, sl = heads[hd], rows(ch)
        lo = hd * HEAD_DIM
        q_eff = (hv["qg"][sl] - aw[hd, ch][:, HEAD_DIM:]).astype(BF16)
        o = _dot(q_eff, s_in[hd, ch]) + aw[hd, ch][:, :HEAD_DIM]
        on = o * lax.rsqrt(jnp.mean(o * o, axis=-1, keepdims=True) + EPS) * dnw * hv["zg"][sl]
        mixed[sl, lo:lo + HEAD_DIM] = on.astype(BF16)

    cf_pre = _dot(hb, wcf_ref[...]) + pw1b_ref[...]
    cfw = cf_ext.shape[1]
    cf_ext[CF_HALO:CF_HALO + ts, :] = cf_pre[:, :cfw] * jax.nn.sigmoid(cf_pre[:, cfw:])
    conv = _causal_conv(cf_ext, dww_ref, CF_KERNEL, CF_HALO - (CF_KERNEL - 1), ts, 32, cfw, bias=dwb_ref[...])
    for ri_, row in enumerate(conv):
        cf = row[0]
        mu = jnp.mean(cf, axis=-1, keepdims=True)
        xc = cf - mu
        var = jnp.mean(xc * xc, axis=-1, keepdims=True)
        cfn = xc * lax.rsqrt(var + EPS) * lnw_ref[...] + lnb_ref[...]
        mixed[ri_ * 32:(ri_ + 1) * 32, DN_WIDTH:DN_WIDTH + cfw] = _silu(cfn).astype(BF16)
    cf_ext[0:CF_HALO, :] = cf_ext[ts:ts + CF_HALO, :]

    out = _dot(mixed[...], wout_ref[...])
    y = out * lax.rsqrt(jnp.mean(out * out, axis=-1, keepdims=True) + EPS) * npost_ref[...]
    o_ref[0] = x + gate1 * y


def _mixer_call(x, mod3, npre, npost, w_in, dn_conv_w, dn_a_log, dn_dt_bias, dn_norm_w,
                cf_pw1_b, cf_dw_w, cf_dw_b, cf_ln_w, cf_ln_b, w_out):
    bsz, seq, d = x.shape
    ts = SEQ_TILE
    cfw = cf_dw_w.shape[1]
    n_qkv = 3 * DN_WIDTH
    wqkv = w_in[:, :n_qkv].astype(BF16)
    wz = w_in[:, n_qkv:n_qkv + DN_WIDTH].astype(BF16)
    wba = jnp.pad(w_in[:, n_qkv + DN_WIDTH:n_qkv + DN_WIDTH + 2 * DN_HEADS],
                  ((0, 0), (0, LANES - 2 * DN_HEADS))).astype(BF16)
    wcf = w_in[:, n_qkv + DN_WIDTH + 2 * DN_HEADS:].astype(BF16)
    alog = jnp.pad(dn_a_log, (DN_HEADS, LANES - 2 * DN_HEADS)).reshape(1, LANES)
    dtb = jnp.pad(dn_dt_bias, (DN_HEADS, LANES - 2 * DN_HEADS)).reshape(1, LANES)
    dww = jnp.pad(cf_dw_w, ((0, 32 - CF_KERNEL), (0, 0)))
    convw = jnp.pad(dn_conv_w, ((0, 8 - DN_CONV), (0, 0)))

    def full(a):
        return pl.BlockSpec(a.shape, lambda b, j: (0,) * a.ndim)

    row = lambda a: a.reshape(1, -1)
    operands = [x, mod3, row(npre), row(npost), wqkv, wz, wba, wcf, convw, alog, dtb, row(dn_norm_w),
                row(cf_pw1_b), dww, row(cf_dw_b), row(cf_ln_w), row(cf_ln_b), w_out.astype(BF16)]
    in_specs = [pl.BlockSpec((1, ts, d), lambda b, j: (b, j, 0)),
                pl.BlockSpec((1,) + mod3.shape[1:], lambda b, j: (b, 0, 0))]
    in_specs += [full(a) for a in operands[2:]]
    return pl.pallas_call(
        _mixer_kernel,
        grid=(bsz, seq // ts),
        in_specs=in_specs,
        out_specs=pl.BlockSpec((1, ts, d), lambda b, j: (b, j, 0)),
        out_shape=jax.ShapeDtypeStruct((bsz, seq, d), F32),
        scratch_shapes=[pltpu.VMEM((QKV_HALO + ts, n_qkv), F32),
                        pltpu.VMEM((ts, n_qkv), F32),
                        pltpu.VMEM((CF_HALO + ts, cfw), F32),
                        pltpu.VMEM((DN_HEADS, HEAD_DIM, HEAD_DIM), F32),
                        pltpu.VMEM((ts, DN_WIDTH + cfw), BF16)],
        compiler_params=pltpu.CompilerParams(dimension_semantics=("arbitrary", "arbitrary"),
                                             vmem_limit_bytes=VMEM_LIMIT),
        name="mixer",
    )(*operands)


def _router_kernel(x_ref, mod_ref, nw_ref, wr_ref, br_ref, h_ref, info_ref, cnt_ref, carry):
    tt = x_ref.shape[1]

    @pl.when((pl.program_id(0) == 0) & (pl.program_id(1) == 0))
    def _():
        carry[...] = jnp.zeros(carry.shape, F32)

    x = x_ref[0]
    mod = mod_ref[0]
    shift2, scale2 = mod[3:4], mod[4:5]
    h = x * lax.rsqrt(jnp.mean(x * x, axis=-1, keepdims=True) + EPS)
    h = h * nw_ref[...] * (1.0 + scale2) + shift2
    h_ref[0] = h

    h_hi, h_mid, h_lo = _split3(h)
    w_hi, w_mid, w_lo = _split3(wr_ref[...])
    logits = _dot(h_hi, w_hi)
    logits += _dot(h_hi, w_mid) + _dot(h_mid, w_hi)
    logits += _dot(h_hi, w_lo) + _dot(h_mid, w_mid) + _dot(h_lo, w_hi)
    logits = logits + br_ref[...]

    lane = lax.broadcasted_iota(jnp.int32, (tt, LANES), 1)
    neg = -jnp.inf
    is_grp = (lane >= N_EXPERTS) & (lane < N_EXPERTS + N_GROUPS)
    gl = jnp.where(is_grp, logits, neg)
    gmax = jnp.max(gl, axis=-1, keepdims=True)
    gsum = jnp.sum(jnp.where(is_grp, jnp.exp(gl - gmax), 0.0), axis=-1, keepdims=True)
    grp_p = 1.0 / gsum
    grp_lane = jnp.min(jnp.where(is_grp & (gl == gmax), lane, LANES), axis=-1, keepdims=True)
    grp_idx = grp_lane - N_EXPERTS

    in_grp = (lane < N_EXPERTS) & (lane // EXPERTS_PER_GROUP == grp_idx)
    el = jnp.where(in_grp, logits, neg)
    m1 = jnp.max(el, axis=-1, keepdims=True)
    e1 = jnp.min(jnp.where(in_grp & (el == m1), lane, LANES), axis=-1, keepdims=True)
    el2 = jnp.where(lane == e1, neg, el)
    m2 = jnp.max(el2, axis=-1, keepdims=True)
    e2 = jnp.min(jnp.where(in_grp & (lane != e1) & (el2 == m2), lane, LANES), axis=-1, keepdims=True)
    r = jnp.exp(m2 - m1)
    w1 = grp_p / (1.0 + r)
    w2 = grp_p * r / (1.0 + r)

    hit1 = lane == e1
    hit2 = lane == e2
    onehot = jnp.where(hit1 | hit2, 1.0, 0.0)
    rr = lax.broadcasted_iota(jnp.int32, (tt, tt), 0)
    cc = lax.broadcasted_iota(jnp.int32, (tt, tt), 1)
    strict = jnp.where(cc < rr, 1.0, 0.0).astype(BF16)
    prefix = _dot(strict, onehot.astype(BF16)) + carry[0:1, :]
    rank1 = jnp.sum(jnp.where(hit1, prefix, 0.0), axis=-1, keepdims=True)
    rank2 = jnp.sum(jnp.where(hit2, prefix, 0.0), axis=-1, keepdims=True)
    total = carry[0:1, :] + jnp.sum(onehot, axis=0, keepdims=True)
    carry[...] = jnp.broadcast_to(total, carry.shape)
    cnt_ref[...] = jnp.broadcast_to(total, cnt_ref.shape)

    info = jnp.where(lane == 0, e1.astype(F32), 0.0)
    info = jnp.where(lane == 1, e2.astype(F32), info)
    info = jnp.where(lane == 2, w1, info)
    info = jnp.where(lane == 3, w2, info)
    info = jnp.where(lane == 4, rank1, info)
    info = jnp.where(lane == 5, rank2, info)
    info_ref[0] = info


def _router_call(x1, mod3, norm_w, w_router_group, b_router_group, w_router_expert, b_router_expert):
    bsz, seq, d = x1.shape
    tt = ROUTER_TILE
    pad = LANES - N_EXPERTS - N_GROUPS
    wr = jnp.pad(jnp.concatenate([w_router_expert, w_router_group], axis=1), ((0, 0), (0, pad)))
    br = jnp.pad(jnp.concatenate([b_router_expert, b_router_group]), (0, pad)).reshape(1, LANES)
    return pl.pallas_call(
        _router_kernel,
        grid=(bsz, seq // tt),
        in_specs=[pl.BlockSpec((1, tt, d), lambda b, j: (b, j, 0)),
                  pl.BlockSpec((1,) + mod3.shape[1:], lambda b, j: (b, 0, 0)),
                  pl.BlockSpec((1, d), lambda b, j: (0, 0)),
                  pl.BlockSpec((d, LANES), lambda b, j: (0, 0)),
                  pl.BlockSpec((1, LANES), lambda b, j: (0, 0))],
        out_specs=[pl.BlockSpec((1, tt, d), lambda b, j: (b, j, 0)),
                   pl.BlockSpec((1, tt, LANES), lambda b, j: (b, j, 0)),
                   pl.BlockSpec((8, LANES), lambda b, j: (0, 0))],
        out_shape=[jax.ShapeDtypeStruct((bsz, seq, d), F32),
                   jax.ShapeDtypeStruct((bsz, seq, LANES), F32),
                   jax.ShapeDtypeStruct((8, LANES), F32)],
        scratch_shapes=[pltpu.VMEM((8, LANES), F32)],
        compiler_params=pltpu.CompilerParams(dimension_semantics=("arbitrary", "arbitrary"),
                                             vmem_limit_bytes=VMEM_LIMIT),
        name="router",
    )(x1, mod3, norm_w.reshape(1, d), wr, br)


def _expert_kernel(nused_ref, be_ref, tok_cur_ref, tok_nxt_ref, h_hbm, wg_ref, wu_ref, wd_ref, y_ref,
                   xbuf, sems):
    i = pl.program_id(0)
    n_used = nused_ref[0]
    bm = xbuf.shape[1]

    def start_gather(tok_ref, slot):
        def body(r, carry):
            tok = tok_ref[0, 0, r]
            pltpu.make_async_copy(h_hbm.at[pl.ds(tok, 1), :], xbuf.at[slot, pl.ds(r, 1), :],
                                  sems.at[slot]).start()
            return carry
        lax.fori_loop(0, bm, body, 0)

    @pl.when((i == 0) & (n_used > 0))
    def _():
        start_gather(tok_cur_ref, 0)

    @pl.when(i + 1 < n_used)
    def _():
        start_gather(tok_nxt_ref, (i + 1) % 2)

    @pl.when(i < n_used)
    def _():
        slot = i % 2
        pltpu.make_async_copy(h_hbm.at[pl.ds(0, bm), :], xbuf.at[slot], sems.at[slot]).wait()
        xb = xbuf[slot].astype(BF16)
        gate = _dot(xb, wg_ref[0].astype(BF16))
        up = _dot(xb, wu_ref[0].astype(BF16))
        hid = (_silu(gate) * up).astype(BF16)
        y_ref[...] = _dot(hid, wd_ref[0].astype(BF16))

    @pl.when(i >= n_used)
    def _():
        y_ref[...] = jnp.zeros(y_ref.shape, F32)


def _expert_call(h2, row_tok, block_e, n_used, w_gate, w_up, w_down):
    t, d = h2.shape
    bm = EXPERT_BLOCK
    n_blocks = row_tok.shape[0] // bm
    de = w_gate.shape[2]
    tok3 = row_tok.reshape(n_blocks, 1, bm)
    smem_blk = lambda f: pl.BlockSpec((1, 1, bm), f, memory_space=pltpu.SMEM)
    grid_spec = pltpu.PrefetchScalarGridSpec(
        num_scalar_prefetch=2,
        grid=(n_blocks,),
        in_specs=[smem_blk(lambda i, nu, be: (i, 0, 0)),
                  smem_blk(lambda i, nu, be: (jnp.minimum(i + 1, n_blocks - 1), 0, 0)),
                  pl.BlockSpec(memory_space=pl.ANY),
                  pl.BlockSpec((1, d, de), lambda i, nu, be: (be[i], 0, 0)),
                  pl.BlockSpec((1, d, de), lambda i, nu, be: (be[i], 0, 0)),
                  pl.BlockSpec((1, de, d), lambda i, nu, be: (be[i], 0, 0))],
        out_specs=pl.BlockSpec((bm, d), lambda i, nu, be: (i, 0)),
        scratch_shapes=[pltpu.VMEM((2, bm, d), F32), pltpu.SemaphoreType.DMA((2,))],
    )
    return pl.pallas_call(
        _expert_kernel,
        grid_spec=grid_spec,
        out_shape=jax.ShapeDtypeStruct((n_blocks * bm, d), F32),
        compiler_params=pltpu.CompilerParams(dimension_semantics=("arbitrary",),
                                             vmem_limit_bytes=VMEM_LIMIT),
        name="experts",
    )(n_used, block_e, tok3, tok3, h2, w_gate, w_up, w_down)


def _combine_kernel(pos_ref, x_ref, info_ref, mod_ref, nw_ref, y_hbm, o_ref, ybuf, sem):
    tt = x_ref.shape[1]

    def body(r, carry):
        for k in range(TOP_K):
            p = pos_ref[0, 0, k * tt + r]
            pltpu.make_async_copy(y_hbm.at[pl.ds(p, 1), :], ybuf.at[k, pl.ds(r, 1), :], sem.at[0]).start()
        return carry
    lax.fori_loop(0, tt, body, 0)
    info = info_ref[0]
    w1 = info[:, 2:3]
    w2 = info[:, 3:4]
    x = x_ref[0]
    gate2 = mod_ref[0][5:6]
    for k in range(TOP_K):
        pltpu.make_async_copy(y_hbm.at[pl.ds(0, tt), :], ybuf.at[k], sem.at[0]).wait()
    moe = ybuf[0] * w1 + ybuf[1] * w2
    y = moe * lax.rsqrt(jnp.mean(moe * moe, axis=-1, keepdims=True) + EPS) * nw_ref[...]
    o_ref[0] = x + gate2 * y


def _combine_call(x1, info, mod3, norm_w, y, pos):
    bsz, seq, d = x1.shape
    tt = COMBINE_TILE
    nj = seq // tt
    pos3 = pos.reshape(bsz * nj, tt, TOP_K).transpose(0, 2, 1).reshape(bsz * nj, 1, TOP_K * tt)
    return pl.pallas_call(
        _combine_kernel,
        grid=(bsz, nj),
        in_specs=[pl.BlockSpec((1, 1, TOP_K * tt), lambda b, j: (b * nj + j, 0, 0), memory_space=pltpu.SMEM),
                  pl.BlockSpec((1, tt, d), lambda b, j: (b, j, 0)),
                  pl.BlockSpec((1, tt, LANES), lambda b, j: (b, j, 0)),
                  pl.BlockSpec((1,) + mod3.shape[1:], lambda b, j: (b, 0, 0)),
                  pl.BlockSpec((1, d), lambda b, j: (0, 0)),
                  pl.BlockSpec(memory_space=pl.ANY)],
        out_specs=pl.BlockSpec((1, tt, d), lambda b, j: (b, j, 0)),
        out_shape=jax.ShapeDtypeStruct((bsz, seq, d), F32),
        scratch_shapes=[pltpu.VMEM((TOP_K, tt, d), F32), pltpu.SemaphoreType.DMA((1,))],
        compiler_params=pltpu.CompilerParams(dimension_semantics=("arbitrary", "arbitrary"),
                                             vmem_limit_bytes=VMEM_LIMIT),
        name="combine",
    )(pos3, x1, info, mod3, norm_w.reshape(1, d), y)


def _layer(x, mod, norm_pre_mix, norm_post_mix, w_in, dn_conv_w, dn_a_log, dn_dt_bias, dn_norm_w,
           cf_pw1_b, cf_dw_w, cf_dw_b, cf_ln_w, cf_ln_b, w_out, norm_pre_ffn, norm_post_ffn,
           w_router_group, b_router_group, w_router_expert, b_router_expert, w_gate, w_up, w_down):
    bsz, seq, d = x.shape
    t = bsz * seq
    mod3 = mod.reshape(bsz, -1, d)
    x1 = _mixer_call(x, mod3, norm_pre_mix, norm_post_mix, w_in, dn_conv_w, dn_a_log, dn_dt_bias, dn_norm_w,
                     cf_pw1_b, cf_dw_w, cf_dw_b, cf_ln_w, cf_ln_b, w_out)
    h2, info, cnt = _router_call(x1, mod3, norm_pre_ffn, w_router_group, b_router_group,
                                 w_router_expert, b_router_expert)

    bm = EXPERT_BLOCK
    info2 = info.reshape(t, LANES)
    expert_id = info2[:, 0:TOP_K].astype(jnp.int32)
    rank = info2[:, 4:4 + TOP_K].astype(jnp.int32)
    counts = cnt[0, :N_EXPERTS].astype(jnp.int32)
    padded = (counts + bm - 1) // bm * bm
    pend = jnp.cumsum(padded)
    pstart = pend - padded
    pos = pstart[expert_id] + rank
    n_blocks = -(-(t * TOP_K) // bm) + N_EXPERTS
    tok = jnp.broadcast_to(jnp.arange(t, dtype=jnp.int32)[:, None], (t, TOP_K))
    row_tok = jnp.zeros((n_blocks * bm,), jnp.int32).at[pos.reshape(-1)].set(tok.reshape(-1))
    block_start = jnp.arange(n_blocks, dtype=jnp.int32) * bm
    block_e = jnp.minimum(jnp.sum(pend[None, :] <= block_start[:, None], axis=1), N_EXPERTS - 1).astype(jnp.int32)
    n_used = (pend[-1] // bm).astype(jnp.int32).reshape(1)

    y = _expert_call(h2.reshape(t, d), row_tok, block_e, n_used, w_gate, w_up, w_down)
    return _combine_call(x1, info, mod3, norm_post_ffn, y, pos)


def kernel(x, c, w_ada, b_ada, norm_pre_mix, norm_post_mix, w_in, dn_conv_w, dn_a_log, dn_dt_bias, dn_norm_w,
           cf_pw1_b, cf_dw_w, cf_dw_b, cf_ln_w, cf_ln_b, w_out, norm_pre_ffn, norm_post_ffn,
           w_router_group, b_router_group, w_router_expert, b_router_expert, w_gate, w_up, w_down):
    depth = w_ada.shape[0]
    for l in range(depth):
        mod = _ada_call(c, w_ada[l], b_ada[l])
        x = _layer(x, mod, norm_pre_mix[l], norm_post_mix[l], w_in[l], dn_conv_w[l], dn_a_log[l],
                   dn_dt_bias[l], dn_norm_w[l], cf_pw1_b[l], cf_dw_w[l], cf_dw_b[l], cf_ln_w[l], cf_ln_b[l],
                   w_out[l], norm_pre_ffn[l], norm_post_ffn[l], w_router_group[l], b_router_group[l],
                   w_router_expert[l], b_router_expert[l], w_gate[l], w_up[l], w_down[l])
    return x
```

```python
import functools

import jax
import jax.numpy as jnp
from jax import lax
from jax.experimental import pallas as pl
from jax.experimental.pallas import tpu as pltpu

F32 = jnp.float32
BF16 = jnp.bfloat16
EPS = 1e-6

DN_HEADS = 4
HEAD_DIM = 128
DN_WIDTH = DN_HEADS * HEAD_DIM
DN_CONV = 4
DN_CHUNK = 64
CF_KERNEL = 31
N_GROUPS = 8
EXPERTS_PER_GROUP = 8
N_EXPERTS = N_GROUPS * EXPERTS_PER_GROUP
TOP_K = 2

LANES = 128
QKV_HALO = 8
CF_HALO = 32
SEQ_TILE = 256
ROUTER_TILE = 512
EXPERT_BLOCK = 256
COMBINE_TILE = 256
VMEM_LIMIT = 56 * 1024 * 1024


def _dot(a, b):
    return jnp.dot(a, b, preferred_element_type=F32)


def _dot_nt(a, b):
    return lax.dot_general(a, b, (((1,), (1,)), ((), ())), preferred_element_type=F32)


def _dot_tn(a, b):
    return lax.dot_general(a, b, (((0,), (0,)), ((), ())), preferred_element_type=F32)


def _split3(x):
    hi = x.astype(BF16)
    r1 = x - hi.astype(F32)
    mid = r1.astype(BF16)
    lo = (r1 - mid.astype(F32)).astype(BF16)
    return hi, mid, lo


def _silu(x):
    return x * jax.nn.sigmoid(x)


def _softplus(x):
    return jnp.maximum(x, 0.0) + jnp.log1p(jnp.exp(-jnp.abs(x)))


def _store_token_major(ref, val, base=0):
    n, d = val.shape
    pitch = d // LANES
    for j in range(pitch):
        ref[pl.ds(base + j, n, stride=pitch), :] = val[:, j * LANES:(j + 1) * LANES]


def _load_token_major(ref, n, d, base=0):
    pitch = d // LANES
    return jnp.concatenate([ref[pl.ds(base + j, n, stride=pitch), :] for j in range(pitch)], axis=1)


def _ada_kernel(c_ref, w_ref, b_ref, o_ref):
    c = c_ref[...]
    ca = _silu(c)
    c_hi, c_mid, c_lo = _split3(ca)
    w_hi, w_mid, w_lo = _split3(w_ref[...])
    acc = _dot(c_hi, w_hi)
    acc += _dot(c_hi, w_mid) + _dot(c_mid, w_hi)
    acc += _dot(c_hi, w_lo) + _dot(c_mid, w_mid) + _dot(c_lo, w_hi)
    o_ref[...] = acc + b_ref[...]


def _ada_call(c, w, b):
    bsz, d = c.shape
    n = w.shape[1]
    tn = 512
    return pl.pallas_call(
        _ada_kernel,
        grid=(n // tn,),
        in_specs=[pl.BlockSpec((bsz, d), lambda i: (0, 0)),
                  pl.BlockSpec((d, tn), lambda i: (0, i)),
                  pl.BlockSpec((1, tn), lambda i: (0, i))],
        out_specs=pl.BlockSpec((bsz, tn), lambda i: (0, i)),
        out_shape=jax.ShapeDtypeStruct((bsz, n), F32),
        compiler_params=pltpu.CompilerParams(dimension_semantics=("arbitrary",),
                                             vmem_limit_bytes=VMEM_LIMIT),
        name="ada",
    )(c, w, b.reshape(1, n))


def _causal_conv(ext_ref, w_ref, n_taps, first, rows, row_blk, col_blk, bias=None):
    cols = ext_ref.shape[1]
    out_rows = []
    for r0 in range(0, rows, row_blk):
        out_cols = []
        for c0 in range(0, cols, col_blk):
            acc = jnp.zeros((row_blk, col_blk), F32)
            for k in range(n_taps):
                acc = acc + w_ref[k:k + 1, c0:c0 + col_blk] * ext_ref[first + k + r0:first + k + r0 + row_blk,
                                                                      c0:c0 + col_blk]
            if bias is not None:
                acc = acc + bias[:, c0:c0 + col_blk]
            out_cols.append(acc)
        out_rows.append(out_cols)
    return out_rows


def _mixer_kernel(x_ref, mod_ref, npre_ref, npost_ref, wqkv_ref, wz_ref, wba_ref, wcf_ref,
                  convw_ref, alog_ref, dtb_ref, dnw_ref, pw1b_ref, dww_ref, dwb_ref, lnw_ref, lnb_ref,
                  wout_ref, o_ref, qkv_ext, qkv_act, cf_ext, state, mixed):
    ts = x_ref.shape[1]
    n_chunks = ts // DN_CHUNK

    @pl.when(pl.program_id(1) == 0)
    def _():
        qkv_ext[0:QKV_HALO, :] = jnp.zeros((QKV_HALO, qkv_ext.shape[1]), F32)
        cf_ext[0:CF_HALO, :] = jnp.zeros((CF_HALO, cf_ext.shape[1]), F32)
        state[...] = jnp.zeros(state.shape, F32)

    x = x_ref[0]
    mod = mod_ref[0]
    shift1, scale1, gate1 = mod[0:1], mod[1:2], mod[2:3]
    h = x * lax.rsqrt(jnp.mean(x * x, axis=-1, keepdims=True) + EPS)
    h = h * npre_ref[...] * (1.0 + scale1) + shift1
    hb = h.astype(BF16)

    qkv_ext[QKV_HALO:QKV_HALO + ts, :] = _dot(hb, wqkv_ref[...])
    conv = _causal_conv(qkv_ext, convw_ref, DN_CONV, QKV_HALO - (DN_CONV - 1), ts, 32, 512)
    for ri, row in enumerate(conv):
        for ci, blk in enumerate(row):
            qkv_act[ri * 32:(ri + 1) * 32, ci * 512:(ci + 1) * 512] = _silu(blk)
    qkv_ext[0:QKV_HALO, :] = qkv_ext[ts:ts + QKV_HALO, :]

    z = _dot(hb, wz_ref[...])
    ba = _dot(hb, wba_ref[...])
    beta_all = jax.nn.sigmoid(ba)
    g_all = -jnp.exp(alog_ref[...]) * _softplus(ba + dtb_ref[...])

    ri = lax.broadcasted_iota(jnp.int32, (ts, ts), 0)
    ci = lax.broadcasted_iota(jnp.int32, (ts, ts), 1)
    tri = jnp.where((ri // DN_CHUNK == ci // DN_CHUNK) & (ci <= ri), 1.0, 0.0).astype(BF16)
    g_hi, g_mid, g_lo = _split3(g_all)
    gcum = _dot(tri, g_hi) + _dot(tri, g_mid) + _dot(tri, g_lo)
    gcum_t = gcum.T
    exp_g = jnp.exp(gcum)

    r64 = lax.broadcasted_iota(jnp.int32, (DN_CHUNK, DN_CHUNK), 0)
    c64 = lax.broadcasted_iota(jnp.int32, (DN_CHUNK, DN_CHUNK), 1)
    causal = c64 <= r64
    strict = c64 < r64
    eye = jnp.where(c64 == r64, 1.0, 0.0).astype(F32)
    dnw = dnw_ref[...]

    heads = []
    for hd in range(DN_HEADS):
        lo = hd * HEAD_DIM
        qh = qkv_act[:, lo:lo + HEAD_DIM]
        kh = qkv_act[:, DN_WIDTH + lo:DN_WIDTH + lo + HEAD_DIM]
        vh = qkv_act[:, 2 * DN_WIDTH + lo:2 * DN_WIDTH + lo + HEAD_DIM]
        qn = qh * lax.rsqrt(jnp.sum(qh * qh, axis=-1, keepdims=True) + EPS) * (HEAD_DIM ** -0.5)
        kn = kh * lax.rsqrt(jnp.sum(kh * kh, axis=-1, keepdims=True) + EPS)
        beta_h = beta_all[:, hd:hd + 1]
        gc_h = gcum[:, DN_HEADS + hd:DN_HEADS + hd + 1]
        eg_h = exp_g[:, DN_HEADS + hd:DN_HEADS + hd + 1]
        k_beta = kn * beta_h
        heads.append(dict(qn=qn, kn=kn, k_beta=k_beta, v_beta=vh * beta_h, kbg=k_beta * eg_h, qg=qn * eg_h,
                          gc=gc_h, zg=_silu(z[:, lo:lo + HEAD_DIM])))

    cells = [(hd, ch) for hd in range(DN_HEADS) for ch in range(n_chunks)]
    rows = lambda ch: slice(ch * DN_CHUNK, (ch + 1) * DN_CHUNK)

    decay, kq = {}, {}
    for hd, ch in cells:
        hv, sl = heads[hd], rows(ch)
        gc_row = gcum_t[DN_HEADS + hd:DN_HEADS + hd + 1, sl]
        decay[hd, ch] = jnp.where(causal, jnp.exp(hv["gc"][sl] - gc_row), 0.0)
        lhs = jnp.concatenate([hv["k_beta"][sl], hv["qn"][sl]], axis=0).astype(BF16)
        kq[hd, ch] = _dot_nt(lhs, hv["kn"][sl].astype(BF16))
    a = {c: jnp.where(strict, kq[c][:DN_CHUNK] * decay[c], 0.0) for c in cells}
    attn = {c: (kq[c][DN_CHUNK:] * decay[c]).astype(BF16) for c in cells}

    t_inv = {c: eye - a[c] for c in cells}
    pw = {c: a[c].astype(BF16) for c in cells}
    for _ in range(5):
        pw = {c: _dot(pw[c], pw[c]).astype(BF16) for c in cells}
        t_inv = {c: t_inv[c] + _dot(t_inv[c].astype(BF16), pw[c]) for c in cells}

    sol, aw, ks, glast = {}, {}, {}, {}
    for hd, ch in cells:
        hv, sl = heads[hd], rows(ch)
        rhs = jnp.concatenate([hv["v_beta"][sl], hv["kbg"][sl]], axis=1).astype(BF16)
        sol[hd, ch] = _dot(t_inv[hd, ch].astype(BF16), rhs).astype(BF16)
    for hd, ch in cells:
        hv, sl = heads[hd], rows(ch)
        gc_col = hv["gc"][sl]
        glast[hd, ch] = gc_col[DN_CHUNK - 1:DN_CHUNK]
        k_dec = (hv["kn"][sl] * jnp.exp(glast[hd, ch] - gc_col)).astype(BF16)
        aw[hd, ch] = _dot(attn[hd, ch], sol[hd, ch])
        ks[hd, ch] = _dot_tn(k_dec, sol[hd, ch])

    s_in = {}
    s_cur = [state[hd] for hd in range(DN_HEADS)]
    for ch in range(n_chunks):
        for hd in range(DN_HEADS):
            s_in[hd, ch] = s_cur[hd].astype(BF16)
            kd_u, kd_w = ks[hd, ch][:, :HEAD_DIM], ks[hd, ch][:, HEAD_DIM:]
            s_cur[hd] = s_cur[hd] * jnp.exp(glast[hd, ch]) + kd_u - _dot(kd_w.astype(BF16), s_in[hd, ch])
    for hd in range(DN_HEADS):
        state[hd] = s_cur[hd]

    for hd, ch in cells:
        hv, sl = heads[hd], rows(ch)
        lo = hd * HEAD_DIM
        q_eff = (hv["qg"][sl] - aw[hd, ch][:, HEAD_DIM:]).astype(BF16)
        o = _dot(q_eff, s_in[hd, ch]) + aw[hd, ch][:, :HEAD_DIM]
        on = o * lax.rsqrt(jnp.mean(o * o, axis=-1, keepdims=True) + EPS) * dnw * hv["zg"][sl]
        mixed[sl, lo:lo + HEAD_DIM] = on.astype(BF16)

    cf_pre = _dot(hb, wcf_ref[...]) + pw1b_ref[...]
    cfw = cf_ext.shape[1]
    cf_ext[CF_HALO:CF_HALO + ts, :] = cf_pre[:, :cfw] * jax.nn.sigmoid(cf_pre[:, cfw:])
    conv = _causal_conv(cf_ext, dww_ref, CF_KERNEL, CF_HALO - (CF_KERNEL - 1), ts, 32, cfw, bias=dwb_ref[...])
    for ri_, row in enumerate(conv):
        cf = row[0]
        mu = jnp.mean(cf, axis=-1, keepdims=True)
        xc = cf - mu
        var = jnp.mean(xc * xc, axis=-1, keepdims=True)
        cfn = xc * lax.rsqrt(var + EPS) * lnw_ref[...] + lnb_ref[...]
        mixed[ri_ * 32:(ri_ + 1) * 32, DN_WIDTH:DN_WIDTH + cfw] = _silu(cfn).astype(BF16)
    cf_ext[0:CF_HALO, :] = cf_ext[ts:ts + CF_HALO, :]

    out = _dot(mixed[...], wout_ref[...])
    y = out * lax.rsqrt(jnp.mean(out * out, axis=-1, keepdims=True) + EPS) * npost_ref[...]
    o_ref[0] = x + gate1 * y


def _mixer_call(x, mod3, npre, npost, w_in, dn_conv_w, dn_a_log, dn_dt_bias, dn_norm_w,
                cf_pw1_b, cf_dw_w, cf_dw_b, cf_ln_w, cf_ln_b, w_out):
    bsz, seq, d = x.shape
    ts = SEQ_TILE
    cfw = cf_dw_w.shape[1]
    n_qkv = 3 * DN_WIDTH
    wqkv = w_in[:, :n_qkv].astype(BF16)
    wz = w_in[:, n_qkv:n_qkv + DN_WIDTH].astype(BF16)
    wba = jnp.pad(w_in[:, n_qkv + DN_WIDTH:n_qkv + DN_WIDTH + 2 * DN_HEADS],
                  ((0, 0), (0, LANES - 2 * DN_HEADS))).astype(BF16)
    wcf = w_in[:, n_qkv + DN_WIDTH + 2 * DN_HEADS:].astype(BF16)
    alog = jnp.pad(dn_a_log, (DN_HEADS, LANES - 2 * DN_HEADS)).reshape(1, LANES)
    dtb = jnp.pad(dn_dt_bias, (DN_HEADS, LANES - 2 * DN_HEADS)).reshape(1, LANES)
    dww = jnp.pad(cf_dw_w, ((0, 32 - CF_KERNEL), (0, 0)))
    convw = jnp.pad(dn_conv_w, ((0, 8 - DN_CONV), (0, 0)))

    def full(a):
        return pl.BlockSpec(a.shape, lambda b, j: (0,) * a.ndim)

    row = lambda a: a.reshape(1, -1)
    operands = [x, mod3, row(npre), row(npost), wqkv, wz, wba, wcf, convw, alog, dtb, row(dn_norm_w),
                row(cf_pw1_b), dww, row(cf_dw_b), row(cf_ln_w), row(cf_ln_b), w_out.astype(BF16)]
    in_specs = [pl.BlockSpec((1, ts, d), lambda b, j: (b, j, 0)),
                pl.BlockSpec((1,) + mod3.shape[1:], lambda b, j: (b, 0, 0))]
    in_specs += [full(a) for a in operands[2:]]
    return pl.pallas_call(
        _mixer_kernel,
        grid=(bsz, seq // ts),
        in_specs=in_specs,
        out_specs=pl.BlockSpec((1, ts, d), lambda b, j: (b, j, 0)),
        out_shape=jax.ShapeDtypeStruct((bsz, seq, d), F32),
        scratch_shapes=[pltpu.VMEM((QKV_HALO + ts, n_qkv), F32),
                        pltpu.VMEM((ts, n_qkv), F32),
                        pltpu.VMEM((CF_HALO + ts, cfw), F32),
                        pltpu.VMEM((DN_HEADS, HEAD_DIM, HEAD_DIM), F32),
                        pltpu.VMEM((ts, DN_WIDTH + cfw), BF16)],
        compiler_params=pltpu.CompilerParams(dimension_semantics=("arbitrary", "arbitrary"),
                                             vmem_limit_bytes=VMEM_LIMIT),
        name="mixer",
    )(*operands)


def _router_kernel(x_ref, mod_ref, nw_ref, wr_ref, br_ref, h_ref, info_ref, cnt_ref, carry):
    tt = x_ref.shape[1]

    @pl.when((pl.program_id(0) == 0) & (pl.program_id(1) == 0))
    def _():
        carry[...] = jnp.zeros(carry.shape, F32)

    x = x_ref[0]
    mod = mod_ref[0]
    shift2, scale2 = mod[3:4], mod[4:5]
    h = x * lax.rsqrt(jnp.mean(x * x, axis=-1, keepdims=True) + EPS)
    h = h * nw_ref[...] * (1.0 + scale2) + shift2
    _store_token_major(h_ref, h)

    h_hi, h_mid, h_lo = _split3(h)
    w_hi, w_mid, w_lo = _split3(wr_ref[...])
    logits = _dot(h_hi, w_hi)
    logits += _dot(h_hi, w_mid) + _dot(h_mid, w_hi)
    logits += _dot(h_hi, w_lo) + _dot(h_mid, w_mid) + _dot(h_lo, w_hi)
    logits = logits + br_ref[...]

    lane = lax.broadcasted_iota(jnp.int32, (tt, LANES), 1)
    neg = -jnp.inf
    is_grp = (lane >= N_EXPERTS) & (lane < N_EXPERTS + N_GROUPS)
    gl = jnp.where(is_grp, logits, neg)
    gmax = jnp.max(gl, axis=-1, keepdims=True)
    gsum = jnp.sum(jnp.where(is_grp, jnp.exp(gl - gmax), 0.0), axis=-1, keepdims=True)
    grp_p = 1.0 / gsum
    grp_lane = jnp.min(jnp.where(is_grp & (gl == gmax), lane, LANES), axis=-1, keepdims=True)
    grp_idx = grp_lane - N_EXPERTS

    in_grp = (lane < N_EXPERTS) & (lane // EXPERTS_PER_GROUP == grp_idx)
    el = jnp.where(in_grp, logits, neg)
    m1 = jnp.max(el, axis=-1, keepdims=True)
    e1 = jnp.min(jnp.where(in_grp & (el == m1), lane, LANES), axis=-1, keepdims=True)
    el2 = jnp.where(lane == e1, neg, el)
    m2 = jnp.max(el2, axis=-1, keepdims=True)
    e2 = jnp.min(jnp.where(in_grp & (lane != e1) & (el2 == m2), lane, LANES), axis=-1, keepdims=True)
    r = jnp.exp(m2 - m1)
    w1 = grp_p / (1.0 + r)
    w2 = grp_p * r / (1.0 + r)

    hit1 = lane == e1
    hit2 = lane == e2
    onehot = jnp.where(hit1 | hit2, 1.0, 0.0)
    rr = lax.broadcasted_iota(jnp.int32, (tt, tt), 0)
    cc = lax.broadcasted_iota(jnp.int32, (tt, tt), 1)
    strict = jnp.where(cc < rr, 1.0, 0.0).astype(BF16)
    prefix = _dot(strict, onehot.astype(BF16)) + carry[0:1, :]
    rank1 = jnp.sum(jnp.where(hit1, prefix, 0.0), axis=-1, keepdims=True)
    rank2 = jnp.sum(jnp.where(hit2, prefix, 0.0), axis=-1, keepdims=True)
    total = carry[0:1, :] + jnp.sum(onehot, axis=0, keepdims=True)
    carry[...] = jnp.broadcast_to(total, carry.shape)
    cnt_ref[...] = jnp.broadcast_to(total, cnt_ref.shape)

    info = jnp.where(lane == 0, e1.astype(F32), 0.0)
    info = jnp.where(lane == 1, e2.astype(F32), info)
    info = jnp.where(lane == 2, w1, info)
    info = jnp.where(lane == 3, w2, info)
    info = jnp.where(lane == 4, rank1, info)
    info = jnp.where(lane == 5, rank2, info)
    info_ref[0] = info


def _router_call(x1, mod3, norm_w, w_router_group, b_router_group, w_router_expert, b_router_expert):
    bsz, seq, d = x1.shape
    tt = ROUTER_TILE
    nj = seq // tt
    pitch = d // LANES
    pad = LANES - N_EXPERTS - N_GROUPS
    wr = jnp.pad(jnp.concatenate([w_router_expert, w_router_group], axis=1), ((0, 0), (0, pad)))
    br = jnp.pad(jnp.concatenate([b_router_expert, b_router_group]), (0, pad)).reshape(1, LANES)
    return pl.pallas_call(
        _router_kernel,
        grid=(bsz, seq // tt),
        in_specs=[pl.BlockSpec((1, tt, d), lambda b, j: (b, j, 0)),
                  pl.BlockSpec((1,) + mod3.shape[1:], lambda b, j: (b, 0, 0)),
                  pl.BlockSpec((1, d), lambda b, j: (0, 0)),
                  pl.BlockSpec((d, LANES), lambda b, j: (0, 0)),
                  pl.BlockSpec((1, LANES), lambda b, j: (0, 0))],
        out_specs=[pl.BlockSpec((tt * pitch, LANES), lambda b, j: (b * nj + j, 0)),
                   pl.BlockSpec((1, tt, LANES), lambda b, j: (b, j, 0)),
                   pl.BlockSpec((8, LANES), lambda b, j: (0, 0))],
        out_shape=[jax.ShapeDtypeStruct((bsz * seq * pitch, LANES), F32),
                   jax.ShapeDtypeStruct((bsz, seq, LANES), F32),
                   jax.ShapeDtypeStruct((8, LANES), F32)],
        scratch_shapes=[pltpu.VMEM((8, LANES), F32)],
        compiler_params=pltpu.CompilerParams(dimension_semantics=("arbitrary", "arbitrary"),
                                             vmem_limit_bytes=VMEM_LIMIT),
        name="router",
    )(x1, mod3, norm_w.reshape(1, d), wr, br)


def _tile_rows(idx, pitch):
    return pl.ds(pl.multiple_of(idx * pitch, pitch), pitch)


def _expert_kernel(nused_ref, be_ref, tok_cur_ref, tok_nxt_ref, h_hbm, wg_ref, wu_ref, wd_ref, y_ref,
                   xbuf, sems):
    i = pl.program_id(0)
    n_used = nused_ref[0]
    d = wg_ref.shape[1]
    pitch = d // LANES
    bm = y_ref.shape[0] // pitch

    def start_gather(tok_ref, slot):
        def body(r, carry):
            tok = tok_ref[0, 0, r]
            pltpu.make_async_copy(h_hbm.at[_tile_rows(tok, pitch), :],
                                  xbuf.at[_tile_rows(slot * bm + r, pitch), :], sems.at[slot]).start()
            return carry
        lax.fori_loop(0, bm, body, 0, unroll=8)

    @pl.when((i == 0) & (n_used > 0))
    def _():
        start_gather(tok_cur_ref, 0)

    @pl.when(i + 1 < n_used)
    def _():
        start_gather(tok_nxt_ref, (i + 1) % 2)

    @pl.when(i < n_used)
    def _():
        slot = i % 2
        base = pl.multiple_of(slot * bm * pitch, bm * pitch)
        pltpu.make_async_copy(h_hbm.at[pl.ds(0, bm * pitch), :], xbuf.at[pl.ds(base, bm * pitch), :],
                              sems.at[slot]).wait()
        xb = _load_token_major(xbuf, bm, d, base).astype(BF16)
        gate = _dot(xb, wg_ref[0].astype(BF16))
        up = _dot(xb, wu_ref[0].astype(BF16))
        hid = (_silu(gate) * up).astype(BF16)
        _store_token_major(y_ref, _dot(hid, wd_ref[0].astype(BF16)))

    @pl.when(i >= n_used)
    def _():
        y_ref[...] = jnp.zeros(y_ref.shape, F32)


def _expert_call(h2, row_tok, block_e, n_used, w_gate, w_up, w_down):
    bm = EXPERT_BLOCK
    n_blocks = row_tok.shape[0] // bm
    d, de = w_gate.shape[1], w_gate.shape[2]
    pitch = d // LANES
    tok3 = row_tok.reshape(n_blocks, 1, bm)
    smem_blk = lambda f: pl.BlockSpec((1, 1, bm), f, memory_space=pltpu.SMEM)
    grid_spec = pltpu.PrefetchScalarGridSpec(
        num_scalar_prefetch=2,
        grid=(n_blocks,),
        in_specs=[smem_blk(lambda i, nu, be: (i, 0, 0)),
                  smem_blk(lambda i, nu, be: (jnp.minimum(i + 1, n_blocks - 1), 0, 0)),
                  pl.BlockSpec(memory_space=pl.ANY),
                  pl.BlockSpec((1, d, de), lambda i, nu, be: (be[i], 0, 0)),
                  pl.BlockSpec((1, d, de), lambda i, nu, be: (be[i], 0, 0)),
                  pl.BlockSpec((1, de, d), lambda i, nu, be: (be[i], 0, 0))],
        out_specs=pl.BlockSpec((bm * pitch, LANES), lambda i, nu, be: (i, 0)),
        scratch_shapes=[pltpu.VMEM((2 * bm * pitch, LANES), F32), pltpu.SemaphoreType.DMA((2,))],
    )
    return pl.pallas_call(
        _expert_kernel,
        grid_spec=grid_spec,
        out_shape=jax.ShapeDtypeStruct((n_blocks * bm * pitch, LANES), F32),
        compiler_params=pltpu.CompilerParams(dimension_semantics=("arbitrary",),
                                             vmem_limit_bytes=VMEM_LIMIT),
        name="experts",
    )(n_used, block_e, tok3, tok3, h2, w_gate, w_up, w_down)


def _combine_kernel(pos_ref, x_ref, info_ref, mod_ref, nw_ref, y_hbm, o_ref, ybuf, sem):
    tt, d = x_ref.shape[1], x_ref.shape[2]
    pitch = d // LANES

    def body(r, carry):
        for k in range(TOP_K):
            p = pos_ref[0, 0, k * tt + r]
            pltpu.make_async_copy(y_hbm.at[_tile_rows(p, pitch), :],
                                  ybuf.at[_tile_rows(k * tt + r, pitch), :], sem.at[0]).start()
        return carry
    lax.fori_loop(0, tt, body, 0, unroll=4)
    info = info_ref[0]
    w1 = info[:, 2:3]
    w2 = info[:, 3:4]
    x = x_ref[0]
    gate2 = mod_ref[0][5:6]
    pltpu.make_async_copy(y_hbm.at[pl.ds(0, TOP_K * tt * pitch), :], ybuf, sem.at[0]).wait()
    moe = _load_token_major(ybuf, tt, d, 0) * w1 + _load_token_major(ybuf, tt, d, tt * pitch) * w2
    y = moe * lax.rsqrt(jnp.mean(moe * moe, axis=-1, keepdims=True) + EPS) * nw_ref[...]
    o_ref[0] = x + gate2 * y


def _combine_call(x1, info, mod3, norm_w, y, pos):
    bsz, seq, d = x1.shape
    tt = COMBINE_TILE
    nj = seq // tt
    pos3 = pos.reshape(bsz * nj, tt, TOP_K).transpose(0, 2, 1).reshape(bsz * nj, 1, TOP_K * tt)
    return pl.pallas_call(
        _combine_kernel,
        grid=(bsz, nj),
        in_specs=[pl.BlockSpec((1, 1, TOP_K * tt), lambda b, j: (b * nj + j, 0, 0), memory_space=pltpu.SMEM),
                  pl.BlockSpec((1, tt, d), lambda b, j: (b, j, 0)),
                  pl.BlockSpec((1, tt, LANES), lambda b, j: (b, j, 0)),
                  pl.BlockSpec((1,) + mod3.shape[1:], lambda b, j: (b, 0, 0)),
                  pl.BlockSpec((1, d), lambda b, j: (0, 0)),
                  pl.BlockSpec(memory_space=pl.ANY)],
        out_specs=pl.BlockSpec((1, tt, d), lambda b, j: (b, j, 0)),
        out_shape=jax.ShapeDtypeStruct((bsz, seq, d), F32),
        scratch_shapes=[pltpu.VMEM((TOP_K * tt * (d // LANES), LANES), F32), pltpu.SemaphoreType.DMA((1,))],
        compiler_params=pltpu.CompilerParams(dimension_semantics=("arbitrary", "arbitrary"),
                                             vmem_limit_bytes=VMEM_LIMIT),
        name="combine",
    )(pos3, x1, info, mod3, norm_w.reshape(1, d), y)


def _layer(x, mod, norm_pre_mix, norm_post_mix, w_in, dn_conv_w, dn_a_log, dn_dt_bias, dn_norm_w,
           cf_pw1_b, cf_dw_w, cf_dw_b, cf_ln_w, cf_ln_b, w_out, norm_pre_ffn, norm_post_ffn,
           w_router_group, b_router_group, w_router_expert, b_router_expert, w_gate, w_up, w_down):
    bsz, seq, d = x.shape
    t = bsz * seq
    mod3 = mod.reshape(bsz, -1, d)
    x1 = _mixer_call(x, mod3, norm_pre_mix, norm_post_mix, w_in, dn_conv_w, dn_a_log, dn_dt_bias, dn_norm_w,
                     cf_pw1_b, cf_dw_w, cf_dw_b, cf_ln_w, cf_ln_b, w_out)
    h2, info, cnt = _router_call(x1, mod3, norm_pre_ffn, w_router_group, b_router_group,
                                 w_router_expert, b_router_expert)

    bm = EXPERT_BLOCK
    info2 = info.reshape(t, LANES)
    expert_id = info2[:, 0:TOP_K].astype(jnp.int32)
    rank = info2[:, 4:4 + TOP_K].astype(jnp.int32)
    counts = cnt[0, :N_EXPERTS].astype(jnp.int32)
    padded = (counts + bm - 1) // bm * bm
    pend = jnp.cumsum(padded)
    pstart = pend - padded
    pos = pstart[expert_id] + rank
    n_blocks = -(-(t * TOP_K) // bm) + N_EXPERTS
    tok = jnp.broadcast_to(jnp.arange(t, dtype=jnp.int32)[:, None], (t, TOP_K))
    row_tok = jnp.zeros((n_blocks * bm,), jnp.int32).at[pos.reshape(-1)].set(tok.reshape(-1))
    block_start = jnp.arange(n_blocks, dtype=jnp.int32) * bm
    block_e = jnp.minimum(jnp.sum(pend[None, :] <= block_start[:, None], axis=1), N_EXPERTS - 1).astype(jnp.int32)
    n_used = (pend[-1] // bm).astype(jnp.int32).reshape(1)

    y = _expert_call(h2, row_tok, block_e, n_used, w_gate, w_up, w_down)
    return _combine_call(x1, info, mod3, norm_post_ffn, y, pos)


def kernel(x, c, w_ada, b_ada, norm_pre_mix, norm_post_mix, w_in, dn_conv_w, dn_a_log, dn_dt_bias, dn_norm_w,
           cf_pw1_b, cf_dw_w, cf_dw_b, cf_ln_w, cf_ln_b, w_out, norm_pre_ffn, norm_post_ffn,
           w_router_group, b_router_group, w_router_expert, b_router_expert, w_gate, w_up, w_down):
    depth = w_ada.shape[0]
    for l in range(depth):
        mod = _ada_call(c, w_ada[l], b_ada[l])
        x = _layer(x, mod, norm_pre_mix[l], norm_post_mix[l], w_in[l], dn_conv_w[l], dn_a_log[l],
                   dn_dt_bias[l], dn_norm_w[l], cf_pw1_b[l], cf_dw_w[l], cf_dw_b[l], cf_ln_w[l], cf_ln_b[l],
                   w_out[l], norm_pre_ffn[l], norm_post_ffn[l], w_router_group[l], b_router_group[l],
                   w_router_expert[l], b_router_expert[l], w_gate[l], w_up[l], w_down[l])
    return x
```

```python
import functools

import jax
import jax.numpy as jnp
from jax import lax
from jax.experimental import pallas as pl
from jax.experimental.pallas import tpu as pltpu

F32 = jnp.float32
BF16 = jnp.bfloat16
EPS = 1e-6

DN_HEADS = 4
HEAD_DIM = 128
DN_WIDTH = DN_HEADS * HEAD_DIM
DN_CONV = 4
DN_CHUNK = 64
CF_KERNEL = 31
N_GROUPS = 8
EXPERTS_PER_GROUP = 8
N_EXPERTS = N_GROUPS * EXPERTS_PER_GROUP
TOP_K = 2

LANES = 128
QKV_HALO = 8
CF_HALO = 32
SEQ_TILE = 256
ROUTER_TILE = 512
EXPERT_BLOCK = 256
COMBINE_TILE = 256
VMEM_LIMIT = 56 * 1024 * 1024


def _dot(a, b):
    return jnp.dot(a, b, preferred_element_type=F32)


def _dot_nt(a, b):
    return lax.dot_general(a, b, (((1,), (1,)), ((), ())), preferred_element_type=F32)


def _dot_tn(a, b):
    return lax.dot_general(a, b, (((0,), (0,)), ((), ())), preferred_element_type=F32)


def _split3(x):
    hi = x.astype(BF16)
    r1 = x - hi.astype(F32)
    mid = r1.astype(BF16)
    lo = (r1 - mid.astype(F32)).astype(BF16)
    return hi, mid, lo


def _silu(x):
    return x * jax.nn.sigmoid(x)


def _softplus(x):
    return jnp.maximum(x, 0.0) + jnp.log1p(jnp.exp(-jnp.abs(x)))


def _store_token_major(ref, val, base=0):
    n, d = val.shape
    pitch = d // LANES
    for j in range(pitch):
        ref[pl.ds(base + j, n, stride=pitch), :] = val[:, j * LANES:(j + 1) * LANES]


def _load_token_major(ref, n, d, base=0):
    pitch = d // LANES
    return jnp.concatenate([ref[pl.ds(base + j, n, stride=pitch), :] for j in range(pitch)], axis=1)


def _ada_kernel(c_ref, w_ref, b_ref, o_ref):
    c = c_ref[...]
    ca = _silu(c)
    c_hi, c_mid, c_lo = _split3(ca)
    w_hi, w_mid, w_lo = _split3(w_ref[...])
    acc = _dot(c_hi, w_hi)
    acc += _dot(c_hi, w_mid) + _dot(c_mid, w_hi)
    acc += _dot(c_hi, w_lo) + _dot(c_mid, w_mid) + _dot(c_lo, w_hi)
    o_ref[...] = acc + b_ref[...]


def _ada_call(c, w, b):
    bsz, d = c.shape
    n = w.shape[1]
    tn = 512
    return pl.pallas_call(
        _ada_kernel,
        grid=(n // tn,),
        in_specs=[pl.BlockSpec((bsz, d), lambda i: (0, 0)),
                  pl.BlockSpec((d, tn), lambda i: (0, i)),
                  pl.BlockSpec((1, tn), lambda i: (0, i))],
        out_specs=pl.BlockSpec((bsz, tn), lambda i: (0, i)),
        out_shape=jax.ShapeDtypeStruct((bsz, n), F32),
        compiler_params=pltpu.CompilerParams(dimension_semantics=("arbitrary",),
                                             vmem_limit_bytes=VMEM_LIMIT),
        name="ada",
    )(c, w, b.reshape(1, n))


def _causal_conv(ext_ref, w_ref, n_taps, first, rows, row_blk, col_blk, bias=None):
    cols = ext_ref.shape[1]
    out_rows = []
    for r0 in range(0, rows, row_blk):
        out_cols = []
        for c0 in range(0, cols, col_blk):
            acc = jnp.zeros((row_blk, col_blk), F32)
            for k in range(n_taps):
                acc = acc + w_ref[k:k + 1, c0:c0 + col_blk] * ext_ref[first + k + r0:first + k + r0 + row_blk,
                                                                      c0:c0 + col_blk]
            if bias is not None:
                acc = acc + bias[:, c0:c0 + col_blk]
            out_cols.append(acc)
        out_rows.append(out_cols)
    return out_rows


def _mixer_kernel(x_ref, mod_ref, npre_ref, npost_ref, wqkv_ref, wz_ref, wba_ref, wcf_ref,
                  convw_ref, alog_ref, dtb_ref, dnw_ref, pw1b_ref, dww_ref, dwb_ref, lnw_ref, lnb_ref,
                  wout_ref, o_ref, qkv_ext, qkv_act, cf_ext, state, mixed):
    ts = x_ref.shape[1]
    n_chunks = ts // DN_CHUNK

    @pl.when(pl.program_id(1) == 0)
    def _():
        qkv_ext[0:QKV_HALO, :] = jnp.zeros((QKV_HALO, qkv_ext.shape[1]), F32)
        cf_ext[0:CF_HALO, :] = jnp.zeros((CF_HALO, cf_ext.shape[1]), F32)
        state[...] = jnp.zeros(state.shape, F32)

    x = x_ref[0]
    mod = mod_ref[0]
    shift1, scale1, gate1 = mod[0:1], mod[1:2], mod[2:3]
    h = x * lax.rsqrt(jnp.mean(x * x, axis=-1, keepdims=True) + EPS)
    h = h * npre_ref[...] * (1.0 + scale1) + shift1
    hb = h.astype(BF16)

    qkv_ext[QKV_HALO:QKV_HALO + ts, :] = _dot(hb, wqkv_ref[...])
    conv = _causal_conv(qkv_ext, convw_ref, DN_CONV, QKV_HALO - (DN_CONV - 1), ts, 32, 512)
    for ri, row in enumerate(conv):
        for ci, blk in enumerate(row):
            qkv_act[ri * 32:(ri + 1) * 32, ci * 512:(ci + 1) * 512] = _silu(blk)
    qkv_ext[0:QKV_HALO, :] = qkv_ext[ts:ts + QKV_HALO, :]

    z = _dot(hb, wz_ref[...])
    ba = _dot(hb, wba_ref[...])
    beta_all = jax.nn.sigmoid(ba)
    g_all = -jnp.exp(alog_ref[...]) * _softplus(ba + dtb_ref[...])

    ri = lax.broadcasted_iota(jnp.int32, (ts, ts), 0)
    ci = lax.broadcasted_iota(jnp.int32, (ts, ts), 1)
    tri = jnp.where((ri // DN_CHUNK == ci // DN_CHUNK) & (ci <= ri), 1.0, 0.0).astype(BF16)
    g_hi, g_mid, g_lo = _split3(g_all)
    gcum = _dot(tri, g_hi) + _dot(tri, g_mid) + _dot(tri, g_lo)
    gcum_t = gcum.T
    exp_g = jnp.exp(gcum)

    r64 = lax.broadcasted_iota(jnp.int32, (DN_CHUNK, DN_CHUNK), 0)
    c64 = lax.broadcasted_iota(jnp.int32, (DN_CHUNK, DN_CHUNK), 1)
    causal = c64 <= r64
    strict = c64 < r64
    eye = jnp.where(c64 == r64, 1.0, 0.0).astype(F32)
    dnw = dnw_ref[...]

    heads = []
    for hd in range(DN_HEADS):
        lo = hd * HEAD_DIM
        qh = qkv_act[:, lo:lo + HEAD_DIM]
        kh = qkv_act[:, DN_WIDTH + lo:DN_WIDTH + lo + HEAD_DIM]
        vh = qkv_act[:, 2 * DN_WIDTH + lo:2 * DN_WIDTH + lo + HEAD_DIM]
        qn = qh * lax.rsqrt(jnp.sum(qh * qh, axis=-1, keepdims=True) + EPS) * (HEAD_DIM ** -0.5)
        kn = kh * lax.rsqrt(jnp.sum(kh * kh, axis=-1, keepdims=True) + EPS)
        beta_h = beta_all[:, hd:hd + 1]
        gc_h = gcum[:, DN_HEADS + hd:DN_HEADS + hd + 1]
        eg_h = exp_g[:, DN_HEADS + hd:DN_HEADS + hd + 1]
        k_beta = kn * beta_h
        heads.append(dict(qn=qn, kn=kn, k_beta=k_beta, v_beta=vh * beta_h, kbg=k_beta * eg_h, qg=qn * eg_h,
                          gc=gc_h, zg=_silu(z[:, lo:lo + HEAD_DIM])))

    cells = [(hd, ch) for hd in range(DN_HEADS) for ch in range(n_chunks)]
    rows = lambda ch: slice(ch * DN_CHUNK, (ch + 1) * DN_CHUNK)

    decay, kq = {}, {}
    for hd, ch in cells:
        hv, sl = heads[hd], rows(ch)
        gc_row = gcum_t[DN_HEADS + hd:DN_HEADS + hd + 1, sl]
        decay[hd, ch] = jnp.where(causal, jnp.exp(hv["gc"][sl] - gc_row), 0.0)
        lhs = jnp.concatenate([hv["k_beta"][sl], hv["qn"][sl]], axis=0).astype(BF16)
        kq[hd, ch] = _dot_nt(lhs, hv["kn"][sl].astype(BF16))
    a = {c: jnp.where(strict, kq[c][:DN_CHUNK] * decay[c], 0.0) for c in cells}
    attn = {c: (kq[c][DN_CHUNK:] * decay[c]).astype(BF16) for c in cells}

    t_inv = {c: eye - a[c] for c in cells}
    pw = {c: a[c].astype(BF16) for c in cells}
    for _ in range(5):
        pw = {c: _dot(pw[c], pw[c]).astype(BF16) for c in cells}
        t_inv = {c: t_inv[c] + _dot(t_inv[c].astype(BF16), pw[c]) for c in cells}

    sol, aw, ks, glast = {}, {}, {}, {}
    for hd, ch in cells:
        hv, sl = heads[hd], rows(ch)
        rhs = jnp.concatenate([hv["v_beta"][sl], hv["kbg"][sl]], axis=1).astype(BF16)
        sol[hd, ch] = _dot(t_inv[hd, ch].astype(BF16), rhs).astype(BF16)
    for hd, ch in cells:
        hv, sl = heads[hd], rows(ch)
        gc_col = hv["gc"][sl]
        glast[hd, ch] = gc_col[DN_CHUNK - 1:DN_CHUNK]
        k_dec = (hv["kn"][sl] * jnp.exp(glast[hd, ch] - gc_col)).astype(BF16)
        aw[hd, ch] = _dot(attn[hd, ch], sol[hd, ch])
        ks[hd, ch] = _dot_tn(k_dec, sol[hd, ch])

    s_in = {}
    s_cur = [state[hd] for hd in range(DN_HEADS)]
    for ch in range(n_chunks):
        for hd in range(DN_HEADS):
            s_in[hd, ch] = s_cur[hd].astype(BF16)
            kd_u, kd_w = ks[hd, ch][:, :HEAD_DIM], ks[hd, ch][:, HEAD_DIM:]
            s_cur[hd] = s_cur[hd] * jnp.exp(glast[hd, ch]) + kd_u - _dot(kd_w.astype(BF16), s_in[hd, ch])
    for hd in range(DN_HEADS):
        state[hd] = s_cur[hd]

    for hd, ch in cells:
        hv, sl = heads[hd], rows(ch)
        lo = hd * HEAD_DIM
        q_eff = (hv["qg"][sl] - aw[hd, ch][:, HEAD_DIM:]).astype(BF16)
        o = _dot(q_eff, s_in[hd, ch]) + aw[hd, ch][:, :HEAD_DIM]
        on = o * lax.rsqrt(jnp.mean(o * o, axis=-1, keepdims=True) + EPS) * dnw * hv["zg"][sl]
        mixed[sl, lo:lo + HEAD_DIM] = on.astype(BF16)

    cf_pre = _dot(hb, wcf_ref[...]) + pw1b_ref[...]
    cfw = cf_ext.shape[1]
    cf_ext[CF_HALO:CF_HALO + ts, :] = cf_pre[:, :cfw] * jax.nn.sigmoid(cf_pre[:, cfw:])
    conv = _causal_conv(cf_ext, dww_ref, CF_KERNEL, CF_HALO - (CF_KERNEL - 1), ts, 32, cfw, bias=dwb_ref[...])
    for ri_, row in enumerate(conv):
        cf = row[0]
        mu = jnp.mean(cf, axis=-1, keepdims=True)
        xc = cf - mu
        var = jnp.mean(xc * xc, axis=-1, keepdims=True)
        cfn = xc * lax.rsqrt(var + EPS) * lnw_ref[...] + lnb_ref[...]
        mixed[ri_ * 32:(ri_ + 1) * 32, DN_WIDTH:DN_WIDTH + cfw] = _silu(cfn).astype(BF16)
    cf_ext[0:CF_HALO, :] = cf_ext[ts:ts + CF_HALO, :]

    out = _dot(mixed[...], wout_ref[...])
    y = out * lax.rsqrt(jnp.mean(out * out, axis=-1, keepdims=True) + EPS) * npost_ref[...]
    o_ref[0] = x + gate1 * y


def _mixer_call(x, mod3, npre, npost, w_in, dn_conv_w, dn_a_log, dn_dt_bias, dn_norm_w,
                cf_pw1_b, cf_dw_w, cf_dw_b, cf_ln_w, cf_ln_b, w_out):
    bsz, seq, d = x.shape
    ts = SEQ_TILE
    cfw = cf_dw_w.shape[1]
    n_qkv = 3 * DN_WIDTH
    wqkv = w_in[:, :n_qkv].astype(BF16)
    wz = w_in[:, n_qkv:n_qkv + DN_WIDTH].astype(BF16)
    wba = jnp.pad(w_in[:, n_qkv + DN_WIDTH:n_qkv + DN_WIDTH + 2 * DN_HEADS],
                  ((0, 0), (0, LANES - 2 * DN_HEADS))).astype(BF16)
    wcf = w_in[:, n_qkv + DN_WIDTH + 2 * DN_HEADS:].astype(BF16)
    alog = jnp.pad(dn_a_log, (DN_HEADS, LANES - 2 * DN_HEADS)).reshape(1, LANES)
    dtb = jnp.pad(dn_dt_bias, (DN_HEADS, LANES - 2 * DN_HEADS)).reshape(1, LANES)
    dww = jnp.pad(cf_dw_w, ((0, 32 - CF_KERNEL), (0, 0)))
    convw = jnp.pad(dn_conv_w, ((0, 8 - DN_CONV), (0, 0)))

    def full(a):
        return pl.BlockSpec(a.shape, lambda b, j: (0,) * a.ndim)

    row = lambda a: a.reshape(1, -1)
    operands = [x, mod3, row(npre), row(npost), wqkv, wz, wba, wcf, convw, alog, dtb, row(dn_norm_w),
                row(cf_pw1_b), dww, row(cf_dw_b), row(cf_ln_w), row(cf_ln_b), w_out.astype(BF16)]
    in_specs = [pl.BlockSpec((1, ts, d), lambda b, j: (b, j, 0)),
                pl.BlockSpec((1,) + mod3.shape[1:], lambda b, j: (b, 0, 0))]
    in_specs += [full(a) for a in operands[2:]]
    return pl.pallas_call(
        _mixer_kernel,
        grid=(bsz, seq // ts),
        in_specs=in_specs,
        out_specs=pl.BlockSpec((1, ts, d), lambda b, j: (b, j, 0)),
        out_shape=jax.ShapeDtypeStruct((bsz, seq, d), F32),
        scratch_shapes=[pltpu.VMEM((QKV_HALO + ts, n_qkv), F32),
                        pltpu.VMEM((ts, n_qkv), F32),
                        pltpu.VMEM((CF_HALO + ts, cfw), F32),
                        pltpu.VMEM((DN_HEADS, HEAD_DIM, HEAD_DIM), F32),
                        pltpu.VMEM((ts, DN_WIDTH + cfw), BF16)],
        compiler_params=pltpu.CompilerParams(dimension_semantics=("arbitrary", "arbitrary"),
                                             vmem_limit_bytes=VMEM_LIMIT),
        name="mixer",
    )(*operands)


def _router_kernel(x_ref, mod_ref, nw_ref, wr_ref, br_ref, h_ref, info_ref, cnt_ref, carry):
    tt = x_ref.shape[1]

    @pl.when((pl.program_id(0) == 0) & (pl.program_id(1) == 0))
    def _():
        carry[...] = jnp.zeros(carry.shape, F32)

    x = x_ref[0]
    mod = mod_ref[0]
    shift2, scale2 = mod[3:4], mod[4:5]
    h = x * lax.rsqrt(jnp.mean(x * x, axis=-1, keepdims=True) + EPS)
    h = h * nw_ref[...] * (1.0 + scale2) + shift2
    _store_token_major(h_ref, h)

    h_hi, h_mid, h_lo = _split3(h)
    w_hi, w_mid, w_lo = _split3(wr_ref[...])
    logits = _dot(h_hi, w_hi)
    logits += _dot(h_hi, w_mid) + _dot(h_mid, w_hi)
    logits += _dot(h_hi, w_lo) + _dot(h_mid, w_mid) + _dot(h_lo, w_hi)
    logits = logits + br_ref[...]

    lane = lax.broadcasted_iota(jnp.int32, (tt, LANES), 1)
    neg = -jnp.inf
    is_grp = (lane >= N_EXPERTS) & (lane < N_EXPERTS + N_GROUPS)
    gl = jnp.where(is_grp, logits, neg)
    gmax = jnp.max(gl, axis=-1, keepdims=True)
    gsum = jnp.sum(jnp.where(is_grp, jnp.exp(gl - gmax), 0.0), axis=-1, keepdims=True)
    grp_p = 1.0 / gsum
    grp_lane = jnp.min(jnp.where(is_grp & (gl == gmax), lane, LANES), axis=-1, keepdims=True)
    grp_idx = grp_lane - N_EXPERTS

    in_grp = (lane < N_EXPERTS) & (lane // EXPERTS_PER_GROUP == grp_idx)
    el = jnp.where(in_grp, logits, neg)
    m1 = jnp.max(el, axis=-1, keepdims=True)
    e1 = jnp.min(jnp.where(in_grp & (el == m1), lane, LANES), axis=-1, keepdims=True)
    el2 = jnp.where(lane == e1, neg, el)
    m2 = jnp.max(el2, axis=-1, keepdims=True)
    e2 = jnp.min(jnp.where(in_grp & (lane != e1) & (el2 == m2), lane, LANES), axis=-1, keepdims=True)
    r = jnp.exp(m2 - m1)
    w1 = grp_p / (1.0 + r)
    w2 = grp_p * r / (1.0 + r)

    hit1 = lane == e1
    hit2 = lane == e2
    onehot = jnp.where(hit1 | hit2, 1.0, 0.0)
    rr = lax.broadcasted_iota(jnp.int32, (tt, tt), 0)
    cc = lax.broadcasted_iota(jnp.int32, (tt, tt), 1)
    strict = jnp.where(cc < rr, 1.0, 0.0).astype(BF16)
    prefix = _dot(strict, onehot.astype(BF16)) + carry[0:1, :]
    rank1 = jnp.sum(jnp.where(hit1, prefix, 0.0), axis=-1, keepdims=True)
    rank2 = jnp.sum(jnp.where(hit2, prefix, 0.0), axis=-1, keepdims=True)
    total = carry[0:1, :] + jnp.sum(onehot, axis=0, keepdims=True)
    carry[...] = jnp.broadcast_to(total, carry.shape)
    cnt_ref[...] = jnp.broadcast_to(total, cnt_ref.shape)

    info = jnp.where(lane == 0, e1.astype(F32), 0.0)
    info = jnp.where(lane == 1, e2.astype(F32), info)
    info = jnp.where(lane == 2, w1, info)
    info = jnp.where(lane == 3, w2, info)
    info = jnp.where(lane == 4, rank1, info)
    info = jnp.where(lane == 5, rank2, info)
    info_ref[0] = info


def _router_call(x1, mod3, norm_w, w_router_group, b_router_group, w_router_expert, b_router_expert):
    bsz, seq, d = x1.shape
    tt = ROUTER_TILE
    nj = seq // tt
    pitch = d // LANES
    pad = LANES - N_EXPERTS - N_GROUPS
    wr = jnp.pad(jnp.concatenate([w_router_expert, w_router_group], axis=1), ((0, 0), (0, pad)))
    br = jnp.pad(jnp.concatenate([b_router_expert, b_router_group]), (0, pad)).reshape(1, LANES)
    return pl.pallas_call(
        _router_kernel,
        grid=(bsz, seq // tt),
        in_specs=[pl.BlockSpec((1, tt, d), lambda b, j: (b, j, 0)),
                  pl.BlockSpec((1,) + mod3.shape[1:], lambda b, j: (b, 0, 0)),
                  pl.BlockSpec((1, d), lambda b, j: (0, 0)),
                  pl.BlockSpec((d, LANES), lambda b, j: (0, 0)),
                  pl.BlockSpec((1, LANES), lambda b, j: (0, 0))],
        out_specs=[pl.BlockSpec((tt * pitch, LANES), lambda b, j: (b * nj + j, 0)),
                   pl.BlockSpec((1, tt, LANES), lambda b, j: (b, j, 0)),
                   pl.BlockSpec((8, LANES), lambda b, j: (0, 0))],
        out_shape=[jax.ShapeDtypeStruct((bsz * seq * pitch, LANES), F32),
                   jax.ShapeDtypeStruct((bsz, seq, LANES), F32),
                   jax.ShapeDtypeStruct((8, LANES), F32)],
        scratch_shapes=[pltpu.VMEM((8, LANES), F32)],
        compiler_params=pltpu.CompilerParams(dimension_semantics=("arbitrary", "arbitrary"),
                                             vmem_limit_bytes=VMEM_LIMIT),
        name="router",
    )(x1, mod3, norm_w.reshape(1, d), wr, br)


def _tile_rows(idx, pitch):
    return pl.ds(pl.multiple_of(idx * pitch, pitch), pitch)


def _expert_kernel(nused_ref, be_ref, first_ref, wslot_ref, nexte_ref,
                   tok_cur_ref, tok_nxt_ref, h_hbm, wg_hbm, wu_hbm, wd_hbm, y_ref,
                   xbuf, wg_f32, wu_f32, wd_f32, wg_b, wu_b, wd_b, gsems, wsems):
    i = pl.program_id(0)
    n_used = nused_ref[0]
    d = wg_b.shape[0]
    pitch = d // LANES
    bm = y_ref.shape[0] // pitch

    def start_gather(tok_ref, slot):
        def body(r, carry):
            tok = tok_ref[0, 0, r]
            pltpu.make_async_copy(h_hbm.at[_tile_rows(tok, pitch), :],
                                  xbuf.at[_tile_rows(slot * bm + r, pitch), :], gsems.at[slot]).start()
            return carry
        lax.fori_loop(0, bm, body, 0, unroll=8)

    def weight_copies(e, slot):
        return [pltpu.make_async_copy(src.at[e], dst.at[slot], wsems.at[slot])
                for src, dst in ((wg_hbm, wg_f32), (wu_hbm, wu_f32), (wd_hbm, wd_f32))]

    def start_weights(e, slot):
        for cp in weight_copies(e, slot):
            cp.start(priority=1)

    @pl.when((i == 0) & (n_used > 0))
    def _():
        start_weights(be_ref[0], 0)
        start_gather(tok_cur_ref, 0)

    @pl.when((i < n_used) & (first_ref[i] == 1))
    def _():
        slot = wslot_ref[i]
        for cp in weight_copies(0, slot):
            cp.wait()

        @pl.when(nexte_ref[i] >= 0)
        def _():
            start_weights(nexte_ref[i], 1 - slot)

        wg_b[...] = wg_f32[slot].astype(BF16)
        wu_b[...] = wu_f32[slot].astype(BF16)
        wd_b[...] = wd_f32[slot].astype(BF16)

    @pl.when(i + 1 < n_used)
    def _():
        start_gather(tok_nxt_ref, (i + 1) % 2)

    @pl.when(i < n_used)
    def _():
        slot = i % 2
        base = pl.multiple_of(slot * bm * pitch, bm * pitch)
        pltpu.make_async_copy(h_hbm.at[pl.ds(0, bm * pitch), :], xbuf.at[pl.ds(base, bm * pitch), :],
                              gsems.at[slot]).wait()
        xb = _load_token_major(xbuf, bm, d, base).astype(BF16)
        gate = _dot(xb, wg_b[...])
        up = _dot(xb, wu_b[...])
        hid = (_silu(gate) * up).astype(BF16)
        _store_token_major(y_ref, _dot(hid, wd_b[...]))

    @pl.when(i >= n_used)
    def _():
        y_ref[...] = jnp.zeros(y_ref.shape, F32)


def _expert_call(h2, row_tok, block_e, n_used, w_gate, w_up, w_down):
    bm = EXPERT_BLOCK
    n_blocks = row_tok.shape[0] // bm
    d, de = w_gate.shape[1], w_gate.shape[2]
    pitch = d // LANES
    tok3 = row_tok.reshape(n_blocks, 1, bm)
    blk = jnp.arange(n_blocks, dtype=jnp.int32)
    used = blk < n_used[0]
    first = jnp.concatenate([jnp.ones((1,), jnp.int32), (block_e[1:] != block_e[:-1]).astype(jnp.int32)])
    first = jnp.where(used, first, 0)
    wslot = (jnp.cumsum(first) - 1) % 2
    nxt_first = jnp.where((first == 1) & (blk > 0), blk, n_blocks)
    nxt_idx = lax.cummin(jnp.concatenate([nxt_first[1:], jnp.full((1,), n_blocks, jnp.int32)]), reverse=True)
    next_e = jnp.where(nxt_idx < n_blocks, block_e[jnp.minimum(nxt_idx, n_blocks - 1)], -1).astype(jnp.int32)
    smem_blk = lambda f: pl.BlockSpec((1, 1, bm), f, memory_space=pltpu.SMEM)
    grid_spec = pltpu.PrefetchScalarGridSpec(
        num_scalar_prefetch=5,
        grid=(n_blocks,),
        in_specs=[smem_blk(lambda i, *_: (i, 0, 0)),
                  smem_blk(lambda i, *_: (jnp.minimum(i + 1, n_blocks - 1), 0, 0)),
                  pl.BlockSpec(memory_space=pl.ANY),
                  pl.BlockSpec(memory_space=pl.ANY),
                  pl.BlockSpec(memory_space=pl.ANY),
                  pl.BlockSpec(memory_space=pl.ANY)],
        out_specs=pl.BlockSpec((bm * pitch, LANES), lambda i, *_: (i, 0)),
        scratch_shapes=[pltpu.VMEM((2 * bm * pitch, LANES), F32),
                        pltpu.VMEM((2, d, de), F32), pltpu.VMEM((2, d, de), F32), pltpu.VMEM((2, de, d), F32),
                        pltpu.VMEM((d, de), BF16), pltpu.VMEM((d, de), BF16), pltpu.VMEM((de, d), BF16),
                        pltpu.SemaphoreType.DMA((2,)), pltpu.SemaphoreType.DMA((2,))],
    )
    return pl.pallas_call(
        _expert_kernel,
        grid_spec=grid_spec,
        out_shape=jax.ShapeDtypeStruct((n_blocks * bm * pitch, LANES), F32),
        compiler_params=pltpu.CompilerParams(dimension_semantics=("arbitrary",),
                                             vmem_limit_bytes=VMEM_LIMIT),
        name="experts",
    )(n_used, block_e, first, wslot.astype(jnp.int32), next_e, tok3, tok3, h2, w_gate, w_up, w_down)


def _combine_kernel(pos_cur_ref, pos_nxt_ref, x_ref, info_ref, mod_ref, nw_ref, y_hbm, o_ref, ybuf, sems):
    tt, d = x_ref.shape[1], x_ref.shape[2]
    pitch = d // LANES
    step = pl.program_id(0) * pl.num_programs(1) + pl.program_id(1)
    n_steps = pl.num_programs(0) * pl.num_programs(1)
    slot_rows = TOP_K * tt * pitch

    def start_gather(pos_ref, slot):
        def body(r, carry):
            for k in range(TOP_K):
                p = pos_ref[0, 0, k * tt + r]
                pltpu.make_async_copy(y_hbm.at[_tile_rows(p, pitch), :],
                                      ybuf.at[_tile_rows((slot * TOP_K + k) * tt + r, pitch), :],
                                      sems.at[slot]).start(priority=k)
            return carry
        lax.fori_loop(0, tt, body, 0, unroll=4)

    @pl.when(step == 0)
    def _():
        start_gather(pos_cur_ref, 0)

    @pl.when(step + 1 < n_steps)
    def _():
        start_gather(pos_nxt_ref, (step + 1) % 2)

    info = info_ref[0]
    w1 = info[:, 2:3]
    w2 = info[:, 3:4]
    x = x_ref[0]
    gate2 = mod_ref[0][5:6]
    base = pl.multiple_of((step % 2) * slot_rows, slot_rows)
    pltpu.make_async_copy(y_hbm.at[pl.ds(0, slot_rows), :], ybuf.at[pl.ds(base, slot_rows), :],
                          sems.at[step % 2]).wait()
    moe = (_load_token_major(ybuf, tt, d, base) * w1
           + _load_token_major(ybuf, tt, d, base + tt * pitch) * w2)
    y = moe * lax.rsqrt(jnp.mean(moe * moe, axis=-1, keepdims=True) + EPS) * nw_ref[...]
    o_ref[0] = x + gate2 * y


def _combine_call(x1, info, mod3, norm_w, y, pos):
    bsz, seq, d = x1.shape
    tt = COMBINE_TILE
    nj = seq // tt
    pos3 = pos.reshape(bsz * nj, tt, TOP_K).transpose(0, 2, 1).reshape(bsz * nj, 1, TOP_K * tt)
    n_tiles = bsz * nj
    pos_blk = lambda f: pl.BlockSpec((1, 1, TOP_K * tt), f, memory_space=pltpu.SMEM)
    return pl.pallas_call(
        _combine_kernel,
        grid=(bsz, nj),
        in_specs=[pos_blk(lambda b, j: (b * nj + j, 0, 0)),
                  pos_blk(lambda b, j: (jnp.minimum(b * nj + j + 1, n_tiles - 1), 0, 0)),
                  pl.BlockSpec((1, tt, d), lambda b, j: (b, j, 0)),
                  pl.BlockSpec((1, tt, LANES), lambda b, j: (b, j, 0)),
                  pl.BlockSpec((1,) + mod3.shape[1:], lambda b, j: (b, 0, 0)),
                  pl.BlockSpec((1, d), lambda b, j: (0, 0)),
                  pl.BlockSpec(memory_space=pl.ANY)],
        out_specs=pl.BlockSpec((1, tt, d), lambda b, j: (b, j, 0)),
        out_shape=jax.ShapeDtypeStruct((bsz, seq, d), F32),
        scratch_shapes=[pltpu.VMEM((2 * TOP_K * tt * (d // LANES), LANES), F32), pltpu.SemaphoreType.DMA((2,))],
        compiler_params=pltpu.CompilerParams(dimension_semantics=("arbitrary", "arbitrary"),
                                             vmem_limit_bytes=VMEM_LIMIT),
        name="combine",
    )(pos3, pos3, x1, info, mod3, norm_w.reshape(1, d), y)


def _layer(x, mod, norm_pre_mix, norm_post_mix, w_in, dn_conv_w, dn_a_log, dn_dt_bias, dn_norm_w,
           cf_pw1_b, cf_dw_w, cf_dw_b, cf_ln_w, cf_ln_b, w_out, norm_pre_ffn, norm_post_ffn,
           w_router_group, b_router_group, w_router_expert, b_router_expert, w_gate, w_up, w_down):
    bsz, seq, d = x.shape
    t = bsz * seq
    mod3 = mod.reshape(bsz, -1, d)
    x1 = _mixer_call(x, mod3, norm_pre_mix, norm_post_mix, w_in, dn_conv_w, dn_a_log, dn_dt_bias, dn_norm_w,
                     cf_pw1_b, cf_dw_w, cf_dw_b, cf_ln_w, cf_ln_b, w_out)
    h2, info, cnt = _router_call(x1, mod3, norm_pre_ffn, w_router_group, b_router_group,
                                 w_router_expert, b_router_expert)

    bm = EXPERT_BLOCK
    info2 = info.reshape(t, LANES)
    expert_id = info2[:, 0:TOP_K].astype(jnp.int32)
    rank = info2[:, 4:4 + TOP_K].astype(jnp.int32)
    counts = cnt[0, :N_EXPERTS].astype(jnp.int32)
    padded = (counts + bm - 1) // bm * bm
    pend = jnp.cumsum(padded)
    pstart = pend - padded
    pos = pstart[expert_id] + rank
    n_blocks = -(-(t * TOP_K) // bm) + N_EXPERTS
    tok = jnp.broadcast_to(jnp.arange(t, dtype=jnp.int32)[:, None], (t, TOP_K))
    row_tok = jnp.zeros((n_blocks * bm,), jnp.int32).at[pos.reshape(-1)].set(tok.reshape(-1))
    block_start = jnp.arange(n_blocks, dtype=jnp.int32) * bm
    block_e = jnp.minimum(jnp.sum(pend[None, :] <= block_start[:, None], axis=1), N_EXPERTS - 1).astype(jnp.int32)
    n_used = (pend[-1] // bm).astype(jnp.int32).reshape(1)

    y = _expert_call(h2, row_tok, block_e, n_used, w_gate, w_up, w_down)
    return _combine_call(x1, info, mod3, norm_post_ffn, y, pos)


def kernel(x, c, w_ada, b_ada, norm_pre_mix, norm_post_mix, w_in, dn_conv_w, dn_a_log, dn_dt_bias, dn_norm_w,
           cf_pw1_b, cf_dw_w, cf_dw_b, cf_ln_w, cf_ln_b, w_out, norm_pre_ffn, norm_post_ffn,
           w_router_group, b_router_group, w_router_expert, b_router_expert, w_gate, w_up, w_down):
    depth = w_ada.shape[0]
    for l in range(depth):
        mod = _ada_call(c, w_ada[l], b_ada[l])
        x = _layer(x, mod, norm_pre_mix[l], norm_post_mix[l], w_in[l], dn_conv_w[l], dn_a_log[l],
                   dn_dt_bias[l], dn_norm_w[l], cf_pw1_b[l], cf_dw_w[l], cf_dw_b[l], cf_ln_w[l], cf_ln_b[l],
                   w_out[l], norm_pre_ffn[l], norm_post_ffn[l], w_router_group[l], b_router_group[l],
                   w_router_expert[l], b_router_expert[l], w_gate[l], w_up[l], w_down[l])
    return x
```

```python
import functools

import jax
import jax.numpy as jnp
from jax import lax
from jax.experimental import pallas as pl
from jax.experimental.pallas import tpu as pltpu

F32 = jnp.float32
BF16 = jnp.bfloat16
EPS = 1e-6

DN_HEADS = 4
HEAD_DIM = 128
DN_WIDTH = DN_HEADS * HEAD_DIM
DN_CONV = 4
DN_CHUNK = 64
CF_KERNEL = 31
N_GROUPS = 8
EXPERTS_PER_GROUP = 8
N_EXPERTS = N_GROUPS * EXPERTS_PER_GROUP
TOP_K = 2

LANES = 128
QKV_HALO = 8
CF_HALO = 32
SEQ_TILE = 256
ROUTER_TILE = 512
EXPERT_BLOCK = 256
COMBINE_TILE = 256
VMEM_LIMIT = 56 * 1024 * 1024


def _dot(a, b):
    return jnp.dot(a, b, preferred_element_type=F32)


def _dot_nt(a, b):
    return lax.dot_general(a, b, (((1,), (1,)), ((), ())), preferred_element_type=F32)


def _dot_tn(a, b):
    return lax.dot_general(a, b, (((0,), (0,)), ((), ())), preferred_element_type=F32)


def _split3(x):
    hi = x.astype(BF16)
    r1 = x - hi.astype(F32)
    mid = r1.astype(BF16)
    lo = (r1 - mid.astype(F32)).astype(BF16)
    return hi, mid, lo


def _silu(x):
    return x * jax.nn.sigmoid(x)


def _softplus(x):
    return jnp.maximum(x, 0.0) + jnp.log1p(jnp.exp(-jnp.abs(x)))


def _store_token_major(ref, val, base=0):
    n, d = val.shape
    pitch = d // LANES
    for j in range(pitch):
        ref[pl.ds(base + j, n, stride=pitch), :] = val[:, j * LANES:(j + 1) * LANES]


def _load_token_major(ref, n, d, base=0):
    pitch = d // LANES
    return jnp.concatenate([ref[pl.ds(base + j, n, stride=pitch), :] for j in range(pitch)], axis=1)


def _ada_kernel(c_ref, w_ref, b_ref, o_ref):
    c = c_ref[...]
    ca = _silu(c)
    c_hi, c_mid, c_lo = _split3(ca)
    w_hi, w_mid, w_lo = _split3(w_ref[...])
    acc = _dot(c_hi, w_hi)
    acc += _dot(c_hi, w_mid) + _dot(c_mid, w_hi)
    acc += _dot(c_hi, w_lo) + _dot(c_mid, w_mid) + _dot(c_lo, w_hi)
    o_ref[...] = acc + b_ref[...]


def _ada_call(c, w, b):
    bsz, d = c.shape
    n = w.shape[1]
    tn = 512
    return pl.pallas_call(
        _ada_kernel,
        grid=(n // tn,),
        in_specs=[pl.BlockSpec((bsz, d), lambda i: (0, 0)),
                  pl.BlockSpec((d, tn), lambda i: (0, i)),
                  pl.BlockSpec((1, tn), lambda i: (0, i))],
        out_specs=pl.BlockSpec((bsz, tn), lambda i: (0, i)),
        out_shape=jax.ShapeDtypeStruct((bsz, n), F32),
        compiler_params=pltpu.CompilerParams(dimension_semantics=("arbitrary",),
                                             vmem_limit_bytes=VMEM_LIMIT),
        name="ada",
    )(c, w, b.reshape(1, n))


def _causal_conv(ext_ref, w_ref, n_taps, first, rows, row_blk, col_blk, bias=None):
    cols = ext_ref.shape[1]
    out_rows = []
    for r0 in range(0, rows, row_blk):
        out_cols = []
        for c0 in range(0, cols, col_blk):
            acc = jnp.zeros((row_blk, col_blk), F32)
            for k in range(n_taps):
                acc = acc + w_ref[k:k + 1, c0:c0 + col_blk] * ext_ref[first + k + r0:first + k + r0 + row_blk,
                                                                      c0:c0 + col_blk]
            if bias is not None:
                acc = acc + bias[:, c0:c0 + col_blk]
            out_cols.append(acc)
        out_rows.append(out_cols)
    return out_rows


def _mixer_kernel(x_ref, mod_ref, npre_ref, npost_ref, wqkv_ref, wz_ref, wba_ref, wcf_ref,
                  convw_ref, alog_ref, dtb_ref, dnw_ref, pw1b_ref, dww_ref, dwb_ref, lnw_ref, lnb_ref,
                  wout_ref, o_ref, qkv_ext, qkv_act, cf_ext, state, mixed):
    ts = x_ref.shape[1]
    n_chunks = ts // DN_CHUNK

    @pl.when(pl.program_id(1) == 0)
    def _():
        qkv_ext[0:QKV_HALO, :] = jnp.zeros((QKV_HALO, qkv_ext.shape[1]), F32)
        cf_ext[0:CF_HALO, :] = jnp.zeros((CF_HALO, cf_ext.shape[1]), F32)
        state[...] = jnp.zeros(state.shape, F32)

    x = x_ref[0]
    mod = mod_ref[0]
    shift1, scale1, gate1 = mod[0:1], mod[1:2], mod[2:3]
    h = x * lax.rsqrt(jnp.mean(x * x, axis=-1, keepdims=True) + EPS)
    h = h * npre_ref[...] * (1.0 + scale1) + shift1
    hb = h.astype(BF16)

    qkv_ext[QKV_HALO:QKV_HALO + ts, :] = _dot(hb, wqkv_ref[...])
    conv = _causal_conv(qkv_ext, convw_ref, DN_CONV, QKV_HALO - (DN_CONV - 1), ts, 32, 512)
    for ri, row in enumerate(conv):
        for ci, blk in enumerate(row):
            qkv_act[ri * 32:(ri + 1) * 32, ci * 512:(ci + 1) * 512] = _silu(blk)
    qkv_ext[0:QKV_HALO, :] = qkv_ext[ts:ts + QKV_HALO, :]

    z = _dot(hb, wz_ref[...])
    ba = _dot(hb, wba_ref[...])
    beta_all = jax.nn.sigmoid(ba)
    g_all = -jnp.exp(alog_ref[...]) * _softplus(ba + dtb_ref[...])

    ri = lax.broadcasted_iota(jnp.int32, (ts, ts), 0)
    ci = lax.broadcasted_iota(jnp.int32, (ts, ts), 1)
    tri = jnp.where((ri // DN_CHUNK == ci // DN_CHUNK) & (ci <= ri), 1.0, 0.0).astype(BF16)
    g_hi, g_mid, g_lo = _split3(g_all)
    gcum = _dot(tri, g_hi) + _dot(tri, g_mid) + _dot(tri, g_lo)
    gcum_t = gcum.T
    exp_g = jnp.exp(gcum)

    r64 = lax.broadcasted_iota(jnp.int32, (DN_CHUNK, DN_CHUNK), 0)
    c64 = lax.broadcasted_iota(jnp.int32, (DN_CHUNK, DN_CHUNK), 1)
    causal = c64 <= r64
    strict = c64 < r64
    eye = jnp.where(c64 == r64, 1.0, 0.0).astype(F32)
    dnw = dnw_ref[...]

    heads = []
    for hd in range(DN_HEADS):
        lo = hd * HEAD_DIM
        qh = qkv_act[:, lo:lo + HEAD_DIM]
        kh = qkv_act[:, DN_WIDTH + lo:DN_WIDTH + lo + HEAD_DIM]
        vh = qkv_act[:, 2 * DN_WIDTH + lo:2 * DN_WIDTH + lo + HEAD_DIM]
        qn = qh * lax.rsqrt(jnp.sum(qh * qh, axis=-1, keepdims=True) + EPS) * (HEAD_DIM ** -0.5)
        kn = kh * lax.rsqrt(jnp.sum(kh * kh, axis=-1, keepdims=True) + EPS)
        beta_h = beta_all[:, hd:hd + 1]
        gc_h = gcum[:, DN_HEADS + hd:DN_HEADS + hd + 1]
        eg_h = exp_g[:, DN_HEADS + hd:DN_HEADS + hd + 1]
        k_beta = kn * beta_h
        heads.append(dict(qn=qn, kn=kn, k_beta=k_beta, v_beta=vh * beta_h, kbg=k_beta * eg_h, qg=qn * eg_h,
                          gc=gc_h, zg=_silu(z[:, lo:lo + HEAD_DIM])))

    cells = [(hd, ch) for hd in range(DN_HEADS) for ch in range(n_chunks)]
    rows = lambda ch: slice(ch * DN_CHUNK, (ch + 1) * DN_CHUNK)

    decay, kq = {}, {}
    for hd, ch in cells:
        hv, sl = heads[hd], rows(ch)
        gc_row = gcum_t[DN_HEADS + hd:DN_HEADS + hd + 1, sl]
        decay[hd, ch] = jnp.where(causal, jnp.exp(hv["gc"][sl] - gc_row), 0.0)
        lhs = jnp.concatenate([hv["k_beta"][sl], hv["qn"][sl]], axis=0).astype(BF16)
        kq[hd, ch] = _dot_nt(lhs, hv["kn"][sl].astype(BF16))
    a = {c: jnp.where(strict, kq[c][:DN_CHUNK] * decay[c], 0.0) for c in cells}
    attn = {c: (kq[c][DN_CHUNK:] * decay[c]).astype(BF16) for c in cells}

    t_inv = {c: eye - a[c] for c in cells}
    pw = {c: a[c].astype(BF16) for c in cells}
    for _ in range(5):
        pw = {c: _dot(pw[c], pw[c]).astype(BF16) for c in cells}
        t_inv = {c: t_inv[c] + _dot(t_inv[c].astype(BF16), pw[c]) for c in cells}

    sol, aw, ks, glast = {}, {}, {}, {}
    for hd, ch in cells:
        hv, sl = heads[hd], rows(ch)
        rhs = jnp.concatenate([hv["v_beta"][sl], hv["kbg"][sl]], axis=1).astype(BF16)
        sol[hd, ch] = _dot(t_inv[hd, ch].astype(BF16), rhs).astype(BF16)
    for hd, ch in cells:
        hv, sl = heads[hd], rows(ch)
        gc_col = hv["gc"][sl]
        glast[hd, ch] = gc_col[DN_CHUNK - 1:DN_CHUNK]
        k_dec = (hv["kn"][sl] * jnp.exp(glast[hd, ch] - gc_col)).astype(BF16)
        aw[hd, ch] = _dot(attn[hd, ch], sol[hd, ch])
        ks[hd, ch] = _dot_tn(k_dec, sol[hd, ch])

    s_in = {}
    s_cur = [state[hd] for hd in range(DN_HEADS)]
    for ch in range(n_chunks):
        for hd in range(DN_HEADS):
            s_in[hd, ch] = s_cur[hd].astype(BF16)
            kd_u, kd_w = ks[hd, ch][:, :HEAD_DIM], ks[hd, ch][:, HEAD_DIM:]
            s_cur[hd] = s_cur[hd] * jnp.exp(glast[hd, ch]) + kd_u - _dot(kd_w.astype(BF16), s_in[hd, ch])
    for hd in range(DN_HEADS):
        state[hd] = s_cur[hd]

    for hd, ch in cells:
        hv, sl = heads[hd], rows(ch)
        lo = hd * HEAD_DIM
        q_eff = (hv["qg"][sl] - aw[hd, ch][:, HEAD_DIM:]).astype(BF16)
        o = _dot(q_eff, s_in[hd, ch]) + aw[hd, ch][:, :HEAD_DIM]
        on = o * lax.rsqrt(jnp.mean(o * o, axis=-1, keepdims=True) + EPS) * dnw * hv["zg"][sl]
        mixed[sl, lo:lo + HEAD_DIM] = on.astype(BF16)

    cf_pre = _dot(hb, wcf_ref[...]) + pw1b_ref[...]
    cfw = cf_ext.shape[1]
    cf_ext[CF_HALO:CF_HALO + ts, :] = cf_pre[:, :cfw] * jax.nn.sigmoid(cf_pre[:, cfw:])
    conv = _causal_conv(cf_ext, dww_ref, CF_KERNEL, CF_HALO - (CF_KERNEL - 1), ts, 32, cfw, bias=dwb_ref[...])
    for ri_, row in enumerate(conv):
        cf = row[0]
        mu = jnp.mean(cf, axis=-1, keepdims=True)
        xc = cf - mu
        var = jnp.mean(xc * xc, axis=-1, keepdims=True)
        cfn = xc * lax.rsqrt(var + EPS) * lnw_ref[...] + lnb_ref[...]
        mixed[ri_ * 32:(ri_ + 1) * 32, DN_WIDTH:DN_WIDTH + cfw] = _silu(cfn).astype(BF16)
    cf_ext[0:CF_HALO, :] = cf_ext[ts:ts + CF_HALO, :]

    out = _dot(mixed[...], wout_ref[...])
    y = out * lax.rsqrt(jnp.mean(out * out, axis=-1, keepdims=True) + EPS) * npost_ref[...]
    o_ref[0] = x + gate1 * y


def _mixer_call(x, mod3, npre, npost, w_in, dn_conv_w, dn_a_log, dn_dt_bias, dn_norm_w,
                cf_pw1_b, cf_dw_w, cf_dw_b, cf_ln_w, cf_ln_b, w_out):
    bsz, seq, d = x.shape
    ts = SEQ_TILE
    cfw = cf_dw_w.shape[1]
    n_qkv = 3 * DN_WIDTH
    wqkv = w_in[:, :n_qkv].astype(BF16)
    wz = w_in[:, n_qkv:n_qkv + DN_WIDTH].astype(BF16)
    wba = jnp.pad(w_in[:, n_qkv + DN_WIDTH:n_qkv + DN_WIDTH + 2 * DN_HEADS],
                  ((0, 0), (0, LANES - 2 * DN_HEADS))).astype(BF16)
    wcf = w_in[:, n_qkv + DN_WIDTH + 2 * DN_HEADS:].astype(BF16)
    alog = jnp.pad(dn_a_log, (DN_HEADS, LANES - 2 * DN_HEADS)).reshape(1, LANES)
    dtb = jnp.pad(dn_dt_bias, (DN_HEADS, LANES - 2 * DN_HEADS)).reshape(1, LANES)
    dww = jnp.pad(cf_dw_w, ((0, 32 - CF_KERNEL), (0, 0)))
    convw = jnp.pad(dn_conv_w, ((0, 8 - DN_CONV), (0, 0)))

    def full(a):
        return pl.BlockSpec(a.shape, lambda b, j: (0,) * a.ndim)

    row = lambda a: a.reshape(1, -1)
    operands = [x, mod3, row(npre), row(npost), wqkv, wz, wba, wcf, convw, alog, dtb, row(dn_norm_w),
                row(cf_pw1_b), dww, row(cf_dw_b), row(cf_ln_w), row(cf_ln_b), w_out.astype(BF16)]
    in_specs = [pl.BlockSpec((1, ts, d), lambda b, j: (b, j, 0)),
                pl.BlockSpec((1,) + mod3.shape[1:], lambda b, j: (b, 0, 0))]
    in_specs += [full(a) for a in operands[2:]]
    return pl.pallas_call(
        _mixer_kernel,
        grid=(bsz, seq // ts),
        in_specs=in_specs,
        out_specs=pl.BlockSpec((1, ts, d), lambda b, j: (b, j, 0)),
        out_shape=jax.ShapeDtypeStruct((bsz, seq, d), F32),
        scratch_shapes=[pltpu.VMEM((QKV_HALO + ts, n_qkv), F32),
                        pltpu.VMEM((ts, n_qkv), F32),
                        pltpu.VMEM((CF_HALO + ts, cfw), F32),
                        pltpu.VMEM((DN_HEADS, HEAD_DIM, HEAD_DIM), F32),
                        pltpu.VMEM((ts, DN_WIDTH + cfw), BF16)],
        compiler_params=pltpu.CompilerParams(dimension_semantics=("arbitrary", "arbitrary"),
                                             vmem_limit_bytes=VMEM_LIMIT),
        name="mixer",
    )(*operands)


def _router_kernel(x_ref, mod_ref, nw_ref, wr_ref, br_ref, h_ref, info_ref, cnt_ref, carry):
    tt = x_ref.shape[1]

    @pl.when((pl.program_id(0) == 0) & (pl.program_id(1) == 0))
    def _():
        carry[...] = jnp.zeros(carry.shape, F32)

    x = x_ref[0]
    mod = mod_ref[0]
    shift2, scale2 = mod[3:4], mod[4:5]
    h = x * lax.rsqrt(jnp.mean(x * x, axis=-1, keepdims=True) + EPS)
    h = h * nw_ref[...] * (1.0 + scale2) + shift2
    _store_token_major(h_ref, h)

    h_hi, h_mid, h_lo = _split3(h)
    w_hi, w_mid, w_lo = _split3(wr_ref[...])
    logits = _dot(h_hi, w_hi)
    logits += _dot(h_hi, w_mid) + _dot(h_mid, w_hi)
    logits += _dot(h_hi, w_lo) + _dot(h_mid, w_mid) + _dot(h_lo, w_hi)
    logits = logits + br_ref[...]

    lane = lax.broadcasted_iota(jnp.int32, (tt, LANES), 1)
    neg = -jnp.inf
    is_grp = (lane >= N_EXPERTS) & (lane < N_EXPERTS + N_GROUPS)
    gl = jnp.where(is_grp, logits, neg)
    gmax = jnp.max(gl, axis=-1, keepdims=True)
    gsum = jnp.sum(jnp.where(is_grp, jnp.exp(gl - gmax), 0.0), axis=-1, keepdims=True)
    grp_p = 1.0 / gsum
    grp_lane = jnp.min(jnp.where(is_grp & (gl == gmax), lane, LANES), axis=-1, keepdims=True)
    grp_idx = grp_lane - N_EXPERTS

    in_grp = (lane < N_EXPERTS) & (lane // EXPERTS_PER_GROUP == grp_idx)
    el = jnp.where(in_grp, logits, neg)
    m1 = jnp.max(el, axis=-1, keepdims=True)
    e1 = jnp.min(jnp.where(in_grp & (el == m1), lane, LANES), axis=-1, keepdims=True)
    el2 = jnp.where(lane == e1, neg, el)
    m2 = jnp.max(el2, axis=-1, keepdims=True)
    e2 = jnp.min(jnp.where(in_grp & (lane != e1) & (el2 == m2), lane, LANES), axis=-1, keepdims=True)
    r = jnp.exp(m2 - m1)
    w1 = grp_p / (1.0 + r)
    w2 = grp_p * r / (1.0 + r)

    hit1 = lane == e1
    hit2 = lane == e2
    onehot = jnp.where(hit1 | hit2, 1.0, 0.0)
    rr = lax.broadcasted_iota(jnp.int32, (tt, tt), 0)
    cc = lax.broadcasted_iota(jnp.int32, (tt, tt), 1)
    strict = jnp.where(cc < rr, 1.0, 0.0).astype(BF16)
    prefix = _dot(strict, onehot.astype(BF16)) + carry[0:1, :]
    rank1 = jnp.sum(jnp.where(hit1, prefix, 0.0), axis=-1, keepdims=True)
    rank2 = jnp.sum(jnp.where(hit2, prefix, 0.0), axis=-1, keepdims=True)
    total = carry[0:1, :] + jnp.sum(onehot, axis=0, keepdims=True)
    carry[...] = jnp.broadcast_to(total, carry.shape)
    cnt_ref[...] = jnp.broadcast_to(total, cnt_ref.shape)

    info = jnp.where(lane == 0, e1.astype(F32), 0.0)
    info = jnp.where(lane == 1, e2.astype(F32), info)
    info = jnp.where(lane == 2, w1, info)
    info = jnp.where(lane == 3, w2, info)
    info = jnp.where(lane == 4, rank1, info)
    info = jnp.where(lane == 5, rank2, info)
    info_ref[0] = info


def _router_call(x1, mod3, norm_w, w_router_group, b_router_group, w_router_expert, b_router_expert):
    bsz, seq, d = x1.shape
    tt = ROUTER_TILE
    nj = seq // tt
    pitch = d // LANES
    pad = LANES - N_EXPERTS - N_GROUPS
    wr = jnp.pad(jnp.concatenate([w_router_expert, w_router_group], axis=1), ((0, 0), (0, pad)))
    br = jnp.pad(jnp.concatenate([b_router_expert, b_router_group]), (0, pad)).reshape(1, LANES)
    return pl.pallas_call(
        _router_kernel,
        grid=(bsz, seq // tt),
        in_specs=[pl.BlockSpec((1, tt, d), lambda b, j: (b, j, 0)),
                  pl.BlockSpec((1,) + mod3.shape[1:], lambda b, j: (b, 0, 0)),
                  pl.BlockSpec((1, d), lambda b, j: (0, 0)),
                  pl.BlockSpec((d, LANES), lambda b, j: (0, 0)),
                  pl.BlockSpec((1, LANES), lambda b, j: (0, 0))],
        out_specs=[pl.BlockSpec((tt * pitch, LANES), lambda b, j: (b * nj + j, 0)),
                   pl.BlockSpec((1, tt, LANES), lambda b, j: (b, j, 0)),
                   pl.BlockSpec((8, LANES), lambda b, j: (0, 0))],
        out_shape=[jax.ShapeDtypeStruct((bsz * seq * pitch, LANES), F32),
                   jax.ShapeDtypeStruct((bsz, seq, LANES), F32),
                   jax.ShapeDtypeStruct((8, LANES), F32)],
        scratch_shapes=[pltpu.VMEM((8, LANES), F32)],
        compiler_params=pltpu.CompilerParams(dimension_semantics=("arbitrary", "arbitrary"),
                                             vmem_limit_bytes=VMEM_LIMIT),
        name="router",
    )(x1, mod3, norm_w.reshape(1, d), wr, br)


def _tile_rows(idx, pitch):
    return pl.ds(pl.multiple_of(idx * pitch, pitch), pitch)


def _expert_kernel(nused_ref, be_ref, first_ref, wslot_ref, nexte_ref,
                   tok_cur_ref, tok_nxt_ref, h_hbm, wg_hbm, wu_hbm, wd_hbm, y_ref,
                   xbuf, wg_f32, wu_f32, wd_f32, wg_b, wu_b, wd_b, gsems, wsems):
    i = pl.program_id(0)
    n_used = nused_ref[0]
    d = wg_b.shape[0]
    pitch = d // LANES
    bm = y_ref.shape[0] // pitch

    def start_gather(tok_ref, slot):
        def body(r, carry):
            tok = tok_ref[0, 0, r]
            pltpu.make_async_copy(h_hbm.at[_tile_rows(tok, pitch), :],
                                  xbuf.at[_tile_rows(slot * bm + r, pitch), :], gsems.at[slot]).start()
            return carry
        lax.fori_loop(0, bm, body, 0, unroll=8)

    def weight_copies(e, slot):
        return [pltpu.make_async_copy(src.at[e], dst.at[slot], wsems.at[slot])
                for src, dst in ((wg_hbm, wg_f32), (wu_hbm, wu_f32), (wd_hbm, wd_f32))]

    def start_weights(e, slot):
        for cp in weight_copies(e, slot):
            cp.start(priority=1)

    @pl.when((i == 0) & (n_used > 0))
    def _():
        start_weights(be_ref[0], 0)
        start_gather(tok_cur_ref, 0)

    @pl.when((i < n_used) & (first_ref[i] == 1))
    def _():
        slot = wslot_ref[i]
        for cp in weight_copies(0, slot):
            cp.wait()

        @pl.when(nexte_ref[i] >= 0)
        def _():
            start_weights(nexte_ref[i], 1 - slot)

        wg_b[...] = wg_f32[slot].astype(BF16)
        wu_b[...] = wu_f32[slot].astype(BF16)
        wd_b[...] = wd_f32[slot].astype(BF16)

    def wait_gather(slot):
        base = pl.multiple_of(slot * bm * pitch, bm * pitch)
        pltpu.make_async_copy(h_hbm.at[pl.ds(0, bm * pitch), :], xbuf.at[pl.ds(base, bm * pitch), :],
                              gsems.at[slot]).wait()
        return base

    @pl.when(i < n_used)
    def _():
        base = wait_gather(i % 2)
        xb = _load_token_major(xbuf, bm, d, base).astype(BF16)
        nslot = (i + 1) % 2
        de = wg_b.shape[1]
        n_parts = 4
        rows_per_part = bm // (2 * n_parts)

        def issue_part(part):
            for r in range(part * rows_per_part, (part + 1) * rows_per_part):
                tok = tok_nxt_ref[0, 0, r]
                pltpu.make_async_copy(h_hbm.at[_tile_rows(tok, pitch), :],
                                      xbuf.at[_tile_rows(nslot * bm + r, pitch), :],
                                      gsems.at[nslot]).start(priority=1 if r % 4 == 3 else 0)

        hid = []
        for c in range(n_parts):
            issue_part(c)
            cs = slice(c * (de // n_parts), (c + 1) * (de // n_parts))
            hid.append((_silu(_dot(xb, wg_b[:, cs])) * _dot(xb, wu_b[:, cs])).astype(BF16))
        hid = jnp.concatenate(hid, axis=1)
        for c in range(n_parts):
            issue_part(n_parts + c)
            cs = slice(c * (d // n_parts), (c + 1) * (d // n_parts))
            yc = _dot(hid, wd_b[:, cs])
            for j in range(d // n_parts // LANES):
                y_ref[pl.ds(c * (d // n_parts // LANES) + j, bm, stride=pitch), :] = yc[:, j * LANES:(j + 1) * LANES]

    @pl.when(i == n_used - 1)
    def _():
        wait_gather((i + 1) % 2)

    @pl.when(i >= n_used)
    def _():
        y_ref[...] = jnp.zeros(y_ref.shape, F32)


def _expert_call(h2, row_tok, block_e, n_used, w_gate, w_up, w_down):
    bm = EXPERT_BLOCK
    n_blocks = row_tok.shape[0] // bm
    d, de = w_gate.shape[1], w_gate.shape[2]
    pitch = d // LANES
    tok3 = row_tok.reshape(n_blocks, 1, bm)
    blk = jnp.arange(n_blocks, dtype=jnp.int32)
    used = blk < n_used[0]
    first = jnp.concatenate([jnp.ones((1,), jnp.int32), (block_e[1:] != block_e[:-1]).astype(jnp.int32)])
    first = jnp.where(used, first, 0)
    wslot = (jnp.cumsum(first) - 1) % 2
    nxt_first = jnp.where((first == 1) & (blk > 0), blk, n_blocks)
    nxt_idx = lax.cummin(jnp.concatenate([nxt_first[1:], jnp.full((1,), n_blocks, jnp.int32)]), reverse=True)
    next_e = jnp.where(nxt_idx < n_blocks, block_e[jnp.minimum(nxt_idx, n_blocks - 1)], -1).astype(jnp.int32)
    smem_blk = lambda f: pl.BlockSpec((1, 1, bm), f, memory_space=pltpu.SMEM)
    grid_spec = pltpu.PrefetchScalarGridSpec(
        num_scalar_prefetch=5,
        grid=(n_blocks,),
        in_specs=[smem_blk(lambda i, *_: (i, 0, 0)),
                  smem_blk(lambda i, *_: (jnp.minimum(i + 1, n_blocks - 1), 0, 0)),
                  pl.BlockSpec(memory_space=pl.ANY),
                  pl.BlockSpec(memory_space=pl.ANY),
                  pl.BlockSpec(memory_space=pl.ANY),
                  pl.BlockSpec(memory_space=pl.ANY)],
        out_specs=pl.BlockSpec((bm * pitch, LANES), lambda i, *_: (i, 0)),
        scratch_shapes=[pltpu.VMEM((2 * bm * pitch, LANES), F32),
                        pltpu.VMEM((2, d, de), F32), pltpu.VMEM((2, d, de), F32), pltpu.VMEM((2, de, d), F32),
                        pltpu.VMEM((d, de), BF16), pltpu.VMEM((d, de), BF16), pltpu.VMEM((de, d), BF16),
                        pltpu.SemaphoreType.DMA((2,)), pltpu.SemaphoreType.DMA((2,))],
    )
    return pl.pallas_call(
        _expert_kernel,
        grid_spec=grid_spec,
        out_shape=jax.ShapeDtypeStruct((n_blocks * bm * pitch, LANES), F32),
        compiler_params=pltpu.CompilerParams(dimension_semantics=("arbitrary",),
                                             vmem_limit_bytes=VMEM_LIMIT),
        name="experts",
    )(n_used, block_e, first, wslot.astype(jnp.int32), next_e, tok3, tok3, h2, w_gate, w_up, w_down)


def _combine_kernel(pos_cur_ref, pos_nxt_ref, x_ref, info_ref, mod_ref, nw_ref, y_hbm, o_ref, ybuf, sems):
    tt, d = x_ref.shape[1], x_ref.shape[2]
    pitch = d // LANES
    step = pl.program_id(0) * pl.num_programs(1) + pl.program_id(1)
    n_steps = pl.num_programs(0) * pl.num_programs(1)
    slot_rows = TOP_K * tt * pitch

    def start_gather(pos_ref, slot):
        def body(r, carry):
            for k in range(TOP_K):
                p = pos_ref[0, 0, k * tt + r]
                pltpu.make_async_copy(y_hbm.at[_tile_rows(p, pitch), :],
                                      ybuf.at[_tile_rows((slot * TOP_K + k) * tt + r, pitch), :],
                                      sems.at[slot]).start(priority=k)
            return carry
        lax.fori_loop(0, tt, body, 0, unroll=4)

    @pl.when(step == 0)
    def _():
        start_gather(pos_cur_ref, 0)

    @pl.when(step + 1 < n_steps)
    def _():
        start_gather(pos_nxt_ref, (step + 1) % 2)

    info = info_ref[0]
    w1 = info[:, 2:3]
    w2 = info[:, 3:4]
    x = x_ref[0]
    gate2 = mod_ref[0][5:6]
    base = pl.multiple_of((step % 2) * slot_rows, slot_rows)
    pltpu.make_async_copy(y_hbm.at[pl.ds(0, slot_rows), :], ybuf.at[pl.ds(base, slot_rows), :],
                          sems.at[step % 2]).wait()
    moe = (_load_token_major(ybuf, tt, d, base) * w1
           + _load_token_major(ybuf, tt, d, base + tt * pitch) * w2)
    y = moe * lax.rsqrt(jnp.mean(moe * moe, axis=-1, keepdims=True) + EPS) * nw_ref[...]
    o_ref[0] = x + gate2 * y


def _combine_call(x1, info, mod3, norm_w, y, pos):
    bsz, seq, d = x1.shape
    tt = COMBINE_TILE
    nj = seq // tt
    pos3 = pos.reshape(bsz * nj, tt, TOP_K).transpose(0, 2, 1).reshape(bsz * nj, 1, TOP_K * tt)
    n_tiles = bsz * nj
    pos_blk = lambda f: pl.BlockSpec((1, 1, TOP_K * tt), f, memory_space=pltpu.SMEM)
    return pl.pallas_call(
        _combine_kernel,
        grid=(bsz, nj),
        in_specs=[pos_blk(lambda b, j: (b * nj + j, 0, 0)),
                  pos_blk(lambda b, j: (jnp.minimum(b * nj + j + 1, n_tiles - 1), 0, 0)),
                  pl.BlockSpec((1, tt, d), lambda b, j: (b, j, 0)),
                  pl.BlockSpec((1, tt, LANES), lambda b, j: (b, j, 0)),
                  pl.BlockSpec((1,) + mod3.shape[1:], lambda b, j: (b, 0, 0)),
                  pl.BlockSpec((1, d), lambda b, j: (0, 0)),
                  pl.BlockSpec(memory_space=pl.ANY)],
        out_specs=pl.BlockSpec((1, tt, d), lambda b, j: (b, j, 0)),
        out_shape=jax.ShapeDtypeStruct((bsz, seq, d), F32),
        scratch_shapes=[pltpu.VMEM((2 * TOP_K * tt * (d // LANES), LANES), F32), pltpu.SemaphoreType.DMA((2,))],
        compiler_params=pltpu.CompilerParams(dimension_semantics=("arbitrary", "arbitrary"),
                                             vmem_limit_bytes=VMEM_LIMIT),
        name="combine",
    )(pos3, pos3, x1, info, mod3, norm_w.reshape(1, d), y)


def _layer(x, mod, norm_pre_mix, norm_post_mix, w_in, dn_conv_w, dn_a_log, dn_dt_bias, dn_norm_w,
           cf_pw1_b, cf_dw_w, cf_dw_b, cf_ln_w, cf_ln_b, w_out, norm_pre_ffn, norm_post_ffn,
           w_router_group, b_router_group, w_router_expert, b_router_expert, w_gate, w_up, w_down):
    bsz, seq, d = x.shape
    t = bsz * seq
    mod3 = mod.reshape(bsz, -1, d)
    x1 = _mixer_call(x, mod3, norm_pre_mix, norm_post_mix, w_in, dn_conv_w, dn_a_log, dn_dt_bias, dn_norm_w,
                     cf_pw1_b, cf_dw_w, cf_dw_b, cf_ln_w, cf_ln_b, w_out)
    h2, info, cnt = _router_call(x1, mod3, norm_pre_ffn, w_router_group, b_router_group,
                                 w_router_expert, b_router_expert)

    bm = EXPERT_BLOCK
    info2 = info.reshape(t, LANES)
    expert_id = info2[:, 0:TOP_K].astype(jnp.int32)
    rank = info2[:, 4:4 + TOP_K].astype(jnp.int32)
    counts = cnt[0, :N_EXPERTS].astype(jnp.int32)
    padded = (counts + bm - 1) // bm * bm
    pend = jnp.cumsum(padded)
    pstart = pend - padded
    pos = pstart[expert_id] + rank
    n_blocks = -(-(t * TOP_K) // bm) + N_EXPERTS
    tok = jnp.broadcast_to(jnp.arange(t, dtype=jnp.int32)[:, None], (t, TOP_K))
    row_tok = jnp.zeros((n_blocks * bm,), jnp.int32).at[pos.reshape(-1)].set(tok.reshape(-1))
    block_start = jnp.arange(n_blocks, dtype=jnp.int32) * bm
    block_e = jnp.minimum(jnp.sum(pend[None, :] <= block_start[:, None], axis=1), N_EXPERTS - 1).astype(jnp.int32)
    n_used = (pend[-1] // bm).astype(jnp.int32).reshape(1)

    y = _expert_call(h2, row_tok, block_e, n_used, w_gate, w_up, w_down)
    return _combine_call(x1, info, mod3, norm_post_ffn, y, pos)


def kernel(x, c, w_ada, b_ada, norm_pre_mix, norm_post_mix, w_in, dn_conv_w, dn_a_log, dn_dt_bias, dn_norm_w,
           cf_pw1_b, cf_dw_w, cf_dw_b, cf_ln_w, cf_ln_b, w_out, norm_pre_ffn, norm_post_ffn,
           w_router_group, b_router_group, w_router_expert, b_router_expert, w_gate, w_up, w_down):
    depth = w_ada.shape[0]
    for l in range(depth):
        mod = _ada_call(c, w_ada[l], b_ada[l])
        x = _layer(x, mod, norm_pre_mix[l], norm_post_mix[l], w_in[l], dn_conv_w[l], dn_a_log[l],
                   dn_dt_bias[l], dn_norm_w[l], cf_pw1_b[l], cf_dw_w[l], cf_dw_b[l], cf_ln_w[l], cf_ln_b[l],
                   w_out[l], norm_pre_ffn[l], norm_post_ffn[l], w_router_group[l], b_router_group[l],
                   w_router_expert[l], b_router_expert[l], w_gate[l], w_up[l], w_down[l])
    return x
```

```python
import functools

import jax
import jax.numpy as jnp
from jax import lax
from jax.experimental import pallas as pl
from jax.experimental.pallas import tpu as pltpu

F32 = jnp.float32
BF16 = jnp.bfloat16
EPS = 1e-6

DN_HEADS = 4
HEAD_DIM = 128
DN_WIDTH = DN_HEADS * HEAD_DIM
DN_CONV = 4
DN_CHUNK = 64
CF_KERNEL = 31
N_GROUPS = 8
EXPERTS_PER_GROUP = 8
N_EXPERTS = N_GROUPS * EXPERTS_PER_GROUP
TOP_K = 2

LANES = 128
QKV_HALO = 8
CF_HALO = 32
SEQ_TILE = 256
ROUTER_TILE = 512
EXPERT_BLOCK = 256
COMBINE_TILE = 256
VMEM_LIMIT = 56 * 1024 * 1024


def _dot(a, b):
    return jnp.dot(a, b, preferred_element_type=F32)


def _dot_nt(a, b):
    return lax.dot_general(a, b, (((1,), (1,)), ((), ())), preferred_element_type=F32)


def _dot_tn(a, b):
    return lax.dot_general(a, b, (((0,), (0,)), ((), ())), preferred_element_type=F32)


def _split3(x):
    hi = x.astype(BF16)
    r1 = x - hi.astype(F32)
    mid = r1.astype(BF16)
    lo = (r1 - mid.astype(F32)).astype(BF16)
    return hi, mid, lo


def _silu(x):
    return x * jax.nn.sigmoid(x)


def _softplus(x):
    return jnp.maximum(x, 0.0) + jnp.log1p(jnp.exp(-jnp.abs(x)))


def _store_token_major(ref, val, base=0):
    n, d = val.shape
    pitch = d // LANES
    for j in range(pitch):
        ref[pl.ds(base + j, n, stride=pitch), :] = val[:, j * LANES:(j + 1) * LANES]


def _load_token_major(ref, n, d, base=0):
    pitch = d // LANES
    return jnp.concatenate([ref[pl.ds(base + j, n, stride=pitch), :] for j in range(pitch)], axis=1)


def _ada_kernel(c_ref, w_ref, b_ref, o_ref):
    c = c_ref[...]
    ca = _silu(c)
    c_hi, c_mid, c_lo = _split3(ca)
    w_hi, w_mid, w_lo = _split3(w_ref[...])
    acc = _dot(c_hi, w_hi)
    acc += _dot(c_hi, w_mid) + _dot(c_mid, w_hi)
    acc += _dot(c_hi, w_lo) + _dot(c_mid, w_mid) + _dot(c_lo, w_hi)
    o_ref[...] = acc + b_ref[...]


def _ada_call(c, w, b):
    bsz, d = c.shape
    n = w.shape[1]
    tn = 512
    return pl.pallas_call(
        _ada_kernel,
        grid=(n // tn,),
        in_specs=[pl.BlockSpec((bsz, d), lambda i: (0, 0)),
                  pl.BlockSpec((d, tn), lambda i: (0, i)),
                  pl.BlockSpec((1, tn), lambda i: (0, i))],
        out_specs=pl.BlockSpec((bsz, tn), lambda i: (0, i)),
        out_shape=jax.ShapeDtypeStruct((bsz, n), F32),
        compiler_params=pltpu.CompilerParams(dimension_semantics=("arbitrary",),
                                             vmem_limit_bytes=VMEM_LIMIT),
        name="ada",
    )(c, w, b.reshape(1, n))


def _causal_conv(ext_ref, w_ref, n_taps, first, rows, row_blk, col_blk, bias=None):
    cols = ext_ref.shape[1]
    out_rows = []
    for r0 in range(0, rows, row_blk):
        out_cols = []
        for c0 in range(0, cols, col_blk):
            acc = jnp.zeros((row_blk, col_blk), F32)
            for k in range(n_taps):
                acc = acc + w_ref[k:k + 1, c0:c0 + col_blk] * ext_ref[first + k + r0:first + k + r0 + row_blk,
                                                                      c0:c0 + col_blk]
            if bias is not None:
                acc = acc + bias[:, c0:c0 + col_blk]
            out_cols.append(acc)
        out_rows.append(out_cols)
    return out_rows


def _mixer_kernel(x_ref, mod_ref, npre_ref, npost_ref, wqkv_ref, wz_ref, wba_ref, wcf_ref,
                  convw_ref, alog_ref, dtb_ref, dnw_ref, pw1b_ref, dww_ref, dwb_ref, lnw_ref, lnb_ref,
                  wout_ref, o_ref, qkv_ext, qkv_act, cf_ext, state, mixed):
    ts = x_ref.shape[1]
    n_chunks = ts // DN_CHUNK

    @pl.when(pl.program_id(1) == 0)
    def _():
        qkv_ext[0:QKV_HALO, :] = jnp.zeros((QKV_HALO, qkv_ext.shape[1]), F32)
        cf_ext[0:CF_HALO, :] = jnp.zeros((CF_HALO, cf_ext.shape[1]), F32)
        state[...] = jnp.zeros(state.shape, F32)

    x = x_ref[0]
    mod = mod_ref[0]
    shift1, scale1, gate1 = mod[0:1], mod[1:2], mod[2:3]
    h = x * lax.rsqrt(jnp.mean(x * x, axis=-1, keepdims=True) + EPS)
    h = h * npre_ref[...] * (1.0 + scale1) + shift1
    hb = h.astype(BF16)

    qkv_ext[QKV_HALO:QKV_HALO + ts, :] = _dot(hb, wqkv_ref[...])
    conv = _causal_conv(qkv_ext, convw_ref, DN_CONV, QKV_HALO - (DN_CONV - 1), ts, 32, 512)
    for ri, row in enumerate(conv):
        for ci, blk in enumerate(row):
            qkv_act[ri * 32:(ri + 1) * 32, ci * 512:(ci + 1) * 512] = _silu(blk)
    qkv_ext[0:QKV_HALO, :] = qkv_ext[ts:ts + QKV_HALO, :]

    z = _dot(hb, wz_ref[...])
    ba = _dot(hb, wba_ref[...])
    beta_all = jax.nn.sigmoid(ba)
    g_all = -jnp.exp(alog_ref[...]) * _softplus(ba + dtb_ref[...])

    ri = lax.broadcasted_iota(jnp.int32, (ts, ts), 0)
    ci = lax.broadcasted_iota(jnp.int32, (ts, ts), 1)
    tri = jnp.where((ri // DN_CHUNK == ci // DN_CHUNK) & (ci <= ri), 1.0, 0.0).astype(BF16)
    g_hi, g_mid, g_lo = _split3(g_all)
    gcum = _dot(tri, g_hi) + _dot(tri, g_mid) + _dot(tri, g_lo)
    gcum_t = gcum.T
    exp_g = jnp.exp(gcum)

    r64 = lax.broadcasted_iota(jnp.int32, (DN_CHUNK, DN_CHUNK), 0)
    c64 = lax.broadcasted_iota(jnp.int32, (DN_CHUNK, DN_CHUNK), 1)
    causal = c64 <= r64
    strict = c64 < r64
    eye = jnp.where(c64 == r64, 1.0, 0.0).astype(F32)
    dnw = dnw_ref[...]

    heads = []
    for hd in range(DN_HEADS):
        lo = hd * HEAD_DIM
        qh = qkv_act[:, lo:lo + HEAD_DIM]
        kh = qkv_act[:, DN_WIDTH + lo:DN_WIDTH + lo + HEAD_DIM]
        vh = qkv_act[:, 2 * DN_WIDTH + lo:2 * DN_WIDTH + lo + HEAD_DIM]
        qn = qh * lax.rsqrt(jnp.sum(qh * qh, axis=-1, keepdims=True) + EPS) * (HEAD_DIM ** -0.5)
        kn = kh * lax.rsqrt(jnp.sum(kh * kh, axis=-1, keepdims=True) + EPS)
        beta_h = beta_all[:, hd:hd + 1]
        gc_h = gcum[:, DN_HEADS + hd:DN_HEADS + hd + 1]
        eg_h = exp_g[:, DN_HEADS + hd:DN_HEADS + hd + 1]
        k_beta = kn * beta_h
        heads.append(dict(qn=qn, kn=kn, k_beta=k_beta, v_beta=vh * beta_h, kbg=k_beta * eg_h, qg=qn * eg_h,
                          gc=gc_h, zg=_silu(z[:, lo:lo + HEAD_DIM])))

    cells = [(hd, ch) for hd in range(DN_HEADS) for ch in range(n_chunks)]
    rows = lambda ch: slice(ch * DN_CHUNK, (ch + 1) * DN_CHUNK)

    decay, kq = {}, {}
    for hd, ch in cells:
        hv, sl = heads[hd], rows(ch)
        gc_row = gcum_t[DN_HEADS + hd:DN_HEADS + hd + 1, sl]
        decay[hd, ch] = jnp.where(causal, jnp.exp(hv["gc"][sl] - gc_row), 0.0)
        lhs = jnp.concatenate([hv["k_beta"][sl], hv["qn"][sl]], axis=0).astype(BF16)
        kq[hd, ch] = _dot_nt(lhs, hv["kn"][sl].astype(BF16))
    a = {c: jnp.where(strict, kq[c][:DN_CHUNK] * decay[c], 0.0) for c in cells}
    attn = {c: (kq[c][DN_CHUNK:] * decay[c]).astype(BF16) for c in cells}

    t_inv = {c: eye - a[c] for c in cells}
    pw = {c: a[c].astype(BF16) for c in cells}
    for _ in range(5):
        pw = {c: _dot(pw[c], pw[c]).astype(BF16) for c in cells}
        t_inv = {c: t_inv[c] + _dot(t_inv[c].astype(BF16), pw[c]) for c in cells}

    sol, aw, ks, glast = {}, {}, {}, {}
    for hd, ch in cells:
        hv, sl = heads[hd], rows(ch)
        rhs = jnp.concatenate([hv["v_beta"][sl], hv["kbg"][sl]], axis=1).astype(BF16)
        sol[hd, ch] = _dot(t_inv[hd, ch].astype(BF16), rhs).astype(BF16)
    for hd, ch in cells:
        hv, sl = heads[hd], rows(ch)
        gc_col = hv["gc"][sl]
        glast[hd, ch] = gc_col[DN_CHUNK - 1:DN_CHUNK]
        k_dec = (hv["kn"][sl] * jnp.exp(glast[hd, ch] - gc_col)).astype(BF16)
        aw[hd, ch] = _dot(attn[hd, ch], sol[hd, ch])
        ks[hd, ch] = _dot_tn(k_dec, sol[hd, ch])

    s_in = {}
    s_cur = [state[hd] for hd in range(DN_HEADS)]
    for ch in range(n_chunks):
        for hd in range(DN_HEADS):
            s_in[hd, ch] = s_cur[hd].astype(BF16)
            kd_u, kd_w = ks[hd, ch][:, :HEAD_DIM], ks[hd, ch][:, HEAD_DIM:]
            s_cur[hd] = s_cur[hd] * jnp.exp(glast[hd, ch]) + kd_u - _dot(kd_w.astype(BF16), s_in[hd, ch])
    for hd in range(DN_HEADS):
        state[hd] = s_cur[hd]

    for hd, ch in cells:
        hv, sl = heads[hd], rows(ch)
        lo = hd * HEAD_DIM
        q_eff = (hv["qg"][sl] - aw[hd, ch][:, HEAD_DIM:]).astype(BF16)
        o = _dot(q_eff, s_in[hd, ch]) + aw[hd, ch][:, :HEAD_DIM]
        on = o * lax.rsqrt(jnp.mean(o * o, axis=-1, keepdims=True) + EPS) * dnw * hv["zg"][sl]
        mixed[sl, lo:lo + HEAD_DIM] = on.astype(BF16)

    cf_pre = _dot(hb, wcf_ref[...]) + pw1b_ref[...]
    cfw = cf_ext.shape[1]
    cf_ext[CF_HALO:CF_HALO + ts, :] = cf_pre[:, :cfw] * jax.nn.sigmoid(cf_pre[:, cfw:])
    conv = _causal_conv(cf_ext, dww_ref, CF_KERNEL, CF_HALO - (CF_KERNEL - 1), ts, 32, cfw, bias=dwb_ref[...])
    for ri_, row in enumerate(conv):
        cf = row[0]
        mu = jnp.mean(cf, axis=-1, keepdims=True)
        xc = cf - mu
        var = jnp.mean(xc * xc, axis=-1, keepdims=True)
        cfn = xc * lax.rsqrt(var + EPS) * lnw_ref[...] + lnb_ref[...]
        mixed[ri_ * 32:(ri_ + 1) * 32, DN_WIDTH:DN_WIDTH + cfw] = _silu(cfn).astype(BF16)
    cf_ext[0:CF_HALO, :] = cf_ext[ts:ts + CF_HALO, :]

    out = _dot(mixed[...], wout_ref[...])
    y = out * lax.rsqrt(jnp.mean(out * out, axis=-1, keepdims=True) + EPS) * npost_ref[...]
    o_ref[0] = x + gate1 * y


def _mixer_call(x, mod3, npre, npost, w_in, dn_conv_w, dn_a_log, dn_dt_bias, dn_norm_w,
                cf_pw1_b, cf_dw_w, cf_dw_b, cf_ln_w, cf_ln_b, w_out):
    bsz, seq, d = x.shape
    ts = SEQ_TILE
    cfw = cf_dw_w.shape[1]
    n_qkv = 3 * DN_WIDTH
    wqkv = w_in[:, :n_qkv].astype(BF16)
    wz = w_in[:, n_qkv:n_qkv + DN_WIDTH].astype(BF16)
    wba = jnp.pad(w_in[:, n_qkv + DN_WIDTH:n_qkv + DN_WIDTH + 2 * DN_HEADS],
                  ((0, 0), (0, LANES - 2 * DN_HEADS))).astype(BF16)
    wcf = w_in[:, n_qkv + DN_WIDTH + 2 * DN_HEADS:].astype(BF16)
    alog = jnp.pad(dn_a_log, (DN_HEADS, LANES - 2 * DN_HEADS)).reshape(1, LANES)
    dtb = jnp.pad(dn_dt_bias, (DN_HEADS, LANES - 2 * DN_HEADS)).reshape(1, LANES)
    dww = jnp.pad(cf_dw_w, ((0, 32 - CF_KERNEL), (0, 0)))
    convw = jnp.pad(dn_conv_w, ((0, 8 - DN_CONV), (0, 0)))

    def full(a):
        return pl.BlockSpec(a.shape, lambda b, j: (0,) * a.ndim)

    row = lambda a: a.reshape(1, -1)
    operands = [x, mod3, row(npre), row(npost), wqkv, wz, wba, wcf, convw, alog, dtb, row(dn_norm_w),
                row(cf_pw1_b), dww, row(cf_dw_b), row(cf_ln_w), row(cf_ln_b), w_out.astype(BF16)]
    in_specs = [pl.BlockSpec((1, ts, d), lambda b, j: (b, j, 0)),
                pl.BlockSpec((1,) + mod3.shape[1:], lambda b, j: (b, 0, 0))]
    in_specs += [full(a) for a in operands[2:]]
    return pl.pallas_call(
        _mixer_kernel,
        grid=(bsz, seq // ts),
        in_specs=in_specs,
        out_specs=pl.BlockSpec((1, ts, d), lambda b, j: (b, j, 0)),
        out_shape=jax.ShapeDtypeStruct((bsz, seq, d), F32),
        scratch_shapes=[pltpu.VMEM((QKV_HALO + ts, n_qkv), F32),
                        pltpu.VMEM((ts, n_qkv), F32),
                        pltpu.VMEM((CF_HALO + ts, cfw), F32),
                        pltpu.VMEM((DN_HEADS, HEAD_DIM, HEAD_DIM), F32),
                        pltpu.VMEM((ts, DN_WIDTH + cfw), BF16)],
        compiler_params=pltpu.CompilerParams(dimension_semantics=("arbitrary", "arbitrary"),
                                             vmem_limit_bytes=VMEM_LIMIT),
        name="mixer",
    )(*operands)


def _router_kernel(x_ref, mod_ref, nw_ref, wr_ref, br_ref, h_ref, info_ref, cnt_ref, carry):
    tt = x_ref.shape[1]

    @pl.when((pl.program_id(0) == 0) & (pl.program_id(1) == 0))
    def _():
        carry[...] = jnp.zeros(carry.shape, F32)

    x = x_ref[0]
    mod = mod_ref[0]
    shift2, scale2 = mod[3:4], mod[4:5]
    h = x * lax.rsqrt(jnp.mean(x * x, axis=-1, keepdims=True) + EPS)
    h = h * nw_ref[...] * (1.0 + scale2) + shift2
    _store_token_major(h_ref, h)

    h_hi, h_mid, h_lo = _split3(h)
    w_hi, w_mid, w_lo = _split3(wr_ref[...])
    logits = _dot(h_hi, w_hi)
    logits += _dot(h_hi, w_mid) + _dot(h_mid, w_hi)
    logits += _dot(h_hi, w_lo) + _dot(h_mid, w_mid) + _dot(h_lo, w_hi)
    logits = logits + br_ref[...]

    lane = lax.broadcasted_iota(jnp.int32, (tt, LANES), 1)
    neg = -jnp.inf
    is_grp = (lane >= N_EXPERTS) & (lane < N_EXPERTS + N_GROUPS)
    gl = jnp.where(is_grp, logits, neg)
    gmax = jnp.max(gl, axis=-1, keepdims=True)
    gsum = jnp.sum(jnp.where(is_grp, jnp.exp(gl - gmax), 0.0), axis=-1, keepdims=True)
    grp_p = 1.0 / gsum
    grp_lane = jnp.min(jnp.where(is_grp & (gl == gmax), lane, LANES), axis=-1, keepdims=True)
    grp_idx = grp_lane - N_EXPERTS

    in_grp = (lane < N_EXPERTS) & (lane // EXPERTS_PER_GROUP == grp_idx)
    el = jnp.where(in_grp, logits, neg)
    m1 = jnp.max(el, axis=-1, keepdims=True)
    e1 = jnp.min(jnp.where(in_grp & (el == m1), lane, LANES), axis=-1, keepdims=True)
    el2 = jnp.where(lane == e1, neg, el)
    m2 = jnp.max(el2, axis=-1, keepdims=True)
    e2 = jnp.min(jnp.where(in_grp & (lane != e1) & (el2 == m2), lane, LANES), axis=-1, keepdims=True)
    r = jnp.exp(m2 - m1)
    w1 = grp_p / (1.0 + r)
    w2 = grp_p * r / (1.0 + r)

    hit1 = lane == e1
    hit2 = lane == e2
    onehot = jnp.where(hit1 | hit2, 1.0, 0.0)
    rr = lax.broadcasted_iota(jnp.int32, (tt, tt), 0)
    cc = lax.broadcasted_iota(jnp.int32, (tt, tt), 1)
    strict = jnp.where(cc < rr, 1.0, 0.0).astype(BF16)
    prefix = _dot(strict, onehot.astype(BF16)) + carry[0:1, :]
    rank1 = jnp.sum(jnp.where(hit1, prefix, 0.0), axis=-1, keepdims=True)
    rank2 = jnp.sum(jnp.where(hit2, prefix, 0.0), axis=-1, keepdims=True)
    total = carry[0:1, :] + jnp.sum(onehot, axis=0, keepdims=True)
    carry[...] = jnp.broadcast_to(total, carry.shape)
    cnt_ref[...] = jnp.broadcast_to(total, cnt_ref.shape)

    info = jnp.where(lane == 0, e1.astype(F32), 0.0)
    info = jnp.where(lane == 1, e2.astype(F32), info)
    info = jnp.where(lane == 2, w1, info)
    info = jnp.where(lane == 3, w2, info)
    info = jnp.where(lane == 4, rank1, info)
    info = jnp.where(lane == 5, rank2, info)
    info_ref[0] = info


def _router_call(x1, mod3, norm_w, w_router_group, b_router_group, w_router_expert, b_router_expert):
    bsz, seq, d = x1.shape
    tt = ROUTER_TILE
    nj = seq // tt
    pitch = d // LANES
    pad = LANES - N_EXPERTS - N_GROUPS
    wr = jnp.pad(jnp.concatenate([w_router_expert, w_router_group], axis=1), ((0, 0), (0, pad)))
    br = jnp.pad(jnp.concatenate([b_router_expert, b_router_group]), (0, pad)).reshape(1, LANES)
    return pl.pallas_call(
        _router_kernel,
        grid=(bsz, seq // tt),
        in_specs=[pl.BlockSpec((1, tt, d), lambda b, j: (b, j, 0)),
                  pl.BlockSpec((1,) + mod3.shape[1:], lambda b, j: (b, 0, 0)),
                  pl.BlockSpec((1, d), lambda b, j: (0, 0)),
                  pl.BlockSpec((d, LANES), lambda b, j: (0, 0)),
                  pl.BlockSpec((1, LANES), lambda b, j: (0, 0))],
        out_specs=[pl.BlockSpec((tt * pitch, LANES), lambda b, j: (b * nj + j, 0)),
                   pl.BlockSpec((1, tt, LANES), lambda b, j: (b, j, 0)),
                   pl.BlockSpec((8, LANES), lambda b, j: (0, 0))],
        out_shape=[jax.ShapeDtypeStruct((bsz * seq * pitch, LANES), F32),
                   jax.ShapeDtypeStruct((bsz, seq, LANES), F32),
                   jax.ShapeDtypeStruct((8, LANES), F32)],
        scratch_shapes=[pltpu.VMEM((8, LANES), F32)],
        compiler_params=pltpu.CompilerParams(dimension_semantics=("arbitrary", "arbitrary"),
                                             vmem_limit_bytes=VMEM_LIMIT),
        name="router",
    )(x1, mod3, norm_w.reshape(1, d), wr, br)


def _tile_rows(idx, pitch):
    return pl.ds(pl.multiple_of(idx * pitch, pitch), pitch)


def _dispatch_kernel(nused_ref, pend_ref, padded_ref, pos_ref, h_ref, xs_hbm, zbuf, zsem, sem, *, pitch, bm):
    step = pl.program_id(0)
    tt = h_ref.shape[0] // pitch

    @pl.when(step == 0)
    def _():
        zbuf[...] = jnp.zeros(zbuf.shape, F32)

        def tail_copy(e):
            start = pl.multiple_of((pend_ref[e] - bm) * pitch, pitch)
            return pltpu.make_async_copy(zbuf, xs_hbm.at[pl.ds(start, bm * pitch), :], zsem.at[0])

        def issue(e, carry):
            @pl.when(padded_ref[e] > 0)
            def _():
                tail_copy(e).start()
            return carry

        def drain(e, carry):
            @pl.when(padded_ref[e] > 0)
            def _():
                tail_copy(e).wait()
            return carry

        def block_copy(b):
            start = pl.multiple_of(b * bm * pitch, bm * pitch)
            return pltpu.make_async_copy(zbuf, xs_hbm.at[pl.ds(start, bm * pitch), :], zsem.at[0])

        def issue_block(b, carry):
            block_copy(b).start()
            return carry

        def drain_block(b, carry):
            block_copy(b).wait()
            return carry

        n_blocks = xs_hbm.shape[0] // (bm * pitch)
        lax.fori_loop(0, N_EXPERTS, issue, 0)
        lax.fori_loop(nused_ref[0], n_blocks, issue_block, 0)
        lax.fori_loop(0, N_EXPERTS, drain, 0)
        lax.fori_loop(nused_ref[0], n_blocks, drain_block, 0)

    def body(r, carry):
        for k in range(TOP_K):
            p = pos_ref[0, 0, k * tt + r]
            pltpu.make_async_copy(h_ref.at[_tile_rows(r, pitch), :], xs_hbm.at[_tile_rows(p, pitch), :],
                                  sem.at[0]).start(priority=k)
        return carry
    lax.fori_loop(0, tt, body, 0, unroll=4)
    for k in range(TOP_K):
        pltpu.make_async_copy(h_ref, xs_hbm.at[pl.ds(0, tt * pitch), :], sem.at[0]).wait()


def _dispatch_call(h2, pos3, n_used, pend, padded, n_rows, d):
    pitch = d // LANES
    bm = EXPERT_BLOCK
    n_tiles, _, two_tt = pos3.shape
    tt = two_tt // TOP_K
    grid_spec = pltpu.PrefetchScalarGridSpec(
        num_scalar_prefetch=3,
        grid=(n_tiles,),
        in_specs=[pl.BlockSpec((1, 1, two_tt), lambda s, *_: (s, 0, 0), memory_space=pltpu.SMEM),
                  pl.BlockSpec((tt * pitch, LANES), lambda s, *_: (s, 0))],
        out_specs=pl.BlockSpec(memory_space=pl.ANY),
        scratch_shapes=[pltpu.VMEM((bm * pitch, LANES), F32), pltpu.SemaphoreType.DMA((1,)),
                        pltpu.SemaphoreType.DMA((1,))],
    )
    return pl.pallas_call(
        functools.partial(_dispatch_kernel, pitch=pitch, bm=bm),
        grid_spec=grid_spec,
        out_shape=jax.ShapeDtypeStruct((n_rows * pitch, LANES), F32),
        compiler_params=pltpu.CompilerParams(dimension_semantics=("arbitrary",),
                                             vmem_limit_bytes=VMEM_LIMIT),
        name="dispatch",
    )(n_used, pend, padded, pos3, h2)


def _expert_kernel(nused_ref, be_ref, first_ref, wslot_ref, nexte_ref, x_ref, wg_hbm, wu_hbm, wd_hbm, y_ref,
                   wg_f32, wu_f32, wd_f32, wg_b, wu_b, wd_b, wsems):
    i = pl.program_id(0)
    n_used = nused_ref[0]
    d = wg_b.shape[0]
    pitch = d // LANES
    bm = y_ref.shape[0] // pitch

    def weight_copies(e, slot):
        return [pltpu.make_async_copy(src.at[e], dst.at[slot], wsems.at[slot])
                for src, dst in ((wg_hbm, wg_f32), (wu_hbm, wu_f32), (wd_hbm, wd_f32))]

    def start_weights(e, slot):
        for prio, cp in zip((1, 1, 0), weight_copies(e, slot)):
            cp.start(priority=prio)

    @pl.when(i == 0)
    def _():
        start_weights(be_ref[0], 0)

    @pl.when((i < n_used) & (first_ref[i] == 1))
    def _():
        slot = wslot_ref[i]
        for cp in weight_copies(0, slot):
            cp.wait()

        @pl.when(nexte_ref[i] >= 0)
        def _():
            start_weights(nexte_ref[i], 1 - slot)

        wg_b[...] = wg_f32[slot].astype(BF16)
        wu_b[...] = wu_f32[slot].astype(BF16)
        wd_b[...] = wd_f32[slot].astype(BF16)

    @pl.when(i < n_used)
    def _():
        xb = _load_token_major(x_ref, bm, d).astype(BF16)
        gate = _dot(xb, wg_b[...])
        up = _dot(xb, wu_b[...])
        hid = (_silu(gate) * up).astype(BF16)
        _store_token_major(y_ref, _dot(hid, wd_b[...]))

    @pl.when(i >= n_used)
    def _():
        y_ref[...] = jnp.zeros(y_ref.shape, F32)


def _expert_call(xs, block_e, n_used, w_gate, w_up, w_down):
    bm = EXPERT_BLOCK
    d, de = w_gate.shape[1], w_gate.shape[2]
    pitch = d // LANES
    n_blocks = xs.shape[0] // (bm * pitch)
    blk = jnp.arange(n_blocks, dtype=jnp.int32)
    used = blk < n_used[0]
    first = jnp.concatenate([jnp.ones((1,), jnp.int32), (block_e[1:] != block_e[:-1]).astype(jnp.int32)])
    first = jnp.where(used, first, 0)
    wslot = (jnp.cumsum(first) - 1) % 2
    nxt_first = jnp.where((first == 1) & (blk > 0), blk, n_blocks)
    nxt_idx = lax.cummin(jnp.concatenate([nxt_first[1:], jnp.full((1,), n_blocks, jnp.int32)]), reverse=True)
    next_e = jnp.where(nxt_idx < n_blocks, block_e[jnp.minimum(nxt_idx, n_blocks - 1)], -1).astype(jnp.int32)
    grid_spec = pltpu.PrefetchScalarGridSpec(
        num_scalar_prefetch=5,
        grid=(n_blocks,),
        in_specs=[pl.BlockSpec((bm * pitch, LANES), lambda i, nu, *_: (jnp.minimum(i, nu[0] - 1), 0)),
                  pl.BlockSpec(memory_space=pl.ANY),
                  pl.BlockSpec(memory_space=pl.ANY),
                  pl.BlockSpec(memory_space=pl.ANY)],
        out_specs=pl.BlockSpec((bm * pitch, LANES), lambda i, *_: (i, 0)),
        scratch_shapes=[pltpu.VMEM((2, d, de), F32), pltpu.VMEM((2, d, de), F32), pltpu.VMEM((2, de, d), F32),
                        pltpu.VMEM((d, de), BF16), pltpu.VMEM((d, de), BF16), pltpu.VMEM((de, d), BF16),
                        pltpu.SemaphoreType.DMA((2,))],
    )
    return pl.pallas_call(
        _expert_kernel,
        grid_spec=grid_spec,
        out_shape=jax.ShapeDtypeStruct((n_blocks * bm * pitch, LANES), F32),
        compiler_params=pltpu.CompilerParams(dimension_semantics=("arbitrary",),
                                             vmem_limit_bytes=VMEM_LIMIT),
        name="experts",
    )(n_used, block_e, first, wslot.astype(jnp.int32), next_e, xs, w_gate, w_up, w_down)


def _combine_kernel(pos_cur_ref, pos_nxt_ref, x_ref, info_ref, mod_ref, nw_ref, y_hbm, o_ref, ybuf, sems):
    tt, d = x_ref.shape[1], x_ref.shape[2]
    pitch = d // LANES
    step = pl.program_id(0) * pl.num_programs(1) + pl.program_id(1)
    n_steps = pl.num_programs(0) * pl.num_programs(1)
    slot_rows = TOP_K * tt * pitch

    def start_gather(pos_ref, slot):
        def body(r, carry):
            for k in range(TOP_K):
                p = pos_ref[0, 0, k * tt + r]
                pltpu.make_async_copy(y_hbm.at[_tile_rows(p, pitch), :],
                                      ybuf.at[_tile_rows((slot * TOP_K + k) * tt + r, pitch), :],
                                      sems.at[slot]).start(priority=k)
            return carry
        lax.fori_loop(0, tt, body, 0, unroll=4)

    @pl.when(step == 0)
    def _():
        start_gather(pos_cur_ref, 0)

    @pl.when(step + 1 < n_steps)
    def _():
        start_gather(pos_nxt_ref, (step + 1) % 2)

    info = info_ref[0]
    w1 = info[:, 2:3]
    w2 = info[:, 3:4]
    x = x_ref[0]
    gate2 = mod_ref[0][5:6]
    base = pl.multiple_of((step % 2) * slot_rows, slot_rows)
    pltpu.make_async_copy(y_hbm.at[pl.ds(0, slot_rows), :], ybuf.at[pl.ds(base, slot_rows), :],
                          sems.at[step % 2]).wait()
    moe = (_load_token_major(ybuf, tt, d, base) * w1
           + _load_token_major(ybuf, tt, d, base + tt * pitch) * w2)
    y = moe * lax.rsqrt(jnp.mean(moe * moe, axis=-1, keepdims=True) + EPS) * nw_ref[...]
    o_ref[0] = x + gate2 * y


def _combine_call(x1, info, mod3, norm_w, y, pos3):
    bsz, seq, d = x1.shape
    tt = COMBINE_TILE
    nj = seq // tt
    n_tiles = bsz * nj
    pos_blk = lambda f: pl.BlockSpec((1, 1, TOP_K * tt), f, memory_space=pltpu.SMEM)
    return pl.pallas_call(
        _combine_kernel,
        grid=(bsz, nj),
        in_specs=[pos_blk(lambda b, j: (b * nj + j, 0, 0)),
                  pos_blk(lambda b, j: (jnp.minimum(b * nj + j + 1, n_tiles - 1), 0, 0)),
                  pl.BlockSpec((1, tt, d), lambda b, j: (b, j, 0)),
                  pl.BlockSpec((1, tt, LANES), lambda b, j: (b, j, 0)),
                  pl.BlockSpec((1,) + mod3.shape[1:], lambda b, j: (b, 0, 0)),
                  pl.BlockSpec((1, d), lambda b, j: (0, 0)),
                  pl.BlockSpec(memory_space=pl.ANY)],
        out_specs=pl.BlockSpec((1, tt, d), lambda b, j: (b, j, 0)),
        out_shape=jax.ShapeDtypeStruct((bsz, seq, d), F32),
        scratch_shapes=[pltpu.VMEM((2 * TOP_K * tt * (d // LANES), LANES), F32), pltpu.SemaphoreType.DMA((2,))],
        compiler_params=pltpu.CompilerParams(dimension_semantics=("arbitrary", "arbitrary"),
                                             vmem_limit_bytes=VMEM_LIMIT),
        name="combine",
    )(pos3, pos3, x1, info, mod3, norm_w.reshape(1, d), y)


def _layer(x, mod, norm_pre_mix, norm_post_mix, w_in, dn_conv_w, dn_a_log, dn_dt_bias, dn_norm_w,
           cf_pw1_b, cf_dw_w, cf_dw_b, cf_ln_w, cf_ln_b, w_out, norm_pre_ffn, norm_post_ffn,
           w_router_group, b_router_group, w_router_expert, b_router_expert, w_gate, w_up, w_down):
    bsz, seq, d = x.shape
    t = bsz * seq
    mod3 = mod.reshape(bsz, -1, d)
    x1 = _mixer_call(x, mod3, norm_pre_mix, norm_post_mix, w_in, dn_conv_w, dn_a_log, dn_dt_bias, dn_norm_w,
                     cf_pw1_b, cf_dw_w, cf_dw_b, cf_ln_w, cf_ln_b, w_out)
    h2, info, cnt = _router_call(x1, mod3, norm_pre_ffn, w_router_group, b_router_group,
                                 w_router_expert, b_router_expert)

    bm = EXPERT_BLOCK
    info2 = info.reshape(t, LANES)
    expert_id = info2[:, 0:TOP_K].astype(jnp.int32)
    rank = info2[:, 4:4 + TOP_K].astype(jnp.int32)
    counts = cnt[0, :N_EXPERTS].astype(jnp.int32)
    padded = (counts + bm - 1) // bm * bm
    pend = jnp.cumsum(padded)
    pstart = pend - padded
    pos = pstart[expert_id] + rank
    tt = COMBINE_TILE
    pos3 = pos.reshape(t // tt, tt, TOP_K).transpose(0, 2, 1).reshape(t // tt, 1, TOP_K * tt)
    n_blocks = -(-(t * TOP_K) // bm) + N_EXPERTS
    block_start = jnp.arange(n_blocks, dtype=jnp.int32) * bm
    block_e = jnp.minimum(jnp.sum(pend[None, :] <= block_start[:, None], axis=1), N_EXPERTS - 1).astype(jnp.int32)
    n_used = (pend[-1] // bm).astype(jnp.int32).reshape(1)

    xs = _dispatch_call(h2, pos3, n_used, pend.astype(jnp.int32), padded, n_blocks * bm, d)
    y = _expert_call(xs, block_e, n_used, w_gate, w_up, w_down)
    return _combine_call(x1, info, mod3, norm_post_ffn, y, pos3)


def kernel(x, c, w_ada, b_ada, norm_pre_mix, norm_post_mix, w_in, dn_conv_w, dn_a_log, dn_dt_bias, dn_norm_w,
           cf_pw1_b, cf_dw_w, cf_dw_b, cf_ln_w, cf_ln_b, w_out, norm_pre_ffn, norm_post_ffn,
           w_router_group, b_router_group, w_router_expert, b_router_expert, w_gate, w_up, w_down):
    depth = w_ada.shape[0]
    for l in range(depth):
        mod = _ada_call(c, w_ada[l], b_ada[l])
        x = _layer(x, mod, norm_pre_mix[l], norm_post_mix[l], w_in[l], dn_conv_w[l], dn_a_log[l],
                   dn_dt_bias[l], dn_norm_w[l], cf_pw1_b[l], cf_dw_w[l], cf_dw_b[l], cf_ln_w[l], cf_ln_b[l],
                   w_out[l], norm_pre_ffn[l], norm_post_ffn[l], w_router_group[l], b_router_group[l],
                   w_router_expert[l], b_router_expert[l], w_gate[l], w_up[l], w_down[l])
    return x
```

```python
import functools

import jax
import jax.numpy as jnp
from jax import lax
from jax.experimental import pallas as pl
from jax.experimental.pallas import tpu as pltpu

F32 = jnp.float32
BF16 = jnp.bfloat16
EPS = 1e-6

DN_HEADS = 4
HEAD_DIM = 128
DN_WIDTH = DN_HEADS * HEAD_DIM
DN_CONV = 4
DN_CHUNK = 64
CF_KERNEL = 31
N_GROUPS = 8
EXPERTS_PER_GROUP = 8
N_EXPERTS = N_GROUPS * EXPERTS_PER_GROUP
TOP_K = 2

LANES = 128
SUBLANES = 8
SEQ_TILE = 256
CONV_ROWS = 32
ROUTER_TILE = 512
EXPERT_BLOCK = 256
COMBINE_TILE = 256
VMEM_LIMIT = 56 * 1024 * 1024


def _dot(a, b):
    return jnp.dot(a, b, preferred_element_type=F32)


def _dot_nt(a, b):
    return lax.dot_general(a, b, (((1,), (1,)), ((), ())), preferred_element_type=F32)


def _dot_tn(a, b):
    return lax.dot_general(a, b, (((0,), (0,)), ((), ())), preferred_element_type=F32)


def _split3(x):
    hi = x.astype(BF16)
    r1 = x - hi.astype(F32)
    mid = r1.astype(BF16)
    lo = (r1 - mid.astype(F32)).astype(BF16)
    return hi, mid, lo


def _silu(x):
    return x * jax.nn.sigmoid(x)


def _softplus(x):
    return jnp.maximum(x, 0.0) + jnp.log1p(jnp.exp(-jnp.abs(x)))


def _store_token_major(ref, val, base=0):
    n, d = val.shape
    pitch = d // LANES
    for j in range(pitch):
        ref[pl.ds(base + j, n, stride=pitch), :] = val[:, j * LANES:(j + 1) * LANES]


def _load_token_major(ref, n, d, base=0):
    pitch = d // LANES
    return jnp.concatenate([ref[pl.ds(base + j, n, stride=pitch), :] for j in range(pitch)], axis=1)


def _ada_kernel(c_ref, w_ref, b_ref, o_ref):
    c = c_ref[...]
    ca = _silu(c)
    c_hi, c_mid, c_lo = _split3(ca)
    w_hi, w_mid, w_lo = _split3(w_ref[...])
    acc = _dot(c_hi, w_hi)
    acc += _dot(c_hi, w_mid) + _dot(c_mid, w_hi)
    acc += _dot(c_hi, w_lo) + _dot(c_mid, w_mid) + _dot(c_lo, w_hi)
    o_ref[...] = acc + b_ref[...]


def _ada_call(c, w, b):
    bsz, d = c.shape
    n = w.shape[1]
    tn = 512
    return pl.pallas_call(
        _ada_kernel,
        grid=(n // tn,),
        in_specs=[pl.BlockSpec((bsz, d), lambda i: (0, 0)),
                  pl.BlockSpec((d, tn), lambda i: (0, i)),
                  pl.BlockSpec((1, tn), lambda i: (0, i))],
        out_specs=pl.BlockSpec((bsz, tn), lambda i: (0, i)),
        out_shape=jax.ShapeDtypeStruct((bsz, n), F32),
        compiler_params=pltpu.CompilerParams(dimension_semantics=("arbitrary",),
                                             vmem_limit_bytes=VMEM_LIMIT),
        name="ada",
    )(c, w, b.reshape(1, n))


def _time_perm(ts, transpose):
    ri = lax.broadcasted_iota(jnp.int32, (ts, ts), 0)
    ci = lax.broadcasted_iota(jnp.int32, (ts, ts), 1)
    strided, natural = (ci, ri) if transpose else (ri, ci)
    return jnp.where(natural == (ts // SUBLANES) * (strided % SUBLANES) + strided // SUBLANES, 1.0, 0.0).astype(BF16)


def _fill_conv_window(ext_ref, prev_ref, cur, n_taps):
    ts = cur.shape[0]
    lead = (n_taps - 1) * SUBLANES
    tail = cur[ts - lead:, :]
    sub = lax.broadcasted_iota(jnp.int32, tail.shape, 0) % SUBLANES
    merged = jnp.where(sub == SUBLANES - 1, prev_ref[...], tail)
    for g in range(n_taps - 1):
        rows = slice(g * SUBLANES, (g + 1) * SUBLANES)
        ext_ref[rows, :] = pltpu.roll(merged[rows, :], 1, 0)
    ext_ref[lead:lead + ts, :] = cur
    prev_ref[...] = tail


def _conv_block(ext_ref, w_ref, n_taps, r0, row_blk, c0, col_blk):
    acc = jnp.zeros((row_blk, col_blk), F32)
    for k in range(n_taps):
        lo = r0 + k * SUBLANES
        acc = acc + w_ref[k:k + 1, c0:c0 + col_blk] * ext_ref[lo:lo + row_blk, c0:c0 + col_blk]
    return acc


def _mixer_kernel(x_ref, mod_ref, npre_ref, npost_ref, wqkv_ref, wz_ref, wba_ref, wcf_ref,
                  convw_ref, alog_ref, dtb_ref, dnw_ref, pw1b_ref, dww_ref, dwb_ref, lnw_ref, lnb_ref,
                  wout_ref, o_ref, qkv_ext, qkv_prev, qkv_p16, qkv_act, cf_ext, cf_prev, cf_p16, state, mixed):
    ts = x_ref.shape[1]
    n_chunks = ts // DN_CHUNK

    @pl.when(pl.program_id(1) == 0)
    def _():
        qkv_prev[...] = jnp.zeros(qkv_prev.shape, F32)
        cf_prev[...] = jnp.zeros(cf_prev.shape, F32)
        state[...] = jnp.zeros(state.shape, F32)

    x = x_ref[0]
    mod = mod_ref[0]
    shift1, scale1, gate1 = mod[0:1], mod[1:2], mod[2:3]
    h = x * lax.rsqrt(jnp.mean(x * x, axis=-1, keepdims=True) + EPS)
    h = h * npre_ref[...] * (1.0 + scale1) + shift1
    hb = h.astype(BF16)
    hb_st = _dot(_time_perm(ts, False), hb).astype(BF16)
    to_natural = _time_perm(ts, True)

    _fill_conv_window(qkv_ext, qkv_prev, _dot(hb_st, wqkv_ref[...]), DN_CONV)
    for r0 in range(0, ts, CONV_ROWS):
        for c0 in range(0, qkv_ext.shape[1], DN_WIDTH):
            blk = _conv_block(qkv_ext, convw_ref, DN_CONV, r0, CONV_ROWS, c0, DN_WIDTH)
            qkv_p16[r0:r0 + CONV_ROWS, c0:c0 + DN_WIDTH] = _silu(blk).astype(BF16)
    qkv_act[...] = _dot(to_natural, qkv_p16[...])

    cf_pre = _dot(hb_st, wcf_ref[...]) + pw1b_ref[...]
    cfw = cf_ext.shape[1]
    _fill_conv_window(cf_ext, cf_prev, cf_pre[:, :cfw] * jax.nn.sigmoid(cf_pre[:, cfw:]), CF_KERNEL)
    cf_pending = list(range(0, ts, CONV_ROWS))

    def cf_step(n=1):
        for _ in range(min(n, len(cf_pending))):
            r0 = cf_pending.pop(0)
            cf = _conv_block(cf_ext, dww_ref, CF_KERNEL, r0, CONV_ROWS, 0, cfw) + dwb_ref[...]
            mu = jnp.mean(cf, axis=-1, keepdims=True)
            xc = cf - mu
            var = jnp.mean(xc * xc, axis=-1, keepdims=True)
            cfn = xc * lax.rsqrt(var + EPS) * lnw_ref[...] + lnb_ref[...]
            cf_p16[r0:r0 + CONV_ROWS, :] = _silu(cfn).astype(BF16)

    z = _dot(hb, wz_ref[...])
    ba = _dot(hb, wba_ref[...])
    beta_all = jax.nn.sigmoid(ba)
    g_all = -jnp.exp(alog_ref[...]) * _softplus(ba + dtb_ref[...])

    ri = lax.broadcasted_iota(jnp.int32, (ts, ts), 0)
    ci = lax.broadcasted_iota(jnp.int32, (ts, ts), 1)
    tri = jnp.where((ri // DN_CHUNK == ci // DN_CHUNK) & (ci <= ri), 1.0, 0.0).astype(BF16)
    g_hi, g_mid, g_lo = _split3(g_all)
    gcum = _dot(tri, g_hi) + _dot(tri, g_mid) + _dot(tri, g_lo)
    gcum_t = gcum.T
    exp_g = jnp.exp(gcum)

    r64 = lax.broadcasted_iota(jnp.int32, (DN_CHUNK, DN_CHUNK), 0)
    c64 = lax.broadcasted_iota(jnp.int32, (DN_CHUNK, DN_CHUNK), 1)
    causal = c64 <= r64
    strict = c64 < r64
    eye = jnp.where(c64 == r64, 1.0, 0.0).astype(F32)
    dnw = dnw_ref[...]

    heads = []
    for hd in range(DN_HEADS):
        lo = hd * HEAD_DIM
        qh = qkv_act[:, lo:lo + HEAD_DIM]
        kh = qkv_act[:, DN_WIDTH + lo:DN_WIDTH + lo + HEAD_DIM]
        vh = qkv_act[:, 2 * DN_WIDTH + lo:2 * DN_WIDTH + lo + HEAD_DIM]
        qn = qh * lax.rsqrt(jnp.sum(qh * qh, axis=-1, keepdims=True) + EPS) * (HEAD_DIM ** -0.5)
        kn = kh * lax.rsqrt(jnp.sum(kh * kh, axis=-1, keepdims=True) + EPS)
        beta_h = beta_all[:, hd:hd + 1]
        gc_h = gcum[:, DN_HEADS + hd:DN_HEADS + hd + 1]
        eg_h = exp_g[:, DN_HEADS + hd:DN_HEADS + hd + 1]
        k_beta = kn * beta_h
        heads.append(dict(qn=qn, kn=kn, k_beta=k_beta, v_beta=vh * beta_h, kbg=k_beta * eg_h, qg=qn * eg_h,
                          gc=gc_h, zg=_silu(z[:, lo:lo + HEAD_DIM])))

    cells = [(hd, ch) for hd in range(DN_HEADS) for ch in range(n_chunks)]
    rows = lambda ch: slice(ch * DN_CHUNK, (ch + 1) * DN_CHUNK)

    decay, kq = {}, {}
    for hd, ch in cells:
        hv, sl = heads[hd], rows(ch)
        gc_row = gcum_t[DN_HEADS + hd:DN_HEADS + hd + 1, sl]
        decay[hd, ch] = jnp.where(causal, jnp.exp(hv["gc"][sl] - gc_row), 0.0)
        lhs = jnp.concatenate([hv["k_beta"][sl], hv["qn"][sl]], axis=0).astype(BF16)
        kq[hd, ch] = _dot_nt(lhs, hv["kn"][sl].astype(BF16))
    a = {c: jnp.where(strict, kq[c][:DN_CHUNK] * decay[c], 0.0) for c in cells}
    attn = {c: (kq[c][DN_CHUNK:] * decay[c]).astype(BF16) for c in cells}

    t_inv = {c: eye - a[c] for c in cells}
    pw = {c: a[c].astype(BF16) for c in cells}
    for _ in range(5):
        pw = {c: _dot(pw[c], pw[c]).astype(BF16) for c in cells}
        t_inv = {c: t_inv[c] + _dot(t_inv[c].astype(BF16), pw[c]) for c in cells}
        cf_step()

    sol, aw, ks, glast = {}, {}, {}, {}
    for hd, ch in cells:
        hv, sl = heads[hd], rows(ch)
        rhs = jnp.concatenate([hv["v_beta"][sl], hv["kbg"][sl]], axis=1).astype(BF16)
        sol[hd, ch] = _dot(t_inv[hd, ch].astype(BF16), rhs).astype(BF16)
    for hd, ch in cells:
        hv, sl = heads[hd], rows(ch)
        gc_col = hv["gc"][sl]
        glast[hd, ch] = gc_col[DN_CHUNK - 1:DN_CHUNK]
        k_dec = (hv["kn"][sl] * jnp.exp(glast[hd, ch] - gc_col)).astype(BF16)
        aw[hd, ch] = _dot(attn[hd, ch], sol[hd, ch])
        ks[hd, ch] = _dot_tn(k_dec, sol[hd, ch])

    s_in = {}
    s_cur = [state[hd] for hd in range(DN_HEADS)]
    for ch in range(n_chunks):
        for hd in range(DN_HEADS):
            s_in[hd, ch] = s_cur[hd].astype(BF16)
            kd_u, kd_w = ks[hd, ch][:, :HEAD_DIM], ks[hd, ch][:, HEAD_DIM:]
            s_cur[hd] = s_cur[hd] * jnp.exp(glast[hd, ch]) + kd_u - _dot(kd_w.astype(BF16), s_in[hd, ch])
        cf_step()
    for hd in range(DN_HEADS):
        state[hd] = s_cur[hd]

    for hd, ch in cells:
        hv, sl = heads[hd], rows(ch)
        lo = hd * HEAD_DIM
        q_eff = (hv["qg"][sl] - aw[hd, ch][:, HEAD_DIM:]).astype(BF16)
        o = _dot(q_eff, s_in[hd, ch]) + aw[hd, ch][:, :HEAD_DIM]
        on = o * lax.rsqrt(jnp.mean(o * o, axis=-1, keepdims=True) + EPS) * dnw * hv["zg"][sl]
        mixed[sl, lo:lo + HEAD_DIM] = on.astype(BF16)

    cf_step(len(cf_pending))
    mixed[:, DN_WIDTH:DN_WIDTH + cfw] = _dot(to_natural, cf_p16[...]).astype(BF16)

    out = _dot(mixed[...], wout_ref[...])
    y = out * lax.rsqrt(jnp.mean(out * out, axis=-1, keepdims=True) + EPS) * npost_ref[...]
    o_ref[0] = x + gate1 * y


def _mixer_call(x, mod3, npre, npost, w_in, dn_conv_w, dn_a_log, dn_dt_bias, dn_norm_w,
                cf_pw1_b, cf_dw_w, cf_dw_b, cf_ln_w, cf_ln_b, w_out):
    bsz, seq, d = x.shape
    ts = SEQ_TILE
    cfw = cf_dw_w.shape[1]
    n_qkv = 3 * DN_WIDTH
    wqkv = w_in[:, :n_qkv].astype(BF16)
    wz = w_in[:, n_qkv:n_qkv + DN_WIDTH].astype(BF16)
    wba = jnp.pad(w_in[:, n_qkv + DN_WIDTH:n_qkv + DN_WIDTH + 2 * DN_HEADS],
                  ((0, 0), (0, LANES - 2 * DN_HEADS))).astype(BF16)
    wcf = w_in[:, n_qkv + DN_WIDTH + 2 * DN_HEADS:].astype(BF16)
    alog = jnp.pad(dn_a_log, (DN_HEADS, LANES - 2 * DN_HEADS)).reshape(1, LANES)
    dtb = jnp.pad(dn_dt_bias, (DN_HEADS, LANES - 2 * DN_HEADS)).reshape(1, LANES)
    dww = jnp.pad(cf_dw_w, ((0, 32 - CF_KERNEL), (0, 0)))
    convw = jnp.pad(dn_conv_w, ((0, 8 - DN_CONV), (0, 0)))

    def full(a):
        return pl.BlockSpec(a.shape, lambda b, j: (0,) * a.ndim)

    row = lambda a: a.reshape(1, -1)
    operands = [x, mod3, row(npre), row(npost), wqkv, wz, wba, wcf, convw, alog, dtb, row(dn_norm_w),
                row(cf_pw1_b), dww, row(cf_dw_b), row(cf_ln_w), row(cf_ln_b), w_out.astype(BF16)]
    in_specs = [pl.BlockSpec((1, ts, d), lambda b, j: (b, j, 0)),
                pl.BlockSpec((1,) + mod3.shape[1:], lambda b, j: (b, 0, 0))]
    in_specs += [full(a) for a in operands[2:]]
    return pl.pallas_call(
        _mixer_kernel,
        grid=(bsz, seq // ts),
        in_specs=in_specs,
        out_specs=pl.BlockSpec((1, ts, d), lambda b, j: (b, j, 0)),
        out_shape=jax.ShapeDtypeStruct((bsz, seq, d), F32),
        scratch_shapes=[pltpu.VMEM(((DN_CONV - 1) * SUBLANES + ts, n_qkv), F32),
                        pltpu.VMEM(((DN_CONV - 1) * SUBLANES, n_qkv), F32),
                        pltpu.VMEM((ts, n_qkv), BF16),
                        pltpu.VMEM((ts, n_qkv), F32),
                        pltpu.VMEM(((CF_KERNEL - 1) * SUBLANES + ts, cfw), F32),
                        pltpu.VMEM(((CF_KERNEL - 1) * SUBLANES, cfw), F32),
                        pltpu.VMEM((ts, cfw), BF16),
                        pltpu.VMEM((DN_HEADS, HEAD_DIM, HEAD_DIM), F32),
                        pltpu.VMEM((ts, DN_WIDTH + cfw), BF16)],
        compiler_params=pltpu.CompilerParams(dimension_semantics=("arbitrary", "arbitrary"),
                                             vmem_limit_bytes=VMEM_LIMIT),
        name="mixer",
    )(*operands)


def _router_kernel(x_ref, mod_ref, nw_ref, wr_ref, br_ref, h_ref, info_ref, cnt_ref, carry):
    tt = x_ref.shape[1]

    @pl.when((pl.program_id(0) == 0) & (pl.program_id(1) == 0))
    def _():
        carry[...] = jnp.zeros(carry.shape, F32)

    x = x_ref[0]
    mod = mod_ref[0]
    shift2, scale2 = mod[3:4], mod[4:5]
    h = x * lax.rsqrt(jnp.mean(x * x, axis=-1, keepdims=True) + EPS)
    h = h * nw_ref[...] * (1.0 + scale2) + shift2
    _store_token_major(h_ref, h)

    h_hi, h_lo, _ = _split3(h)
    w_hi, w_lo, _ = _split3(wr_ref[...])
    logits = _dot(h_hi, w_hi) + (_dot(h_hi, w_lo) + _dot(h_lo, w_hi)) + br_ref[...]

    lane = lax.broadcasted_iota(jnp.int32, (tt, LANES), 1)
    neg = -jnp.inf
    is_grp = (lane >= N_EXPERTS) & (lane < N_EXPERTS + N_GROUPS)
    gl = jnp.where(is_grp, logits, neg)
    gmax = jnp.max(gl, axis=-1, keepdims=True)
    gsum = jnp.sum(jnp.where(is_grp, jnp.exp(gl - gmax), 0.0), axis=-1, keepdims=True)
    grp_p = 1.0 / gsum
    grp_lane = jnp.min(jnp.where(is_grp & (gl == gmax), lane, LANES), axis=-1, keepdims=True)
    grp_idx = grp_lane - N_EXPERTS

    in_grp = (lane < N_EXPERTS) & (lane // EXPERTS_PER_GROUP == grp_idx)
    el = jnp.where(in_grp, logits, neg)
    m1 = jnp.max(el, axis=-1, keepdims=True)
    e1 = jnp.min(jnp.where(in_grp & (el == m1), lane, LANES), axis=-1, keepdims=True)
    el2 = jnp.where(lane == e1, neg, el)
    m2 = jnp.max(el2, axis=-1, keepdims=True)
    e2 = jnp.min(jnp.where(in_grp & (lane != e1) & (el2 == m2), lane, LANES), axis=-1, keepdims=True)
    r = jnp.exp(m2 - m1)
    w1 = grp_p / (1.0 + r)
    w2 = grp_p * r / (1.0 + r)

    hit1 = lane == e1
    hit2 = lane == e2
    onehot = jnp.where(hit1 | hit2, 1.0, 0.0)
    rr = lax.broadcasted_iota(jnp.int32, (tt, tt), 0)
    cc = lax.broadcasted_iota(jnp.int32, (tt, tt), 1)
    strict = jnp.where(cc < rr, 1.0, 0.0).astype(BF16)
    prefix = _dot(strict, onehot.astype(BF16)) + carry[0:1, :]
    rank1 = jnp.sum(jnp.where(hit1, prefix, 0.0), axis=-1, keepdims=True)
    rank2 = jnp.sum(jnp.where(hit2, prefix, 0.0), axis=-1, keepdims=True)
    total = carry[0:1, :] + jnp.sum(onehot, axis=0, keepdims=True)
    carry[...] = jnp.broadcast_to(total, carry.shape)
    cnt_ref[...] = jnp.broadcast_to(total, cnt_ref.shape)

    info = jnp.where(lane == 0, e1.astype(F32), 0.0)
    info = jnp.where(lane == 1, e2.astype(F32), info)
    info = jnp.where(lane == 2, w1, info)
    info = jnp.where(lane == 3, w2, info)
    info = jnp.where(lane == 4, rank1, info)
    info = jnp.where(lane == 5, rank2, info)
    info_ref[0] = info


def _router_call(x1, mod3, norm_w, w_router_group, b_router_group, w_router_expert, b_router_expert):
    bsz, seq, d = x1.shape
    tt = ROUTER_TILE
    nj = seq // tt
    pitch = d // LANES
    pad = LANES - N_EXPERTS - N_GROUPS
    wr = jnp.pad(jnp.concatenate([w_router_expert, w_router_group], axis=1), ((0, 0), (0, pad)))
    br = jnp.pad(jnp.concatenate([b_router_expert, b_router_group]), (0, pad)).reshape(1, LANES)
    return pl.pallas_call(
        _router_kernel,
        grid=(bsz, seq // tt),
        in_specs=[pl.BlockSpec((1, tt, d), lambda b, j: (b, j, 0)),
                  pl.BlockSpec((1,) + mod3.shape[1:], lambda b, j: (b, 0, 0)),
                  pl.BlockSpec((1, d), lambda b, j: (0, 0)),
                  pl.BlockSpec((d, LANES), lambda b, j: (0, 0)),
                  pl.BlockSpec((1, LANES), lambda b, j: (0, 0))],
        out_specs=[pl.BlockSpec((tt * pitch, LANES), lambda b, j: (b * nj + j, 0)),
                   pl.BlockSpec((1, tt, LANES), lambda b, j: (b, j, 0)),
                   pl.BlockSpec((8, LANES), lambda b, j: (0, 0))],
        out_shape=[jax.ShapeDtypeStruct((bsz * seq * pitch, LANES), F32),
                   jax.ShapeDtypeStruct((bsz, seq, LANES), F32),
                   jax.ShapeDtypeStruct((8, LANES), F32)],
        scratch_shapes=[pltpu.VMEM((8, LANES), F32)],
        compiler_params=pltpu.CompilerParams(dimension_semantics=("arbitrary", "arbitrary"),
                                             vmem_limit_bytes=VMEM_LIMIT),
        name="router",
    )(x1, mod3, norm_w.reshape(1, d), wr, br)


def _tile_rows(idx, pitch):
    return pl.ds(pl.multiple_of(idx * pitch, pitch), pitch)


def _dispatch_kernel(nused_ref, pend_ref, padded_ref, pos_ref, h_ref, xs_hbm, zbuf, zsem, sem, *, pitch, bm):
    step = pl.program_id(0)
    tt = h_ref.shape[0] // pitch

    @pl.when(step == 0)
    def _():
        zbuf[...] = jnp.zeros(zbuf.shape, F32)

        def tail_copy(e):
            start = pl.multiple_of((pend_ref[e] - bm) * pitch, pitch)
            return pltpu.make_async_copy(zbuf, xs_hbm.at[pl.ds(start, bm * pitch), :], zsem.at[0])

        def issue(e, carry):
            @pl.when(padded_ref[e] > 0)
            def _():
                tail_copy(e).start()
            return carry

        def drain(e, carry):
            @pl.when(padded_ref[e] > 0)
            def _():
                tail_copy(e).wait()
            return carry

        def block_copy(b):
            start = pl.multiple_of(b * bm * pitch, bm * pitch)
            return pltpu.make_async_copy(zbuf, xs_hbm.at[pl.ds(start, bm * pitch), :], zsem.at[0])

        def issue_block(b, carry):
            block_copy(b).start()
            return carry

        def drain_block(b, carry):
            block_copy(b).wait()
            return carry

        n_blocks = xs_hbm.shape[0] // (bm * pitch)
        lax.fori_loop(0, N_EXPERTS, issue, 0)
        lax.fori_loop(nused_ref[0], n_blocks, issue_block, 0)
        lax.fori_loop(0, N_EXPERTS, drain, 0)
        lax.fori_loop(nused_ref[0], n_blocks, drain_block, 0)

    def body(r, carry):
        for k in range(TOP_K):
            p = pos_ref[0, 0, k * tt + r]
            pltpu.make_async_copy(h_ref.at[_tile_rows(r, pitch), :], xs_hbm.at[_tile_rows(p, pitch), :],
                                  sem.at[0]).start(priority=k)
        return carry
    lax.fori_loop(0, tt, body, 0, unroll=4)
    for k in range(TOP_K):
        pltpu.make_async_copy(h_ref, xs_hbm.at[pl.ds(0, tt * pitch), :], sem.at[0]).wait()


def _dispatch_call(h2, pos3, n_used, pend, padded, n_rows, d):
    pitch = d // LANES
    bm = EXPERT_BLOCK
    n_tiles, _, two_tt = pos3.shape
    tt = two_tt // TOP_K
    grid_spec = pltpu.PrefetchScalarGridSpec(
        num_scalar_prefetch=3,
        grid=(n_tiles,),
        in_specs=[pl.BlockSpec((1, 1, two_tt), lambda s, *_: (s, 0, 0), memory_space=pltpu.SMEM),
                  pl.BlockSpec((tt * pitch, LANES), lambda s, *_: (s, 0))],
        out_specs=pl.BlockSpec(memory_space=pl.ANY),
        scratch_shapes=[pltpu.VMEM((bm * pitch, LANES), F32), pltpu.SemaphoreType.DMA((1,)),
                        pltpu.SemaphoreType.DMA((1,))],
    )
    return pl.pallas_call(
        functools.partial(_dispatch_kernel, pitch=pitch, bm=bm),
        grid_spec=grid_spec,
        out_shape=jax.ShapeDtypeStruct((n_rows * pitch, LANES), F32),
        compiler_params=pltpu.CompilerParams(dimension_semantics=("arbitrary",),
                                             vmem_limit_bytes=VMEM_LIMIT),
        name="dispatch",
    )(n_used, pend, padded, pos3, h2)


def _expert_kernel(nused_ref, be_ref, first_ref, wslot_ref, nexte_ref, x_ref, wg_hbm, wu_hbm, wd_hbm, y_ref,
                   wg_f32, wu_f32, wd_f32, wg_b, wu_b, wd_b, wsems):
    i = pl.program_id(0)
    n_used = nused_ref[0]
    d = wg_b.shape[0]
    pitch = d // LANES
    bm = y_ref.shape[0] // pitch

    def weight_copies(e, slot):
        return [pltpu.make_async_copy(src.at[e], dst.at[slot], wsems.at[slot])
                for src, dst in ((wg_hbm, wg_f32), (wu_hbm, wu_f32), (wd_hbm, wd_f32))]

    def start_weights(e, slot):
        for prio, cp in zip((1, 1, 0), weight_copies(e, slot)):
            cp.start(priority=prio)

    @pl.when(i == 0)
    def _():
        start_weights(be_ref[0], 0)

    @pl.when((i < n_used) & (first_ref[i] == 1))
    def _():
        slot = wslot_ref[i]
        for cp in weight_copies(0, slot):
            cp.wait()

        @pl.when(nexte_ref[i] >= 0)
        def _():
            start_weights(nexte_ref[i], 1 - slot)

        wg_b[...] = wg_f32[slot].astype(BF16)
        wu_b[...] = wu_f32[slot].astype(BF16)
        wd_b[...] = wd_f32[slot].astype(BF16)

    @pl.when(i < n_used)
    def _():
        xb = _load_token_major(x_ref, bm, d).astype(BF16)
        gate = _dot(xb, wg_b[...])
        up = _dot(xb, wu_b[...])
        hid = (_silu(gate) * up).astype(BF16)
        _store_token_major(y_ref, _dot(hid, wd_b[...]))

    @pl.when(i >= n_used)
    def _():
        y_ref[...] = jnp.zeros(y_ref.shape, F32)


def _expert_call(xs, block_e, n_used, w_gate, w_up, w_down):
    bm = EXPERT_BLOCK
    d, de = w_gate.shape[1], w_gate.shape[2]
    pitch = d // LANES
    n_blocks = xs.shape[0] // (bm * pitch)
    blk = jnp.arange(n_blocks, dtype=jnp.int32)
    used = blk < n_used[0]
    first = jnp.concatenate([jnp.ones((1,), jnp.int32), (block_e[1:] != block_e[:-1]).astype(jnp.int32)])
    first = jnp.where(used, first, 0)
    wslot = (jnp.cumsum(first) - 1) % 2
    nxt_first = jnp.where((first == 1) & (blk > 0), blk, n_blocks)
    nxt_idx = lax.cummin(jnp.concatenate([nxt_first[1:], jnp.full((1,), n_blocks, jnp.int32)]), reverse=True)
    next_e = jnp.where(nxt_idx < n_blocks, block_e[jnp.minimum(nxt_idx, n_blocks - 1)], -1).astype(jnp.int32)
    grid_spec = pltpu.PrefetchScalarGridSpec(
        num_scalar_prefetch=5,
        grid=(n_blocks,),
        in_specs=[pl.BlockSpec((bm * pitch, LANES), lambda i, nu, *_: (jnp.minimum(i, nu[0] - 1), 0)),
                  pl.BlockSpec(memory_space=pl.ANY),
                  pl.BlockSpec(memory_space=pl.ANY),
                  pl.BlockSpec(memory_space=pl.ANY)],
        out_specs=pl.BlockSpec((bm * pitch, LANES), lambda i, *_: (i, 0)),
        scratch_shapes=[pltpu.VMEM((2, d, de), F32), pltpu.VMEM((2, d, de), F32), pltpu.VMEM((2, de, d), F32),
                        pltpu.VMEM((d, de), BF16), pltpu.VMEM((d, de), BF16), pltpu.VMEM((de, d), BF16),
                        pltpu.SemaphoreType.DMA((2,))],
    )
    return pl.pallas_call(
        _expert_kernel,
        grid_spec=grid_spec,
        out_shape=jax.ShapeDtypeStruct((n_blocks * bm * pitch, LANES), F32),
        compiler_params=pltpu.CompilerParams(dimension_semantics=("arbitrary",),
                                             vmem_limit_bytes=VMEM_LIMIT),
        name="experts",
    )(n_used, block_e, first, wslot.astype(jnp.int32), next_e, xs, w_gate, w_up, w_down)


def _combine_kernel(pos_cur_ref, pos_nxt_ref, x_ref, info_ref, mod_ref, nw_ref, y_hbm, o_ref, ybuf, sems):
    tt, d = x_ref.shape[1], x_ref.shape[2]
    pitch = d // LANES
    step = pl.program_id(0) * pl.num_programs(1) + pl.program_id(1)
    n_steps = pl.num_programs(0) * pl.num_programs(1)
    slot_rows = TOP_K * tt * pitch

    def start_gather(pos_ref, slot):
        def body(r, carry):
            for k in range(TOP_K):
                p = pos_ref[0, 0, k * tt + r]
                pltpu.make_async_copy(y_hbm.at[_tile_rows(p, pitch), :],
                                      ybuf.at[_tile_rows((slot * TOP_K + k) * tt + r, pitch), :],
                                      sems.at[slot]).start(priority=k)
            return carry
        lax.fori_loop(0, tt, body, 0, unroll=4)

    @pl.when(step == 0)
    def _():
        start_gather(pos_cur_ref, 0)

    @pl.when(step + 1 < n_steps)
    def _():
        start_gather(pos_nxt_ref, (step + 1) % 2)

    info = info_ref[0]
    w1 = info[:, 2:3]
    w2 = info[:, 3:4]
    x = x_ref[0]
    gate2 = mod_ref[0][5:6]
    base = pl.multiple_of((step % 2) * slot_rows, slot_rows)
    pltpu.make_async_copy(y_hbm.at[pl.ds(0, slot_rows), :], ybuf.at[pl.ds(base, slot_rows), :],
                          sems.at[step % 2]).wait()
    moe = (_load_token_major(ybuf, tt, d, base) * w1
           + _load_token_major(ybuf, tt, d, base + tt * pitch) * w2)
    y = moe * lax.rsqrt(jnp.mean(moe * moe, axis=-1, keepdims=True) + EPS) * nw_ref[...]
    o_ref[0] = x + gate2 * y


def _combine_call(x1, info, mod3, norm_w, y, pos3):
    bsz, seq, d = x1.shape
    tt = COMBINE_TILE
    nj = seq // tt
    n_tiles = bsz * nj
    pos_blk = lambda f: pl.BlockSpec((1, 1, TOP_K * tt), f, memory_space=pltpu.SMEM)
    return pl.pallas_call(
        _combine_kernel,
        grid=(bsz, nj),
        in_specs=[pos_blk(lambda b, j: (b * nj + j, 0, 0)),
                  pos_blk(lambda b, j: (jnp.minimum(b * nj + j + 1, n_tiles - 1), 0, 0)),
                  pl.BlockSpec((1, tt, d), lambda b, j: (b, j, 0)),
                  pl.BlockSpec((1, tt, LANES), lambda b, j: (b, j, 0)),
                  pl.BlockSpec((1,) + mod3.shape[1:], lambda b, j: (b, 0, 0)),
                  pl.BlockSpec((1, d), lambda b, j: (0, 0)),
                  pl.BlockSpec(memory_space=pl.ANY)],
        out_specs=pl.BlockSpec((1, tt, d), lambda b, j: (b, j, 0)),
        out_shape=jax.ShapeDtypeStruct((bsz, seq, d), F32),
        scratch_shapes=[pltpu.VMEM((2 * TOP_K * tt * (d // LANES), LANES), F32), pltpu.SemaphoreType.DMA((2,))],
        compiler_params=pltpu.CompilerParams(dimension_semantics=("arbitrary", "arbitrary"),
                                             vmem_limit_bytes=VMEM_LIMIT),
        name="combine",
    )(pos3, pos3, x1, info, mod3, norm_w.reshape(1, d), y)


def _layer(x, mod, norm_pre_mix, norm_post_mix, w_in, dn_conv_w, dn_a_log, dn_dt_bias, dn_norm_w,
           cf_pw1_b, cf_dw_w, cf_dw_b, cf_ln_w, cf_ln_b, w_out, norm_pre_ffn, norm_post_ffn,
           w_router_group, b_router_group, w_router_expert, b_router_expert, w_gate, w_up, w_down):
    bsz, seq, d = x.shape
    t = bsz * seq
    mod3 = mod.reshape(bsz, -1, d)
    x1 = _mixer_call(x, mod3, norm_pre_mix, norm_post_mix, w_in, dn_conv_w, dn_a_log, dn_dt_bias, dn_norm_w,
                     cf_pw1_b, cf_dw_w, cf_dw_b, cf_ln_w, cf_ln_b, w_out)
    h2, info, cnt = _router_call(x1, mod3, norm_pre_ffn, w_router_group, b_router_group,
                                 w_router_expert, b_router_expert)

    bm = EXPERT_BLOCK
    info2 = info.reshape(t, LANES)
    expert_id = info2[:, 0:TOP_K].astype(jnp.int32)
    rank = info2[:, 4:4 + TOP_K].astype(jnp.int32)
    counts = cnt[0, :N_EXPERTS].astype(jnp.int32)
    padded = (counts + bm - 1) // bm * bm
    pend = jnp.cumsum(padded)
    pstart = pend - padded
    pos = pstart[expert_id] + rank
    tt = COMBINE_TILE
    pos3 = pos.reshape(t // tt, tt, TOP_K).transpose(0, 2, 1).reshape(t // tt, 1, TOP_K * tt)
    n_blocks = -(-(t * TOP_K) // bm) + N_EXPERTS
    block_start = jnp.arange(n_blocks, dtype=jnp.int32) * bm
    block_e = jnp.minimum(jnp.sum(pend[None, :] <= block_start[:, None], axis=1), N_EXPERTS - 1).astype(jnp.int32)
    n_used = (pend[-1] // bm).astype(jnp.int32).reshape(1)

    xs = _dispatch_call(h2, pos3, n_used, pend.astype(jnp.int32), padded, n_blocks * bm, d)
    y = _expert_call(xs, block_e, n_used, w_gate, w_up, w_down)
    return _combine_call(x1, info, mod3, norm_post_ffn, y, pos3)


def kernel(x, c, w_ada, b_ada, norm_pre_mix, norm_post_mix, w_in, dn_conv_w, dn_a_log, dn_dt_bias, dn_norm_w,
           cf_pw1_b, cf_dw_w, cf_dw_b, cf_ln_w, cf_ln_b, w_out, norm_pre_ffn, norm_post_ffn,
           w_router_group, b_router_group, w_router_expert, b_router_expert, w_gate, w_up, w_down):
    depth = w_ada.shape[0]
    for l in range(depth):
        mod = _ada_call(c, w_ada[l], b_ada[l])
        x = _layer(x, mod, norm_pre_mix[l], norm_post_mix[l], w_in[l], dn_conv_w[l], dn_a_log[l],
                   dn_dt_bias[l], dn_norm_w[l], cf_pw1_b[l], cf_dw_w[l], cf_dw_b[l], cf_ln_w[l], cf_ln_b[l],
                   w_out[l], norm_pre_ffn[l], norm_post_ffn[l], w_router_group[l], b_router_group[l],
                   w_router_expert[l], b_router_expert[l], w_gate[l], w_up[l], w_down[l])
    return x
```

```python
import functools

import jax
import jax.numpy as jnp
from jax import lax
from jax.experimental import pallas as pl
from jax.experimental.pallas import tpu as pltpu

F32 = jnp.float32
BF16 = jnp.bfloat16
EPS = 1e-6

DN_HEADS = 4
HEAD_DIM = 128
DN_WIDTH = DN_HEADS * HEAD_DIM
DN_CONV = 4
DN_CHUNK = 64
CF_KERNEL = 31
N_GROUPS = 8
EXPERTS_PER_GROUP = 8
N_EXPERTS = N_GROUPS * EXPERTS_PER_GROUP
TOP_K = 2

LANES = 128
SUBLANES = 8
SEQ_TILE = 256
CONV_ROWS = 32
ROUTER_TILE = 512
EXPERT_BLOCK = 256
COMBINE_TILE = 256
WEIGHT_SLOTS = 3
DISPATCH_TILE = 1024
VMEM_LIMIT = 56 * 1024 * 1024


def _dot(a, b):
    return jnp.dot(a, b, preferred_element_type=F32)


def _dot_nt(a, b):
    return lax.dot_general(a, b, (((1,), (1,)), ((), ())), preferred_element_type=F32)


def _dot_tn(a, b):
    return lax.dot_general(a, b, (((0,), (0,)), ((), ())), preferred_element_type=F32)


def _split3(x):
    hi = x.astype(BF16)
    r1 = x - hi.astype(F32)
    mid = r1.astype(BF16)
    lo = (r1 - mid.astype(F32)).astype(BF16)
    return hi, mid, lo


def _silu(x):
    return x * jax.nn.sigmoid(x)


def _softplus(x):
    return jnp.maximum(x, 0.0) + jnp.log1p(jnp.exp(-jnp.abs(x)))


def _store_token_major(ref, val, base=0):
    n, d = val.shape
    pitch = d // LANES
    for j in range(pitch):
        ref[pl.ds(base + j, n, stride=pitch), :] = val[:, j * LANES:(j + 1) * LANES]


def _load_token_major(ref, n, d, base=0):
    pitch = d // LANES
    return jnp.concatenate([ref[pl.ds(base + j, n, stride=pitch), :] for j in range(pitch)], axis=1)


def _ada_kernel(c_ref, w_ref, b_ref, o_ref):
    c = c_ref[...]
    ca = _silu(c)
    c_hi, c_mid, c_lo = _split3(ca)
    w_hi, w_mid, w_lo = _split3(w_ref[...])
    acc = _dot(c_hi, w_hi)
    acc += _dot(c_hi, w_mid) + _dot(c_mid, w_hi)
    acc += _dot(c_hi, w_lo) + _dot(c_mid, w_mid) + _dot(c_lo, w_hi)
    o_ref[...] = acc + b_ref[...]


def _ada_call(c, w, b):
    bsz, d = c.shape
    n = w.shape[1]
    tn = 512
    return pl.pallas_call(
        _ada_kernel,
        grid=(n // tn,),
        in_specs=[pl.BlockSpec((bsz, d), lambda i: (0, 0)),
                  pl.BlockSpec((d, tn), lambda i: (0, i)),
                  pl.BlockSpec((1, tn), lambda i: (0, i))],
        out_specs=pl.BlockSpec((bsz, tn), lambda i: (0, i)),
        out_shape=jax.ShapeDtypeStruct((bsz, n), F32),
        compiler_params=pltpu.CompilerParams(dimension_semantics=("arbitrary",),
                                             vmem_limit_bytes=VMEM_LIMIT),
        name="ada",
    )(c, w, b.reshape(1, n))


def _time_perm(ts, transpose):
    ri = lax.broadcasted_iota(jnp.int32, (ts, ts), 0)
    ci = lax.broadcasted_iota(jnp.int32, (ts, ts), 1)
    strided, natural = (ci, ri) if transpose else (ri, ci)
    return jnp.where(natural == (ts // SUBLANES) * (strided % SUBLANES) + strided // SUBLANES, 1.0, 0.0).astype(BF16)


def _fill_conv_window(ext_ref, prev_ref, cur, n_taps):
    ts = cur.shape[0]
    lead = (n_taps - 1) * SUBLANES
    tail = cur[ts - lead:, :]
    sub = lax.broadcasted_iota(jnp.int32, tail.shape, 0) % SUBLANES
    merged = jnp.where(sub == SUBLANES - 1, prev_ref[...], tail)
    for g in range(n_taps - 1):
        rows = slice(g * SUBLANES, (g + 1) * SUBLANES)
        ext_ref[rows, :] = pltpu.roll(merged[rows, :], 1, 0)
    ext_ref[lead:lead + ts, :] = cur
    prev_ref[...] = tail


def _conv_block(ext_ref, w_ref, n_taps, r0, row_blk, c0, col_blk):
    groups = row_blk // SUBLANES
    acc = [jnp.zeros((SUBLANES, col_blk), F32) for _ in range(groups)]
    for k in range(n_taps):
        w = w_ref[k * SUBLANES:(k + 1) * SUBLANES, c0:c0 + col_blk]
        for g in range(groups):
            lo = r0 + (k + g) * SUBLANES
            acc[g] = acc[g] + w * ext_ref[lo:lo + SUBLANES, c0:c0 + col_blk]
    return jnp.concatenate(acc, axis=0)


def _mixer_kernel(x_ref, mod_ref, npre_ref, npost_ref, wqkv_ref, wz_ref, wba_ref, wcf_ref,
                  convw_ref, alog_ref, dtb_ref, dnw_ref, pw1b_ref, dww_ref, dwb_ref, lnw_ref, lnb_ref,
                  wout_ref, o_ref, qkv_ext, qkv_prev, qkv_p16, qkv_act, cf_ext, cf_prev, cf_p16, state, mixed):
    ts = x_ref.shape[1]
    n_chunks = ts // DN_CHUNK

    @pl.when(pl.program_id(1) == 0)
    def _():
        qkv_prev[...] = jnp.zeros(qkv_prev.shape, F32)
        cf_prev[...] = jnp.zeros(cf_prev.shape, F32)
        state[...] = jnp.zeros(state.shape, F32)

    x = x_ref[0]
    mod = mod_ref[0]
    shift1, scale1, gate1 = mod[0:1], mod[1:2], mod[2:3]
    h = x * lax.rsqrt(jnp.mean(x * x, axis=-1, keepdims=True) + EPS)
    h = h * npre_ref[...] * (1.0 + scale1) + shift1
    hb = h.astype(BF16)
    hb_st = _dot(_time_perm(ts, False), hb).astype(BF16)
    to_natural = _time_perm(ts, True)

    _fill_conv_window(qkv_ext, qkv_prev, _dot(hb_st, wqkv_ref[...]), DN_CONV)
    for r0 in range(0, ts, CONV_ROWS):
        for c0 in range(0, qkv_ext.shape[1], DN_WIDTH):
            blk = _conv_block(qkv_ext, convw_ref, DN_CONV, r0, CONV_ROWS, c0, DN_WIDTH)
            qkv_p16[r0:r0 + CONV_ROWS, c0:c0 + DN_WIDTH] = _silu(blk).astype(BF16)
    qkv_act[...] = _dot(to_natural, qkv_p16[...])

    cf_pre = _dot(hb_st, wcf_ref[...]) + pw1b_ref[...]
    cfw = cf_ext.shape[1]
    _fill_conv_window(cf_ext, cf_prev, cf_pre[:, :cfw] * jax.nn.sigmoid(cf_pre[:, cfw:]), CF_KERNEL)
    cf_pending = list(range(0, ts, CONV_ROWS))

    def cf_step(n=1):
        for _ in range(min(n, len(cf_pending))):
            r0 = cf_pending.pop(0)
            cf = _conv_block(cf_ext, dww_ref, CF_KERNEL, r0, CONV_ROWS, 0, cfw) + dwb_ref[...]
            mu = jnp.mean(cf, axis=-1, keepdims=True)
            xc = cf - mu
            var = jnp.mean(xc * xc, axis=-1, keepdims=True)
            cfn = xc * lax.rsqrt(var + EPS) * lnw_ref[...] + lnb_ref[...]
            cf_p16[r0:r0 + CONV_ROWS, :] = _silu(cfn).astype(BF16)

    z = _dot(hb, wz_ref[...])
    ba = _dot(hb, wba_ref[...])
    beta_all = jax.nn.sigmoid(ba)
    g_all = -jnp.exp(alog_ref[...]) * _softplus(ba + dtb_ref[...])

    ri = lax.broadcasted_iota(jnp.int32, (ts, ts), 0)
    ci = lax.broadcasted_iota(jnp.int32, (ts, ts), 1)
    tri = jnp.where((ri // DN_CHUNK == ci // DN_CHUNK) & (ci <= ri), 1.0, 0.0).astype(BF16)
    g_hi, g_mid, g_lo = _split3(g_all)
    gcum = _dot(tri, g_hi) + _dot(tri, g_mid) + _dot(tri, g_lo)
    gcum_t = gcum.T
    exp_g = jnp.exp(gcum)

    r64 = lax.broadcasted_iota(jnp.int32, (DN_CHUNK, DN_CHUNK), 0)
    c64 = lax.broadcasted_iota(jnp.int32, (DN_CHUNK, DN_CHUNK), 1)
    causal = c64 <= r64
    strict = c64 < r64
    eye = jnp.where(c64 == r64, 1.0, 0.0).astype(F32)
    dnw = dnw_ref[...]

    heads = []
    for hd in range(DN_HEADS):
        lo = hd * HEAD_DIM
        qh = qkv_act[:, lo:lo + HEAD_DIM]
        kh = qkv_act[:, DN_WIDTH + lo:DN_WIDTH + lo + HEAD_DIM]
        vh = qkv_act[:, 2 * DN_WIDTH + lo:2 * DN_WIDTH + lo + HEAD_DIM]
        qn = qh * lax.rsqrt(jnp.sum(qh * qh, axis=-1, keepdims=True) + EPS) * (HEAD_DIM ** -0.5)
        kn = kh * lax.rsqrt(jnp.sum(kh * kh, axis=-1, keepdims=True) + EPS)
        beta_h = beta_all[:, hd:hd + 1]
        gc_h = gcum[:, DN_HEADS + hd:DN_HEADS + hd + 1]
        eg_h = exp_g[:, DN_HEADS + hd:DN_HEADS + hd + 1]
        k_beta = kn * beta_h
        heads.append(dict(qn=qn, kn=kn, k_beta=k_beta, v_beta=vh * beta_h, kbg=k_beta * eg_h, qg=qn * eg_h,
                          gc=gc_h, zg=_silu(z[:, lo:lo + HEAD_DIM])))

    cells = [(hd, ch) for hd in range(DN_HEADS) for ch in range(n_chunks)]
    rows = lambda ch: slice(ch * DN_CHUNK, (ch + 1) * DN_CHUNK)

    decay, kq = {}, {}
    for hd, ch in cells:
        hv, sl = heads[hd], rows(ch)
        gc_row = gcum_t[DN_HEADS + hd:DN_HEADS + hd + 1, sl]
        decay[hd, ch] = jnp.where(causal, jnp.exp(hv["gc"][sl] - gc_row), 0.0)
        lhs = jnp.concatenate([hv["k_beta"][sl], hv["qn"][sl]], axis=0).astype(BF16)
        kq[hd, ch] = _dot_nt(lhs, hv["kn"][sl].astype(BF16))
    a = {c: jnp.where(strict, kq[c][:DN_CHUNK] * decay[c], 0.0) for c in cells}
    attn = {c: (kq[c][DN_CHUNK:] * decay[c]).astype(BF16) for c in cells}

    t_inv = {c: eye - a[c] for c in cells}
    pw = {c: a[c].astype(BF16) for c in cells}
    for _ in range(5):
        pw = {c: _dot(pw[c], pw[c]).astype(BF16) for c in cells}
        t_inv = {c: t_inv[c] + _dot(t_inv[c].astype(BF16), pw[c]) for c in cells}
        cf_step()

    sol, aw, ks, glast = {}, {}, {}, {}
    for hd, ch in cells:
        hv, sl = heads[hd], rows(ch)
        rhs = jnp.concatenate([hv["v_beta"][sl], hv["kbg"][sl]], axis=1).astype(BF16)
        sol[hd, ch] = _dot(t_inv[hd, ch].astype(BF16), rhs).astype(BF16)
    for hd, ch in cells:
        hv, sl = heads[hd], rows(ch)
        gc_col = hv["gc"][sl]
        glast[hd, ch] = gc_col[DN_CHUNK - 1:DN_CHUNK]
        k_dec = (hv["kn"][sl] * jnp.exp(glast[hd, ch] - gc_col)).astype(BF16)
        aw[hd, ch] = _dot(attn[hd, ch], sol[hd, ch])
        ks[hd, ch] = _dot_tn(k_dec, sol[hd, ch])

    s_in = {}
    s_cur = [state[hd] for hd in range(DN_HEADS)]
    for ch in range(n_chunks):
        for hd in range(DN_HEADS):
            s_in[hd, ch] = s_cur[hd].astype(BF16)
            kd_u, kd_w = ks[hd, ch][:, :HEAD_DIM], ks[hd, ch][:, HEAD_DIM:]
            s_cur[hd] = s_cur[hd] * jnp.exp(glast[hd, ch]) + kd_u - _dot(kd_w.astype(BF16), s_in[hd, ch])
        cf_step()
    for hd in range(DN_HEADS):
        state[hd] = s_cur[hd]

    for hd, ch in cells:
        hv, sl = heads[hd], rows(ch)
        lo = hd * HEAD_DIM
        q_eff = (hv["qg"][sl] - aw[hd, ch][:, HEAD_DIM:]).astype(BF16)
        o = _dot(q_eff, s_in[hd, ch]) + aw[hd, ch][:, :HEAD_DIM]
        on = o * lax.rsqrt(jnp.mean(o * o, axis=-1, keepdims=True) + EPS) * dnw * hv["zg"][sl]
        mixed[sl, lo:lo + HEAD_DIM] = on.astype(BF16)

    cf_step(len(cf_pending))
    mixed[:, DN_WIDTH:DN_WIDTH + cfw] = _dot(to_natural, cf_p16[...]).astype(BF16)

    out = _dot(mixed[...], wout_ref[...])
    y = out * lax.rsqrt(jnp.mean(out * out, axis=-1, keepdims=True) + EPS) * npost_ref[...]
    o_ref[0] = x + gate1 * y


def _mixer_call(x, mod3, npre, npost, w_in, dn_conv_w, dn_a_log, dn_dt_bias, dn_norm_w,
                cf_pw1_b, cf_dw_w, cf_dw_b, cf_ln_w, cf_ln_b, w_out):
    bsz, seq, d = x.shape
    ts = SEQ_TILE
    cfw = cf_dw_w.shape[1]
    n_qkv = 3 * DN_WIDTH
    wqkv = w_in[:, :n_qkv].astype(BF16)
    wz = w_in[:, n_qkv:n_qkv + DN_WIDTH].astype(BF16)
    wba = jnp.pad(w_in[:, n_qkv + DN_WIDTH:n_qkv + DN_WIDTH + 2 * DN_HEADS],
                  ((0, 0), (0, LANES - 2 * DN_HEADS))).astype(BF16)
    wcf = w_in[:, n_qkv + DN_WIDTH + 2 * DN_HEADS:].astype(BF16)
    alog = jnp.pad(dn_a_log, (DN_HEADS, LANES - 2 * DN_HEADS)).reshape(1, LANES)
    dtb = jnp.pad(dn_dt_bias, (DN_HEADS, LANES - 2 * DN_HEADS)).reshape(1, LANES)
    dww = jnp.repeat(cf_dw_w, SUBLANES, axis=0)
    convw = jnp.repeat(dn_conv_w, SUBLANES, axis=0)

    def full(a):
        return pl.BlockSpec(a.shape, lambda b, j: (0,) * a.ndim)

    row = lambda a: a.reshape(1, -1)
    operands = [x, mod3, row(npre), row(npost), wqkv, wz, wba, wcf, convw, alog, dtb, row(dn_norm_w),
                row(cf_pw1_b), dww, row(cf_dw_b), row(cf_ln_w), row(cf_ln_b), w_out.astype(BF16)]
    in_specs = [pl.BlockSpec((1, ts, d), lambda b, j: (b, j, 0)),
                pl.BlockSpec((1,) + mod3.shape[1:], lambda b, j: (b, 0, 0))]
    in_specs += [full(a) for a in operands[2:]]
    return pl.pallas_call(
        _mixer_kernel,
        grid=(bsz, seq // ts),
        in_specs=in_specs,
        out_specs=pl.BlockSpec((1, ts, d), lambda b, j: (b, j, 0)),
        out_shape=jax.ShapeDtypeStruct((bsz, seq, d), F32),
        scratch_shapes=[pltpu.VMEM(((DN_CONV - 1) * SUBLANES + ts, n_qkv), F32),
                        pltpu.VMEM(((DN_CONV - 1) * SUBLANES, n_qkv), F32),
                        pltpu.VMEM((ts, n_qkv), BF16),
                        pltpu.VMEM((ts, n_qkv), F32),
                        pltpu.VMEM(((CF_KERNEL - 1) * SUBLANES + ts, cfw), F32),
                        pltpu.VMEM(((CF_KERNEL - 1) * SUBLANES, cfw), F32),
                        pltpu.VMEM((ts, cfw), BF16),
                        pltpu.VMEM((DN_HEADS, HEAD_DIM, HEAD_DIM), F32),
                        pltpu.VMEM((ts, DN_WIDTH + cfw), BF16)],
        compiler_params=pltpu.CompilerParams(dimension_semantics=("arbitrary", "arbitrary"),
                                             vmem_limit_bytes=VMEM_LIMIT),
        name="mixer",
    )(*operands)


def _router_kernel(x_ref, mod_ref, nw_ref, wr_ref, br_ref, h_ref, info_ref, cnt_ref, carry):
    tt = x_ref.shape[1]

    @pl.when((pl.program_id(0) == 0) & (pl.program_id(1) == 0))
    def _():
        carry[...] = jnp.zeros(carry.shape, F32)

    x = x_ref[0]
    mod = mod_ref[0]
    shift2, scale2 = mod[3:4], mod[4:5]
    h = x * lax.rsqrt(jnp.mean(x * x, axis=-1, keepdims=True) + EPS)
    h = h * nw_ref[...] * (1.0 + scale2) + shift2
    _store_token_major(h_ref, h)

    h_hi, h_lo, _ = _split3(h)
    w_hi, w_lo, _ = _split3(wr_ref[...])
    logits = _dot(h_hi, w_hi) + (_dot(h_hi, w_lo) + _dot(h_lo, w_hi)) + br_ref[...]

    lane = lax.broadcasted_iota(jnp.int32, (tt, LANES), 1)
    neg = -jnp.inf
    is_grp = (lane >= N_EXPERTS) & (lane < N_EXPERTS + N_GROUPS)
    gl = jnp.where(is_grp, logits, neg)
    gmax = jnp.max(gl, axis=-1, keepdims=True)
    gsum = jnp.sum(jnp.where(is_grp, jnp.exp(gl - gmax), 0.0), axis=-1, keepdims=True)
    grp_p = 1.0 / gsum
    grp_lane = jnp.min(jnp.where(is_grp & (gl == gmax), lane, LANES), axis=-1, keepdims=True)
    grp_idx = grp_lane - N_EXPERTS

    in_grp = (lane < N_EXPERTS) & (lane // EXPERTS_PER_GROUP == grp_idx)
    el = jnp.where(in_grp, logits, neg)
    m1 = jnp.max(el, axis=-1, keepdims=True)
    e1 = jnp.min(jnp.where(in_grp & (el == m1), lane, LANES), axis=-1, keepdims=True)
    el2 = jnp.where(lane == e1, neg, el)
    m2 = jnp.max(el2, axis=-1, keepdims=True)
    e2 = jnp.min(jnp.where(in_grp & (lane != e1) & (el2 == m2), lane, LANES), axis=-1, keepdims=True)
    r = jnp.exp(m2 - m1)
    w1 = grp_p / (1.0 + r)
    w2 = grp_p * r / (1.0 + r)

    hit1 = lane == e1
    hit2 = lane == e2
    onehot = jnp.where(hit1 | hit2, 1.0, 0.0)
    rr = lax.broadcasted_iota(jnp.int32, (tt, tt), 0)
    cc = lax.broadcasted_iota(jnp.int32, (tt, tt), 1)
    strict = jnp.where(cc < rr, 1.0, 0.0).astype(BF16)
    prefix = _dot(strict, onehot.astype(BF16)) + carry[0:1, :]
    rank1 = jnp.sum(jnp.where(hit1, prefix, 0.0), axis=-1, keepdims=True)
    rank2 = jnp.sum(jnp.where(hit2, prefix, 0.0), axis=-1, keepdims=True)
    total = carry[0:1, :] + jnp.sum(onehot, axis=0, keepdims=True)
    carry[...] = jnp.broadcast_to(total, carry.shape)
    cnt_ref[...] = jnp.broadcast_to(total, cnt_ref.shape)

    info = jnp.where(lane == 0, e1.astype(F32), 0.0)
    info = jnp.where(lane == 1, e2.astype(F32), info)
    info = jnp.where(lane == 2, w1, info)
    info = jnp.where(lane == 3, w2, info)
    info = jnp.where(lane == 4, rank1, info)
    info = jnp.where(lane == 5, rank2, info)
    info_ref[0] = info


def _router_call(x1, mod3, norm_w, w_router_group, b_router_group, w_router_expert, b_router_expert):
    bsz, seq, d = x1.shape
    tt = ROUTER_TILE
    nj = seq // tt
    pitch = d // LANES
    pad = LANES - N_EXPERTS - N_GROUPS
    wr = jnp.pad(jnp.concatenate([w_router_expert, w_router_group], axis=1), ((0, 0), (0, pad)))
    br = jnp.pad(jnp.concatenate([b_router_expert, b_router_group]), (0, pad)).reshape(1, LANES)
    return pl.pallas_call(
        _router_kernel,
        grid=(bsz, seq // tt),
        in_specs=[pl.BlockSpec((1, tt, d), lambda b, j: (b, j, 0)),
                  pl.BlockSpec((1,) + mod3.shape[1:], lambda b, j: (b, 0, 0)),
                  pl.BlockSpec((1, d), lambda b, j: (0, 0)),
                  pl.BlockSpec((d, LANES), lambda b, j: (0, 0)),
                  pl.BlockSpec((1, LANES), lambda b, j: (0, 0))],
        out_specs=[pl.BlockSpec((tt * pitch, LANES), lambda b, j: (b * nj + j, 0)),
                   pl.BlockSpec((1, tt, LANES), lambda b, j: (b, j, 0)),
                   pl.BlockSpec((8, LANES), lambda b, j: (0, 0))],
        out_shape=[jax.ShapeDtypeStruct((bsz * seq * pitch, LANES), F32),
                   jax.ShapeDtypeStruct((bsz, seq, LANES), F32),
                   jax.ShapeDtypeStruct((8, LANES), F32)],
        scratch_shapes=[pltpu.VMEM((8, LANES), F32)],
        compiler_params=pltpu.CompilerParams(dimension_semantics=("arbitrary", "arbitrary"),
                                             vmem_limit_bytes=VMEM_LIMIT),
        name="router",
    )(x1, mod3, norm_w.reshape(1, d), wr, br)


def _tile_rows(idx, pitch):
    return pl.ds(pl.multiple_of(idx * pitch, pitch), pitch)


def _dispatch_kernel(nused_ref, pend_ref, padded_ref, pos_ref, h_ref, xs_hbm, zbuf, zsem, sem, *, pitch, bm):
    step = pl.program_id(0)
    tt = h_ref.shape[0] // pitch

    @pl.when(step == 0)
    def _():
        zbuf[...] = jnp.zeros(zbuf.shape, F32)

        def tail_copy(e):
            start = pl.multiple_of((pend_ref[e] - bm) * pitch, pitch)
            return pltpu.make_async_copy(zbuf, xs_hbm.at[pl.ds(start, bm * pitch), :], zsem.at[0])

        def issue(e, carry):
            @pl.when(padded_ref[e] > 0)
            def _():
                tail_copy(e).start()
            return carry

        def drain(e, carry):
            @pl.when(padded_ref[e] > 0)
            def _():
                tail_copy(e).wait()
            return carry

        def block_copy(b):
            start = pl.multiple_of(b * bm * pitch, bm * pitch)
            return pltpu.make_async_copy(zbuf, xs_hbm.at[pl.ds(start, bm * pitch), :], zsem.at[0])

        def issue_block(b, carry):
            block_copy(b).start()
            return carry

        def drain_block(b, carry):
            block_copy(b).wait()
            return carry

        n_blocks = xs_hbm.shape[0] // (bm * pitch)
        lax.fori_loop(0, N_EXPERTS, issue, 0)
        lax.fori_loop(nused_ref[0], n_blocks, issue_block, 0)
        lax.fori_loop(0, N_EXPERTS, drain, 0)
        lax.fori_loop(nused_ref[0], n_blocks, drain_block, 0)

    def body(r, carry):
        for k in range(TOP_K):
            p = pos_ref[0, 0, k * tt + r]
            pltpu.make_async_copy(h_ref.at[_tile_rows(r, pitch), :], xs_hbm.at[_tile_rows(p, pitch), :],
                                  sem.at[0]).start(priority=k)
        return carry
    lax.fori_loop(0, tt, body, 0, unroll=4)
    for k in range(TOP_K):
        pltpu.make_async_copy(h_ref, xs_hbm.at[pl.ds(0, tt * pitch), :], sem.at[0]).wait()


def _dispatch_call(h2, pos3, n_used, pend, padded, n_rows, d):
    pitch = d // LANES
    bm = EXPERT_BLOCK
    n_tiles, _, two_tt = pos3.shape
    tt = two_tt // TOP_K
    grid_spec = pltpu.PrefetchScalarGridSpec(
        num_scalar_prefetch=3,
        grid=(n_tiles,),
        in_specs=[pl.BlockSpec((1, 1, two_tt), lambda s, *_: (s, 0, 0), memory_space=pltpu.SMEM),
                  pl.BlockSpec((tt * pitch, LANES), lambda s, *_: (s, 0))],
        out_specs=pl.BlockSpec(memory_space=pl.ANY),
        scratch_shapes=[pltpu.VMEM((bm * pitch, LANES), F32), pltpu.SemaphoreType.DMA((1,)),
                        pltpu.SemaphoreType.DMA((1,))],
    )
    return pl.pallas_call(
        functools.partial(_dispatch_kernel, pitch=pitch, bm=bm),
        grid_spec=grid_spec,
        out_shape=jax.ShapeDtypeStruct((n_rows * pitch, LANES), F32),
        compiler_params=pltpu.CompilerParams(dimension_semantics=("arbitrary",),
                                             vmem_limit_bytes=VMEM_LIMIT),
        name="dispatch",
    )(n_used, pend, padded, pos3, h2)


def _expert_kernel(nused_ref, nruns_ref, first_ref, run_ref, rune_ref, x_ref, wg_hbm, wu_hbm, wd_hbm, y_ref,
                   wg_f32, wu_f32, wd_f32, wg_b, wu_b, wd_b, wsems):
    i = pl.program_id(0)
    n_used = nused_ref[0]
    n_runs = nruns_ref[0]
    d = wg_b.shape[0]
    pitch = d // LANES
    bm = y_ref.shape[0] // pitch
    n_slots = wg_f32.shape[0]

    def weight_copies(e, slot):
        return [pltpu.make_async_copy(src.at[e], dst.at[slot], wsems.at[slot])
                for src, dst in ((wg_hbm, wg_f32), (wu_hbm, wu_f32), (wd_hbm, wd_f32))]

    def start_run(r):
        @pl.when(r < n_runs)
        def _():
            for prio, cp in zip((1, 1, 0), weight_copies(rune_ref[r], r % n_slots)):
                cp.start(priority=prio)

    @pl.when(i == 0)
    def _():
        for r in range(n_slots - 1):
            start_run(r)

    @pl.when((i < n_used) & (first_ref[i] == 1))
    def _():
        run = run_ref[i]
        slot = run % n_slots
        for cp in weight_copies(0, slot):
            cp.wait()
        start_run(run + n_slots - 1)

        wg_b[...] = wg_f32[slot].astype(BF16)
        wu_b[...] = wu_f32[slot].astype(BF16)
        wd_b[...] = wd_f32[slot].astype(BF16)

    @pl.when(i < n_used)
    def _():
        xb = _load_token_major(x_ref, bm, d).astype(BF16)
        gate = _dot(xb, wg_b[...])
        up = _dot(xb, wu_b[...])
        hid = (_silu(gate) * up).astype(BF16)
        _store_token_major(y_ref, _dot(hid, wd_b[...]))

    @pl.when(i >= n_used)
    def _():
        y_ref[...] = jnp.zeros(y_ref.shape, F32)


def _expert_call(xs, block_e, n_used, w_gate, w_up, w_down):
    bm = EXPERT_BLOCK
    d, de = w_gate.shape[1], w_gate.shape[2]
    pitch = d // LANES
    n_blocks = xs.shape[0] // (bm * pitch)
    blk = jnp.arange(n_blocks, dtype=jnp.int32)
    used = blk < n_used[0]
    first = jnp.concatenate([jnp.ones((1,), jnp.int32), (block_e[1:] != block_e[:-1]).astype(jnp.int32)])
    first = jnp.where(used, first, 0)
    run = (jnp.cumsum(first) - 1).astype(jnp.int32)
    n_runs = jnp.sum(first).astype(jnp.int32).reshape(1)
    eids = jnp.arange(N_EXPERTS, dtype=jnp.int32)
    run_of_e = jnp.where((first[:, None] == 1) & (block_e[:, None] == eids[None, :]), run[:, None], -1).max(axis=0)
    run_e = jnp.sum(jnp.where(run_of_e[None, :] == eids[:, None], eids[None, :], 0), axis=1).astype(jnp.int32)
    grid_spec = pltpu.PrefetchScalarGridSpec(
        num_scalar_prefetch=5,
        grid=(n_blocks,),
        in_specs=[pl.BlockSpec((bm * pitch, LANES), lambda i, nu, *_: (jnp.minimum(i, nu[0] - 1), 0)),
                  pl.BlockSpec(memory_space=pl.ANY),
                  pl.BlockSpec(memory_space=pl.ANY),
                  pl.BlockSpec(memory_space=pl.ANY)],
        out_specs=pl.BlockSpec((bm * pitch, LANES), lambda i, *_: (i, 0)),
        scratch_shapes=[pltpu.VMEM((WEIGHT_SLOTS, d, de), F32), pltpu.VMEM((WEIGHT_SLOTS, d, de), F32),
                        pltpu.VMEM((WEIGHT_SLOTS, de, d), F32),
                        pltpu.VMEM((d, de), BF16), pltpu.VMEM((d, de), BF16), pltpu.VMEM((de, d), BF16),
                        pltpu.SemaphoreType.DMA((WEIGHT_SLOTS,))],
    )
    return pl.pallas_call(
        _expert_kernel,
        grid_spec=grid_spec,
        out_shape=jax.ShapeDtypeStruct((n_blocks * bm * pitch, LANES), F32),
        compiler_params=pltpu.CompilerParams(dimension_semantics=("arbitrary",),
                                             vmem_limit_bytes=VMEM_LIMIT),
        name="experts",
    )(n_used, n_runs, first, run, run_e, xs, w_gate, w_up, w_down)


def _combine_kernel(pos_cur_ref, pos_nxt_ref, x_ref, info_ref, mod_ref, nw_ref, y_hbm, o_ref, ybuf, sems):
    tt, d = x_ref.shape[1], x_ref.shape[2]
    pitch = d // LANES
    step = pl.program_id(0) * pl.num_programs(1) + pl.program_id(1)
    n_steps = pl.num_programs(0) * pl.num_programs(1)
    slot_rows = TOP_K * tt * pitch

    def start_gather(pos_ref, slot):
        def body(r, carry):
            for k in range(TOP_K):
                p = pos_ref[0, 0, k * tt + r]
                pltpu.make_async_copy(y_hbm.at[_tile_rows(p, pitch), :],
                                      ybuf.at[_tile_rows((slot * TOP_K + k) * tt + r, pitch), :],
                                      sems.at[slot]).start(priority=k)
            return carry
        lax.fori_loop(0, tt, body, 0, unroll=4)

    @pl.when(step == 0)
    def _():
        start_gather(pos_cur_ref, 0)

    @pl.when(step + 1 < n_steps)
    def _():
        start_gather(pos_nxt_ref, (step + 1) % 2)

    info = info_ref[0]
    w1 = info[:, 2:3]
    w2 = info[:, 3:4]
    x = x_ref[0]
    gate2 = mod_ref[0][5:6]
    base = pl.multiple_of((step % 2) * slot_rows, slot_rows)
    pltpu.make_async_copy(y_hbm.at[pl.ds(0, slot_rows), :], ybuf.at[pl.ds(base, slot_rows), :],
                          sems.at[step % 2]).wait()
    moe = (_load_token_major(ybuf, tt, d, base) * w1
           + _load_token_major(ybuf, tt, d, base + tt * pitch) * w2)
    y = moe * lax.rsqrt(jnp.mean(moe * moe, axis=-1, keepdims=True) + EPS) * nw_ref[...]
    o_ref[0] = x + gate2 * y


def _combine_call(x1, info, mod3, norm_w, y, pos3):
    bsz, seq, d = x1.shape
    tt = COMBINE_TILE
    nj = seq // tt
    n_tiles = bsz * nj
    pos_blk = lambda f: pl.BlockSpec((1, 1, TOP_K * tt), f, memory_space=pltpu.SMEM)
    return pl.pallas_call(
        _combine_kernel,
        grid=(bsz, nj),
        in_specs=[pos_blk(lambda b, j: (b * nj + j, 0, 0)),
                  pos_blk(lambda b, j: (jnp.minimum(b * nj + j + 1, n_tiles - 1), 0, 0)),
                  pl.BlockSpec((1, tt, d), lambda b, j: (b, j, 0)),
                  pl.BlockSpec((1, tt, LANES), lambda b, j: (b, j, 0)),
                  pl.BlockSpec((1,) + mod3.shape[1:], lambda b, j: (b, 0, 0)),
                  pl.BlockSpec((1, d), lambda b, j: (0, 0)),
                  pl.BlockSpec(memory_space=pl.ANY)],
        out_specs=pl.BlockSpec((1, tt, d), lambda b, j: (b, j, 0)),
        out_shape=jax.ShapeDtypeStruct((bsz, seq, d), F32),
        scratch_shapes=[pltpu.VMEM((2 * TOP_K * tt * (d // LANES), LANES), F32), pltpu.SemaphoreType.DMA((2,))],
        compiler_params=pltpu.CompilerParams(dimension_semantics=("arbitrary", "arbitrary"),
                                             vmem_limit_bytes=VMEM_LIMIT),
        name="combine",
    )(pos3, pos3, x1, info, mod3, norm_w.reshape(1, d), y)


def _layer(x, mod, norm_pre_mix, norm_post_mix, w_in, dn_conv_w, dn_a_log, dn_dt_bias, dn_norm_w,
           cf_pw1_b, cf_dw_w, cf_dw_b, cf_ln_w, cf_ln_b, w_out, norm_pre_ffn, norm_post_ffn,
           w_router_group, b_router_group, w_router_expert, b_router_expert, w_gate, w_up, w_down):
    bsz, seq, d = x.shape
    t = bsz * seq
    mod3 = mod.reshape(bsz, -1, d)
    x1 = _mixer_call(x, mod3, norm_pre_mix, norm_post_mix, w_in, dn_conv_w, dn_a_log, dn_dt_bias, dn_norm_w,
                     cf_pw1_b, cf_dw_w, cf_dw_b, cf_ln_w, cf_ln_b, w_out)
    h2, info, cnt = _router_call(x1, mod3, norm_pre_ffn, w_router_group, b_router_group,
                                 w_router_expert, b_router_expert)

    bm = EXPERT_BLOCK
    info2 = info.reshape(t, LANES)
    expert_id = info2[:, 0:TOP_K].astype(jnp.int32)
    rank = info2[:, 4:4 + TOP_K].astype(jnp.int32)
    counts = cnt[0, :N_EXPERTS].astype(jnp.int32)
    padded = (counts + bm - 1) // bm * bm
    pend = jnp.cumsum(padded)
    pstart = pend - padded
    onehot = expert_id[..., None] == jnp.arange(N_EXPERTS, dtype=jnp.int32)
    pos = jnp.sum(jnp.where(onehot, pstart, 0), axis=-1) + rank

    def tiled(tt):
        return pos.reshape(t // tt, tt, TOP_K).transpose(0, 2, 1).reshape(t // tt, 1, TOP_K * tt)
    n_blocks = -(-(t * TOP_K) // bm) + N_EXPERTS
    block_start = jnp.arange(n_blocks, dtype=jnp.int32) * bm
    block_e = jnp.minimum(jnp.sum(pend[None, :] <= block_start[:, None], axis=1), N_EXPERTS - 1).astype(jnp.int32)
    n_used = (pend[-1] // bm).astype(jnp.int32).reshape(1)

    xs = _dispatch_call(h2, tiled(DISPATCH_TILE), n_used, pend.astype(jnp.int32), padded, n_blocks * bm, d)
    y = _expert_call(xs, block_e, n_used, w_gate, w_up, w_down)
    return _combine_call(x1, info, mod3, norm_post_ffn, y, tiled(COMBINE_TILE))


def kernel(x, c, w_ada, b_ada, norm_pre_mix, norm_post_mix, w_in, dn_conv_w, dn_a_log, dn_dt_bias, dn_norm_w,
           cf_pw1_b, cf_dw_w, cf_dw_b, cf_ln_w, cf_ln_b, w_out, norm_pre_ffn, norm_post_ffn,
           w_router_group, b_router_group, w_router_expert, b_router_expert, w_gate, w_up, w_down):
    depth = w_ada.shape[0]
    for l in range(depth):
        mod = _ada_call(c, w_ada[l], b_ada[l])
        x = _layer(x, mod, norm_pre_mix[l], norm_post_mix[l], w_in[l], dn_conv_w[l], dn_a_log[l],
                   dn_dt_bias[l], dn_norm_w[l], cf_pw1_b[l], cf_dw_w[l], cf_dw_b[l], cf_ln_w[l], cf_ln_b[l],
                   w_out[l], norm_pre_ffn[l], norm_post_ffn[l], w_router_group[l], b_router_group[l],
                   w_router_expert[l], b_router_expert[l], w_gate[l], w_up[l], w_down[l])
    return x
```

```python
import functools

import jax
import jax.numpy as jnp
from jax import lax
from jax.experimental import pallas as pl
from jax.experimental.pallas import tpu as pltpu

F32 = jnp.float32
BF16 = jnp.bfloat16
EPS = 1e-6

DN_HEADS = 4
HEAD_DIM = 128
DN_WIDTH = DN_HEADS * HEAD_DIM
DN_CONV = 4
DN_CHUNK = 64
CF_KERNEL = 31
N_GROUPS = 8
EXPERTS_PER_GROUP = 8
N_EXPERTS = N_GROUPS * EXPERTS_PER_GROUP
TOP_K = 2

LANES = 128
SUBLANES = 8
SEQ_TILE = 256
CONV_ROWS = 32
ROUTER_TILE = 512
EXPERT_BLOCK = 256
COMBINE_TILE = 512
WEIGHT_SLOTS = 3
DISPATCH_TILE = 1024
VMEM_LIMIT = 56 * 1024 * 1024


def _dot(a, b):
    return jnp.dot(a, b, preferred_element_type=F32)


def _dot_nt(a, b):
    return lax.dot_general(a, b, (((1,), (1,)), ((), ())), preferred_element_type=F32)


def _dot_tn(a, b):
    return lax.dot_general(a, b, (((0,), (0,)), ((), ())), preferred_element_type=F32)


def _split3(x):
    hi = x.astype(BF16)
    r1 = x - hi.astype(F32)
    mid = r1.astype(BF16)
    lo = (r1 - mid.astype(F32)).astype(BF16)
    return hi, mid, lo


def _silu(x):
    return x * jax.nn.sigmoid(x)


def _softplus(x):
    return jnp.maximum(x, 0.0) + jnp.log1p(jnp.exp(-jnp.abs(x)))


def _store_token_major(ref, val, base=0):
    n, d = val.shape
    pitch = d // LANES
    for j in range(pitch):
        ref[pl.ds(base + j, n, stride=pitch), :] = val[:, j * LANES:(j + 1) * LANES]


def _load_token_major(ref, n, d, base=0):
    pitch = d // LANES
    return jnp.concatenate([ref[pl.ds(base + j, n, stride=pitch), :] for j in range(pitch)], axis=1)


def _ada_kernel(c_ref, w_ref, b_ref, o_ref):
    c = c_ref[...]
    ca = _silu(c)
    c_hi, c_lo, _ = _split3(ca)
    w_hi, w_lo, _ = _split3(w_ref[...])
    o_ref[...] = _dot(c_hi, w_hi) + (_dot(c_hi, w_lo) + _dot(c_lo, w_hi)) + b_ref[...]


def _ada_call(c, w, b):
    bsz, d = c.shape
    n = w.shape[1]
    tn = 512
    return pl.pallas_call(
        _ada_kernel,
        grid=(n // tn,),
        in_specs=[pl.BlockSpec((bsz, d), lambda i: (0, 0)),
                  pl.BlockSpec((d, tn), lambda i: (0, i)),
                  pl.BlockSpec((1, tn), lambda i: (0, i))],
        out_specs=pl.BlockSpec((bsz, tn), lambda i: (0, i)),
        out_shape=jax.ShapeDtypeStruct((bsz, n), F32),
        compiler_params=pltpu.CompilerParams(dimension_semantics=("arbitrary",),
                                             vmem_limit_bytes=VMEM_LIMIT),
        name="ada",
    )(c, w, b.reshape(1, n))


def _time_perm(ts, transpose):
    ri = lax.broadcasted_iota(jnp.int32, (ts, ts), 0)
    ci = lax.broadcasted_iota(jnp.int32, (ts, ts), 1)
    strided, natural = (ci, ri) if transpose else (ri, ci)
    return jnp.where(natural == (ts // SUBLANES) * (strided % SUBLANES) + strided // SUBLANES, 1.0, 0.0).astype(BF16)


def _fill_conv_window(ext_ref, prev_ref, cur, n_taps):
    ts = cur.shape[0]
    lead = (n_taps - 1) * SUBLANES
    tail = cur[ts - lead:, :]
    sub = lax.broadcasted_iota(jnp.int32, tail.shape, 0) % SUBLANES
    merged = jnp.where(sub == SUBLANES - 1, prev_ref[...], tail)
    for g in range(n_taps - 1):
        rows = slice(g * SUBLANES, (g + 1) * SUBLANES)
        ext_ref[rows, :] = pltpu.roll(merged[rows, :], 1, 0)
    ext_ref[lead:lead + ts, :] = cur
    prev_ref[...] = tail


def _conv_block(ext_ref, w_ref, n_taps, r0, row_blk, c0, col_blk):
    groups = row_blk // SUBLANES
    acc = [jnp.zeros((SUBLANES, col_blk), F32) for _ in range(groups)]
    for k in range(n_taps):
        w = w_ref[k * SUBLANES:(k + 1) * SUBLANES, c0:c0 + col_blk]
        for g in range(groups):
            lo = r0 + (k + g) * SUBLANES
            acc[g] = acc[g] + w * ext_ref[lo:lo + SUBLANES, c0:c0 + col_blk]
    return jnp.concatenate(acc, axis=0)


def _mixer_kernel(x_ref, mod_ref, npre_ref, npost_ref, wqkv_ref, wz_ref, wba_ref, wcf_ref,
                  convw_ref, alog_ref, dtb_ref, dnw_ref, pw1b_ref, dww_ref, dwb_ref, lnw_ref, lnb_ref,
                  wout_ref, o_ref, qkv_ext, qkv_prev, qkv_p16, qkv_act, cf_ext, cf_prev, cf_p16, state, mixed,
                  consts):
    ts = x_ref.shape[1]
    n_chunks = ts // DN_CHUNK

    @pl.when(pl.program_id(1) == 0)
    def _():
        qkv_prev[...] = jnp.zeros(qkv_prev.shape, F32)
        cf_prev[...] = jnp.zeros(cf_prev.shape, F32)
        state[...] = jnp.zeros(state.shape, F32)

    @pl.when((pl.program_id(0) == 0) & (pl.program_id(1) == 0))
    def _():
        ri = lax.broadcasted_iota(jnp.int32, (ts, ts), 0)
        ci = lax.broadcasted_iota(jnp.int32, (ts, ts), 1)
        consts[0] = _time_perm(ts, False)
        consts[1] = _time_perm(ts, True)
        consts[2] = jnp.where((ri // DN_CHUNK == ci // DN_CHUNK) & (ci <= ri), 1.0, 0.0).astype(BF16)

    x = x_ref[0]
    mod = mod_ref[0]
    shift1, scale1, gate1 = mod[0:1], mod[1:2], mod[2:3]
    h = x * lax.rsqrt(jnp.mean(x * x, axis=-1, keepdims=True) + EPS)
    h = h * npre_ref[...] * (1.0 + scale1) + shift1
    hb = h.astype(BF16)
    hb_st = _dot(consts[0], hb).astype(BF16)
    to_natural = consts[1]

    _fill_conv_window(qkv_ext, qkv_prev, _dot(hb_st, wqkv_ref[...]), DN_CONV)
    for r0 in range(0, ts, CONV_ROWS):
        for c0 in range(0, qkv_ext.shape[1], DN_WIDTH):
            blk = _conv_block(qkv_ext, convw_ref, DN_CONV, r0, CONV_ROWS, c0, DN_WIDTH)
            qkv_p16[r0:r0 + CONV_ROWS, c0:c0 + DN_WIDTH] = _silu(blk).astype(BF16)
    qkv_act[...] = _dot(to_natural, qkv_p16[...])

    cf_pre = _dot(hb_st, wcf_ref[...]) + pw1b_ref[...]
    cfw = cf_ext.shape[1]
    _fill_conv_window(cf_ext, cf_prev, cf_pre[:, :cfw] * jax.nn.sigmoid(cf_pre[:, cfw:]), CF_KERNEL)
    cf_pending = list(range(0, ts, CONV_ROWS))

    def cf_step(n=1):
        for _ in range(min(n, len(cf_pending))):
            r0 = cf_pending.pop(0)
            cf = _conv_block(cf_ext, dww_ref, CF_KERNEL, r0, CONV_ROWS, 0, cfw) + dwb_ref[...]
            mu = jnp.mean(cf, axis=-1, keepdims=True)
            xc = cf - mu
            var = jnp.mean(xc * xc, axis=-1, keepdims=True)
            cfn = xc * lax.rsqrt(var + EPS) * lnw_ref[...] + lnb_ref[...]
            cf_p16[r0:r0 + CONV_ROWS, :] = _silu(cfn).astype(BF16)

    z = _dot(hb, wz_ref[...])
    ba = _dot(hb, wba_ref[...])
    beta_all = jax.nn.sigmoid(ba)
    g_all = -jnp.exp(alog_ref[...]) * _softplus(ba + dtb_ref[...])

    tri = consts[2]
    g_hi, g_mid, g_lo = _split3(g_all)
    gcum = _dot(tri, g_hi) + _dot(tri, g_mid) + _dot(tri, g_lo)
    gcum_t = gcum.T
    exp_g = jnp.exp(gcum)

    r64 = lax.broadcasted_iota(jnp.int32, (DN_CHUNK, DN_CHUNK), 0)
    c64 = lax.broadcasted_iota(jnp.int32, (DN_CHUNK, DN_CHUNK), 1)
    causal = c64 <= r64
    strict = c64 < r64
    eye = jnp.where(c64 == r64, 1.0, 0.0).astype(F32)
    dnw = dnw_ref[...]

    heads = []
    for hd in range(DN_HEADS):
        lo = hd * HEAD_DIM
        qh = qkv_act[:, lo:lo + HEAD_DIM]
        kh = qkv_act[:, DN_WIDTH + lo:DN_WIDTH + lo + HEAD_DIM]
        vh = qkv_act[:, 2 * DN_WIDTH + lo:2 * DN_WIDTH + lo + HEAD_DIM]
        qn = qh * lax.rsqrt(jnp.sum(qh * qh, axis=-1, keepdims=True) + EPS) * (HEAD_DIM ** -0.5)
        kn = kh * lax.rsqrt(jnp.sum(kh * kh, axis=-1, keepdims=True) + EPS)
        beta_h = beta_all[:, hd:hd + 1]
        gc_h = gcum[:, DN_HEADS + hd:DN_HEADS + hd + 1]
        eg_h = exp_g[:, DN_HEADS + hd:DN_HEADS + hd + 1]
        k_beta = kn * beta_h
        heads.append(dict(qn=qn, kn=kn, k_beta=k_beta, v_beta=vh * beta_h, kbg=k_beta * eg_h, qg=qn * eg_h,
                          gc=gc_h, zg=_silu(z[:, lo:lo + HEAD_DIM])))

    cells = [(hd, ch) for hd in range(DN_HEADS) for ch in range(n_chunks)]
    rows = lambda ch: slice(ch * DN_CHUNK, (ch + 1) * DN_CHUNK)

    decay, kq = {}, {}
    for hd, ch in cells:
        hv, sl = heads[hd], rows(ch)
        gc_row = gcum_t[DN_HEADS + hd:DN_HEADS + hd + 1, sl]
        decay[hd, ch] = jnp.where(causal, jnp.exp(hv["gc"][sl] - gc_row), 0.0)
        lhs = jnp.concatenate([hv["k_beta"][sl], hv["qn"][sl]], axis=0).astype(BF16)
        kq[hd, ch] = _dot_nt(lhs, hv["kn"][sl].astype(BF16))
    a = {c: jnp.where(strict, kq[c][:DN_CHUNK] * decay[c], 0.0) for c in cells}
    attn = {c: (kq[c][DN_CHUNK:] * decay[c]).astype(BF16) for c in cells}

    t_inv = {c: eye - a[c] for c in cells}
    pw = {c: a[c].astype(BF16) for c in cells}
    for _ in range(5):
        pw = {c: _dot(pw[c], pw[c]).astype(BF16) for c in cells}
        t_inv = {c: t_inv[c] + _dot(t_inv[c].astype(BF16), pw[c]) for c in cells}
        cf_step()

    sol, aw, ks, glast = {}, {}, {}, {}
    for hd, ch in cells:
        hv, sl = heads[hd], rows(ch)
        rhs = jnp.concatenate([hv["v_beta"][sl], hv["kbg"][sl]], axis=1).astype(BF16)
        sol[hd, ch] = _dot(t_inv[hd, ch].astype(BF16), rhs).astype(BF16)
    for hd, ch in cells:
        hv, sl = heads[hd], rows(ch)
        gc_col = hv["gc"][sl]
        glast[hd, ch] = gc_col[DN_CHUNK - 1:DN_CHUNK]
        k_dec = (hv["kn"][sl] * jnp.exp(glast[hd, ch] - gc_col)).astype(BF16)
        aw[hd, ch] = _dot(attn[hd, ch], sol[hd, ch])
        ks[hd, ch] = _dot_tn(k_dec, sol[hd, ch])

    s_in = {}
    s_cur = [state[hd] for hd in range(DN_HEADS)]
    for ch in range(n_chunks):
        for hd in range(DN_HEADS):
            s_in[hd, ch] = s_cur[hd].astype(BF16)
            kd_u, kd_w = ks[hd, ch][:, :HEAD_DIM], ks[hd, ch][:, HEAD_DIM:]
            s_cur[hd] = s_cur[hd] * jnp.exp(glast[hd, ch]) + kd_u - _dot(kd_w.astype(BF16), s_in[hd, ch])
        cf_step()
    for hd in range(DN_HEADS):
        state[hd] = s_cur[hd]

    for hd, ch in cells:
        hv, sl = heads[hd], rows(ch)
        lo = hd * HEAD_DIM
        q_eff = (hv["qg"][sl] - aw[hd, ch][:, HEAD_DIM:]).astype(BF16)
        o = _dot(q_eff, s_in[hd, ch]) + aw[hd, ch][:, :HEAD_DIM]
        on = o * lax.rsqrt(jnp.mean(o * o, axis=-1, keepdims=True) + EPS) * dnw * hv["zg"][sl]
        mixed[sl, lo:lo + HEAD_DIM] = on.astype(BF16)

    cf_step(len(cf_pending))
    mixed[:, DN_WIDTH:DN_WIDTH + cfw] = _dot(to_natural, cf_p16[...]).astype(BF16)

    out = _dot(mixed[...], wout_ref[...])
    y = out * lax.rsqrt(jnp.mean(out * out, axis=-1, keepdims=True) + EPS) * npost_ref[...]
    o_ref[0] = x + gate1 * y


def _mixer_call(x, mod3, npre, npost, w_in, dn_conv_w, dn_a_log, dn_dt_bias, dn_norm_w,
                cf_pw1_b, cf_dw_w, cf_dw_b, cf_ln_w, cf_ln_b, w_out):
    bsz, seq, d = x.shape
    ts = SEQ_TILE
    cfw = cf_dw_w.shape[1]
    n_qkv = 3 * DN_WIDTH
    wqkv = w_in[:, :n_qkv].astype(BF16)
    wz = w_in[:, n_qkv:n_qkv + DN_WIDTH].astype(BF16)
    wba = jnp.pad(w_in[:, n_qkv + DN_WIDTH:n_qkv + DN_WIDTH + 2 * DN_HEADS],
                  ((0, 0), (0, LANES - 2 * DN_HEADS))).astype(BF16)
    wcf = w_in[:, n_qkv + DN_WIDTH + 2 * DN_HEADS:].astype(BF16)
    alog = jnp.pad(dn_a_log, (DN_HEADS, LANES - 2 * DN_HEADS)).reshape(1, LANES)
    dtb = jnp.pad(dn_dt_bias, (DN_HEADS, LANES - 2 * DN_HEADS)).reshape(1, LANES)
    dww = jnp.repeat(cf_dw_w, SUBLANES, axis=0)
    convw = jnp.repeat(dn_conv_w, SUBLANES, axis=0)

    def full(a):
        return pl.BlockSpec(a.shape, lambda b, j: (0,) * a.ndim)

    row = lambda a: a.reshape(1, -1)
    operands = [x, mod3, row(npre), row(npost), wqkv, wz, wba, wcf, convw, alog, dtb, row(dn_norm_w),
                row(cf_pw1_b), dww, row(cf_dw_b), row(cf_ln_w), row(cf_ln_b), w_out.astype(BF16)]
    in_specs = [pl.BlockSpec((1, ts, d), lambda b, j: (b, j, 0)),
                pl.BlockSpec((1,) + mod3.shape[1:], lambda b, j: (b, 0, 0))]
    in_specs += [full(a) for a in operands[2:]]
    return pl.pallas_call(
        _mixer_kernel,
        grid=(bsz, seq // ts),
        in_specs=in_specs,
        out_specs=pl.BlockSpec((1, ts, d), lambda b, j: (b, j, 0)),
        out_shape=jax.ShapeDtypeStruct((bsz, seq, d), F32),
        scratch_shapes=[pltpu.VMEM(((DN_CONV - 1) * SUBLANES + ts, n_qkv), F32),
                        pltpu.VMEM(((DN_CONV - 1) * SUBLANES, n_qkv), F32),
                        pltpu.VMEM((ts, n_qkv), BF16),
                        pltpu.VMEM((ts, n_qkv), F32),
                        pltpu.VMEM(((CF_KERNEL - 1) * SUBLANES + ts, cfw), F32),
                        pltpu.VMEM(((CF_KERNEL - 1) * SUBLANES, cfw), F32),
                        pltpu.VMEM((ts, cfw), BF16),
                        pltpu.VMEM((DN_HEADS, HEAD_DIM, HEAD_DIM), F32),
                        pltpu.VMEM((ts, DN_WIDTH + cfw), BF16),
                        pltpu.VMEM((3, ts, ts), BF16)],
        compiler_params=pltpu.CompilerParams(dimension_semantics=("arbitrary", "arbitrary"),
                                             vmem_limit_bytes=VMEM_LIMIT),
        name="mixer",
    )(*operands)


def _router_kernel(x_ref, mod_ref, nw_ref, wr_ref, br_ref, h_ref, info_ref, cnt_ref, carry, strict_ref):
    tt = x_ref.shape[1]

    @pl.when((pl.program_id(0) == 0) & (pl.program_id(1) == 0))
    def _():
        carry[...] = jnp.zeros(carry.shape, F32)
        rr = lax.broadcasted_iota(jnp.int32, (tt, tt), 0)
        cc = lax.broadcasted_iota(jnp.int32, (tt, tt), 1)
        strict_ref[...] = jnp.where(cc < rr, 1.0, 0.0).astype(BF16)

    x = x_ref[0]
    mod = mod_ref[0]
    shift2, scale2 = mod[3:4], mod[4:5]
    h = x * lax.rsqrt(jnp.mean(x * x, axis=-1, keepdims=True) + EPS)
    h = h * nw_ref[...] * (1.0 + scale2) + shift2
    _store_token_major(h_ref, h)

    h_hi, h_lo, _ = _split3(h)
    w_hi, w_lo, _ = _split3(wr_ref[...])
    logits = _dot(h_hi, w_hi) + (_dot(h_hi, w_lo) + _dot(h_lo, w_hi)) + br_ref[...]

    lane = lax.broadcasted_iota(jnp.int32, (tt, LANES), 1)
    neg = -jnp.inf
    is_grp = (lane >= N_EXPERTS) & (lane < N_EXPERTS + N_GROUPS)
    gl = jnp.where(is_grp, logits, neg)
    gmax = jnp.max(gl, axis=-1, keepdims=True)
    gsum = jnp.sum(jnp.where(is_grp, jnp.exp(gl - gmax), 0.0), axis=-1, keepdims=True)
    grp_p = 1.0 / gsum
    grp_lane = jnp.min(jnp.where(is_grp & (gl == gmax), lane, LANES), axis=-1, keepdims=True)
    grp_idx = grp_lane - N_EXPERTS

    in_grp = (lane < N_EXPERTS) & (lane // EXPERTS_PER_GROUP == grp_idx)
    el = jnp.where(in_grp, logits, neg)
    m1 = jnp.max(el, axis=-1, keepdims=True)
    e1 = jnp.min(jnp.where(in_grp & (el == m1), lane, LANES), axis=-1, keepdims=True)
    el2 = jnp.where(lane == e1, neg, el)
    m2 = jnp.max(el2, axis=-1, keepdims=True)
    e2 = jnp.min(jnp.where(in_grp & (lane != e1) & (el2 == m2), lane, LANES), axis=-1, keepdims=True)
    r = jnp.exp(m2 - m1)
    w1 = grp_p / (1.0 + r)
    w2 = grp_p * r / (1.0 + r)

    hit1 = lane == e1
    hit2 = lane == e2
    onehot = jnp.where(hit1 | hit2, 1.0, 0.0)
    prefix = _dot(strict_ref[...], onehot.astype(BF16)) + carry[0:1, :]
    rank1 = jnp.sum(jnp.where(hit1, prefix, 0.0), axis=-1, keepdims=True)
    rank2 = jnp.sum(jnp.where(hit2, prefix, 0.0), axis=-1, keepdims=True)
    total = carry[0:1, :] + jnp.sum(onehot, axis=0, keepdims=True)
    carry[...] = jnp.broadcast_to(total, carry.shape)
    cnt_ref[...] = jnp.broadcast_to(total, cnt_ref.shape)

    info = jnp.where(lane == 0, e1.astype(F32), 0.0)
    info = jnp.where(lane == 1, e2.astype(F32), info)
    info = jnp.where(lane == 2, w1, info)
    info = jnp.where(lane == 3, w2, info)
    info = jnp.where(lane == 4, rank1, info)
    info = jnp.where(lane == 5, rank2, info)
    info_ref[0] = info


def _router_call(x1, mod3, norm_w, w_router_group, b_router_group, w_router_expert, b_router_expert):
    bsz, seq, d = x1.shape
    tt = ROUTER_TILE
    nj = seq // tt
    pitch = d // LANES
    pad = LANES - N_EXPERTS - N_GROUPS
    wr = jnp.pad(jnp.concatenate([w_router_expert, w_router_group], axis=1), ((0, 0), (0, pad)))
    br = jnp.pad(jnp.concatenate([b_router_expert, b_router_group]), (0, pad)).reshape(1, LANES)
    return pl.pallas_call(
        _router_kernel,
        grid=(bsz, seq // tt),
        in_specs=[pl.BlockSpec((1, tt, d), lambda b, j: (b, j, 0)),
                  pl.BlockSpec((1,) + mod3.shape[1:], lambda b, j: (b, 0, 0)),
                  pl.BlockSpec((1, d), lambda b, j: (0, 0)),
                  pl.BlockSpec((d, LANES), lambda b, j: (0, 0)),
                  pl.BlockSpec((1, LANES), lambda b, j: (0, 0))],
        out_specs=[pl.BlockSpec((tt * pitch, LANES), lambda b, j: (b * nj + j, 0)),
                   pl.BlockSpec((1, tt, LANES), lambda b, j: (b, j, 0)),
                   pl.BlockSpec((8, LANES), lambda b, j: (0, 0))],
        out_shape=[jax.ShapeDtypeStruct((bsz * seq * pitch, LANES), F32),
                   jax.ShapeDtypeStruct((bsz, seq, LANES), F32),
                   jax.ShapeDtypeStruct((8, LANES), F32)],
        scratch_shapes=[pltpu.VMEM((8, LANES), F32), pltpu.VMEM((tt, tt), BF16)],
        compiler_params=pltpu.CompilerParams(dimension_semantics=("arbitrary", "arbitrary"),
                                             vmem_limit_bytes=VMEM_LIMIT),
        name="router",
    )(x1, mod3, norm_w.reshape(1, d), wr, br)


def _tile_rows(idx, pitch):
    return pl.ds(pl.multiple_of(idx * pitch, pitch), pitch)


def _dispatch_kernel(nused_ref, padfrom_ref, padlen_ref, pos_ref, h_ref, xs_hbm, zbuf, zsem, sem, *, pitch, bm):
    step = pl.program_id(0)
    tt = h_ref.shape[0] // pitch

    def fill_copies(action):
        def per_expert(e, carry):
            cursor = padfrom_ref[e]
            n = padlen_ref[e]
            bit = bm // 2
            while bit >= 1:
                take = (n & bit) != 0

                @pl.when(take)
                def _(cursor=cursor, bit=bit):
                    action(pltpu.make_async_copy(zbuf.at[pl.ds(0, bit * pitch), :],
                                                 xs_hbm.at[pl.ds(pl.multiple_of(cursor * pitch, pitch), bit * pitch), :],
                                                 zsem.at[0]))
                cursor = cursor + jnp.where(take, bit, 0)
                bit //= 2
            return carry

        def per_block(b, carry):
            start = pl.multiple_of(b * bm * pitch, bm * pitch)
            action(pltpu.make_async_copy(zbuf, xs_hbm.at[pl.ds(start, bm * pitch), :], zsem.at[0]))
            return carry

        lax.fori_loop(0, N_EXPERTS, per_expert, 0)
        lax.fori_loop(nused_ref[0], xs_hbm.shape[0] // (bm * pitch), per_block, 0)

    @pl.when(step == 0)
    def _():
        zbuf[...] = jnp.zeros(zbuf.shape, F32)
        fill_copies(lambda cp: cp.start())

    def body(r, carry):
        for k in range(TOP_K):
            p = pos_ref[0, 0, k * tt + r]
            pltpu.make_async_copy(h_ref.at[_tile_rows(r, pitch), :], xs_hbm.at[_tile_rows(p, pitch), :],
                                  sem.at[0]).start(priority=k)
        return carry
    lax.fori_loop(0, tt, body, 0, unroll=4)
    for k in range(TOP_K):
        pltpu.make_async_copy(h_ref, xs_hbm.at[pl.ds(0, tt * pitch), :], sem.at[0]).wait()

    @pl.when(step == pl.num_programs(0) - 1)
    def _():
        fill_copies(lambda cp: cp.wait())


def _dispatch_call(h2, pos3, n_used, padfrom, padlen, n_rows, d):
    pitch = d // LANES
    bm = EXPERT_BLOCK
    n_tiles, _, two_tt = pos3.shape
    tt = two_tt // TOP_K
    grid_spec = pltpu.PrefetchScalarGridSpec(
        num_scalar_prefetch=3,
        grid=(n_tiles,),
        in_specs=[pl.BlockSpec((1, 1, two_tt), lambda s, *_: (s, 0, 0), memory_space=pltpu.SMEM),
                  pl.BlockSpec((tt * pitch, LANES), lambda s, *_: (s, 0))],
        out_specs=pl.BlockSpec(memory_space=pl.ANY),
        scratch_shapes=[pltpu.VMEM((bm * pitch, LANES), F32), pltpu.SemaphoreType.DMA((1,)),
                        pltpu.SemaphoreType.DMA((1,))],
    )
    return pl.pallas_call(
        functools.partial(_dispatch_kernel, pitch=pitch, bm=bm),
        grid_spec=grid_spec,
        out_shape=jax.ShapeDtypeStruct((n_rows * pitch, LANES), F32),
        compiler_params=pltpu.CompilerParams(dimension_semantics=("arbitrary",),
                                             vmem_limit_bytes=VMEM_LIMIT),
        name="dispatch",
    )(n_used, padfrom, padlen, pos3, h2)


def _expert_kernel(nused_ref, nruns_ref, first_ref, run_ref, rune_ref, x_ref, wg_hbm, wu_hbm, wd_hbm, y_ref,
                   wg_f32, wu_f32, wd_f32, wg_b, wu_b, wd_b, wsems):
    i = pl.program_id(0)
    n_used = nused_ref[0]
    n_runs = nruns_ref[0]
    d = wg_b.shape[0]
    pitch = d // LANES
    bm = y_ref.shape[0] // pitch
    n_slots = wg_f32.shape[0]

    def weight_copies(e, slot):
        return [pltpu.make_async_copy(src.at[e], dst.at[slot], wsems.at[slot])
                for src, dst in ((wg_hbm, wg_f32), (wu_hbm, wu_f32), (wd_hbm, wd_f32))]

    def start_run(r):
        @pl.when(r < n_runs)
        def _():
            for prio, cp in zip((1, 1, 0), weight_copies(rune_ref[r], r % n_slots)):
                cp.start(priority=prio)

    @pl.when(i == 0)
    def _():
        for r in range(n_slots - 1):
            start_run(r)

    @pl.when((i < n_used) & (first_ref[i] == 1))
    def _():
        run = run_ref[i]
        slot = run % n_slots
        for cp in weight_copies(0, slot):
            cp.wait()
        start_run(run + n_slots - 1)

        wg_b[...] = wg_f32[slot].astype(BF16)
        wu_b[...] = wu_f32[slot].astype(BF16)
        wd_b[...] = wd_f32[slot].astype(BF16)

    @pl.when(i < n_used)
    def _():
        xb = _load_token_major(x_ref, bm, d).astype(BF16)
        gate = _dot(xb, wg_b[...])
        up = _dot(xb, wu_b[...])
        hid = (_silu(gate) * up).astype(BF16)
        _store_token_major(y_ref, _dot(hid, wd_b[...]))

    @pl.when(i >= n_used)
    def _():
        y_ref[...] = jnp.zeros(y_ref.shape, F32)


def _expert_call(xs, block_e, n_used, w_gate, w_up, w_down):
    bm = EXPERT_BLOCK
    d, de = w_gate.shape[1], w_gate.shape[2]
    pitch = d // LANES
    n_blocks = xs.shape[0] // (bm * pitch)
    blk = jnp.arange(n_blocks, dtype=jnp.int32)
    used = blk < n_used[0]
    first = jnp.concatenate([jnp.ones((1,), jnp.int32), (block_e[1:] != block_e[:-1]).astype(jnp.int32)])
    first = jnp.where(used, first, 0)
    run = (jnp.cumsum(first) - 1).astype(jnp.int32)
    n_runs = jnp.sum(first).astype(jnp.int32).reshape(1)
    eids = jnp.arange(N_EXPERTS, dtype=jnp.int32)
    run_of_e = jnp.where((first[:, None] == 1) & (block_e[:, None] == eids[None, :]), run[:, None], -1).max(axis=0)
    run_e = jnp.sum(jnp.where(run_of_e[None, :] == eids[:, None], eids[None, :], 0), axis=1).astype(jnp.int32)
    grid_spec = pltpu.PrefetchScalarGridSpec(
        num_scalar_prefetch=5,
        grid=(n_blocks,),
        in_specs=[pl.BlockSpec((bm * pitch, LANES), lambda i, nu, *_: (jnp.minimum(i, nu[0] - 1), 0)),
                  pl.BlockSpec(memory_space=pl.ANY),
                  pl.BlockSpec(memory_space=pl.ANY),
                  pl.BlockSpec(memory_space=pl.ANY)],
        out_specs=pl.BlockSpec((bm * pitch, LANES), lambda i, *_: (i, 0)),
        scratch_shapes=[pltpu.VMEM((WEIGHT_SLOTS, d, de), F32), pltpu.VMEM((WEIGHT_SLOTS, d, de), F32),
                        pltpu.VMEM((WEIGHT_SLOTS, de, d), F32),
                        pltpu.VMEM((d, de), BF16), pltpu.VMEM((d, de), BF16), pltpu.VMEM((de, d), BF16),
                        pltpu.SemaphoreType.DMA((WEIGHT_SLOTS,))],
    )
    return pl.pallas_call(
        _expert_kernel,
        grid_spec=grid_spec,
        out_shape=jax.ShapeDtypeStruct((n_blocks * bm * pitch, LANES), F32),
        compiler_params=pltpu.CompilerParams(dimension_semantics=("arbitrary",),
                                             vmem_limit_bytes=VMEM_LIMIT),
        name="experts",
    )(n_used, n_runs, first, run, run_e, xs, w_gate, w_up, w_down)


def _combine_kernel(pos_cur_ref, pos_nxt_ref, x_ref, info_ref, mod_ref, nw_ref, y_hbm, o_ref, ybuf, sems):
    tt, d = x_ref.shape[1], x_ref.shape[2]
    pitch = d // LANES
    step = pl.program_id(0) * pl.num_programs(1) + pl.program_id(1)
    n_steps = pl.num_programs(0) * pl.num_programs(1)
    slot_rows = TOP_K * tt * pitch

    def start_gather(pos_ref, slot):
        def body(r, carry):
            for k in range(TOP_K):
                p = pos_ref[0, 0, k * tt + r]
                pltpu.make_async_copy(y_hbm.at[_tile_rows(p, pitch), :],
                                      ybuf.at[_tile_rows((slot * TOP_K + k) * tt + r, pitch), :],
                                      sems.at[slot]).start(priority=k)
            return carry
        lax.fori_loop(0, tt, body, 0, unroll=4)

    @pl.when(step == 0)
    def _():
        start_gather(pos_cur_ref, 0)

    @pl.when(step + 1 < n_steps)
    def _():
        start_gather(pos_nxt_ref, (step + 1) % 2)

    info = info_ref[0]
    w1 = info[:, 2:3]
    w2 = info[:, 3:4]
    x = x_ref[0]
    gate2 = mod_ref[0][5:6]
    base = pl.multiple_of((step % 2) * slot_rows, slot_rows)
    pltpu.make_async_copy(y_hbm.at[pl.ds(0, slot_rows), :], ybuf.at[pl.ds(base, slot_rows), :],
                          sems.at[step % 2]).wait()
    moe = (_load_token_major(ybuf, tt, d, base) * w1
           + _load_token_major(ybuf, tt, d, base + tt * pitch) * w2)
    y = moe * lax.rsqrt(jnp.mean(moe * moe, axis=-1, keepdims=True) + EPS) * nw_ref[...]
    o_ref[0] = x + gate2 * y


def _combine_call(x1, info, mod3, norm_w, y, pos3):
    bsz, seq, d = x1.shape
    tt = COMBINE_TILE
    nj = seq // tt
    n_tiles = bsz * nj
    pos_blk = lambda f: pl.BlockSpec((1, 1, TOP_K * tt), f, memory_space=pltpu.SMEM)
    return pl.pallas_call(
        _combine_kernel,
        grid=(bsz, nj),
        in_specs=[pos_blk(lambda b, j: (b * nj + j, 0, 0)),
                  pos_blk(lambda b, j: (jnp.minimum(b * nj + j + 1, n_tiles - 1), 0, 0)),
                  pl.BlockSpec((1, tt, d), lambda b, j: (b, j, 0)),
                  pl.BlockSpec((1, tt, LANES), lambda b, j: (b, j, 0)),
                  pl.BlockSpec((1,) + mod3.shape[1:], lambda b, j: (b, 0, 0)),
                  pl.BlockSpec((1, d), lambda b, j: (0, 0)),
                  pl.BlockSpec(memory_space=pl.ANY)],
        out_specs=pl.BlockSpec((1, tt, d), lambda b, j: (b, j, 0)),
        out_shape=jax.ShapeDtypeStruct((bsz, seq, d), F32),
        scratch_shapes=[pltpu.VMEM((2 * TOP_K * tt * (d // LANES), LANES), F32), pltpu.SemaphoreType.DMA((2,))],
        compiler_params=pltpu.CompilerParams(dimension_semantics=("arbitrary", "arbitrary"),
                                             vmem_limit_bytes=VMEM_LIMIT),
        name="combine",
    )(pos3, pos3, x1, info, mod3, norm_w.reshape(1, d), y)


def _layer(x, mod, norm_pre_mix, norm_post_mix, w_in, dn_conv_w, dn_a_log, dn_dt_bias, dn_norm_w,
           cf_pw1_b, cf_dw_w, cf_dw_b, cf_ln_w, cf_ln_b, w_out, norm_pre_ffn, norm_post_ffn,
           w_router_group, b_router_group, w_router_expert, b_router_expert, w_gate, w_up, w_down):
    bsz, seq, d = x.shape
    t = bsz * seq
    mod3 = mod.reshape(bsz, -1, d)
    x1 = _mixer_call(x, mod3, norm_pre_mix, norm_post_mix, w_in, dn_conv_w, dn_a_log, dn_dt_bias, dn_norm_w,
                     cf_pw1_b, cf_dw_w, cf_dw_b, cf_ln_w, cf_ln_b, w_out)
    h2, info, cnt = _router_call(x1, mod3, norm_pre_ffn, w_router_group, b_router_group,
                                 w_router_expert, b_router_expert)

    bm = EXPERT_BLOCK
    info2 = info.reshape(t, LANES)
    expert_id = info2[:, 0:TOP_K].astype(jnp.int32)
    rank = info2[:, 4:4 + TOP_K].astype(jnp.int32)
    counts = cnt[0, :N_EXPERTS].astype(jnp.int32)
    padded = (counts + bm - 1) // bm * bm
    pend = jnp.cumsum(padded)
    pstart = pend - padded
    onehot = expert_id[..., None] == jnp.arange(N_EXPERTS, dtype=jnp.int32)
    pos = jnp.sum(jnp.where(onehot, pstart, 0), axis=-1) + rank

    def tiled(tt):
        return pos.reshape(t // tt, tt, TOP_K).transpose(0, 2, 1).reshape(t // tt, 1, TOP_K * tt)
    n_blocks = -(-(t * TOP_K) // bm) + N_EXPERTS
    block_start = jnp.arange(n_blocks, dtype=jnp.int32) * bm
    block_e = jnp.minimum(jnp.sum(pend[None, :] <= block_start[:, None], axis=1), N_EXPERTS - 1).astype(jnp.int32)
    n_used = (pend[-1] // bm).astype(jnp.int32).reshape(1)

    xs = _dispatch_call(h2, tiled(DISPATCH_TILE), n_used, (pstart + counts).astype(jnp.int32),
                        (padded - counts).astype(jnp.int32), n_blocks * bm, d)
    y = _expert_call(xs, block_e, n_used, w_gate, w_up, w_down)
    return _combine_call(x1, info, mod3, norm_post_ffn, y, tiled(COMBINE_TILE))


def kernel(x, c, w_ada, b_ada, norm_pre_mix, norm_post_mix, w_in, dn_conv_w, dn_a_log, dn_dt_bias, dn_norm_w,
           cf_pw1_b, cf_dw_w, cf_dw_b, cf_ln_w, cf_ln_b, w_out, norm_pre_ffn, norm_post_ffn,
           w_router_group, b_router_group, w_router_expert, b_router_expert, w_gate, w_up, w_down):
    depth = w_ada.shape[0]
    for l in range(depth):
        mod = _ada_call(c, w_ada[l], b_ada[l])
        x = _layer(x, mod, norm_pre_mix[l], norm_post_mix[l], w_in[l], dn_conv_w[l], dn_a_log[l],
                   dn_dt_bias[l], dn_norm_w[l], cf_pw1_b[l], cf_dw_w[l], cf_dw_b[l], cf_ln_w[l], cf_ln_b[l],
                   w_out[l], norm_pre_ffn[l], norm_post_ffn[l], w_router_group[l], b_router_group[l],
                   w_router_expert[l], b_router_expert[l], w_gate[l], w_up[l], w_down[l])
    return x
```

```python
import functools

import jax
import jax.numpy as jnp
from jax import lax
from jax.experimental import pallas as pl
from jax.experimental.pallas import tpu as pltpu

F32 = jnp.float32
BF16 = jnp.bfloat16
EPS = 1e-6

DN_HEADS = 4
HEAD_DIM = 128
DN_WIDTH = DN_HEADS * HEAD_DIM
DN_CONV = 4
DN_CHUNK = 64
CF_KERNEL = 31
N_GROUPS = 8
EXPERTS_PER_GROUP = 8
N_EXPERTS = N_GROUPS * EXPERTS_PER_GROUP
TOP_K = 2

LANES = 128
SUBLANES = 8
SEQ_TILE = 256
CONV_ROWS = 32
ROUTER_TILE = 512
EXPERT_BLOCK = 256
COMBINE_TILE = 512
WEIGHT_SLOTS = 3
DISPATCH_TILE = 1024
VMEM_LIMIT = 56 * 1024 * 1024


def _dot(a, b):
    return jnp.dot(a, b, preferred_element_type=F32)


def _dot_nt(a, b):
    return lax.dot_general(a, b, (((1,), (1,)), ((), ())), preferred_element_type=F32)


def _dot_tn(a, b):
    return lax.dot_general(a, b, (((0,), (0,)), ((), ())), preferred_element_type=F32)


def _split3(x):
    hi = x.astype(BF16)
    r1 = x - hi.astype(F32)
    mid = r1.astype(BF16)
    lo = (r1 - mid.astype(F32)).astype(BF16)
    return hi, mid, lo


def _silu(x):
    return x * jax.nn.sigmoid(x)


def _softplus(x):
    return jnp.maximum(x, 0.0) + jnp.log1p(jnp.exp(-jnp.abs(x)))


def _store_token_major(ref, val, base=0):
    n, d = val.shape
    pitch = d // LANES
    for j in range(pitch):
        ref[pl.ds(base + j, n, stride=pitch), :] = val[:, j * LANES:(j + 1) * LANES]


def _load_token_major(ref, n, d, base=0):
    pitch = d // LANES
    return jnp.concatenate([ref[pl.ds(base + j, n, stride=pitch), :] for j in range(pitch)], axis=1)


def _ada_kernel(c_ref, w_ref, b_ref, o_ref):
    c = c_ref[...]
    ca = _silu(c)
    c_hi, c_lo, _ = _split3(ca)
    w_hi, w_lo, _ = _split3(w_ref[...])
    o_ref[...] = _dot(c_hi, w_hi) + (_dot(c_hi, w_lo) + _dot(c_lo, w_hi)) + b_ref[...]


def _ada_call(c, w, b):
    bsz, d = c.shape
    n = w.shape[1]
    tn = 512
    return pl.pallas_call(
        _ada_kernel,
        grid=(n // tn,),
        in_specs=[pl.BlockSpec((bsz, d), lambda i: (0, 0)),
                  pl.BlockSpec((d, tn), lambda i: (0, i)),
                  pl.BlockSpec((1, tn), lambda i: (0, i))],
        out_specs=pl.BlockSpec((bsz, tn), lambda i: (0, i)),
        out_shape=jax.ShapeDtypeStruct((bsz, n), F32),
        compiler_params=pltpu.CompilerParams(dimension_semantics=("arbitrary",),
                                             vmem_limit_bytes=VMEM_LIMIT),
        name="ada",
    )(c, w, b.reshape(1, n))


def _time_perm(ts, transpose):
    ri = lax.broadcasted_iota(jnp.int32, (ts, ts), 0)
    ci = lax.broadcasted_iota(jnp.int32, (ts, ts), 1)
    strided, natural = (ci, ri) if transpose else (ri, ci)
    return jnp.where(natural == (ts // SUBLANES) * (strided % SUBLANES) + strided // SUBLANES, 1.0, 0.0).astype(BF16)


def _fill_conv_window(ext_ref, prev_ref, cur, n_taps):
    ts = cur.shape[0]
    lead = (n_taps - 1) * SUBLANES
    tail = cur[ts - lead:, :]
    sub = lax.broadcasted_iota(jnp.int32, tail.shape, 0) % SUBLANES
    merged = jnp.where(sub == SUBLANES - 1, prev_ref[...], tail)
    for g in range(n_taps - 1):
        rows = slice(g * SUBLANES, (g + 1) * SUBLANES)
        ext_ref[rows, :] = pltpu.roll(merged[rows, :], 1, 0)
    ext_ref[lead:lead + ts, :] = cur
    prev_ref[...] = tail


def _conv_block(ext_ref, w_ref, n_taps, r0, row_blk, c0, col_blk):
    groups = row_blk // SUBLANES
    acc = [jnp.zeros((SUBLANES, col_blk), F32) for _ in range(groups)]
    for k in range(n_taps):
        w = w_ref[k * SUBLANES:(k + 1) * SUBLANES, c0:c0 + col_blk]
        for g in range(groups):
            lo = r0 + (k + g) * SUBLANES
            acc[g] = acc[g] + w * ext_ref[lo:lo + SUBLANES, c0:c0 + col_blk]
    return jnp.concatenate(acc, axis=0)


def _mixer_kernel(x_ref, mod_ref, npre_ref, npost_ref, win_ref,
                  convw_ref, alog_ref, dtb_ref, dnw_ref, pw1b_ref, dww_ref, dwb_ref, lnw_ref, lnb_ref,
                  woutf_ref, o_ref, qkv_ext, qkv_prev, qkv_p16, qkv_act, cf_ext, cf_prev, cf_p16, state, mixed,
                  consts, wqkv_ref, wz_ref, wba_ref, wcf_ref, wout_ref):
    ts = x_ref.shape[1]
    n_chunks = ts // DN_CHUNK

    @pl.when(pl.program_id(1) == 0)
    def _():
        qkv_prev[...] = jnp.zeros(qkv_prev.shape, F32)
        cf_prev[...] = jnp.zeros(cf_prev.shape, F32)
        state[...] = jnp.zeros(state.shape, F32)

    @pl.when((pl.program_id(0) == 0) & (pl.program_id(1) == 0))
    def _():
        ri = lax.broadcasted_iota(jnp.int32, (ts, ts), 0)
        ci = lax.broadcasted_iota(jnp.int32, (ts, ts), 1)
        consts[0] = _time_perm(ts, False)
        consts[1] = _time_perm(ts, True)
        consts[2] = jnp.where((ri // DN_CHUNK == ci // DN_CHUNK) & (ci <= ri), 1.0, 0.0).astype(BF16)
        n_qkv, n_z, n_ba = wqkv_ref.shape[1], wz_ref.shape[1], 2 * DN_HEADS
        for c0 in range(0, n_qkv, DN_WIDTH):
            wqkv_ref[:, c0:c0 + DN_WIDTH] = win_ref[:, c0:c0 + DN_WIDTH].astype(BF16)
        wz_ref[...] = win_ref[:, n_qkv:n_qkv + n_z].astype(BF16)
        lane = lax.broadcasted_iota(jnp.int32, wba_ref.shape, 1)
        wba_ref[...] = jnp.where(lane < n_ba, win_ref[:, n_qkv + n_z:n_qkv + n_z + LANES], 0.0).astype(BF16)
        cf0 = n_qkv + n_z + n_ba
        for c0 in range(0, wcf_ref.shape[1], DN_WIDTH):
            wcf_ref[:, c0:c0 + DN_WIDTH] = win_ref[:, cf0 + c0:cf0 + c0 + DN_WIDTH].astype(BF16)
        wout_ref[...] = woutf_ref[...].astype(BF16)

    x = x_ref[0]
    mod = mod_ref[0]
    shift1, scale1, gate1 = mod[0:1], mod[1:2], mod[2:3]
    h = x * lax.rsqrt(jnp.mean(x * x, axis=-1, keepdims=True) + EPS)
    h = h * npre_ref[...] * (1.0 + scale1) + shift1
    hb = h.astype(BF16)
    hb_st = _dot(consts[0], hb).astype(BF16)
    to_natural = consts[1]

    _fill_conv_window(qkv_ext, qkv_prev, _dot(hb_st, wqkv_ref[...]), DN_CONV)
    for r0 in range(0, ts, CONV_ROWS):
        for c0 in range(0, qkv_ext.shape[1], DN_WIDTH):
            blk = _conv_block(qkv_ext, convw_ref, DN_CONV, r0, CONV_ROWS, c0, DN_WIDTH)
            qkv_p16[r0:r0 + CONV_ROWS, c0:c0 + DN_WIDTH] = _silu(blk).astype(BF16)
    qkv_act[...] = _dot(to_natural, qkv_p16[...])

    cf_pre = _dot(hb_st, wcf_ref[...]) + pw1b_ref[...]
    cfw = cf_ext.shape[1]
    _fill_conv_window(cf_ext, cf_prev, cf_pre[:, :cfw] * jax.nn.sigmoid(cf_pre[:, cfw:]), CF_KERNEL)
    cf_pending = list(range(0, ts, CONV_ROWS))

    def cf_step(n=1):
        for _ in range(min(n, len(cf_pending))):
            r0 = cf_pending.pop(0)
            cf = _conv_block(cf_ext, dww_ref, CF_KERNEL, r0, CONV_ROWS, 0, cfw) + dwb_ref[...]
            mu = jnp.mean(cf, axis=-1, keepdims=True)
            xc = cf - mu
            var = jnp.mean(xc * xc, axis=-1, keepdims=True)
            cfn = xc * lax.rsqrt(var + EPS) * lnw_ref[...] + lnb_ref[...]
            cf_p16[r0:r0 + CONV_ROWS, :] = _silu(cfn).astype(BF16)

    z = _dot(hb, wz_ref[...])
    ba = _dot(hb, wba_ref[...])
    beta_all = jax.nn.sigmoid(ba)
    g_all = -jnp.exp(alog_ref[...]) * _softplus(ba + dtb_ref[...])

    tri = consts[2]
    g_hi, g_mid, g_lo = _split3(g_all)
    gcum = _dot(tri, g_hi) + _dot(tri, g_mid) + _dot(tri, g_lo)
    gcum_t = gcum.T
    exp_g = jnp.exp(gcum)

    r64 = lax.broadcasted_iota(jnp.int32, (DN_CHUNK, DN_CHUNK), 0)
    c64 = lax.broadcasted_iota(jnp.int32, (DN_CHUNK, DN_CHUNK), 1)
    causal = c64 <= r64
    strict = c64 < r64
    eye = jnp.where(c64 == r64, 1.0, 0.0).astype(F32)
    dnw = dnw_ref[...]

    heads = []
    for hd in range(DN_HEADS):
        lo = hd * HEAD_DIM
        qh = qkv_act[:, lo:lo + HEAD_DIM]
        kh = qkv_act[:, DN_WIDTH + lo:DN_WIDTH + lo + HEAD_DIM]
        vh = qkv_act[:, 2 * DN_WIDTH + lo:2 * DN_WIDTH + lo + HEAD_DIM]
        qn = qh * lax.rsqrt(jnp.sum(qh * qh, axis=-1, keepdims=True) + EPS) * (HEAD_DIM ** -0.5)
        kn = kh * lax.rsqrt(jnp.sum(kh * kh, axis=-1, keepdims=True) + EPS)
        beta_h = beta_all[:, hd:hd + 1]
        gc_h = gcum[:, DN_HEADS + hd:DN_HEADS + hd + 1]
        eg_h = exp_g[:, DN_HEADS + hd:DN_HEADS + hd + 1]
        k_beta = kn * beta_h
        heads.append(dict(qn=qn, kn=kn, k_beta=k_beta, v_beta=vh * beta_h, kbg=k_beta * eg_h, qg=qn * eg_h,
                          gc=gc_h, zg=_silu(z[:, lo:lo + HEAD_DIM])))

    cells = [(hd, ch) for hd in range(DN_HEADS) for ch in range(n_chunks)]
    rows = lambda ch: slice(ch * DN_CHUNK, (ch + 1) * DN_CHUNK)

    decay, kq = {}, {}
    for hd, ch in cells:
        hv, sl = heads[hd], rows(ch)
        gc_row = gcum_t[DN_HEADS + hd:DN_HEADS + hd + 1, sl]
        decay[hd, ch] = jnp.where(causal, jnp.exp(hv["gc"][sl] - gc_row), 0.0)
        lhs = jnp.concatenate([hv["k_beta"][sl], hv["qn"][sl]], axis=0).astype(BF16)
        kq[hd, ch] = _dot_nt(lhs, hv["kn"][sl].astype(BF16))
    a = {c: jnp.where(strict, kq[c][:DN_CHUNK] * decay[c], 0.0) for c in cells}
    attn = {c: (kq[c][DN_CHUNK:] * decay[c]).astype(BF16) for c in cells}

    t_inv = {c: eye - a[c] for c in cells}
    pw = {c: a[c].astype(BF16) for c in cells}
    for _ in range(5):
        pw = {c: _dot(pw[c], pw[c]).astype(BF16) for c in cells}
        t_inv = {c: t_inv[c] + _dot(t_inv[c].astype(BF16), pw[c]) for c in cells}
        cf_step()

    sol, aw, ks, glast = {}, {}, {}, {}
    for hd, ch in cells:
        hv, sl = heads[hd], rows(ch)
        rhs = jnp.concatenate([hv["v_beta"][sl], hv["kbg"][sl]], axis=1).astype(BF16)
        sol[hd, ch] = _dot(t_inv[hd, ch].astype(BF16), rhs).astype(BF16)
    for hd, ch in cells:
        hv, sl = heads[hd], rows(ch)
        gc_col = hv["gc"][sl]
        glast[hd, ch] = gc_col[DN_CHUNK - 1:DN_CHUNK]
        k_dec = (hv["kn"][sl] * jnp.exp(glast[hd, ch] - gc_col)).astype(BF16)
        aw[hd, ch] = _dot(attn[hd, ch], sol[hd, ch])
        ks[hd, ch] = _dot_tn(k_dec, sol[hd, ch])

    s_in = {}
    s_cur = [state[hd] for hd in range(DN_HEADS)]
    for ch in range(n_chunks):
        for hd in range(DN_HEADS):
            s_in[hd, ch] = s_cur[hd].astype(BF16)
            kd_u, kd_w = ks[hd, ch][:, :HEAD_DIM], ks[hd, ch][:, HEAD_DIM:]
            s_cur[hd] = s_cur[hd] * jnp.exp(glast[hd, ch]) + kd_u - _dot(kd_w.astype(BF16), s_in[hd, ch])
        cf_step()
    for hd in range(DN_HEADS):
        state[hd] = s_cur[hd]

    for hd, ch in cells:
        hv, sl = heads[hd], rows(ch)
        lo = hd * HEAD_DIM
        q_eff = (hv["qg"][sl] - aw[hd, ch][:, HEAD_DIM:]).astype(BF16)
        o = _dot(q_eff, s_in[hd, ch]) + aw[hd, ch][:, :HEAD_DIM]
        on = o * lax.rsqrt(jnp.mean(o * o, axis=-1, keepdims=True) + EPS) * dnw * hv["zg"][sl]
        mixed[sl, lo:lo + HEAD_DIM] = on.astype(BF16)

    cf_step(len(cf_pending))
    mixed[:, DN_WIDTH:DN_WIDTH + cfw] = _dot(to_natural, cf_p16[...]).astype(BF16)

    out = _dot(mixed[...], wout_ref[...])
    y = out * lax.rsqrt(jnp.mean(out * out, axis=-1, keepdims=True) + EPS) * npost_ref[...]
    o_ref[0] = x + gate1 * y


def _mixer_call(x, mod3, npre, npost, w_in, dn_conv_w, dn_a_log, dn_dt_bias, dn_norm_w,
                cf_pw1_b, cf_dw_w, cf_dw_b, cf_ln_w, cf_ln_b, w_out):
    bsz, seq, d = x.shape
    ts = SEQ_TILE
    cfw = cf_dw_w.shape[1]
    n_qkv = 3 * DN_WIDTH
    alog = jnp.pad(dn_a_log, (DN_HEADS, LANES - 2 * DN_HEADS)).reshape(1, LANES)
    dtb = jnp.pad(dn_dt_bias, (DN_HEADS, LANES - 2 * DN_HEADS)).reshape(1, LANES)
    dww = jnp.repeat(cf_dw_w, SUBLANES, axis=0)
    convw = jnp.repeat(dn_conv_w, SUBLANES, axis=0)

    def full(a):
        mode = dict(pipeline_mode=pl.Buffered(1)) if a.size >= d * d else {}
        return pl.BlockSpec(a.shape, lambda b, j: (0,) * a.ndim, **mode)

    row = lambda a: a.reshape(1, -1)
    operands = [x, mod3, row(npre), row(npost), w_in, convw, alog, dtb, row(dn_norm_w),
                row(cf_pw1_b), dww, row(cf_dw_b), row(cf_ln_w), row(cf_ln_b), w_out]
    in_specs = [pl.BlockSpec((1, ts, d), lambda b, j: (b, j, 0)),
                pl.BlockSpec((1,) + mod3.shape[1:], lambda b, j: (b, 0, 0))]
    in_specs += [full(a) for a in operands[2:]]
    return pl.pallas_call(
        _mixer_kernel,
        grid=(bsz, seq // ts),
        in_specs=in_specs,
        out_specs=pl.BlockSpec((1, ts, d), lambda b, j: (b, j, 0)),
        out_shape=jax.ShapeDtypeStruct((bsz, seq, d), F32),
        scratch_shapes=[pltpu.VMEM(((DN_CONV - 1) * SUBLANES + ts, n_qkv), F32),
                        pltpu.VMEM(((DN_CONV - 1) * SUBLANES, n_qkv), F32),
                        pltpu.VMEM((ts, n_qkv), BF16),
                        pltpu.VMEM((ts, n_qkv), F32),
                        pltpu.VMEM(((CF_KERNEL - 1) * SUBLANES + ts, cfw), F32),
                        pltpu.VMEM(((CF_KERNEL - 1) * SUBLANES, cfw), F32),
                        pltpu.VMEM((ts, cfw), BF16),
                        pltpu.VMEM((DN_HEADS, HEAD_DIM, HEAD_DIM), F32),
                        pltpu.VMEM((ts, DN_WIDTH + cfw), BF16),
                        pltpu.VMEM((3, ts, ts), BF16),
                        pltpu.VMEM((d, n_qkv), BF16), pltpu.VMEM((d, DN_WIDTH), BF16), pltpu.VMEM((d, LANES), BF16),
                        pltpu.VMEM((d, 2 * cfw), BF16), pltpu.VMEM(w_out.shape, BF16)],
        compiler_params=pltpu.CompilerParams(dimension_semantics=("arbitrary", "arbitrary"),
                                             vmem_limit_bytes=VMEM_LIMIT),
        name="mixer",
    )(*operands)


def _router_kernel(x_ref, mod_ref, nw_ref, wr_ref, br_ref, h_ref, info_ref, cnt_ref, carry, strict_ref):
    tt = x_ref.shape[1]

    @pl.when((pl.program_id(0) == 0) & (pl.program_id(1) == 0))
    def _():
        carry[...] = jnp.zeros(carry.shape, F32)
        rr = lax.broadcasted_iota(jnp.int32, (tt, tt), 0)
        cc = lax.broadcasted_iota(jnp.int32, (tt, tt), 1)
        strict_ref[...] = jnp.where(cc < rr, 1.0, 0.0).astype(BF16)

    x = x_ref[0]
    mod = mod_ref[0]
    shift2, scale2 = mod[3:4], mod[4:5]
    h = x * lax.rsqrt(jnp.mean(x * x, axis=-1, keepdims=True) + EPS)
    h = h * nw_ref[...] * (1.0 + scale2) + shift2
    _store_token_major(h_ref, h)

    h_hi, h_lo, _ = _split3(h)
    w_hi, w_lo, _ = _split3(wr_ref[...])
    logits = _dot(h_hi, w_hi) + (_dot(h_hi, w_lo) + _dot(h_lo, w_hi)) + br_ref[...]

    lane = lax.broadcasted_iota(jnp.int32, (tt, LANES), 1)
    neg = -jnp.inf
    is_grp = (lane >= N_EXPERTS) & (lane < N_EXPERTS + N_GROUPS)
    gl = jnp.where(is_grp, logits, neg)
    gmax = jnp.max(gl, axis=-1, keepdims=True)
    gsum = jnp.sum(jnp.where(is_grp, jnp.exp(gl - gmax), 0.0), axis=-1, keepdims=True)
    grp_p = 1.0 / gsum
    grp_lane = jnp.min(jnp.where(is_grp & (gl == gmax), lane, LANES), axis=-1, keepdims=True)
    grp_idx = grp_lane - N_EXPERTS

    in_grp = (lane < N_EXPERTS) & (lane // EXPERTS_PER_GROUP == grp_idx)
    el = jnp.where(in_grp, logits, neg)
    m1 = jnp.max(el, axis=-1, keepdims=True)
    e1 = jnp.min(jnp.where(in_grp & (el == m1), lane, LANES), axis=-1, keepdims=True)
    el2 = jnp.where(lane == e1, neg, el)
    m2 = jnp.max(el2, axis=-1, keepdims=True)
    e2 = jnp.min(jnp.where(in_grp & (lane != e1) & (el2 == m2), lane, LANES), axis=-1, keepdims=True)
    r = jnp.exp(m2 - m1)
    w1 = grp_p / (1.0 + r)
    w2 = grp_p * r / (1.0 + r)

    hit1 = lane == e1
    hit2 = lane == e2
    onehot = jnp.where(hit1 | hit2, 1.0, 0.0)
    prefix = _dot(strict_ref[...], onehot.astype(BF16)) + carry[0:1, :]
    rank1 = jnp.sum(jnp.where(hit1, prefix, 0.0), axis=-1, keepdims=True)
    rank2 = jnp.sum(jnp.where(hit2, prefix, 0.0), axis=-1, keepdims=True)
    total = carry[0:1, :] + jnp.sum(onehot, axis=0, keepdims=True)
    carry[...] = jnp.broadcast_to(total, carry.shape)
    cnt_ref[...] = jnp.broadcast_to(total, cnt_ref.shape)

    info = jnp.where(lane == 0, e1.astype(F32), 0.0)
    info = jnp.where(lane == 1, e2.astype(F32), info)
    info = jnp.where(lane == 2, w1, info)
    info = jnp.where(lane == 3, w2, info)
    info = jnp.where(lane == 4, rank1, info)
    info = jnp.where(lane == 5, rank2, info)
    info_ref[0] = info


def _router_call(x1, mod3, norm_w, w_router_group, b_router_group, w_router_expert, b_router_expert):
    bsz, seq, d = x1.shape
    tt = ROUTER_TILE
    nj = seq // tt
    pitch = d // LANES
    pad = LANES - N_EXPERTS - N_GROUPS
    wr = jnp.pad(jnp.concatenate([w_router_expert, w_router_group], axis=1), ((0, 0), (0, pad)))
    br = jnp.pad(jnp.concatenate([b_router_expert, b_router_group]), (0, pad)).reshape(1, LANES)
    return pl.pallas_call(
        _router_kernel,
        grid=(bsz, seq // tt),
        in_specs=[pl.BlockSpec((1, tt, d), lambda b, j: (b, j, 0)),
                  pl.BlockSpec((1,) + mod3.shape[1:], lambda b, j: (b, 0, 0)),
                  pl.BlockSpec((1, d), lambda b, j: (0, 0)),
                  pl.BlockSpec((d, LANES), lambda b, j: (0, 0)),
                  pl.BlockSpec((1, LANES), lambda b, j: (0, 0))],
        out_specs=[pl.BlockSpec((tt * pitch, LANES), lambda b, j: (b * nj + j, 0)),
                   pl.BlockSpec((1, tt, LANES), lambda b, j: (b, j, 0)),
                   pl.BlockSpec((8, LANES), lambda b, j: (0, 0))],
        out_shape=[jax.ShapeDtypeStruct((bsz * seq * pitch, LANES), F32),
                   jax.ShapeDtypeStruct((bsz, seq, LANES), F32),
                   jax.ShapeDtypeStruct((8, LANES), F32)],
        scratch_shapes=[pltpu.VMEM((8, LANES), F32), pltpu.VMEM((tt, tt), BF16)],
        compiler_params=pltpu.CompilerParams(dimension_semantics=("arbitrary", "arbitrary"),
                                             vmem_limit_bytes=VMEM_LIMIT),
        name="router",
    )(x1, mod3, norm_w.reshape(1, d), wr, br)


def _tile_rows(idx, pitch):
    return pl.ds(pl.multiple_of(idx * pitch, pitch), pitch)


def _dispatch_kernel(nused_ref, padfrom_ref, padlen_ref, pos_ref, h_ref, xs_hbm, zbuf, zsem, sem, *, pitch, bm):
    step = pl.program_id(0)
    tt = h_ref.shape[0] // pitch

    def fill_copies(action):
        def per_expert(e, carry):
            cursor = padfrom_ref[e]
            n = padlen_ref[e]
            bit = bm // 2
            while bit >= 1:
                take = (n & bit) != 0

                @pl.when(take)
                def _(cursor=cursor, bit=bit):
                    action(pltpu.make_async_copy(zbuf.at[pl.ds(0, bit * pitch), :],
                                                 xs_hbm.at[pl.ds(pl.multiple_of(cursor * pitch, pitch), bit * pitch), :],
                                                 zsem.at[0]))
                cursor = cursor + jnp.where(take, bit, 0)
                bit //= 2
            return carry

        def per_block(b, carry):
            start = pl.multiple_of(b * bm * pitch, bm * pitch)
            action(pltpu.make_async_copy(zbuf, xs_hbm.at[pl.ds(start, bm * pitch), :], zsem.at[0]))
            return carry

        lax.fori_loop(0, N_EXPERTS, per_expert, 0)
        lax.fori_loop(nused_ref[0], xs_hbm.shape[0] // (bm * pitch), per_block, 0)

    @pl.when(step == 0)
    def _():
        zbuf[...] = jnp.zeros(zbuf.shape, F32)
        fill_copies(lambda cp: cp.start())

    def body(r, carry):
        for k in range(TOP_K):
            p = pos_ref[0, 0, k * tt + r]
            pltpu.make_async_copy(h_ref.at[_tile_rows(r, pitch), :], xs_hbm.at[_tile_rows(p, pitch), :],
                                  sem.at[0]).start(priority=k)
        return carry
    lax.fori_loop(0, tt, body, 0, unroll=4)
    for k in range(TOP_K):
        pltpu.make_async_copy(h_ref, xs_hbm.at[pl.ds(0, tt * pitch), :], sem.at[0]).wait()

    @pl.when(step == pl.num_programs(0) - 1)
    def _():
        fill_copies(lambda cp: cp.wait())


def _dispatch_call(h2, pos3, n_used, padfrom, padlen, n_rows, d):
    pitch = d // LANES
    bm = EXPERT_BLOCK
    n_tiles, _, two_tt = pos3.shape
    tt = two_tt // TOP_K
    grid_spec = pltpu.PrefetchScalarGridSpec(
        num_scalar_prefetch=3,
        grid=(n_tiles,),
        in_specs=[pl.BlockSpec((1, 1, two_tt), lambda s, *_: (s, 0, 0), memory_space=pltpu.SMEM),
                  pl.BlockSpec((tt * pitch, LANES), lambda s, *_: (s, 0))],
        out_specs=pl.BlockSpec(memory_space=pl.ANY),
        scratch_shapes=[pltpu.VMEM((bm * pitch, LANES), F32), pltpu.SemaphoreType.DMA((1,)),
                        pltpu.SemaphoreType.DMA((1,))],
    )
    return pl.pallas_call(
        functools.partial(_dispatch_kernel, pitch=pitch, bm=bm),
        grid_spec=grid_spec,
        out_shape=jax.ShapeDtypeStruct((n_rows * pitch, LANES), F32),
        compiler_params=pltpu.CompilerParams(dimension_semantics=("arbitrary",),
                                             vmem_limit_bytes=VMEM_LIMIT),
        name="dispatch",
    )(n_used, padfrom, padlen, pos3, h2)


def _expert_kernel(nused_ref, nruns_ref, first_ref, run_ref, rune_ref, x_ref, wg_hbm, wu_hbm, wd_hbm, y_ref,
                   wg_f32, wu_f32, wd_f32, wg_b, wu_b, wd_b, wsems):
    i = pl.program_id(0)
    n_used = nused_ref[0]
    n_runs = nruns_ref[0]
    d = wg_b.shape[0]
    pitch = d // LANES
    bm = y_ref.shape[0] // pitch
    n_slots = wg_f32.shape[0]

    def weight_copies(e, slot):
        return [pltpu.make_async_copy(src.at[e], dst.at[slot], wsems.at[slot])
                for src, dst in ((wg_hbm, wg_f32), (wu_hbm, wu_f32), (wd_hbm, wd_f32))]

    def start_run(r):
        @pl.when(r < n_runs)
        def _():
            for prio, cp in zip((1, 1, 0), weight_copies(rune_ref[r], r % n_slots)):
                cp.start(priority=prio)

    @pl.when(i == 0)
    def _():
        for r in range(n_slots - 1):
            start_run(r)

    @pl.when((i < n_used) & (first_ref[i] == 1))
    def _():
        run = run_ref[i]
        slot = run % n_slots
        for cp in weight_copies(0, slot):
            cp.wait()
        start_run(run + n_slots - 1)

        wg_b[...] = wg_f32[slot].astype(BF16)
        wu_b[...] = wu_f32[slot].astype(BF16)
        wd_b[...] = wd_f32[slot].astype(BF16)

    @pl.when(i < n_used)
    def _():
        xb = _load_token_major(x_ref, bm, d).astype(BF16)
        gate = _dot(xb, wg_b[...])
        up = _dot(xb, wu_b[...])
        hid = (_silu(gate) * up).astype(BF16)
        _store_token_major(y_ref, _dot(hid, wd_b[...]))

    @pl.when(i >= n_used)
    def _():
        y_ref[...] = jnp.zeros(y_ref.shape, F32)


def _expert_call(xs, block_e, n_used, w_gate, w_up, w_down):
    bm = EXPERT_BLOCK
    d, de = w_gate.shape[1], w_gate.shape[2]
    pitch = d // LANES
    n_blocks = xs.shape[0] // (bm * pitch)
    blk = jnp.arange(n_blocks, dtype=jnp.int32)
    used = blk < n_used[0]
    first = jnp.concatenate([jnp.ones((1,), jnp.int32), (block_e[1:] != block_e[:-1]).astype(jnp.int32)])
    first = jnp.where(used, first, 0)
    run = (jnp.cumsum(first) - 1).astype(jnp.int32)
    n_runs = jnp.sum(first).astype(jnp.int32).reshape(1)
    eids = jnp.arange(N_EXPERTS, dtype=jnp.int32)
    run_of_e = jnp.where((first[:, None] == 1) & (block_e[:, None] == eids[None, :]), run[:, None], -1).max(axis=0)
    run_e = jnp.sum(jnp.where(run_of_e[None, :] == eids[:, None], eids[None, :], 0), axis=1).astype(jnp.int32)
    grid_spec = pltpu.PrefetchScalarGridSpec(
        num_scalar_prefetch=5,
        grid=(n_blocks,),
        in_specs=[pl.BlockSpec((bm * pitch, LANES), lambda i, nu, *_: (jnp.minimum(i, nu[0] - 1), 0)),
                  pl.BlockSpec(memory_space=pl.ANY),
                  pl.BlockSpec(memory_space=pl.ANY),
                  pl.BlockSpec(memory_space=pl.ANY)],
        out_specs=pl.BlockSpec((bm * pitch, LANES), lambda i, *_: (i, 0)),
        scratch_shapes=[pltpu.VMEM((WEIGHT_SLOTS, d, de), F32), pltpu.VMEM((WEIGHT_SLOTS, d, de), F32),
                        pltpu.VMEM((WEIGHT_SLOTS, de, d), F32),
                        pltpu.VMEM((d, de), BF16), pltpu.VMEM((d, de), BF16), pltpu.VMEM((de, d), BF16),
                        pltpu.SemaphoreType.DMA((WEIGHT_SLOTS,))],
    )
    return pl.pallas_call(
        _expert_kernel,
        grid_spec=grid_spec,
        out_shape=jax.ShapeDtypeStruct((n_blocks * bm * pitch, LANES), F32),
        compiler_params=pltpu.CompilerParams(dimension_semantics=("arbitrary",),
                                             vmem_limit_bytes=VMEM_LIMIT),
        name="experts",
    )(n_used, n_runs, first, run, run_e, xs, w_gate, w_up, w_down)


def _combine_kernel(pos_cur_ref, pos_nxt_ref, x_ref, info_ref, mod_ref, nw_ref, y_hbm, o_ref, ybuf, sems):
    tt, d = x_ref.shape[1], x_ref.shape[2]
    pitch = d // LANES
    step = pl.program_id(0) * pl.num_programs(1) + pl.program_id(1)
    n_steps = pl.num_programs(0) * pl.num_programs(1)
    slot_rows = TOP_K * tt * pitch

    def start_gather(pos_ref, slot):
        def body(r, carry):
            for k in range(TOP_K):
                p = pos_ref[0, 0, k * tt + r]
                pltpu.make_async_copy(y_hbm.at[_tile_rows(p, pitch), :],
                                      ybuf.at[_tile_rows((slot * TOP_K + k) * tt + r, pitch), :],
                                      sems.at[slot]).start(priority=k)
            return carry
        lax.fori_loop(0, tt, body, 0, unroll=4)

    @pl.when(step == 0)
    def _():
        start_gather(pos_cur_ref, 0)

    @pl.when(step + 1 < n_steps)
    def _():
        start_gather(pos_nxt_ref, (step + 1) % 2)

    info = info_ref[0]
    w1 = info[:, 2:3]
    w2 = info[:, 3:4]
    x = x_ref[0]
    gate2 = mod_ref[0][5:6]
    base = pl.multiple_of((step % 2) * slot_rows, slot_rows)
    pltpu.make_async_copy(y_hbm.at[pl.ds(0, slot_rows), :], ybuf.at[pl.ds(base, slot_rows), :],
                          sems.at[step % 2]).wait()
    moe = (_load_token_major(ybuf, tt, d, base) * w1
           + _load_token_major(ybuf, tt, d, base + tt * pitch) * w2)
    y = moe * lax.rsqrt(jnp.mean(moe * moe, axis=-1, keepdims=True) + EPS) * nw_ref[...]
    o_ref[0] = x + gate2 * y


def _combine_call(x1, info, mod3, norm_w, y, pos3):
    bsz, seq, d = x1.shape
    tt = COMBINE_TILE
    nj = seq // tt
    n_tiles = bsz * nj
    pos_blk = lambda f: pl.BlockSpec((1, 1, TOP_K * tt), f, memory_space=pltpu.SMEM)
    return pl.pallas_call(
        _combine_kernel,
        grid=(bsz, nj),
        in_specs=[pos_blk(lambda b, j: (b * nj + j, 0, 0)),
                  pos_blk(lambda b, j: (jnp.minimum(b * nj + j + 1, n_tiles - 1), 0, 0)),
                  pl.BlockSpec((1, tt, d), lambda b, j: (b, j, 0)),
                  pl.BlockSpec((1, tt, LANES), lambda b, j: (b, j, 0)),
                  pl.BlockSpec((1,) + mod3.shape[1:], lambda b, j: (b, 0, 0)),
                  pl.BlockSpec((1, d), lambda b, j: (0, 0)),
                  pl.BlockSpec(memory_space=pl.ANY)],
        out_specs=pl.BlockSpec((1, tt, d), lambda b, j: (b, j, 0)),
        out_shape=jax.ShapeDtypeStruct((bsz, seq, d), F32),
        scratch_shapes=[pltpu.VMEM((2 * TOP_K * tt * (d // LANES), LANES), F32), pltpu.SemaphoreType.DMA((2,))],
        compiler_params=pltpu.CompilerParams(dimension_semantics=("arbitrary", "arbitrary"),
                                             vmem_limit_bytes=VMEM_LIMIT),
        name="combine",
    )(pos3, pos3, x1, info, mod3, norm_w.reshape(1, d), y)


def _layer(x, mod, norm_pre_mix, norm_post_mix, w_in, dn_conv_w, dn_a_log, dn_dt_bias, dn_norm_w,
           cf_pw1_b, cf_dw_w, cf_dw_b, cf_ln_w, cf_ln_b, w_out, norm_pre_ffn, norm_post_ffn,
           w_router_group, b_router_group, w_router_expert, b_router_expert, w_gate, w_up, w_down):
    bsz, seq, d = x.shape
    t = bsz * seq
    mod3 = mod.reshape(bsz, -1, d)
    x1 = _mixer_call(x, mod3, norm_pre_mix, norm_post_mix, w_in, dn_conv_w, dn_a_log, dn_dt_bias, dn_norm_w,
                     cf_pw1_b, cf_dw_w, cf_dw_b, cf_ln_w, cf_ln_b, w_out)
    h2, info, cnt = _router_call(x1, mod3, norm_pre_ffn, w_router_group, b_router_group,
                                 w_router_expert, b_router_expert)

    bm = EXPERT_BLOCK
    info2 = info.reshape(t, LANES)
    expert_id = info2[:, 0:TOP_K].astype(jnp.int32)
    rank = info2[:, 4:4 + TOP_K].astype(jnp.int32)
    counts = cnt[0, :N_EXPERTS].astype(jnp.int32)
    padded = (counts + bm - 1) // bm * bm
    pend = jnp.cumsum(padded)
    pstart = pend - padded
    onehot = expert_id[..., None] == jnp.arange(N_EXPERTS, dtype=jnp.int32)
    pos = jnp.sum(jnp.where(onehot, pstart, 0), axis=-1) + rank

    def tiled(tt):
        return pos.reshape(t // tt, tt, TOP_K).transpose(0, 2, 1).reshape(t // tt, 1, TOP_K * tt)
    n_blocks = -(-(t * TOP_K) // bm) + N_EXPERTS
    block_start = jnp.arange(n_blocks, dtype=jnp.int32) * bm
    block_e = jnp.minimum(jnp.sum(pend[None, :] <= block_start[:, None], axis=1), N_EXPERTS - 1).astype(jnp.int32)
    n_used = (pend[-1] // bm).astype(jnp.int32).reshape(1)

    xs = _dispatch_call(h2, tiled(DISPATCH_TILE), n_used, (pstart + counts).astype(jnp.int32),
                        (padded - counts).astype(jnp.int32), n_blocks * bm, d)
    y = _expert_call(xs, block_e, n_used, w_gate, w_up, w_down)
    return _combine_call(x1, info, mod3, norm_post_ffn, y, tiled(COMBINE_TILE))


def kernel(x, c, w_ada, b_ada, norm_pre_mix, norm_post_mix, w_in, dn_conv_w, dn_a_log, dn_dt_bias, dn_norm_w,
           cf_pw1_b, cf_dw_w, cf_dw_b, cf_ln_w, cf_ln_b, w_out, norm_pre_ffn, norm_post_ffn,
           w_router_group, b_router_group, w_router_expert, b_router_expert, w_gate, w_up, w_down):
    depth = w_ada.shape[0]
    for l in range(depth):
        mod = _ada_call(c, w_ada[l], b_ada[l])
        x = _layer(x, mod, norm_pre_mix[l], norm_post_mix[l], w_in[l], dn_conv_w[l], dn_a_log[l],
                   dn_dt_bias[l], dn_norm_w[l], cf_pw1_b[l], cf_dw_w[l], cf_dw_b[l], cf_ln_w[l], cf_ln_b[l],
                   w_out[l], norm_pre_ffn[l], norm_post_ffn[l], w_router_group[l], b_router_group[l],
                   w_router_expert[l], b_router_expert[l], w_gate[l], w_up[l], w_down[l])
    return x
```

```python
import functools

import jax
import jax.numpy as jnp
from jax import lax
from jax.experimental import pallas as pl
from jax.experimental.pallas import tpu as pltpu

F32 = jnp.float32
BF16 = jnp.bfloat16
EPS = 1e-6

DN_HEADS = 4
HEAD_DIM = 128
DN_WIDTH = DN_HEADS * HEAD_DIM
DN_CONV = 4
DN_CHUNK = 64
CF_KERNEL = 31
N_GROUPS = 8
EXPERTS_PER_GROUP = 8
N_EXPERTS = N_GROUPS * EXPERTS_PER_GROUP
TOP_K = 2

LANES = 128
SUBLANES = 8
SEQ_TILE = 256
CONV_ROWS = 32
ROUTER_TILE = 512
EXPERT_BLOCK = 256
COMBINE_TILE = 512
WEIGHT_SLOTS = 3
DISPATCH_TILE = 1024
VMEM_LIMIT = 56 * 1024 * 1024


def _dot(a, b):
    return jnp.dot(a, b, preferred_element_type=F32)


def _dot_nt(a, b):
    return lax.dot_general(a, b, (((1,), (1,)), ((), ())), preferred_element_type=F32)


def _dot_tn(a, b):
    return lax.dot_general(a, b, (((0,), (0,)), ((), ())), preferred_element_type=F32)


def _split3(x):
    hi = x.astype(BF16)
    r1 = x - hi.astype(F32)
    mid = r1.astype(BF16)
    lo = (r1 - mid.astype(F32)).astype(BF16)
    return hi, mid, lo


def _silu(x):
    return x * jax.nn.sigmoid(x)


def _softplus(x):
    return jnp.maximum(x, 0.0) + jnp.log1p(jnp.exp(-jnp.abs(x)))


def _store_token_major(ref, val, base=0):
    n, d = val.shape
    pitch = d // LANES
    for j in range(pitch):
        ref[pl.ds(base + j, n, stride=pitch), :] = val[:, j * LANES:(j + 1) * LANES]


def _load_token_major(ref, n, d, base=0):
    pitch = d // LANES
    return jnp.concatenate([ref[pl.ds(base + j, n, stride=pitch), :] for j in range(pitch)], axis=1)


def _ada_kernel(c_ref, w_ref, b_ref, o_ref):
    c = c_ref[...]
    ca = _silu(c)
    c_hi, c_lo, _ = _split3(ca)
    w_hi, w_lo, _ = _split3(w_ref[...])
    o_ref[...] = _dot(c_hi, w_hi) + (_dot(c_hi, w_lo) + _dot(c_lo, w_hi)) + b_ref[...]


def _ada_call(c, w, b):
    bsz, d = c.shape
    n = w.shape[1]
    tn = 512
    return pl.pallas_call(
        _ada_kernel,
        grid=(n // tn,),
        in_specs=[pl.BlockSpec((bsz, d), lambda i: (0, 0)),
                  pl.BlockSpec((d, tn), lambda i: (0, i)),
                  pl.BlockSpec((1, tn), lambda i: (0, i))],
        out_specs=pl.BlockSpec((bsz, tn), lambda i: (0, i)),
        out_shape=jax.ShapeDtypeStruct((bsz, n), F32),
        compiler_params=pltpu.CompilerParams(dimension_semantics=("arbitrary",),
                                             vmem_limit_bytes=VMEM_LIMIT),
        name="ada",
    )(c, w, b.reshape(1, n))


def _time_perm(ts, transpose):
    ri = lax.broadcasted_iota(jnp.int32, (ts, ts), 0)
    ci = lax.broadcasted_iota(jnp.int32, (ts, ts), 1)
    strided, natural = (ci, ri) if transpose else (ri, ci)
    return jnp.where(natural == (ts // SUBLANES) * (strided % SUBLANES) + strided // SUBLANES, 1.0, 0.0).astype(BF16)


def _fill_conv_window(ext_ref, prev_ref, cur, n_taps):
    ts = cur.shape[0]
    lead = (n_taps - 1) * SUBLANES
    tail = cur[ts - lead:, :]
    sub = lax.broadcasted_iota(jnp.int32, tail.shape, 0) % SUBLANES
    merged = jnp.where(sub == SUBLANES - 1, prev_ref[...], tail)
    for g in range(n_taps - 1):
        rows = slice(g * SUBLANES, (g + 1) * SUBLANES)
        ext_ref[rows, :] = pltpu.roll(merged[rows, :], 1, 0)
    ext_ref[lead:lead + ts, :] = cur
    prev_ref[...] = tail


def _conv_block(ext_ref, w_ref, n_taps, r0, row_blk, c0, col_blk):
    groups = row_blk // SUBLANES
    acc = [jnp.zeros((SUBLANES, col_blk), F32) for _ in range(groups)]
    for k in range(n_taps):
        w = w_ref[k * SUBLANES:(k + 1) * SUBLANES, c0:c0 + col_blk]
        for g in range(groups):
            lo = r0 + (k + g) * SUBLANES
            acc[g] = acc[g] + w * ext_ref[lo:lo + SUBLANES, c0:c0 + col_blk]
    return jnp.concatenate(acc, axis=0)


def _mixer_kernel(x_ref, mod_ref, npre_ref, npost_ref, win_ref,
                  convw_ref, alog_ref, dtb_ref, dnw_ref, pw1b_ref, dww_ref, dwb_ref, lnw_ref, lnb_ref,
                  woutf_ref, o_ref, qkv_ext, qkv_prev, qkv_p16, qkv_act, cf_ext, cf_prev, cf_p16, state, mixed,
                  consts, wqkv_ref, wz_ref, wba_ref, wcf_ref, wout_ref):
    ts = x_ref.shape[1]
    n_chunks = ts // DN_CHUNK

    @pl.when(pl.program_id(1) == 0)
    def _():
        qkv_prev[...] = jnp.zeros(qkv_prev.shape, F32)
        cf_prev[...] = jnp.zeros(cf_prev.shape, F32)
        state[...] = jnp.zeros(state.shape, F32)

    @pl.when((pl.program_id(0) == 0) & (pl.program_id(1) == 0))
    def _():
        ri = lax.broadcasted_iota(jnp.int32, (ts, ts), 0)
        ci = lax.broadcasted_iota(jnp.int32, (ts, ts), 1)
        consts[0] = _time_perm(ts, False)
        consts[1] = _time_perm(ts, True)
        consts[2] = jnp.where((ri // DN_CHUNK == ci // DN_CHUNK) & (ci <= ri), 1.0, 0.0).astype(BF16)
        n_qkv, n_z, n_ba = wqkv_ref.shape[1], wz_ref.shape[1], 2 * DN_HEADS
        for c0 in range(0, n_qkv, DN_WIDTH):
            wqkv_ref[:, c0:c0 + DN_WIDTH] = win_ref[c0:c0 + DN_WIDTH, :].T.astype(BF16)
        wz_ref[...] = win_ref[n_qkv:n_qkv + n_z, :].T.astype(BF16)
        lane = lax.broadcasted_iota(jnp.int32, wba_ref.shape, 1)
        wba_ref[...] = jnp.where(lane < n_ba, win_ref[n_qkv + n_z:n_qkv + n_z + LANES, :].T, 0.0).astype(BF16)
        cf0 = n_qkv + n_z + n_ba
        for c0 in range(0, wcf_ref.shape[1], DN_WIDTH):
            wcf_ref[:, c0:c0 + DN_WIDTH] = win_ref[cf0 + c0:cf0 + c0 + DN_WIDTH, :].T.astype(BF16)
        wout_ref[...] = woutf_ref[...].astype(BF16)

    x = x_ref[0]
    mod = mod_ref[0]
    shift1, scale1, gate1 = mod[0:1], mod[1:2], mod[2:3]
    h = x * lax.rsqrt(jnp.mean(x * x, axis=-1, keepdims=True) + EPS)
    h = h * npre_ref[...] * (1.0 + scale1) + shift1
    hb = h.astype(BF16)
    hb_st = _dot(consts[0], hb).astype(BF16)
    to_natural = consts[1]

    _fill_conv_window(qkv_ext, qkv_prev, _dot(hb_st, wqkv_ref[...]), DN_CONV)
    for r0 in range(0, ts, CONV_ROWS):
        for c0 in range(0, qkv_ext.shape[1], DN_WIDTH):
            blk = _conv_block(qkv_ext, convw_ref, DN_CONV, r0, CONV_ROWS, c0, DN_WIDTH)
            qkv_p16[r0:r0 + CONV_ROWS, c0:c0 + DN_WIDTH] = _silu(blk).astype(BF16)
    qkv_act[...] = _dot(to_natural, qkv_p16[...])

    cf_pre = _dot(hb_st, wcf_ref[...]) + pw1b_ref[...]
    cfw = cf_ext.shape[1]
    _fill_conv_window(cf_ext, cf_prev, cf_pre[:, :cfw] * jax.nn.sigmoid(cf_pre[:, cfw:]), CF_KERNEL)
    cf_pending = list(range(0, ts, CONV_ROWS))

    def cf_step(n=1):
        for _ in range(min(n, len(cf_pending))):
            r0 = cf_pending.pop(0)
            cf = _conv_block(cf_ext, dww_ref, CF_KERNEL, r0, CONV_ROWS, 0, cfw) + dwb_ref[...]
            mu = jnp.mean(cf, axis=-1, keepdims=True)
            xc = cf - mu
            var = jnp.mean(xc * xc, axis=-1, keepdims=True)
            cfn = xc * lax.rsqrt(var + EPS) * lnw_ref[...] + lnb_ref[...]
            cf_p16[r0:r0 + CONV_ROWS, :] = _silu(cfn).astype(BF16)

    z = _dot(hb, wz_ref[...])
    ba = _dot(hb, wba_ref[...])
    beta_all = jax.nn.sigmoid(ba)
    g_all = -jnp.exp(alog_ref[...]) * _softplus(ba + dtb_ref[...])

    tri = consts[2]
    g_hi, g_mid, g_lo = _split3(g_all)
    gcum = _dot(tri, g_hi) + _dot(tri, g_mid) + _dot(tri, g_lo)
    gcum_t = gcum.T
    exp_g = jnp.exp(gcum)

    r64 = lax.broadcasted_iota(jnp.int32, (DN_CHUNK, DN_CHUNK), 0)
    c64 = lax.broadcasted_iota(jnp.int32, (DN_CHUNK, DN_CHUNK), 1)
    causal = c64 <= r64
    strict = c64 < r64
    eye = jnp.where(c64 == r64, 1.0, 0.0).astype(F32)
    dnw = dnw_ref[...]

    heads = []
    for hd in range(DN_HEADS):
        lo = hd * HEAD_DIM
        qh = qkv_act[:, lo:lo + HEAD_DIM]
        kh = qkv_act[:, DN_WIDTH + lo:DN_WIDTH + lo + HEAD_DIM]
        vh = qkv_act[:, 2 * DN_WIDTH + lo:2 * DN_WIDTH + lo + HEAD_DIM]
        qn = qh * lax.rsqrt(jnp.sum(qh * qh, axis=-1, keepdims=True) + EPS) * (HEAD_DIM ** -0.5)
        kn = kh * lax.rsqrt(jnp.sum(kh * kh, axis=-1, keepdims=True) + EPS)
        beta_h = beta_all[:, hd:hd + 1]
        gc_h = gcum[:, DN_HEADS + hd:DN_HEADS + hd + 1]
        eg_h = exp_g[:, DN_HEADS + hd:DN_HEADS + hd + 1]
        k_beta = kn * beta_h
        heads.append(dict(qn=qn, kn=kn, k_beta=k_beta, v_beta=vh * beta_h, kbg=k_beta * eg_h, qg=qn * eg_h,
                          gc=gc_h, zg=_silu(z[:, lo:lo + HEAD_DIM])))

    cells = [(hd, ch) for hd in range(DN_HEADS) for ch in range(n_chunks)]
    rows = lambda ch: slice(ch * DN_CHUNK, (ch + 1) * DN_CHUNK)

    decay, kq = {}, {}
    for hd, ch in cells:
        hv, sl = heads[hd], rows(ch)
        gc_row = gcum_t[DN_HEADS + hd:DN_HEADS + hd + 1, sl]
        decay[hd, ch] = jnp.where(causal, jnp.exp(hv["gc"][sl] - gc_row), 0.0)
        lhs = jnp.concatenate([hv["k_beta"][sl], hv["qn"][sl]], axis=0).astype(BF16)
        kq[hd, ch] = _dot_nt(lhs, hv["kn"][sl].astype(BF16))
    a = {c: jnp.where(strict, kq[c][:DN_CHUNK] * decay[c], 0.0) for c in cells}
    attn = {c: (kq[c][DN_CHUNK:] * decay[c]).astype(BF16) for c in cells}

    t_inv = {c: eye - a[c] for c in cells}
    pw = {c: a[c].astype(BF16) for c in cells}
    for _ in range(5):
        pw = {c: _dot(pw[c], pw[c]).astype(BF16) for c in cells}
        t_inv = {c: t_inv[c] + _dot(t_inv[c].astype(BF16), pw[c]) for c in cells}
        cf_step()

    sol, aw, ks, glast = {}, {}, {}, {}
    for hd, ch in cells:
        hv, sl = heads[hd], rows(ch)
        rhs = jnp.concatenate([hv["v_beta"][sl], hv["kbg"][sl]], axis=1).astype(BF16)
        sol[hd, ch] = _dot(t_inv[hd, ch].astype(BF16), rhs).astype(BF16)
    for hd, ch in cells:
        hv, sl = heads[hd], rows(ch)
        gc_col = hv["gc"][sl]
        glast[hd, ch] = gc_col[DN_CHUNK - 1:DN_CHUNK]
        k_dec = (hv["kn"][sl] * jnp.exp(glast[hd, ch] - gc_col)).astype(BF16)
        aw[hd, ch] = _dot(attn[hd, ch], sol[hd, ch])
        ks[hd, ch] = _dot_tn(k_dec, sol[hd, ch])

    s_in = {}
    s_cur = [state[hd] for hd in range(DN_HEADS)]
    for ch in range(n_chunks):
        for hd in range(DN_HEADS):
            s_in[hd, ch] = s_cur[hd].astype(BF16)
            kd_u, kd_w = ks[hd, ch][:, :HEAD_DIM], ks[hd, ch][:, HEAD_DIM:]
            s_cur[hd] = s_cur[hd] * jnp.exp(glast[hd, ch]) + kd_u - _dot(kd_w.astype(BF16), s_in[hd, ch])
        cf_step()
    for hd in range(DN_HEADS):
        state[hd] = s_cur[hd]

    for hd, ch in cells:
        hv, sl = heads[hd], rows(ch)
        lo = hd * HEAD_DIM
        q_eff = (hv["qg"][sl] - aw[hd, ch][:, HEAD_DIM:]).astype(BF16)
        o = _dot(q_eff, s_in[hd, ch]) + aw[hd, ch][:, :HEAD_DIM]
        on = o * lax.rsqrt(jnp.mean(o * o, axis=-1, keepdims=True) + EPS) * dnw * hv["zg"][sl]
        mixed[sl, lo:lo + HEAD_DIM] = on.astype(BF16)

    cf_step(len(cf_pending))
    mixed[:, DN_WIDTH:DN_WIDTH + cfw] = _dot(to_natural, cf_p16[...]).astype(BF16)

    out = _dot(mixed[...], wout_ref[...])
    y = out * lax.rsqrt(jnp.mean(out * out, axis=-1, keepdims=True) + EPS) * npost_ref[...]
    o_ref[0] = x + gate1 * y


def _mixer_call(x, mod3, npre, npost, w_in, dn_conv_w, dn_a_log, dn_dt_bias, dn_norm_w,
                cf_pw1_b, cf_dw_w, cf_dw_b, cf_ln_w, cf_ln_b, w_out):
    bsz, seq, d = x.shape
    ts = SEQ_TILE
    cfw = cf_dw_w.shape[1]
    n_qkv = 3 * DN_WIDTH
    alog = jnp.pad(dn_a_log, (DN_HEADS, LANES - 2 * DN_HEADS)).reshape(1, LANES)
    dtb = jnp.pad(dn_dt_bias, (DN_HEADS, LANES - 2 * DN_HEADS)).reshape(1, LANES)
    dww = jnp.repeat(cf_dw_w, SUBLANES, axis=0)
    convw = jnp.repeat(dn_conv_w, SUBLANES, axis=0)

    def full(a):
        mode = dict(pipeline_mode=pl.Buffered(1)) if a.size >= d * d else {}
        return pl.BlockSpec(a.shape, lambda b, j: (0,) * a.ndim, **mode)

    row = lambda a: a.reshape(1, -1)
    operands = [x, mod3, row(npre), row(npost), w_in.T, convw, alog, dtb, row(dn_norm_w),
                row(cf_pw1_b), dww, row(cf_dw_b), row(cf_ln_w), row(cf_ln_b), w_out]
    in_specs = [pl.BlockSpec((1, ts, d), lambda b, j: (b, j, 0)),
                pl.BlockSpec((1,) + mod3.shape[1:], lambda b, j: (b, 0, 0))]
    in_specs += [full(a) for a in operands[2:]]
    return pl.pallas_call(
        _mixer_kernel,
        grid=(bsz, seq // ts),
        in_specs=in_specs,
        out_specs=pl.BlockSpec((1, ts, d), lambda b, j: (b, j, 0)),
        out_shape=jax.ShapeDtypeStruct((bsz, seq, d), F32),
        scratch_shapes=[pltpu.VMEM(((DN_CONV - 1) * SUBLANES + ts, n_qkv), F32),
                        pltpu.VMEM(((DN_CONV - 1) * SUBLANES, n_qkv), F32),
                        pltpu.VMEM((ts, n_qkv), BF16),
                        pltpu.VMEM((ts, n_qkv), F32),
                        pltpu.VMEM(((CF_KERNEL - 1) * SUBLANES + ts, cfw), F32),
                        pltpu.VMEM(((CF_KERNEL - 1) * SUBLANES, cfw), F32),
                        pltpu.VMEM((ts, cfw), BF16),
                        pltpu.VMEM((DN_HEADS, HEAD_DIM, HEAD_DIM), F32),
                        pltpu.VMEM((ts, DN_WIDTH + cfw), BF16),
                        pltpu.VMEM((3, ts, ts), BF16),
                        pltpu.VMEM((d, n_qkv), BF16), pltpu.VMEM((d, DN_WIDTH), BF16), pltpu.VMEM((d, LANES), BF16),
                        pltpu.VMEM((d, 2 * cfw), BF16), pltpu.VMEM(w_out.shape, BF16)],
        compiler_params=pltpu.CompilerParams(dimension_semantics=("arbitrary", "arbitrary"),
                                             vmem_limit_bytes=VMEM_LIMIT),
        name="mixer",
    )(*operands)


def _router_kernel(x_ref, mod_ref, nw_ref, wr_ref, br_ref, h_ref, info_ref, infot_ref, cnt_ref, carry, strict_ref):
    tt = x_ref.shape[1]

    @pl.when((pl.program_id(0) == 0) & (pl.program_id(1) == 0))
    def _():
        carry[...] = jnp.zeros(carry.shape, F32)
        rr = lax.broadcasted_iota(jnp.int32, (tt, tt), 0)
        cc = lax.broadcasted_iota(jnp.int32, (tt, tt), 1)
        strict_ref[...] = jnp.where(cc < rr, 1.0, 0.0).astype(BF16)

    x = x_ref[0]
    mod = mod_ref[0]
    shift2, scale2 = mod[3:4], mod[4:5]
    h = x * lax.rsqrt(jnp.mean(x * x, axis=-1, keepdims=True) + EPS)
    h = h * nw_ref[...] * (1.0 + scale2) + shift2
    _store_token_major(h_ref, h)

    h_hi, h_lo, _ = _split3(h)
    w_hi, w_lo, _ = _split3(wr_ref[...])
    logits = _dot(h_hi, w_hi) + (_dot(h_hi, w_lo) + _dot(h_lo, w_hi)) + br_ref[...]

    lane = lax.broadcasted_iota(jnp.int32, (tt, LANES), 1)
    neg = -jnp.inf
    is_grp = (lane >= N_EXPERTS) & (lane < N_EXPERTS + N_GROUPS)
    gl = jnp.where(is_grp, logits, neg)
    gmax = jnp.max(gl, axis=-1, keepdims=True)
    gsum = jnp.sum(jnp.where(is_grp, jnp.exp(gl - gmax), 0.0), axis=-1, keepdims=True)
    grp_p = 1.0 / gsum
    grp_lane = jnp.min(jnp.where(is_grp & (gl == gmax), lane, LANES), axis=-1, keepdims=True)
    grp_idx = grp_lane - N_EXPERTS

    in_grp = (lane < N_EXPERTS) & (lane // EXPERTS_PER_GROUP == grp_idx)
    el = jnp.where(in_grp, logits, neg)
    m1 = jnp.max(el, axis=-1, keepdims=True)
    e1 = jnp.min(jnp.where(in_grp & (el == m1), lane, LANES), axis=-1, keepdims=True)
    el2 = jnp.where(lane == e1, neg, el)
    m2 = jnp.max(el2, axis=-1, keepdims=True)
    e2 = jnp.min(jnp.where(in_grp & (lane != e1) & (el2 == m2), lane, LANES), axis=-1, keepdims=True)
    r = jnp.exp(m2 - m1)
    w1 = grp_p / (1.0 + r)
    w2 = grp_p * r / (1.0 + r)

    hit1 = lane == e1
    hit2 = lane == e2
    onehot = jnp.where(hit1 | hit2, 1.0, 0.0)
    prefix = _dot(strict_ref[...], onehot.astype(BF16)) + carry[0:1, :]
    rank1 = jnp.sum(jnp.where(hit1, prefix, 0.0), axis=-1, keepdims=True)
    rank2 = jnp.sum(jnp.where(hit2, prefix, 0.0), axis=-1, keepdims=True)
    total = carry[0:1, :] + jnp.sum(onehot, axis=0, keepdims=True)
    carry[...] = jnp.broadcast_to(total, carry.shape)
    cnt_ref[...] = jnp.broadcast_to(total, cnt_ref.shape)

    info = jnp.where(lane == 0, e1.astype(F32), 0.0)
    info = jnp.where(lane == 1, e2.astype(F32), info)
    info = jnp.where(lane == 2, w1, info)
    info = jnp.where(lane == 3, w2, info)
    info = jnp.where(lane == 4, rank1, info)
    info = jnp.where(lane == 5, rank2, info)
    info_ref[0] = info
    infot_ref[...] = info.T[0:SUBLANES, :]


def _router_call(x1, mod3, norm_w, w_router_group, b_router_group, w_router_expert, b_router_expert):
    bsz, seq, d = x1.shape
    tt = ROUTER_TILE
    nj = seq // tt
    pitch = d // LANES
    pad = LANES - N_EXPERTS - N_GROUPS
    wr = jnp.pad(jnp.concatenate([w_router_expert, w_router_group], axis=1), ((0, 0), (0, pad)))
    br = jnp.pad(jnp.concatenate([b_router_expert, b_router_group]), (0, pad)).reshape(1, LANES)
    return pl.pallas_call(
        _router_kernel,
        grid=(bsz, seq // tt),
        in_specs=[pl.BlockSpec((1, tt, d), lambda b, j: (b, j, 0)),
                  pl.BlockSpec((1,) + mod3.shape[1:], lambda b, j: (b, 0, 0)),
                  pl.BlockSpec((1, d), lambda b, j: (0, 0)),
                  pl.BlockSpec((d, LANES), lambda b, j: (0, 0)),
                  pl.BlockSpec((1, LANES), lambda b, j: (0, 0))],
        out_specs=[pl.BlockSpec((tt * pitch, LANES), lambda b, j: (b * nj + j, 0)),
                   pl.BlockSpec((1, tt, LANES), lambda b, j: (b, j, 0)),
                   pl.BlockSpec((SUBLANES, tt), lambda b, j: (0, b * nj + j)),
                   pl.BlockSpec((8, LANES), lambda b, j: (0, 0))],
        out_shape=[jax.ShapeDtypeStruct((bsz * seq * pitch, LANES), F32),
                   jax.ShapeDtypeStruct((bsz, seq, LANES), F32),
                   jax.ShapeDtypeStruct((SUBLANES, bsz * seq), F32),
                   jax.ShapeDtypeStruct((8, LANES), F32)],
        scratch_shapes=[pltpu.VMEM((8, LANES), F32), pltpu.VMEM((tt, tt), BF16)],
        compiler_params=pltpu.CompilerParams(dimension_semantics=("arbitrary", "arbitrary"),
                                             vmem_limit_bytes=VMEM_LIMIT),
        name="router",
    )(x1, mod3, norm_w.reshape(1, d), wr, br)


def _tile_rows(idx, pitch):
    return pl.ds(pl.multiple_of(idx * pitch, pitch), pitch)


def _dispatch_kernel(nused_ref, padfrom_ref, padlen_ref, pos_ref, h_ref, xs_hbm, zbuf, zsem, sem, *, pitch, bm):
    step = pl.program_id(0)
    tt = h_ref.shape[0] // pitch

    def fill_copies(action):
        def per_expert(e, carry):
            cursor = padfrom_ref[e]
            n = padlen_ref[e]
            bit = bm // 2
            while bit >= 1:
                take = (n & bit) != 0

                @pl.when(take)
                def _(cursor=cursor, bit=bit):
                    action(pltpu.make_async_copy(zbuf.at[pl.ds(0, bit * pitch), :],
                                                 xs_hbm.at[pl.ds(pl.multiple_of(cursor * pitch, pitch), bit * pitch), :],
                                                 zsem.at[0]))
                cursor = cursor + jnp.where(take, bit, 0)
                bit //= 2
            return carry

        def per_block(b, carry):
            start = pl.multiple_of(b * bm * pitch, bm * pitch)
            action(pltpu.make_async_copy(zbuf, xs_hbm.at[pl.ds(start, bm * pitch), :], zsem.at[0]))
            return carry

        lax.fori_loop(0, N_EXPERTS, per_expert, 0)
        lax.fori_loop(nused_ref[0], xs_hbm.shape[0] // (bm * pitch), per_block, 0)

    @pl.when(step == 0)
    def _():
        zbuf[...] = jnp.zeros(zbuf.shape, F32)
        fill_copies(lambda cp: cp.start())

    def body(r, carry):
        for k in range(TOP_K):
            p = pos_ref[0, 0, k * tt + r]
            pltpu.make_async_copy(h_ref.at[_tile_rows(r, pitch), :], xs_hbm.at[_tile_rows(p, pitch), :],
                                  sem.at[0]).start(priority=k)
        return carry
    lax.fori_loop(0, tt, body, 0, unroll=4)
    for k in range(TOP_K):
        pltpu.make_async_copy(h_ref, xs_hbm.at[pl.ds(0, tt * pitch), :], sem.at[0]).wait()

    @pl.when(step == pl.num_programs(0) - 1)
    def _():
        fill_copies(lambda cp: cp.wait())


def _dispatch_call(h2, pos3, n_used, padfrom, padlen, n_rows, d):
    pitch = d // LANES
    bm = EXPERT_BLOCK
    n_tiles, _, two_tt = pos3.shape
    tt = two_tt // TOP_K
    grid_spec = pltpu.PrefetchScalarGridSpec(
        num_scalar_prefetch=3,
        grid=(n_tiles,),
        in_specs=[pl.BlockSpec((1, 1, two_tt), lambda s, *_: (s, 0, 0), memory_space=pltpu.SMEM),
                  pl.BlockSpec((tt * pitch, LANES), lambda s, *_: (s, 0))],
        out_specs=pl.BlockSpec(memory_space=pl.ANY),
        scratch_shapes=[pltpu.VMEM((bm * pitch, LANES), F32), pltpu.SemaphoreType.DMA((1,)),
                        pltpu.SemaphoreType.DMA((1,))],
    )
    return pl.pallas_call(
        functools.partial(_dispatch_kernel, pitch=pitch, bm=bm),
        grid_spec=grid_spec,
        out_shape=jax.ShapeDtypeStruct((n_rows * pitch, LANES), F32),
        compiler_params=pltpu.CompilerParams(dimension_semantics=("arbitrary",),
                                             vmem_limit_bytes=VMEM_LIMIT),
        name="dispatch",
    )(n_used, padfrom, padlen, pos3, h2)


def _expert_kernel(nused_ref, nruns_ref, first_ref, run_ref, rune_ref, x_ref, wg_hbm, wu_hbm, wd_hbm, y_ref,
                   wg_f32, wu_f32, wd_f32, wg_b, wu_b, wd_b, wsems):
    i = pl.program_id(0)
    n_used = nused_ref[0]
    n_runs = nruns_ref[0]
    d = wg_b.shape[0]
    pitch = d // LANES
    bm = y_ref.shape[0] // pitch
    n_slots = wg_f32.shape[0]

    def weight_copies(e, slot):
        return [pltpu.make_async_copy(src.at[e], dst.at[slot], wsems.at[slot])
                for src, dst in ((wg_hbm, wg_f32), (wu_hbm, wu_f32), (wd_hbm, wd_f32))]

    def start_run(r):
        @pl.when(r < n_runs)
        def _():
            for prio, cp in zip((1, 1, 0), weight_copies(rune_ref[r], r % n_slots)):
                cp.start(priority=prio)

    @pl.when(i == 0)
    def _():
        for r in range(n_slots - 1):
            start_run(r)

    @pl.when((i < n_used) & (first_ref[i] == 1))
    def _():
        run = run_ref[i]
        slot = run % n_slots
        for cp in weight_copies(0, slot):
            cp.wait()
        start_run(run + n_slots - 1)

        wg_b[...] = wg_f32[slot].astype(BF16)
        wu_b[...] = wu_f32[slot].astype(BF16)
        wd_b[...] = wd_f32[slot].astype(BF16)

    @pl.when(i < n_used)
    def _():
        xb = _load_token_major(x_ref, bm, d).astype(BF16)
        gate = _dot(xb, wg_b[...])
        up = _dot(xb, wu_b[...])
        hid = (_silu(gate) * up).astype(BF16)
        _store_token_major(y_ref, _dot(hid, wd_b[...]))

    @pl.when(i >= n_used)
    def _():
        y_ref[...] = jnp.zeros(y_ref.shape, F32)


def _expert_call(xs, block_e, n_used, w_gate, w_up, w_down):
    bm = EXPERT_BLOCK
    d, de = w_gate.shape[1], w_gate.shape[2]
    pitch = d // LANES
    n_blocks = xs.shape[0] // (bm * pitch)
    blk = jnp.arange(n_blocks, dtype=jnp.int32)
    used = blk < n_used[0]
    first = jnp.concatenate([jnp.ones((1,), jnp.int32), (block_e[1:] != block_e[:-1]).astype(jnp.int32)])
    first = jnp.where(used, first, 0)
    run = (jnp.cumsum(first) - 1).astype(jnp.int32)
    n_runs = jnp.sum(first).astype(jnp.int32).reshape(1)
    eids = jnp.arange(N_EXPERTS, dtype=jnp.int32)
    run_of_e = jnp.where((first[:, None] == 1) & (block_e[:, None] == eids[None, :]), run[:, None], -1).max(axis=0)
    run_e = jnp.sum(jnp.where(run_of_e[None, :] == eids[:, None], eids[None, :], 0), axis=1).astype(jnp.int32)
    grid_spec = pltpu.PrefetchScalarGridSpec(
        num_scalar_prefetch=5,
        grid=(n_blocks,),
        in_specs=[pl.BlockSpec((bm * pitch, LANES), lambda i, nu, *_: (jnp.minimum(i, nu[0] - 1), 0)),
                  pl.BlockSpec(memory_space=pl.ANY),
                  pl.BlockSpec(memory_space=pl.ANY),
                  pl.BlockSpec(memory_space=pl.ANY)],
        out_specs=pl.BlockSpec((bm * pitch, LANES), lambda i, *_: (i, 0)),
        scratch_shapes=[pltpu.VMEM((WEIGHT_SLOTS, d, de), F32), pltpu.VMEM((WEIGHT_SLOTS, d, de), F32),
                        pltpu.VMEM((WEIGHT_SLOTS, de, d), F32),
                        pltpu.VMEM((d, de), BF16), pltpu.VMEM((d, de), BF16), pltpu.VMEM((de, d), BF16),
                        pltpu.SemaphoreType.DMA((WEIGHT_SLOTS,))],
    )
    return pl.pallas_call(
        _expert_kernel,
        grid_spec=grid_spec,
        out_shape=jax.ShapeDtypeStruct((n_blocks * bm * pitch, LANES), F32),
        compiler_params=pltpu.CompilerParams(dimension_semantics=("arbitrary",),
                                             vmem_limit_bytes=VMEM_LIMIT),
        name="experts",
    )(n_used, n_runs, first, run, run_e, xs, w_gate, w_up, w_down)


def _combine_kernel(pos_cur_ref, pos_nxt_ref, x_ref, info_ref, mod_ref, nw_ref, y_hbm, o_ref, ybuf, sems):
    tt, d = x_ref.shape[1], x_ref.shape[2]
    pitch = d // LANES
    step = pl.program_id(0) * pl.num_programs(1) + pl.program_id(1)
    n_steps = pl.num_programs(0) * pl.num_programs(1)
    slot_rows = TOP_K * tt * pitch

    def start_gather(pos_ref, slot):
        def body(r, carry):
            for k in range(TOP_K):
                p = pos_ref[0, 0, k * tt + r]
                pltpu.make_async_copy(y_hbm.at[_tile_rows(p, pitch), :],
                                      ybuf.at[_tile_rows((slot * TOP_K + k) * tt + r, pitch), :],
                                      sems.at[slot]).start(priority=k)
            return carry
        lax.fori_loop(0, tt, body, 0, unroll=4)

    @pl.when(step == 0)
    def _():
        start_gather(pos_cur_ref, 0)

    @pl.when(step + 1 < n_steps)
    def _():
        start_gather(pos_nxt_ref, (step + 1) % 2)

    info = info_ref[0]
    w1 = info[:, 2:3]
    w2 = info[:, 3:4]
    x = x_ref[0]
    gate2 = mod_ref[0][5:6]
    base = pl.multiple_of((step % 2) * slot_rows, slot_rows)
    pltpu.make_async_copy(y_hbm.at[pl.ds(0, slot_rows), :], ybuf.at[pl.ds(base, slot_rows), :],
                          sems.at[step % 2]).wait()
    moe = (_load_token_major(ybuf, tt, d, base) * w1
           + _load_token_major(ybuf, tt, d, base + tt * pitch) * w2)
    y = moe * lax.rsqrt(jnp.mean(moe * moe, axis=-1, keepdims=True) + EPS) * nw_ref[...]
    o_ref[0] = x + gate2 * y


def _combine_call(x1, info, mod3, norm_w, y, pos3):
    bsz, seq, d = x1.shape
    tt = COMBINE_TILE
    nj = seq // tt
    n_tiles = bsz * nj
    pos_blk = lambda f: pl.BlockSpec((1, 1, TOP_K * tt), f, memory_space=pltpu.SMEM)
    return pl.pallas_call(
        _combine_kernel,
        grid=(bsz, nj),
        in_specs=[pos_blk(lambda b, j: (b * nj + j, 0, 0)),
                  pos_blk(lambda b, j: (jnp.minimum(b * nj + j + 1, n_tiles - 1), 0, 0)),
                  pl.BlockSpec((1, tt, d), lambda b, j: (b, j, 0)),
                  pl.BlockSpec((1, tt, LANES), lambda b, j: (b, j, 0)),
                  pl.BlockSpec((1,) + mod3.shape[1:], lambda b, j: (b, 0, 0)),
                  pl.BlockSpec((1, d), lambda b, j: (0, 0)),
                  pl.BlockSpec(memory_space=pl.ANY)],
        out_specs=pl.BlockSpec((1, tt, d), lambda b, j: (b, j, 0)),
        out_shape=jax.ShapeDtypeStruct((bsz, seq, d), F32),
        scratch_shapes=[pltpu.VMEM((2 * TOP_K * tt * (d // LANES), LANES), F32), pltpu.SemaphoreType.DMA((2,))],
        compiler_params=pltpu.CompilerParams(dimension_semantics=("arbitrary", "arbitrary"),
                                             vmem_limit_bytes=VMEM_LIMIT),
        name="combine",
    )(pos3, pos3, x1, info, mod3, norm_w.reshape(1, d), y)


def _layer(x, mod, norm_pre_mix, norm_post_mix, w_in, dn_conv_w, dn_a_log, dn_dt_bias, dn_norm_w,
           cf_pw1_b, cf_dw_w, cf_dw_b, cf_ln_w, cf_ln_b, w_out, norm_pre_ffn, norm_post_ffn,
           w_router_group, b_router_group, w_router_expert, b_router_expert, w_gate, w_up, w_down):
    bsz, seq, d = x.shape
    t = bsz * seq
    mod3 = mod.reshape(bsz, -1, d)
    x1 = _mixer_call(x, mod3, norm_pre_mix, norm_post_mix, w_in, dn_conv_w, dn_a_log, dn_dt_bias, dn_norm_w,
                     cf_pw1_b, cf_dw_w, cf_dw_b, cf_ln_w, cf_ln_b, w_out)
    h2, info, info_t, cnt = _router_call(x1, mod3, norm_pre_ffn, w_router_group, b_router_group,
                                         w_router_expert, b_router_expert)

    bm = EXPERT_BLOCK
    expert_id = info_t[0:TOP_K].astype(jnp.int32)
    rank = info_t[4:4 + TOP_K].astype(jnp.int32)
    counts = cnt[0, :N_EXPERTS].astype(jnp.int32)
    padded = (counts + bm - 1) // bm * bm
    pend = jnp.cumsum(padded)
    pstart = pend - padded
    eids = jnp.arange(N_EXPERTS, dtype=jnp.int32)
    onehot = expert_id[:, None, :] == eids[None, :, None]
    pos = jnp.sum(jnp.where(onehot, pstart[None, :, None], 0), axis=1) + rank

    def tiled(tt):
        return pos.reshape(TOP_K, t // tt, tt).transpose(1, 0, 2).reshape(t // tt, 1, TOP_K * tt)
    n_blocks = -(-(t * TOP_K) // bm) + N_EXPERTS
    block_start = jnp.arange(n_blocks, dtype=jnp.int32) * bm
    block_e = jnp.minimum(jnp.sum(pend[None, :] <= block_start[:, None], axis=1), N_EXPERTS - 1).astype(jnp.int32)
    n_used = (pend[-1] // bm).astype(jnp.int32).reshape(1)

    xs = _dispatch_call(h2, tiled(DISPATCH_TILE), n_used, (pstart + counts).astype(jnp.int32),
                        (padded - counts).astype(jnp.int32), n_blocks * bm, d)
    y = _expert_call(xs, block_e, n_used, w_gate, w_up, w_down)
    return _combine_call(x1, info, mod3, norm_post_ffn, y, tiled(COMBINE_TILE))


def kernel(x, c, w_ada, b_ada, norm_pre_mix, norm_post_mix, w_in, dn_conv_w, dn_a_log, dn_dt_bias, dn_norm_w,
           cf_pw1_b, cf_dw_w, cf_dw_b, cf_ln_w, cf_ln_b, w_out, norm_pre_ffn, norm_post_ffn,
           w_router_group, b_router_group, w_router_expert, b_router_expert, w_gate, w_up, w_down):
    depth = w_ada.shape[0]
    for l in range(depth):
        mod = _ada_call(c, w_ada[l], b_ada[l])
        x = _layer(x, mod, norm_pre_mix[l], norm_post_mix[l], w_in[l], dn_conv_w[l], dn_a_log[l],
                   dn_dt_bias[l], dn_norm_w[l], cf_pw1_b[l], cf_dw_w[l], cf_dw_b[l], cf_ln_w[l], cf_ln_b[l],
                   w_out[l], norm_pre_ffn[l], norm_post_ffn[l], w_router_group[l], b_router_group[l],
                   w_router_expert[l], b_router_expert[l], w_gate[l], w_up[l], w_down[l])
    return x
```

```python
import functools

import jax
import jax.numpy as jnp
from jax import lax
from jax.experimental import pallas as pl
from jax.experimental.pallas import tpu as pltpu

F32 = jnp.float32
BF16 = jnp.bfloat16
EPS = 1e-6

DN_HEADS = 4
HEAD_DIM = 128
DN_WIDTH = DN_HEADS * HEAD_DIM
DN_CONV = 4
DN_CHUNK = 64
CF_KERNEL = 31
N_GROUPS = 8
EXPERTS_PER_GROUP = 8
N_EXPERTS = N_GROUPS * EXPERTS_PER_GROUP
TOP_K = 2

LANES = 128
SUBLANES = 8
SEQ_TILE = 256
CONV_ROWS = 32
ROUTER_TILE = 512
EXPERT_BLOCK = 256
COMBINE_TILE = 512
COMBINE_ROWS = 32
WEIGHT_SLOTS = 3
DISPATCH_TILE = 1024
VMEM_LIMIT = 56 * 1024 * 1024


def _dot(a, b):
    return jnp.dot(a, b, preferred_element_type=F32)


def _dot_nt(a, b):
    return lax.dot_general(a, b, (((1,), (1,)), ((), ())), preferred_element_type=F32)


def _dot_tn(a, b):
    return lax.dot_general(a, b, (((0,), (0,)), ((), ())), preferred_element_type=F32)


def _split3(x):
    hi = x.astype(BF16)
    r1 = x - hi.astype(F32)
    mid = r1.astype(BF16)
    lo = (r1 - mid.astype(F32)).astype(BF16)
    return hi, mid, lo


def _silu(x):
    return x * jax.nn.sigmoid(x)


def _softplus(x):
    return jnp.maximum(x, 0.0) + jnp.log1p(jnp.exp(-jnp.abs(x)))


def _store_token_major(ref, val, base=0):
    n, d = val.shape
    pitch = d // LANES
    for j in range(pitch):
        ref[pl.ds(base + j, n, stride=pitch), :] = val[:, j * LANES:(j + 1) * LANES]


def _load_token_major(ref, n, d, base=0):
    pitch = d // LANES
    return jnp.concatenate([ref[pl.ds(base + j, n, stride=pitch), :] for j in range(pitch)], axis=1)


def _ada_kernel(c_ref, w_ref, b_ref, o_ref):
    c = c_ref[...]
    ca = _silu(c)
    c_hi, c_lo, _ = _split3(ca)
    w_hi, w_lo, _ = _split3(w_ref[...])
    o_ref[...] = _dot(c_hi, w_hi) + (_dot(c_hi, w_lo) + _dot(c_lo, w_hi)) + b_ref[...]


def _ada_call(c, w, b):
    bsz, d = c.shape
    n = w.shape[1]
    tn = 512
    return pl.pallas_call(
        _ada_kernel,
        grid=(n // tn,),
        in_specs=[pl.BlockSpec((bsz, d), lambda i: (0, 0)),
                  pl.BlockSpec((d, tn), lambda i: (0, i)),
                  pl.BlockSpec((1, tn), lambda i: (0, i))],
        out_specs=pl.BlockSpec((bsz, tn), lambda i: (0, i)),
        out_shape=jax.ShapeDtypeStruct((bsz, n), F32),
        compiler_params=pltpu.CompilerParams(dimension_semantics=("arbitrary",),
                                             vmem_limit_bytes=VMEM_LIMIT),
        name="ada",
    )(c, w, b.reshape(1, n))


def _time_perm(ts, transpose):
    ri = lax.broadcasted_iota(jnp.int32, (ts, ts), 0)
    ci = lax.broadcasted_iota(jnp.int32, (ts, ts), 1)
    strided, natural = (ci, ri) if transpose else (ri, ci)
    return jnp.where(natural == (ts // SUBLANES) * (strided % SUBLANES) + strided // SUBLANES, 1.0, 0.0).astype(BF16)


def _fill_conv_window(ext_ref, prev_ref, cur, n_taps):
    ts = cur.shape[0]
    lead = (n_taps - 1) * SUBLANES
    tail = cur[ts - lead:, :]
    sub = lax.broadcasted_iota(jnp.int32, tail.shape, 0) % SUBLANES
    merged = jnp.where(sub == SUBLANES - 1, prev_ref[...], tail)
    for g in range(n_taps - 1):
        rows = slice(g * SUBLANES, (g + 1) * SUBLANES)
        ext_ref[rows, :] = pltpu.roll(merged[rows, :], 1, 0)
    ext_ref[lead:lead + ts, :] = cur
    prev_ref[...] = tail


def _conv_block(ext_ref, w_ref, n_taps, r0, row_blk, c0, col_blk):
    groups = row_blk // SUBLANES
    acc = [jnp.zeros((SUBLANES, col_blk), F32) for _ in range(groups)]
    for k in range(n_taps):
        w = w_ref[k * SUBLANES:(k + 1) * SUBLANES, c0:c0 + col_blk]
        for g in range(groups):
            lo = r0 + (k + g) * SUBLANES
            acc[g] = acc[g] + w * ext_ref[lo:lo + SUBLANES, c0:c0 + col_blk]
    return jnp.concatenate(acc, axis=0)


def _mixer_kernel(x_ref, mod_ref, npre_ref, npost_ref, win_ref,
                  convw_ref, alog_ref, dtb_ref, dnw_ref, pw1b_ref, dww_ref, dwb_ref, lnw_ref, lnb_ref,
                  woutf_ref, o_ref, qkv_ext, qkv_prev, qkv_p16, qkv_act, cf_ext, cf_prev, cf_p16, state, mixed,
                  consts, wqkv_ref, wz_ref, wba_ref, wcf_ref, wout_ref):
    ts = x_ref.shape[1]
    n_chunks = ts // DN_CHUNK

    @pl.when(pl.program_id(1) == 0)
    def _():
        qkv_prev[...] = jnp.zeros(qkv_prev.shape, F32)
        cf_prev[...] = jnp.zeros(cf_prev.shape, F32)
        state[...] = jnp.zeros(state.shape, F32)

    @pl.when((pl.program_id(0) == 0) & (pl.program_id(1) == 0))
    def _():
        ri = lax.broadcasted_iota(jnp.int32, (ts, ts), 0)
        ci = lax.broadcasted_iota(jnp.int32, (ts, ts), 1)
        consts[0] = _time_perm(ts, False)
        consts[1] = _time_perm(ts, True)
        consts[2] = jnp.where((ri // DN_CHUNK == ci // DN_CHUNK) & (ci <= ri), 1.0, 0.0).astype(BF16)
        n_qkv, n_z, n_ba = wqkv_ref.shape[1], wz_ref.shape[1], 2 * DN_HEADS
        for c0 in range(0, n_qkv, DN_WIDTH):
            wqkv_ref[:, c0:c0 + DN_WIDTH] = win_ref[c0:c0 + DN_WIDTH, :].T.astype(BF16)
        wz_ref[...] = win_ref[n_qkv:n_qkv + n_z, :].T.astype(BF16)
        lane = lax.broadcasted_iota(jnp.int32, wba_ref.shape, 1)
        wba_ref[...] = jnp.where(lane < n_ba, win_ref[n_qkv + n_z:n_qkv + n_z + LANES, :].T, 0.0).astype(BF16)
        cf0 = n_qkv + n_z + n_ba
        for c0 in range(0, wcf_ref.shape[1], DN_WIDTH):
            wcf_ref[:, c0:c0 + DN_WIDTH] = win_ref[cf0 + c0:cf0 + c0 + DN_WIDTH, :].T.astype(BF16)
        wout_ref[...] = woutf_ref[...].astype(BF16)

    x = x_ref[0]
    mod = mod_ref[0]
    shift1, scale1, gate1 = mod[0:1], mod[1:2], mod[2:3]
    h = x * lax.rsqrt(jnp.mean(x * x, axis=-1, keepdims=True) + EPS)
    h = h * npre_ref[...] * (1.0 + scale1) + shift1
    hb = h.astype(BF16)
    hb_st = _dot(consts[0], hb).astype(BF16)
    to_natural = consts[1]

    _fill_conv_window(qkv_ext, qkv_prev, _dot(hb_st, wqkv_ref[...]), DN_CONV)
    for r0 in range(0, ts, CONV_ROWS):
        for c0 in range(0, qkv_ext.shape[1], DN_WIDTH):
            blk = _conv_block(qkv_ext, convw_ref, DN_CONV, r0, CONV_ROWS, c0, DN_WIDTH)
            qkv_p16[r0:r0 + CONV_ROWS, c0:c0 + DN_WIDTH] = _silu(blk).astype(BF16)
    qkv_act[...] = _dot(to_natural, qkv_p16[...])

    cf_pre = _dot(hb_st, wcf_ref[...]) + pw1b_ref[...]
    cfw = cf_ext.shape[1]
    _fill_conv_window(cf_ext, cf_prev, cf_pre[:, :cfw] * jax.nn.sigmoid(cf_pre[:, cfw:]), CF_KERNEL)
    cf_pending = list(range(0, ts, CONV_ROWS))

    def cf_step(n=1):
        for _ in range(min(n, len(cf_pending))):
            r0 = cf_pending.pop(0)
            cf = _conv_block(cf_ext, dww_ref, CF_KERNEL, r0, CONV_ROWS, 0, cfw) + dwb_ref[...]
            mu = jnp.mean(cf, axis=-1, keepdims=True)
            xc = cf - mu
            var = jnp.mean(xc * xc, axis=-1, keepdims=True)
            cfn = xc * lax.rsqrt(var + EPS) * lnw_ref[...] + lnb_ref[...]
            cf_p16[r0:r0 + CONV_ROWS, :] = _silu(cfn).astype(BF16)

    z = _dot(hb, wz_ref[...])
    ba = _dot(hb, wba_ref[...])
    beta_all = jax.nn.sigmoid(ba)
    g_all = -jnp.exp(alog_ref[...]) * _softplus(ba + dtb_ref[...])

    tri = consts[2]
    g_hi, g_mid, g_lo = _split3(g_all)
    gcum = _dot(tri, g_hi) + _dot(tri, g_mid) + _dot(tri, g_lo)
    gcum_t = gcum.T
    exp_g = jnp.exp(gcum)

    r64 = lax.broadcasted_iota(jnp.int32, (DN_CHUNK, DN_CHUNK), 0)
    c64 = lax.broadcasted_iota(jnp.int32, (DN_CHUNK, DN_CHUNK), 1)
    causal = c64 <= r64
    strict = c64 < r64
    eye = jnp.where(c64 == r64, 1.0, 0.0).astype(F32)
    dnw = dnw_ref[...]

    heads = []
    for hd in range(DN_HEADS):
        lo = hd * HEAD_DIM
        qh = qkv_act[:, lo:lo + HEAD_DIM]
        kh = qkv_act[:, DN_WIDTH + lo:DN_WIDTH + lo + HEAD_DIM]
        vh = qkv_act[:, 2 * DN_WIDTH + lo:2 * DN_WIDTH + lo + HEAD_DIM]
        qn = qh * lax.rsqrt(jnp.sum(qh * qh, axis=-1, keepdims=True) + EPS) * (HEAD_DIM ** -0.5)
        kn = kh * lax.rsqrt(jnp.sum(kh * kh, axis=-1, keepdims=True) + EPS)
        beta_h = beta_all[:, hd:hd + 1]
        gc_h = gcum[:, DN_HEADS + hd:DN_HEADS + hd + 1]
        eg_h = exp_g[:, DN_HEADS + hd:DN_HEADS + hd + 1]
        k_beta = kn * beta_h
        heads.append(dict(qn=qn, kn=kn, k_beta=k_beta, v_beta=vh * beta_h, kbg=k_beta * eg_h, qg=qn * eg_h,
                          gc=gc_h, zg=_silu(z[:, lo:lo + HEAD_DIM])))

    cells = [(hd, ch) for hd in range(DN_HEADS) for ch in range(n_chunks)]
    rows = lambda ch: slice(ch * DN_CHUNK, (ch + 1) * DN_CHUNK)

    decay, kq = {}, {}
    for hd, ch in cells:
        hv, sl = heads[hd], rows(ch)
        gc_row = gcum_t[DN_HEADS + hd:DN_HEADS + hd + 1, sl]
        decay[hd, ch] = jnp.where(causal, jnp.exp(hv["gc"][sl] - gc_row), 0.0)
        lhs = jnp.concatenate([hv["k_beta"][sl], hv["qn"][sl]], axis=0).astype(BF16)
        kq[hd, ch] = _dot_nt(lhs, hv["kn"][sl].astype(BF16))
    a = {c: jnp.where(strict, kq[c][:DN_CHUNK] * decay[c], 0.0) for c in cells}
    attn = {c: (kq[c][DN_CHUNK:] * decay[c]).astype(BF16) for c in cells}

    t_inv = {c: eye - a[c] for c in cells}
    pw = {c: a[c].astype(BF16) for c in cells}
    for _ in range(5):
        pw = {c: _dot(pw[c], pw[c]).astype(BF16) for c in cells}
        t_inv = {c: t_inv[c] + _dot(t_inv[c].astype(BF16), pw[c]) for c in cells}
        cf_step()

    sol, aw, ks, glast = {}, {}, {}, {}
    for hd, ch in cells:
        hv, sl = heads[hd], rows(ch)
        rhs = jnp.concatenate([hv["v_beta"][sl], hv["kbg"][sl]], axis=1).astype(BF16)
        sol[hd, ch] = _dot(t_inv[hd, ch].astype(BF16), rhs).astype(BF16)
    for hd, ch in cells:
        hv, sl = heads[hd], rows(ch)
        gc_col = hv["gc"][sl]
        glast[hd, ch] = gc_col[DN_CHUNK - 1:DN_CHUNK]
        k_dec = (hv["kn"][sl] * jnp.exp(glast[hd, ch] - gc_col)).astype(BF16)
        aw[hd, ch] = _dot(attn[hd, ch], sol[hd, ch])
        ks[hd, ch] = _dot_tn(k_dec, sol[hd, ch])

    s_in = {}
    s_cur = [state[hd] for hd in range(DN_HEADS)]
    for ch in range(n_chunks):
        for hd in range(DN_HEADS):
            s_in[hd, ch] = s_cur[hd].astype(BF16)
            kd_u, kd_w = ks[hd, ch][:, :HEAD_DIM], ks[hd, ch][:, HEAD_DIM:]
            s_cur[hd] = s_cur[hd] * jnp.exp(glast[hd, ch]) + kd_u - _dot(kd_w.astype(BF16), s_in[hd, ch])
        cf_step()
    for hd in range(DN_HEADS):
        state[hd] = s_cur[hd]

    for hd, ch in cells:
        hv, sl = heads[hd], rows(ch)
        lo = hd * HEAD_DIM
        q_eff = (hv["qg"][sl] - aw[hd, ch][:, HEAD_DIM:]).astype(BF16)
        o = _dot(q_eff, s_in[hd, ch]) + aw[hd, ch][:, :HEAD_DIM]
        on = o * lax.rsqrt(jnp.mean(o * o, axis=-1, keepdims=True) + EPS) * dnw * hv["zg"][sl]
        mixed[sl, lo:lo + HEAD_DIM] = on.astype(BF16)

    cf_step(len(cf_pending))
    mixed[:, DN_WIDTH:DN_WIDTH + cfw] = _dot(to_natural, cf_p16[...]).astype(BF16)

    out = _dot(mixed[...], wout_ref[...])
    y = out * lax.rsqrt(jnp.mean(out * out, axis=-1, keepdims=True) + EPS) * npost_ref[...]
    o_ref[0] = x + gate1 * y


def _mixer_call(x, mod3, npre, npost, w_in, dn_conv_w, dn_a_log, dn_dt_bias, dn_norm_w,
                cf_pw1_b, cf_dw_w, cf_dw_b, cf_ln_w, cf_ln_b, w_out):
    bsz, seq, d = x.shape
    ts = SEQ_TILE
    cfw = cf_dw_w.shape[1]
    n_qkv = 3 * DN_WIDTH
    alog = jnp.pad(dn_a_log, (DN_HEADS, LANES - 2 * DN_HEADS)).reshape(1, LANES)
    dtb = jnp.pad(dn_dt_bias, (DN_HEADS, LANES - 2 * DN_HEADS)).reshape(1, LANES)
    dww = jnp.repeat(cf_dw_w, SUBLANES, axis=0)
    convw = jnp.repeat(dn_conv_w, SUBLANES, axis=0)

    def full(a):
        mode = dict(pipeline_mode=pl.Buffered(1)) if a.size >= d * d else {}
        return pl.BlockSpec(a.shape, lambda b, j: (0,) * a.ndim, **mode)

    row = lambda a: a.reshape(1, -1)
    operands = [x, mod3, row(npre), row(npost), w_in.T, convw, alog, dtb, row(dn_norm_w),
                row(cf_pw1_b), dww, row(cf_dw_b), row(cf_ln_w), row(cf_ln_b), w_out]
    in_specs = [pl.BlockSpec((1, ts, d), lambda b, j: (b, j, 0)),
                pl.BlockSpec((1,) + mod3.shape[1:], lambda b, j: (b, 0, 0))]
    in_specs += [full(a) for a in operands[2:]]
    return pl.pallas_call(
        _mixer_kernel,
        grid=(bsz, seq // ts),
        in_specs=in_specs,
        out_specs=pl.BlockSpec((1, ts, d), lambda b, j: (b, j, 0)),
        out_shape=jax.ShapeDtypeStruct((bsz, seq, d), F32),
        scratch_shapes=[pltpu.VMEM(((DN_CONV - 1) * SUBLANES + ts, n_qkv), F32),
                        pltpu.VMEM(((DN_CONV - 1) * SUBLANES, n_qkv), F32),
                        pltpu.VMEM((ts, n_qkv), BF16),
                        pltpu.VMEM((ts, n_qkv), F32),
                        pltpu.VMEM(((CF_KERNEL - 1) * SUBLANES + ts, cfw), F32),
                        pltpu.VMEM(((CF_KERNEL - 1) * SUBLANES, cfw), F32),
                        pltpu.VMEM((ts, cfw), BF16),
                        pltpu.VMEM((DN_HEADS, HEAD_DIM, HEAD_DIM), F32),
                        pltpu.VMEM((ts, DN_WIDTH + cfw), BF16),
                        pltpu.VMEM((3, ts, ts), BF16),
                        pltpu.VMEM((d, n_qkv), BF16), pltpu.VMEM((d, DN_WIDTH), BF16), pltpu.VMEM((d, LANES), BF16),
                        pltpu.VMEM((d, 2 * cfw), BF16), pltpu.VMEM(w_out.shape, BF16)],
        compiler_params=pltpu.CompilerParams(dimension_semantics=("arbitrary", "arbitrary"),
                                             vmem_limit_bytes=VMEM_LIMIT),
        name="mixer",
    )(*operands)


def _router_kernel(x_ref, mod_ref, nw_ref, wr_ref, br_ref, h_ref, info_ref, infot_ref, cnt_ref, carry, strict_ref):
    tt = x_ref.shape[1]

    @pl.when((pl.program_id(0) == 0) & (pl.program_id(1) == 0))
    def _():
        carry[...] = jnp.zeros(carry.shape, F32)
        rr = lax.broadcasted_iota(jnp.int32, (tt, tt), 0)
        cc = lax.broadcasted_iota(jnp.int32, (tt, tt), 1)
        strict_ref[...] = jnp.where(cc < rr, 1.0, 0.0).astype(BF16)

    x = x_ref[0]
    mod = mod_ref[0]
    shift2, scale2 = mod[3:4], mod[4:5]
    h = x * lax.rsqrt(jnp.mean(x * x, axis=-1, keepdims=True) + EPS)
    h = h * nw_ref[...] * (1.0 + scale2) + shift2
    _store_token_major(h_ref, h)

    h_hi, h_lo, _ = _split3(h)
    w_hi, w_lo, _ = _split3(wr_ref[...])
    logits = _dot(h_hi, w_hi) + (_dot(h_hi, w_lo) + _dot(h_lo, w_hi)) + br_ref[...]

    lane = lax.broadcasted_iota(jnp.int32, (tt, LANES), 1)
    neg = -jnp.inf
    is_grp = (lane >= N_EXPERTS) & (lane < N_EXPERTS + N_GROUPS)
    gl = jnp.where(is_grp, logits, neg)
    gmax = jnp.max(gl, axis=-1, keepdims=True)
    gsum = jnp.sum(jnp.where(is_grp, jnp.exp(gl - gmax), 0.0), axis=-1, keepdims=True)
    grp_p = 1.0 / gsum
    grp_lane = jnp.min(jnp.where(is_grp & (gl == gmax), lane, LANES), axis=-1, keepdims=True)
    grp_idx = grp_lane - N_EXPERTS

    in_grp = (lane < N_EXPERTS) & (lane // EXPERTS_PER_GROUP == grp_idx)
    el = jnp.where(in_grp, logits, neg)
    m1 = jnp.max(el, axis=-1, keepdims=True)
    e1 = jnp.min(jnp.where(in_grp & (el == m1), lane, LANES), axis=-1, keepdims=True)
    el2 = jnp.where(lane == e1, neg, el)
    m2 = jnp.max(el2, axis=-1, keepdims=True)
    e2 = jnp.min(jnp.where(in_grp & (lane != e1) & (el2 == m2), lane, LANES), axis=-1, keepdims=True)
    r = jnp.exp(m2 - m1)
    w1 = grp_p / (1.0 + r)
    w2 = grp_p * r / (1.0 + r)

    hit1 = lane == e1
    hit2 = lane == e2
    onehot = jnp.where(hit1 | hit2, 1.0, 0.0)
    prefix = _dot(strict_ref[...], onehot.astype(BF16)) + carry[0:1, :]
    rank1 = jnp.sum(jnp.where(hit1, prefix, 0.0), axis=-1, keepdims=True)
    rank2 = jnp.sum(jnp.where(hit2, prefix, 0.0), axis=-1, keepdims=True)
    total = carry[0:1, :] + jnp.sum(onehot, axis=0, keepdims=True)
    carry[...] = jnp.broadcast_to(total, carry.shape)
    cnt_ref[...] = jnp.broadcast_to(total, cnt_ref.shape)

    info = jnp.where(lane == 0, e1.astype(F32), 0.0)
    info = jnp.where(lane == 1, e2.astype(F32), info)
    info = jnp.where(lane == 2, w1, info)
    info = jnp.where(lane == 3, w2, info)
    info = jnp.where(lane == 4, rank1, info)
    info = jnp.where(lane == 5, rank2, info)
    info_ref[0] = info
    infot_ref[...] = info.T[0:SUBLANES, :]


def _router_call(x1, mod3, norm_w, w_router_group, b_router_group, w_router_expert, b_router_expert):
    bsz, seq, d = x1.shape
    tt = ROUTER_TILE
    nj = seq // tt
    pitch = d // LANES
    pad = LANES - N_EXPERTS - N_GROUPS
    wr = jnp.pad(jnp.concatenate([w_router_expert, w_router_group], axis=1), ((0, 0), (0, pad)))
    br = jnp.pad(jnp.concatenate([b_router_expert, b_router_group]), (0, pad)).reshape(1, LANES)
    return pl.pallas_call(
        _router_kernel,
        grid=(bsz, seq // tt),
        in_specs=[pl.BlockSpec((1, tt, d), lambda b, j: (b, j, 0)),
                  pl.BlockSpec((1,) + mod3.shape[1:], lambda b, j: (b, 0, 0)),
                  pl.BlockSpec((1, d), lambda b, j: (0, 0)),
                  pl.BlockSpec((d, LANES), lambda b, j: (0, 0)),
                  pl.BlockSpec((1, LANES), lambda b, j: (0, 0))],
        out_specs=[pl.BlockSpec((tt * pitch, LANES), lambda b, j: (b * nj + j, 0)),
                   pl.BlockSpec((1, tt, LANES), lambda b, j: (b, j, 0)),
                   pl.BlockSpec((SUBLANES, tt), lambda b, j: (0, b * nj + j)),
                   pl.BlockSpec((8, LANES), lambda b, j: (0, 0))],
        out_shape=[jax.ShapeDtypeStruct((bsz * seq * pitch, LANES), F32),
                   jax.ShapeDtypeStruct((bsz, seq, LANES), F32),
                   jax.ShapeDtypeStruct((SUBLANES, bsz * seq), F32),
                   jax.ShapeDtypeStruct((8, LANES), F32)],
        scratch_shapes=[pltpu.VMEM((8, LANES), F32), pltpu.VMEM((tt, tt), BF16)],
        compiler_params=pltpu.CompilerParams(dimension_semantics=("arbitrary", "arbitrary"),
                                             vmem_limit_bytes=VMEM_LIMIT),
        name="router",
    )(x1, mod3, norm_w.reshape(1, d), wr, br)


def _tile_rows(idx, pitch):
    if isinstance(idx, int):
        return pl.ds(idx * pitch, pitch)
    return pl.ds(pl.multiple_of(idx * pitch, pitch), pitch)


def _dispatch_kernel(nused_ref, padfrom_ref, padlen_ref, pos_ref, h_ref, xs_hbm, zbuf, zsem, sem, *, pitch, bm):
    step = pl.program_id(0)
    tt = h_ref.shape[0] // pitch

    def fill_copies(action):
        def per_expert(e, carry):
            cursor = padfrom_ref[e]
            n = padlen_ref[e]
            bit = bm // 2
            while bit >= 1:
                take = (n & bit) != 0

                @pl.when(take)
                def _(cursor=cursor, bit=bit):
                    action(pltpu.make_async_copy(zbuf.at[pl.ds(0, bit * pitch), :],
                                                 xs_hbm.at[pl.ds(pl.multiple_of(cursor * pitch, pitch), bit * pitch), :],
                                                 zsem.at[0]))
                cursor = cursor + jnp.where(take, bit, 0)
                bit //= 2
            return carry

        def per_block(b, carry):
            start = pl.multiple_of(b * bm * pitch, bm * pitch)
            action(pltpu.make_async_copy(zbuf, xs_hbm.at[pl.ds(start, bm * pitch), :], zsem.at[0]))
            return carry

        lax.fori_loop(0, N_EXPERTS, per_expert, 0)
        lax.fori_loop(nused_ref[0], xs_hbm.shape[0] // (bm * pitch), per_block, 0)

    @pl.when(step == 0)
    def _():
        zbuf[...] = jnp.zeros(zbuf.shape, F32)
        fill_copies(lambda cp: cp.start())

    for r in range(tt):
        for k in range(TOP_K):
            p = pos_ref[0, 0, k * tt + r]
            pltpu.make_async_copy(h_ref.at[_tile_rows(r, pitch), :], xs_hbm.at[_tile_rows(p, pitch), :],
                                  sem.at[0]).start(priority=k)
    for k in range(TOP_K):
        pltpu.make_async_copy(h_ref, xs_hbm.at[pl.ds(0, tt * pitch), :], sem.at[0]).wait()

    @pl.when(step == pl.num_programs(0) - 1)
    def _():
        fill_copies(lambda cp: cp.wait())


def _dispatch_call(h2, pos3, n_used, padfrom, padlen, n_rows, d):
    pitch = d // LANES
    bm = EXPERT_BLOCK
    n_tiles, _, two_tt = pos3.shape
    tt = two_tt // TOP_K
    grid_spec = pltpu.PrefetchScalarGridSpec(
        num_scalar_prefetch=3,
        grid=(n_tiles,),
        in_specs=[pl.BlockSpec((1, 1, two_tt), lambda s, *_: (s, 0, 0), memory_space=pltpu.SMEM),
                  pl.BlockSpec((tt * pitch, LANES), lambda s, *_: (s, 0))],
        out_specs=pl.BlockSpec(memory_space=pl.ANY),
        scratch_shapes=[pltpu.VMEM((bm * pitch, LANES), F32), pltpu.SemaphoreType.DMA((1,)),
                        pltpu.SemaphoreType.DMA((1,))],
    )
    return pl.pallas_call(
        functools.partial(_dispatch_kernel, pitch=pitch, bm=bm),
        grid_spec=grid_spec,
        out_shape=jax.ShapeDtypeStruct((n_rows * pitch, LANES), F32),
        compiler_params=pltpu.CompilerParams(dimension_semantics=("arbitrary",),
                                             vmem_limit_bytes=VMEM_LIMIT),
        name="dispatch",
    )(n_used, padfrom, padlen, pos3, h2)


def _expert_kernel(nused_ref, nruns_ref, first_ref, run_ref, rune_ref, x_ref, wg_hbm, wu_hbm, wd_hbm, y_ref,
                   wg_f32, wu_f32, wd_f32, wg_b, wu_b, wd_b, wsems):
    i = pl.program_id(0)
    n_used = nused_ref[0]
    n_runs = nruns_ref[0]
    d = wg_b.shape[0]
    pitch = d // LANES
    bm = y_ref.shape[0] // pitch
    n_slots = wg_f32.shape[0]

    def weight_copies(e, slot):
        return [pltpu.make_async_copy(src.at[e], dst.at[slot], wsems.at[slot])
                for src, dst in ((wg_hbm, wg_f32), (wu_hbm, wu_f32), (wd_hbm, wd_f32))]

    def start_run(r):
        @pl.when(r < n_runs)
        def _():
            for prio, cp in zip((1, 1, 0), weight_copies(rune_ref[r], r % n_slots)):
                cp.start(priority=prio)

    @pl.when(i == 0)
    def _():
        for r in range(n_slots - 1):
            start_run(r)

    @pl.when((i < n_used) & (first_ref[i] == 1))
    def _():
        run = run_ref[i]
        slot = run % n_slots
        for cp in weight_copies(0, slot):
            cp.wait()
        start_run(run + n_slots - 1)

        wg_b[...] = wg_f32[slot].astype(BF16)
        wu_b[...] = wu_f32[slot].astype(BF16)
        wd_b[...] = wd_f32[slot].astype(BF16)

    @pl.when(i < n_used)
    def _():
        xb = _load_token_major(x_ref, bm, d).astype(BF16)
        gate = _dot(xb, wg_b[...])
        up = _dot(xb, wu_b[...])
        hid = (_silu(gate) * up).astype(BF16)
        _store_token_major(y_ref, _dot(hid, wd_b[...]))

    @pl.when(i >= n_used)
    def _():
        y_ref[...] = jnp.zeros(y_ref.shape, F32)


def _expert_call(xs, block_e, n_used, w_gate, w_up, w_down):
    bm = EXPERT_BLOCK
    d, de = w_gate.shape[1], w_gate.shape[2]
    pitch = d // LANES
    n_blocks = xs.shape[0] // (bm * pitch)
    blk = jnp.arange(n_blocks, dtype=jnp.int32)
    used = blk < n_used[0]
    first = jnp.concatenate([jnp.ones((1,), jnp.int32), (block_e[1:] != block_e[:-1]).astype(jnp.int32)])
    first = jnp.where(used, first, 0)
    run = (jnp.cumsum(first) - 1).astype(jnp.int32)
    n_runs = jnp.sum(first).astype(jnp.int32).reshape(1)
    eids = jnp.arange(N_EXPERTS, dtype=jnp.int32)
    run_of_e = jnp.where((first[:, None] == 1) & (block_e[:, None] == eids[None, :]), run[:, None], -1).max(axis=0)
    run_e = jnp.sum(jnp.where(run_of_e[None, :] == eids[:, None], eids[None, :], 0), axis=1).astype(jnp.int32)
    grid_spec = pltpu.PrefetchScalarGridSpec(
        num_scalar_prefetch=5,
        grid=(n_blocks,),
        in_specs=[pl.BlockSpec((bm * pitch, LANES), lambda i, nu, *_: (jnp.minimum(i, nu[0] - 1), 0)),
                  pl.BlockSpec(memory_space=pl.ANY),
                  pl.BlockSpec(memory_space=pl.ANY),
                  pl.BlockSpec(memory_space=pl.ANY)],
        out_specs=pl.BlockSpec((bm * pitch, LANES), lambda i, *_: (i, 0)),
        scratch_shapes=[pltpu.VMEM((WEIGHT_SLOTS, d, de), F32), pltpu.VMEM((WEIGHT_SLOTS, d, de), F32),
                        pltpu.VMEM((WEIGHT_SLOTS, de, d), F32),
                        pltpu.VMEM((d, de), BF16), pltpu.VMEM((d, de), BF16), pltpu.VMEM((de, d), BF16),
                        pltpu.SemaphoreType.DMA((WEIGHT_SLOTS,))],
    )
    return pl.pallas_call(
        _expert_kernel,
        grid_spec=grid_spec,
        out_shape=jax.ShapeDtypeStruct((n_blocks * bm * pitch, LANES), F32),
        compiler_params=pltpu.CompilerParams(dimension_semantics=("arbitrary",),
                                             vmem_limit_bytes=VMEM_LIMIT),
        name="experts",
    )(n_used, n_runs, first, run, run_e, xs, w_gate, w_up, w_down)


def _combine_kernel(pos_cur_ref, pos_nxt_ref, x_ref, info_ref, mod_ref, nw_ref, y_hbm, o_ref, ybuf_a, ybuf_b, sems):
    tt, d = x_ref.shape[1], x_ref.shape[2]
    pitch = d // LANES
    step = pl.program_id(0) * pl.num_programs(1) + pl.program_id(1)
    n_steps = pl.num_programs(0) * pl.num_programs(1)
    n_chunks = tt // COMBINE_ROWS

    def row_copy(pos_ref, buf, sem, k, r):
        p = pos_ref[0, 0, k * tt + r]
        return pltpu.make_async_copy(y_hbm.at[_tile_rows(p, pitch), :], buf.at[_tile_rows(k * tt + r, pitch), :], sem)

    def wait_tile(buf, sem):
        pltpu.make_async_copy(y_hbm.at[pl.ds(0, TOP_K * tt * pitch), :], buf, sem).wait()

    @pl.when(step == 0)
    def _():
        def body(r, carry):
            for k in range(TOP_K):
                row_copy(pos_cur_ref, ybuf_a, sems.at[0], k, r).start(priority=k)
            return carry
        lax.fori_loop(0, tt, body, 0, unroll=4)

    gate2 = mod_ref[0][5:6]

    def run_tile(cur, cur_sem, nxt, nxt_sem):
        wait_tile(cur, cur_sem)

        def chunk(c, carry):
            r0 = c * COMBINE_ROWS
            for r in range(COMBINE_ROWS):
                for k in range(TOP_K):
                    row_copy(pos_nxt_ref, nxt, nxt_sem, k, r0 + r).start(priority=k)
            rows = pl.ds(r0, COMBINE_ROWS)
            info = info_ref[0, rows, :]
            moe = (_load_token_major(cur, COMBINE_ROWS, d, r0 * pitch) * info[:, 2:3]
                   + _load_token_major(cur, COMBINE_ROWS, d, (tt + r0) * pitch) * info[:, 3:4])
            y = moe * lax.rsqrt(jnp.mean(moe * moe, axis=-1, keepdims=True) + EPS) * nw_ref[...]
            o_ref[0, rows, :] = x_ref[0, rows, :] + gate2 * y
            return carry
        for c in range(n_chunks):
            chunk(c, 0)

        @pl.when(step == n_steps - 1)
        def _():
            wait_tile(nxt, nxt_sem)

    @pl.when(step % 2 == 0)
    def _():
        run_tile(ybuf_a, sems.at[0], ybuf_b, sems.at[1])

    @pl.when(step % 2 == 1)
    def _():
        run_tile(ybuf_b, sems.at[1], ybuf_a, sems.at[0])


def _combine_call(x1, info, mod3, norm_w, y, pos3):
    bsz, seq, d = x1.shape
    tt = COMBINE_TILE
    nj = seq // tt
    n_tiles = bsz * nj
    pos_blk = lambda f: pl.BlockSpec((1, 1, TOP_K * tt), f, memory_space=pltpu.SMEM)
    return pl.pallas_call(
        _combine_kernel,
        grid=(bsz, nj),
        in_specs=[pos_blk(lambda b, j: (b * nj + j, 0, 0)),
                  pos_blk(lambda b, j: (jnp.minimum(b * nj + j + 1, n_tiles - 1), 0, 0)),
                  pl.BlockSpec((1, tt, d), lambda b, j: (b, j, 0)),
                  pl.BlockSpec((1, tt, LANES), lambda b, j: (b, j, 0)),
                  pl.BlockSpec((1,) + mod3.shape[1:], lambda b, j: (b, 0, 0)),
                  pl.BlockSpec((1, d), lambda b, j: (0, 0)),
                  pl.BlockSpec(memory_space=pl.ANY)],
        out_specs=pl.BlockSpec((1, tt, d), lambda b, j: (b, j, 0)),
        out_shape=jax.ShapeDtypeStruct((bsz, seq, d), F32),
        scratch_shapes=[pltpu.VMEM((TOP_K * tt * (d // LANES), LANES), F32),
                        pltpu.VMEM((TOP_K * tt * (d // LANES), LANES), F32), pltpu.SemaphoreType.DMA((2,))],
        compiler_params=pltpu.CompilerParams(dimension_semantics=("arbitrary", "arbitrary"),
                                             vmem_limit_bytes=VMEM_LIMIT),
        name="combine",
    )(pos3, pos3, x1, info, mod3, norm_w.reshape(1, d), y)


def _layer(x, mod, norm_pre_mix, norm_post_mix, w_in, dn_conv_w, dn_a_log, dn_dt_bias, dn_norm_w,
           cf_pw1_b, cf_dw_w, cf_dw_b, cf_ln_w, cf_ln_b, w_out, norm_pre_ffn, norm_post_ffn,
           w_router_group, b_router_group, w_router_expert, b_router_expert, w_gate, w_up, w_down):
    bsz, seq, d = x.shape
    t = bsz * seq
    mod3 = mod.reshape(bsz, -1, d)
    x1 = _mixer_call(x, mod3, norm_pre_mix, norm_post_mix, w_in, dn_conv_w, dn_a_log, dn_dt_bias, dn_norm_w,
                     cf_pw1_b, cf_dw_w, cf_dw_b, cf_ln_w, cf_ln_b, w_out)
    h2, info, info_t, cnt = _router_call(x1, mod3, norm_pre_ffn, w_router_group, b_router_group,
                                         w_router_expert, b_router_expert)

    bm = EXPERT_BLOCK
    expert_id = info_t[0:TOP_K].astype(jnp.int32)
    rank = info_t[4:4 + TOP_K].astype(jnp.int32)
    counts = cnt[0, :N_EXPERTS].astype(jnp.int32)
    padded = (counts + bm - 1) // bm * bm
    pend = jnp.cumsum(padded)
    pstart = pend - padded
    eids = jnp.arange(N_EXPERTS, dtype=jnp.int32)
    onehot = expert_id[:, None, :] == eids[None, :, None]
    pos = jnp.sum(jnp.where(onehot, pstart[None, :, None], 0), axis=1) + rank

    def tiled(tt):
        return pos.reshape(TOP_K, t // tt, tt).transpose(1, 0, 2).reshape(t // tt, 1, TOP_K * tt)
    n_blocks = -(-(t * TOP_K) // bm) + N_EXPERTS
    block_start = jnp.arange(n_blocks, dtype=jnp.int32) * bm
    block_e = jnp.minimum(jnp.sum(pend[None, :] <= block_start[:, None], axis=1), N_EXPERTS - 1).astype(jnp.int32)
    n_used = (pend[-1] // bm).astype(jnp.int32).reshape(1)

    xs = _dispatch_call(h2, tiled(DISPATCH_TILE), n_used, (pstart + counts).astype(jnp.int32),
                        (padded - counts).astype(jnp.int32), n_blocks * bm, d)
    y = _expert_call(xs, block_e, n_used, w_gate, w_up, w_down)
    return _combine_call(x1, info, mod3, norm_post_ffn, y, tiled(COMBINE_TILE))


def kernel(x, c, w_ada, b_ada, norm_pre_mix, norm_post_mix, w_in, dn_conv_w, dn_a_log, dn_dt_bias, dn_norm_w,
           cf_pw1_b, cf_dw_w, cf_dw_b, cf_ln_w, cf_ln_b, w_out, norm_pre_ffn, norm_post_ffn,
           w_router_group, b_router_group, w_router_expert, b_router_expert, w_gate, w_up, w_down):
    depth = w_ada.shape[0]
    for l in range(depth):
        mod = _ada_call(c, w_ada[l], b_ada[l])
        x = _layer(x, mod, norm_pre_mix[l], norm_post_mix[l], w_in[l], dn_conv_w[l], dn_a_log[l],
                   dn_dt_bias[l], dn_norm_w[l], cf_pw1_b[l], cf_dw_w[l], cf_dw_b[l], cf_ln_w[l], cf_ln_b[l],
                   w_out[l], norm_pre_ffn[l], norm_post_ffn[l], w_router_group[l], b_router_group[l],
                   w_router_expert[l], b_router_expert[l], w_gate[l], w_up[l], w_down[l])
    return x
```

```python
import functools

import jax
import jax.numpy as jnp
from jax import lax
from jax.experimental import pallas as pl
from jax.experimental.pallas import tpu as pltpu

F32 = jnp.float32
BF16 = jnp.bfloat16
EPS = 1e-6

DN_HEADS = 4
HEAD_DIM = 128
DN_WIDTH = DN_HEADS * HEAD_DIM
DN_CONV = 4
DN_CHUNK = 64
CF_KERNEL = 31
N_GROUPS = 8
EXPERTS_PER_GROUP = 8
N_EXPERTS = N_GROUPS * EXPERTS_PER_GROUP
TOP_K = 2

LANES = 128
SUBLANES = 8
SEQ_TILE = 256
CONV_ROWS = 32
MIXER_PAIR = 2
MIXER_SKEW = 4
ROUTER_TILE = 512
EXPERT_BLOCK = 256
COMBINE_TILE = 512
COMBINE_ROWS = 32
WEIGHT_SLOTS = 3
DISPATCH_TILE = 1024
VMEM_LIMIT = 56 * 1024 * 1024


def _dot(a, b):
    return jnp.dot(a, b, preferred_element_type=F32)


def _dot_nt(a, b):
    return lax.dot_general(a, b, (((1,), (1,)), ((), ())), preferred_element_type=F32)


def _dot_tn(a, b):
    return lax.dot_general(a, b, (((0,), (0,)), ((), ())), preferred_element_type=F32)


def _split3(x):
    hi = x.astype(BF16)
    r1 = x - hi.astype(F32)
    mid = r1.astype(BF16)
    lo = (r1 - mid.astype(F32)).astype(BF16)
    return hi, mid, lo


def _silu(x):
    return x * jax.nn.sigmoid(x)


def _softplus(x):
    return jnp.maximum(x, 0.0) + jnp.log1p(jnp.exp(-jnp.abs(x)))


def _store_token_major(ref, val, base=0):
    n, d = val.shape
    pitch = d // LANES
    for j in range(pitch):
        ref[pl.ds(base + j, n, stride=pitch), :] = val[:, j * LANES:(j + 1) * LANES]


def _load_token_major(ref, n, d, base=0):
    pitch = d // LANES
    return jnp.concatenate([ref[pl.ds(base + j, n, stride=pitch), :] for j in range(pitch)], axis=1)


def _ada_kernel(c_ref, w_ref, b_ref, o_ref):
    c = c_ref[...]
    ca = _silu(c)
    c_hi, c_lo, _ = _split3(ca)
    w_hi, w_lo, _ = _split3(w_ref[...])
    o_ref[...] = _dot(c_hi, w_hi) + (_dot(c_hi, w_lo) + _dot(c_lo, w_hi)) + b_ref[...]


def _ada_call(c, w, b):
    bsz, d = c.shape
    n = w.shape[1]
    tn = 512
    return pl.pallas_call(
        _ada_kernel,
        grid=(n // tn,),
        in_specs=[pl.BlockSpec((bsz, d), lambda i: (0, 0)),
                  pl.BlockSpec((d, tn), lambda i: (0, i)),
                  pl.BlockSpec((1, tn), lambda i: (0, i))],
        out_specs=pl.BlockSpec((bsz, tn), lambda i: (0, i)),
        out_shape=jax.ShapeDtypeStruct((bsz, n), F32),
        compiler_params=pltpu.CompilerParams(dimension_semantics=("arbitrary",),
                                             vmem_limit_bytes=VMEM_LIMIT),
        name="ada",
    )(c, w, b.reshape(1, n))


def _time_perm(ts, transpose):
    ri = lax.broadcasted_iota(jnp.int32, (ts, ts), 0)
    ci = lax.broadcasted_iota(jnp.int32, (ts, ts), 1)
    strided, natural = (ci, ri) if transpose else (ri, ci)
    return jnp.where(natural == (ts // SUBLANES) * (strided % SUBLANES) + strided // SUBLANES, 1.0, 0.0).astype(BF16)


def _fill_conv_window(ext_ref, prev_ref, cur, n_taps):
    ts = cur.shape[0]
    lead = (n_taps - 1) * SUBLANES
    tail = cur[ts - lead:, :]
    sub = lax.broadcasted_iota(jnp.int32, tail.shape, 0) % SUBLANES
    merged = jnp.where(sub == SUBLANES - 1, prev_ref[...], tail)
    for g in range(n_taps - 1):
        rows = slice(g * SUBLANES, (g + 1) * SUBLANES)
        ext_ref[rows, :] = pltpu.roll(merged[rows, :], 1, 0)
    ext_ref[lead:lead + ts, :] = cur
    prev_ref[...] = tail


def _conv_block(ext_ref, w_ref, n_taps, r0, row_blk, c0, col_blk):
    groups = row_blk // SUBLANES
    acc = [jnp.zeros((SUBLANES, col_blk), F32) for _ in range(groups)]
    for k in range(n_taps):
        w = w_ref[k * SUBLANES:(k + 1) * SUBLANES, c0:c0 + col_blk]
        for g in range(groups):
            lo = r0 + (k + g) * SUBLANES
            acc[g] = acc[g] + w * ext_ref[lo:lo + SUBLANES, c0:c0 + col_blk]
    return jnp.concatenate(acc, axis=0)


def _mixer_kernel(x_ref, mod_ref, npre_ref, npost_ref, win_ref,
                  convw_ref, alog_ref, dtb_ref, dnw_ref, pw1b_ref, dww_ref, dwb_ref, lnw_ref, lnb_ref,
                  woutf_ref, o_ref, qkv_ext, qkv_prev, qkv_p16, qkv_act, cf_ext, cf_prev, cf_p16, state, mixed,
                  consts, wqkv_ref, wz_ref, wba_ref, wcf_ref, wout_ref):
    ts = x_ref.shape[2]
    n_chunks = ts // DN_CHUNK

    @pl.when(pl.program_id(1) == 0)
    def _():
        qkv_prev[...] = jnp.zeros(qkv_prev.shape, F32)
        cf_prev[...] = jnp.zeros(cf_prev.shape, F32)
        state[...] = jnp.zeros(state.shape, F32)

    @pl.when((pl.program_id(0) == 0) & (pl.program_id(1) == 0))
    def _():
        ri = lax.broadcasted_iota(jnp.int32, (ts, ts), 0)
        ci = lax.broadcasted_iota(jnp.int32, (ts, ts), 1)
        consts[0] = _time_perm(ts, False)
        consts[1] = _time_perm(ts, True)
        consts[2] = jnp.where((ri // DN_CHUNK == ci // DN_CHUNK) & (ci <= ri), 1.0, 0.0).astype(BF16)
        n_qkv, n_z, n_ba = wqkv_ref.shape[1], wz_ref.shape[1], 2 * DN_HEADS
        for c0 in range(0, n_qkv, DN_WIDTH):
            wqkv_ref[:, c0:c0 + DN_WIDTH] = win_ref[c0:c0 + DN_WIDTH, :].T.astype(BF16)
        wz_ref[...] = win_ref[n_qkv:n_qkv + n_z, :].T.astype(BF16)
        lane = lax.broadcasted_iota(jnp.int32, wba_ref.shape, 1)
        wba_ref[...] = jnp.where(lane < n_ba, win_ref[n_qkv + n_z:n_qkv + n_z + LANES, :].T, 0.0).astype(BF16)
        cf0 = n_qkv + n_z + n_ba
        for c0 in range(0, wcf_ref.shape[1], DN_WIDTH):
            wcf_ref[:, c0:c0 + DN_WIDTH] = win_ref[cf0 + c0:cf0 + c0 + DN_WIDTH, :].T.astype(BF16)
        wout_ref[...] = woutf_ref[...].astype(BF16)

    r64 = lax.broadcasted_iota(jnp.int32, (DN_CHUNK, DN_CHUNK), 0)
    c64 = lax.broadcasted_iota(jnp.int32, (DN_CHUNK, DN_CHUNK), 1)
    causal = c64 <= r64
    strict = c64 < r64
    eye = jnp.where(c64 == r64, 1.0, 0.0).astype(F32)
    dnw = dnw_ref[...]
    to_natural = consts[1]
    tri = consts[2]
    cells = [(hd, ch) for hd in range(DN_HEADS) for ch in range(n_chunks)]
    rows = lambda ch: slice(ch * DN_CHUNK, (ch + 1) * DN_CHUNK)

    def tile_stages(sid, qkv_ext, qkv_prev, qkv_p16, qkv_act, cf_ext, cf_prev, cf_p16, state, mixed):
        x = x_ref[sid, 0]
        mod = mod_ref[sid, 0]
        shift1, scale1, gate1 = mod[0:1], mod[1:2], mod[2:3]
        h = x * lax.rsqrt(jnp.mean(x * x, axis=-1, keepdims=True) + EPS)
        h = h * npre_ref[...] * (1.0 + scale1) + shift1
        hb = h.astype(BF16)
        yield
        hb_st = _dot(consts[0], hb).astype(BF16)

        _fill_conv_window(qkv_ext, qkv_prev, _dot(hb_st, wqkv_ref[...]), DN_CONV)
        yield
        for r0 in range(0, ts, CONV_ROWS):
            for c0 in range(0, qkv_ext.shape[1], DN_WIDTH):
                blk = _conv_block(qkv_ext, convw_ref, DN_CONV, r0, CONV_ROWS, c0, DN_WIDTH)
                qkv_p16[r0:r0 + CONV_ROWS, c0:c0 + DN_WIDTH] = _silu(blk).astype(BF16)
        yield
        qkv_act[...] = _dot(to_natural, qkv_p16[...])

        cf_pre = _dot(hb_st, wcf_ref[...]) + pw1b_ref[...]
        cfw = cf_ext.shape[1]
        _fill_conv_window(cf_ext, cf_prev, cf_pre[:, :cfw] * jax.nn.sigmoid(cf_pre[:, cfw:]), CF_KERNEL)
        cf_pending = list(range(0, ts, CONV_ROWS))
        yield

        def cf_step(n=1):
            for _ in range(min(n, len(cf_pending))):
                r0 = cf_pending.pop(0)
                cf = _conv_block(cf_ext, dww_ref, CF_KERNEL, r0, CONV_ROWS, 0, cfw) + dwb_ref[...]
                mu = jnp.mean(cf, axis=-1, keepdims=True)
                xc = cf - mu
                var = jnp.mean(xc * xc, axis=-1, keepdims=True)
                cfn = xc * lax.rsqrt(var + EPS) * lnw_ref[...] + lnb_ref[...]
                cf_p16[r0:r0 + CONV_ROWS, :] = _silu(cfn).astype(BF16)

        z = _dot(hb, wz_ref[...])
        ba = _dot(hb, wba_ref[...])
        beta_all = jax.nn.sigmoid(ba)
        g_all = -jnp.exp(alog_ref[...]) * _softplus(ba + dtb_ref[...])

        g_hi, g_mid, g_lo = _split3(g_all)
        gcum = _dot(tri, g_hi) + _dot(tri, g_mid) + _dot(tri, g_lo)
        gcum_t = gcum.T
        exp_g = jnp.exp(gcum)
        yield

        heads = []
        for hd in range(DN_HEADS):
            lo = hd * HEAD_DIM
            qh = qkv_act[:, lo:lo + HEAD_DIM]
            kh = qkv_act[:, DN_WIDTH + lo:DN_WIDTH + lo + HEAD_DIM]
            vh = qkv_act[:, 2 * DN_WIDTH + lo:2 * DN_WIDTH + lo + HEAD_DIM]
            qn = qh * lax.rsqrt(jnp.sum(qh * qh, axis=-1, keepdims=True) + EPS) * (HEAD_DIM ** -0.5)
            kn = kh * lax.rsqrt(jnp.sum(kh * kh, axis=-1, keepdims=True) + EPS)
            beta_h = beta_all[:, hd:hd + 1]
            gc_h = gcum[:, DN_HEADS + hd:DN_HEADS + hd + 1]
            eg_h = exp_g[:, DN_HEADS + hd:DN_HEADS + hd + 1]
            k_beta = kn * beta_h
            heads.append(dict(qn=qn, kn=kn, k_beta=k_beta, v_beta=vh * beta_h, kbg=k_beta * eg_h, qg=qn * eg_h,
                              gc=gc_h, zg=_silu(z[:, lo:lo + HEAD_DIM])))
        yield

        decay, kq = {}, {}
        for hd, ch in cells:
            hv, sl = heads[hd], rows(ch)
            gc_row = gcum_t[DN_HEADS + hd:DN_HEADS + hd + 1, sl]
            decay[hd, ch] = jnp.where(causal, jnp.exp(hv["gc"][sl] - gc_row), 0.0)
            lhs = jnp.concatenate([hv["k_beta"][sl], hv["qn"][sl]], axis=0).astype(BF16)
            kq[hd, ch] = _dot_nt(lhs, hv["kn"][sl].astype(BF16))
        a = {c: jnp.where(strict, kq[c][:DN_CHUNK] * decay[c], 0.0) for c in cells}
        attn = {c: (kq[c][DN_CHUNK:] * decay[c]).astype(BF16) for c in cells}
        yield

        t_inv = {c: eye - a[c] for c in cells}
        pw = {c: a[c].astype(BF16) for c in cells}
        pw = {c: _dot(pw[c], pw[c]).astype(BF16) for c in cells}
        cf_step()
        yield
        for _ in range(4):
            both = {c: _dot(jnp.concatenate([t_inv[c].astype(BF16), pw[c]], axis=0), pw[c]) for c in cells}
            t_inv = {c: t_inv[c] + both[c][:DN_CHUNK] for c in cells}
            pw = {c: both[c][DN_CHUNK:].astype(BF16) for c in cells}
            cf_step()
            yield
        t_inv = {c: t_inv[c] + _dot(t_inv[c].astype(BF16), pw[c]) for c in cells}

        sol, aw, ks, glast = {}, {}, {}, {}
        for hd, ch in cells:
            hv, sl = heads[hd], rows(ch)
            rhs = jnp.concatenate([hv["v_beta"][sl], hv["kbg"][sl]], axis=1).astype(BF16)
            sol[hd, ch] = _dot(t_inv[hd, ch].astype(BF16), rhs).astype(BF16)
        yield
        for hd, ch in cells:
            hv, sl = heads[hd], rows(ch)
            gc_col = hv["gc"][sl]
            glast[hd, ch] = gc_col[DN_CHUNK - 1:DN_CHUNK]
            k_dec = (hv["kn"][sl] * jnp.exp(glast[hd, ch] - gc_col)).astype(BF16)
            aw[hd, ch] = _dot(attn[hd, ch], sol[hd, ch])
            ks[hd, ch] = _dot_tn(k_dec, sol[hd, ch])
        yield

        s_in = {}
        s_cur = [state[hd] for hd in range(DN_HEADS)]
        for ch in range(n_chunks):
            for hd in range(DN_HEADS):
                s_in[hd, ch] = s_cur[hd].astype(BF16)
                kd_u, kd_w = ks[hd, ch][:, :HEAD_DIM], ks[hd, ch][:, HEAD_DIM:]
                s_cur[hd] = s_cur[hd] * jnp.exp(glast[hd, ch]) + kd_u - _dot(kd_w.astype(BF16), s_in[hd, ch])
            cf_step()
            yield
        for hd in range(DN_HEADS):
            state[hd] = s_cur[hd]

        for hd, ch in cells:
            hv, sl = heads[hd], rows(ch)
            lo = hd * HEAD_DIM
            q_eff = (hv["qg"][sl] - aw[hd, ch][:, HEAD_DIM:]).astype(BF16)
            o = _dot(q_eff, s_in[hd, ch]) + aw[hd, ch][:, :HEAD_DIM]
            on = o * lax.rsqrt(jnp.mean(o * o, axis=-1, keepdims=True) + EPS) * dnw * hv["zg"][sl]
            mixed[sl, lo:lo + HEAD_DIM] = on.astype(BF16)
        yield

        cf_step(len(cf_pending))
        mixed[:, DN_WIDTH:DN_WIDTH + cfw] = _dot(to_natural, cf_p16[...]).astype(BF16)
        yield

        out = _dot(mixed[...], wout_ref[...])
        yield
        y = out * lax.rsqrt(jnp.mean(out * out, axis=-1, keepdims=True) + EPS) * npost_ref[...]
        o_ref[sid, 0] = x + gate1 * y

    scratch = (qkv_ext, qkv_prev, qkv_p16, qkv_act, cf_ext, cf_prev, cf_p16, state, mixed)
    streams = [tile_stages(sid, *(ref.at[sid] for ref in scratch)) for sid in range(x_ref.shape[0])]
    live = [True] * len(streams)

    def advance(i, n=1):
        for _ in range(n):
            if live[i]:
                try:
                    next(streams[i])
                except StopIteration:
                    live[i] = False

    advance(0, MIXER_SKEW)
    while any(live):
        for i in range(len(streams)):
            advance(i)


def _mixer_call(x, mod3, npre, npost, w_in, dn_conv_w, dn_a_log, dn_dt_bias, dn_norm_w,
                cf_pw1_b, cf_dw_w, cf_dw_b, cf_ln_w, cf_ln_b, w_out):
    bsz, seq, d = x.shape
    ts = SEQ_TILE
    cfw = cf_dw_w.shape[1]
    n_qkv = 3 * DN_WIDTH
    alog = jnp.pad(dn_a_log, (DN_HEADS, LANES - 2 * DN_HEADS)).reshape(1, LANES)
    dtb = jnp.pad(dn_dt_bias, (DN_HEADS, LANES - 2 * DN_HEADS)).reshape(1, LANES)
    dww = jnp.repeat(cf_dw_w, SUBLANES, axis=0)
    convw = jnp.repeat(dn_conv_w, SUBLANES, axis=0)

    def full(a):
        mode = dict(pipeline_mode=pl.Buffered(1)) if a.size >= d * d else {}
        return pl.BlockSpec(a.shape, lambda b, j: (0,) * a.ndim, **mode)

    row = lambda a: a.reshape(1, -1)
    nb = MIXER_PAIR
    x4 = x.reshape(nb, bsz // nb, seq, d)
    mod4 = mod3.reshape((nb, bsz // nb) + mod3.shape[1:])
    operands = [x4, mod4, row(npre), row(npost), w_in.T, convw, alog, dtb, row(dn_norm_w),
                row(cf_pw1_b), dww, row(cf_dw_b), row(cf_ln_w), row(cf_ln_b), w_out]
    in_specs = [pl.BlockSpec((nb, 1, ts, d), lambda b, j: (0, b, j, 0)),
                pl.BlockSpec((nb, 1) + mod3.shape[1:], lambda b, j: (0, b, 0, 0))]
    in_specs += [full(a) for a in operands[2:]]
    out = pl.pallas_call(
        _mixer_kernel,
        grid=(bsz // nb, seq // ts),
        in_specs=in_specs,
        out_specs=pl.BlockSpec((nb, 1, ts, d), lambda b, j: (0, b, j, 0)),
        out_shape=jax.ShapeDtypeStruct((nb, bsz // nb, seq, d), F32),
        scratch_shapes=[pltpu.VMEM((nb, (DN_CONV - 1) * SUBLANES + ts, n_qkv), F32),
                        pltpu.VMEM((nb, (DN_CONV - 1) * SUBLANES, n_qkv), F32),
                        pltpu.VMEM((nb, ts, n_qkv), BF16),
                        pltpu.VMEM((nb, ts, n_qkv), F32),
                        pltpu.VMEM((nb, (CF_KERNEL - 1) * SUBLANES + ts, cfw), F32),
                        pltpu.VMEM((nb, (CF_KERNEL - 1) * SUBLANES, cfw), F32),
                        pltpu.VMEM((nb, ts, cfw), BF16),
                        pltpu.VMEM((nb, DN_HEADS, HEAD_DIM, HEAD_DIM), F32),
                        pltpu.VMEM((nb, ts, DN_WIDTH + cfw), BF16),
                        pltpu.VMEM((3, ts, ts), BF16),
                        pltpu.VMEM((d, n_qkv), BF16), pltpu.VMEM((d, DN_WIDTH), BF16), pltpu.VMEM((d, LANES), BF16),
                        pltpu.VMEM((d, 2 * cfw), BF16), pltpu.VMEM(w_out.shape, BF16)],
        compiler_params=pltpu.CompilerParams(dimension_semantics=("arbitrary", "arbitrary"),
                                             vmem_limit_bytes=VMEM_LIMIT),
        name="mixer",
    )(*operands)
    return out.reshape(bsz, seq, d)


def _router_kernel(x_ref, mod_ref, nw_ref, wr_ref, br_ref, h_ref, info_ref, infot_ref, cnt_ref, carry, strict_ref):
    tt = x_ref.shape[1]

    @pl.when((pl.program_id(0) == 0) & (pl.program_id(1) == 0))
    def _():
        carry[...] = jnp.zeros(carry.shape, F32)
        rr = lax.broadcasted_iota(jnp.int32, (tt, tt), 0)
        cc = lax.broadcasted_iota(jnp.int32, (tt, tt), 1)
        strict_ref[...] = jnp.where(cc < rr, 1.0, 0.0).astype(BF16)

    x = x_ref[0]
    mod = mod_ref[0]
    shift2, scale2 = mod[3:4], mod[4:5]
    h = x * lax.rsqrt(jnp.mean(x * x, axis=-1, keepdims=True) + EPS)
    h = h * nw_ref[...] * (1.0 + scale2) + shift2
    _store_token_major(h_ref, h)

    h_hi, h_lo, _ = _split3(h)
    w_hi, w_lo, _ = _split3(wr_ref[...])
    logits = _dot(h_hi, w_hi) + (_dot(h_hi, w_lo) + _dot(h_lo, w_hi)) + br_ref[...]

    lane = lax.broadcasted_iota(jnp.int32, (tt, LANES), 1)
    neg = -jnp.inf
    is_grp = (lane >= N_EXPERTS) & (lane < N_EXPERTS + N_GROUPS)
    gl = jnp.where(is_grp, logits, neg)
    gmax = jnp.max(gl, axis=-1, keepdims=True)
    gsum = jnp.sum(jnp.where(is_grp, jnp.exp(gl - gmax), 0.0), axis=-1, keepdims=True)
    grp_p = 1.0 / gsum
    grp_lane = jnp.min(jnp.where(is_grp & (gl == gmax), lane, LANES), axis=-1, keepdims=True)
    grp_idx = grp_lane - N_EXPERTS

    in_grp = (lane < N_EXPERTS) & (lane // EXPERTS_PER_GROUP == grp_idx)
    el = jnp.where(in_grp, logits, neg)
    m1 = jnp.max(el, axis=-1, keepdims=True)
    e1 = jnp.min(jnp.where(in_grp & (el == m1), lane, LANES), axis=-1, keepdims=True)
    el2 = jnp.where(lane == e1, neg, el)
    m2 = jnp.max(el2, axis=-1, keepdims=True)
    e2 = jnp.min(jnp.where(in_grp & (lane != e1) & (el2 == m2), lane, LANES), axis=-1, keepdims=True)
    r = jnp.exp(m2 - m1)
    w1 = grp_p / (1.0 + r)
    w2 = grp_p * r / (1.0 + r)

    hit1 = lane == e1
    hit2 = lane == e2
    onehot = jnp.where(hit1 | hit2, 1.0, 0.0)
    prefix = _dot(strict_ref[...], onehot.astype(BF16)) + carry[0:1, :]
    rank1 = jnp.sum(jnp.where(hit1, prefix, 0.0), axis=-1, keepdims=True)
    rank2 = jnp.sum(jnp.where(hit2, prefix, 0.0), axis=-1, keepdims=True)
    total = carry[0:1, :] + jnp.sum(onehot, axis=0, keepdims=True)
    carry[...] = jnp.broadcast_to(total, carry.shape)
    cnt_ref[...] = jnp.broadcast_to(total, cnt_ref.shape)

    info = jnp.where(lane == 0, e1.astype(F32), 0.0)
    info = jnp.where(lane == 1, e2.astype(F32), info)
    info = jnp.where(lane == 2, w1, info)
    info = jnp.where(lane == 3, w2, info)
    info = jnp.where(lane == 4, rank1, info)
    info = jnp.where(lane == 5, rank2, info)
    info_ref[0] = info
    infot_ref[...] = info.T[0:SUBLANES, :]


def _router_call(x1, mod3, norm_w, w_router_group, b_router_group, w_router_expert, b_router_expert):
    bsz, seq, d = x1.shape
    tt = ROUTER_TILE
    nj = seq // tt
    pitch = d // LANES
    pad = LANES - N_EXPERTS - N_GROUPS
    wr = jnp.pad(jnp.concatenate([w_router_expert, w_router_group], axis=1), ((0, 0), (0, pad)))
    br = jnp.pad(jnp.concatenate([b_router_expert, b_router_group]), (0, pad)).reshape(1, LANES)
    return pl.pallas_call(
        _router_kernel,
        grid=(bsz, seq // tt),
        in_specs=[pl.BlockSpec((1, tt, d), lambda b, j: (b, j, 0)),
                  pl.BlockSpec((1,) + mod3.shape[1:], lambda b, j: (b, 0, 0)),
                  pl.BlockSpec((1, d), lambda b, j: (0, 0)),
                  pl.BlockSpec((d, LANES), lambda b, j: (0, 0)),
                  pl.BlockSpec((1, LANES), lambda b, j: (0, 0))],
        out_specs=[pl.BlockSpec((tt * pitch, LANES), lambda b, j: (b * nj + j, 0)),
                   pl.BlockSpec((1, tt, LANES), lambda b, j: (b, j, 0)),
                   pl.BlockSpec((SUBLANES, tt), lambda b, j: (0, b * nj + j)),
                   pl.BlockSpec((8, LANES), lambda b, j: (0, 0))],
        out_shape=[jax.ShapeDtypeStruct((bsz * seq * pitch, LANES), F32),
                   jax.ShapeDtypeStruct((bsz, seq, LANES), F32),
                   jax.ShapeDtypeStruct((SUBLANES, bsz * seq), F32),
                   jax.ShapeDtypeStruct((8, LANES), F32)],
        scratch_shapes=[pltpu.VMEM((8, LANES), F32), pltpu.VMEM((tt, tt), BF16)],
        compiler_params=pltpu.CompilerParams(dimension_semantics=("arbitrary", "arbitrary"),
                                             vmem_limit_bytes=VMEM_LIMIT),
        name="router",
    )(x1, mod3, norm_w.reshape(1, d), wr, br)


def _tile_rows(idx, pitch):
    if isinstance(idx, int):
        return pl.ds(idx * pitch, pitch)
    return pl.ds(pl.multiple_of(idx * pitch, pitch), pitch)


def _dispatch_kernel(nused_ref, padfrom_ref, padlen_ref, pos_ref, h_ref, xs_hbm, zbuf, zsem, sem, *, pitch, bm):
    step = pl.program_id(0)
    tt = h_ref.shape[0] // pitch

    def fill_copies(action):
        def per_expert(e, carry):
            cursor = padfrom_ref[e]
            n = padlen_ref[e]
            bit = bm // 2
            while bit >= 1:
                take = (n & bit) != 0

                @pl.when(take)
                def _(cursor=cursor, bit=bit):
                    action(pltpu.make_async_copy(zbuf.at[pl.ds(0, bit * pitch), :],
                                                 xs_hbm.at[pl.ds(pl.multiple_of(cursor * pitch, pitch), bit * pitch), :],
                                                 zsem.at[0]))
                cursor = cursor + jnp.where(take, bit, 0)
                bit //= 2
            return carry

        def per_block(b, carry):
            start = pl.multiple_of(b * bm * pitch, bm * pitch)
            action(pltpu.make_async_copy(zbuf, xs_hbm.at[pl.ds(start, bm * pitch), :], zsem.at[0]))
            return carry

        lax.fori_loop(0, N_EXPERTS, per_expert, 0)
        lax.fori_loop(nused_ref[0], xs_hbm.shape[0] // (bm * pitch), per_block, 0)

    @pl.when(step == 0)
    def _():
        zbuf[...] = jnp.zeros(zbuf.shape, F32)
        fill_copies(lambda cp: cp.start())

    for r in range(tt):
        for k in range(TOP_K):
            p = pos_ref[0, 0, k * tt + r]
            pltpu.make_async_copy(h_ref.at[_tile_rows(r, pitch), :], xs_hbm.at[_tile_rows(p, pitch), :],
                                  sem.at[0]).start(priority=k)
    for k in range(TOP_K):
        pltpu.make_async_copy(h_ref, xs_hbm.at[pl.ds(0, tt * pitch), :], sem.at[0]).wait()

    @pl.when(step == pl.num_programs(0) - 1)
    def _():
        fill_copies(lambda cp: cp.wait())


def _dispatch_call(h2, pos3, n_used, padfrom, padlen, n_rows, d):
    pitch = d // LANES
    bm = EXPERT_BLOCK
    n_tiles, _, two_tt = pos3.shape
    tt = two_tt // TOP_K
    grid_spec = pltpu.PrefetchScalarGridSpec(
        num_scalar_prefetch=3,
        grid=(n_tiles,),
        in_specs=[pl.BlockSpec((1, 1, two_tt), lambda s, *_: (s, 0, 0), memory_space=pltpu.SMEM),
                  pl.BlockSpec((tt * pitch, LANES), lambda s, *_: (s, 0))],
        out_specs=pl.BlockSpec(memory_space=pl.ANY),
        scratch_shapes=[pltpu.VMEM((bm * pitch, LANES), F32), pltpu.SemaphoreType.DMA((1,)),
                        pltpu.SemaphoreType.DMA((1,))],
    )
    return pl.pallas_call(
        functools.partial(_dispatch_kernel, pitch=pitch, bm=bm),
        grid_spec=grid_spec,
        out_shape=jax.ShapeDtypeStruct((n_rows * pitch, LANES), F32),
        compiler_params=pltpu.CompilerParams(dimension_semantics=("arbitrary",),
                                             vmem_limit_bytes=VMEM_LIMIT),
        name="dispatch",
    )(n_used, padfrom, padlen, pos3, h2)


def _expert_kernel(nused_ref, nruns_ref, first_ref, run_ref, rune_ref, x_ref, wg_hbm, wu_hbm, wd_hbm, y_ref,
                   wg_f32, wu_f32, wd_f32, wg_b, wu_b, wd_b, wsems):
    i = pl.program_id(0)
    n_used = nused_ref[0]
    n_runs = nruns_ref[0]
    d = wg_b.shape[0]
    pitch = d // LANES
    bm = y_ref.shape[0] // pitch
    n_slots = wg_f32.shape[0]

    def weight_copies(e, slot):
        return [pltpu.make_async_copy(src.at[e], dst.at[slot], wsems.at[slot])
                for src, dst in ((wg_hbm, wg_f32), (wu_hbm, wu_f32), (wd_hbm, wd_f32))]

    def start_run(r):
        @pl.when(r < n_runs)
        def _():
            for prio, cp in zip((1, 1, 0), weight_copies(rune_ref[r], r % n_slots)):
                cp.start(priority=prio)

    @pl.when(i == 0)
    def _():
        for r in range(n_slots - 1):
            start_run(r)

    @pl.when((i < n_used) & (first_ref[i] == 1))
    def _():
        run = run_ref[i]
        slot = run % n_slots
        for cp in weight_copies(0, slot):
            cp.wait()
        start_run(run + n_slots - 1)

        wg_b[...] = wg_f32[slot].astype(BF16)
        wu_b[...] = wu_f32[slot].astype(BF16)
        wd_b[...] = wd_f32[slot].astype(BF16)

    @pl.when(i < n_used)
    def _():
        xb = _load_token_major(x_ref, bm, d).astype(BF16)
        gate = _dot(xb, wg_b[...])
        up = _dot(xb, wu_b[...])
        hid = (_silu(gate) * up).astype(BF16)
        _store_token_major(y_ref, _dot(hid, wd_b[...]))

    @pl.when(i >= n_used)
    def _():
        y_ref[...] = jnp.zeros(y_ref.shape, F32)


def _expert_call(xs, block_e, n_used, w_gate, w_up, w_down):
    bm = EXPERT_BLOCK
    d, de = w_gate.shape[1], w_gate.shape[2]
    pitch = d // LANES
    n_blocks = xs.shape[0] // (bm * pitch)
    blk = jnp.arange(n_blocks, dtype=jnp.int32)
    used = blk < n_used[0]
    first = jnp.concatenate([jnp.ones((1,), jnp.int32), (block_e[1:] != block_e[:-1]).astype(jnp.int32)])
    first = jnp.where(used, first, 0)
    run = (jnp.cumsum(first) - 1).astype(jnp.int32)
    n_runs = jnp.sum(first).astype(jnp.int32).reshape(1)
    eids = jnp.arange(N_EXPERTS, dtype=jnp.int32)
    run_of_e = jnp.where((first[:, None] == 1) & (block_e[:, None] == eids[None, :]), run[:, None], -1).max(axis=0)
    run_e = jnp.sum(jnp.where(run_of_e[None, :] == eids[:, None], eids[None, :], 0), axis=1).astype(jnp.int32)
    grid_spec = pltpu.PrefetchScalarGridSpec(
        num_scalar_prefetch=5,
        grid=(n_blocks,),
        in_specs=[pl.BlockSpec((bm * pitch, LANES), lambda i, nu, *_: (jnp.minimum(i, nu[0] - 1), 0)),
                  pl.BlockSpec(memory_space=pl.ANY),
                  pl.BlockSpec(memory_space=pl.ANY),
                  pl.BlockSpec(memory_space=pl.ANY)],
        out_specs=pl.BlockSpec((bm * pitch, LANES), lambda i, *_: (i, 0)),
        scratch_shapes=[pltpu.VMEM((WEIGHT_SLOTS, d, de), F32), pltpu.VMEM((WEIGHT_SLOTS, d, de), F32),
                        pltpu.VMEM((WEIGHT_SLOTS, de, d), F32),
                        pltpu.VMEM((d, de), BF16), pltpu.VMEM((d, de), BF16), pltpu.VMEM((de, d), BF16),
                        pltpu.SemaphoreType.DMA((WEIGHT_SLOTS,))],
    )
    return pl.pallas_call(
        _expert_kernel,
        grid_spec=grid_spec,
        out_shape=jax.ShapeDtypeStruct((n_blocks * bm * pitch, LANES), F32),
        compiler_params=pltpu.CompilerParams(dimension_semantics=("arbitrary",),
                                             vmem_limit_bytes=VMEM_LIMIT),
        name="experts",
    )(n_used, n_runs, first, run, run_e, xs, w_gate, w_up, w_down)


def _combine_kernel(pos_cur_ref, pos_nxt_ref, x_ref, info_ref, mod_ref, nw_ref, y_hbm, o_ref, ybuf_a, ybuf_b, sems):
    tt, d = x_ref.shape[1], x_ref.shape[2]
    pitch = d // LANES
    step = pl.program_id(0) * pl.num_programs(1) + pl.program_id(1)
    n_steps = pl.num_programs(0) * pl.num_programs(1)
    n_chunks = tt // COMBINE_ROWS

    def row_copy(pos_ref, buf, sem, k, r):
        p = pos_ref[0, 0, k * tt + r]
        return pltpu.make_async_copy(y_hbm.at[_tile_rows(p, pitch), :], buf.at[_tile_rows(k * tt + r, pitch), :], sem)

    def wait_tile(buf, sem):
        pltpu.make_async_copy(y_hbm.at[pl.ds(0, TOP_K * tt * pitch), :], buf, sem).wait()

    @pl.when(step == 0)
    def _():
        def body(r, carry):
            for k in range(TOP_K):
                row_copy(pos_cur_ref, ybuf_a, sems.at[0], k, r).start(priority=k)
            return carry
        lax.fori_loop(0, tt, body, 0, unroll=4)

    gate2 = mod_ref[0][5:6]

    def run_tile(cur, cur_sem, nxt, nxt_sem):
        wait_tile(cur, cur_sem)

        def chunk(c, carry):
            r0 = c * COMBINE_ROWS
            for r in range(COMBINE_ROWS):
                for k in range(TOP_K):
                    row_copy(pos_nxt_ref, nxt, nxt_sem, k, r0 + r).start(priority=k)
            rows = pl.ds(r0, COMBINE_ROWS)
            info = info_ref[0, rows, :]
            moe = (_load_token_major(cur, COMBINE_ROWS, d, r0 * pitch) * info[:, 2:3]
                   + _load_token_major(cur, COMBINE_ROWS, d, (tt + r0) * pitch) * info[:, 3:4])
            y = moe * lax.rsqrt(jnp.mean(moe * moe, axis=-1, keepdims=True) + EPS) * nw_ref[...]
            o_ref[0, rows, :] = x_ref[0, rows, :] + gate2 * y
            return carry
        for c in range(n_chunks):
            chunk(c, 0)

        @pl.when(step == n_steps - 1)
        def _():
            wait_tile(nxt, nxt_sem)

    @pl.when(step % 2 == 0)
    def _():
        run_tile(ybuf_a, sems.at[0], ybuf_b, sems.at[1])

    @pl.when(step % 2 == 1)
    def _():
        run_tile(ybuf_b, sems.at[1], ybuf_a, sems.at[0])


def _combine_call(x1, info, mod3, norm_w, y, pos3):
    bsz, seq, d = x1.shape
    tt = COMBINE_TILE
    nj = seq // tt
    n_tiles = bsz * nj
    pos_blk = lambda f: pl.BlockSpec((1, 1, TOP_K * tt), f, memory_space=pltpu.SMEM)
    return pl.pallas_call(
        _combine_kernel,
        grid=(bsz, nj),
        in_specs=[pos_blk(lambda b, j: (b * nj + j, 0, 0)),
                  pos_blk(lambda b, j: (jnp.minimum(b * nj + j + 1, n_tiles - 1), 0, 0)),
                  pl.BlockSpec((1, tt, d), lambda b, j: (b, j, 0)),
                  pl.BlockSpec((1, tt, LANES), lambda b, j: (b, j, 0)),
                  pl.BlockSpec((1,) + mod3.shape[1:], lambda b, j: (b, 0, 0)),
                  pl.BlockSpec((1, d), lambda b, j: (0, 0)),
                  pl.BlockSpec(memory_space=pl.ANY)],
        out_specs=pl.BlockSpec((1, tt, d), lambda b, j: (b, j, 0)),
        out_shape=jax.ShapeDtypeStruct((bsz, seq, d), F32),
        scratch_shapes=[pltpu.VMEM((TOP_K * tt * (d // LANES), LANES), F32),
                        pltpu.VMEM((TOP_K * tt * (d // LANES), LANES), F32), pltpu.SemaphoreType.DMA((2,))],
        compiler_params=pltpu.CompilerParams(dimension_semantics=("arbitrary", "arbitrary"),
                                             vmem_limit_bytes=VMEM_LIMIT),
        name="combine",
    )(pos3, pos3, x1, info, mod3, norm_w.reshape(1, d), y)


def _layer(x, mod, norm_pre_mix, norm_post_mix, w_in, dn_conv_w, dn_a_log, dn_dt_bias, dn_norm_w,
           cf_pw1_b, cf_dw_w, cf_dw_b, cf_ln_w, cf_ln_b, w_out, norm_pre_ffn, norm_post_ffn,
           w_router_group, b_router_group, w_router_expert, b_router_expert, w_gate, w_up, w_down):
    bsz, seq, d = x.shape
    t = bsz * seq
    mod3 = mod.reshape(bsz, -1, d)
    x1 = _mixer_call(x, mod3, norm_pre_mix, norm_post_mix, w_in, dn_conv_w, dn_a_log, dn_dt_bias, dn_norm_w,
                     cf_pw1_b, cf_dw_w, cf_dw_b, cf_ln_w, cf_ln_b, w_out)
    h2, info, info_t, cnt = _router_call(x1, mod3, norm_pre_ffn, w_router_group, b_router_group,
                                         w_router_expert, b_router_expert)

    bm = EXPERT_BLOCK
    expert_id = info_t[0:TOP_K].astype(jnp.int32)
    rank = info_t[4:4 + TOP_K].astype(jnp.int32)
    counts = cnt[0, :N_EXPERTS].astype(jnp.int32)
    padded = (counts + bm - 1) // bm * bm
    pend = jnp.cumsum(padded)
    pstart = pend - padded
    eids = jnp.arange(N_EXPERTS, dtype=jnp.int32)
    onehot = expert_id[:, None, :] == eids[None, :, None]
    pos = jnp.sum(jnp.where(onehot, pstart[None, :, None], 0), axis=1) + rank

    def tiled(tt):
        return pos.reshape(TOP_K, t // tt, tt).transpose(1, 0, 2).reshape(t // tt, 1, TOP_K * tt)
    n_blocks = -(-(t * TOP_K) // bm) + N_EXPERTS
    block_start = jnp.arange(n_blocks, dtype=jnp.int32) * bm
    block_e = jnp.minimum(jnp.sum(pend[None, :] <= block_start[:, None], axis=1), N_EXPERTS - 1).astype(jnp.int32)
    n_used = (pend[-1] // bm).astype(jnp.int32).reshape(1)

    xs = _dispatch_call(h2, tiled(DISPATCH_TILE), n_used, (pstart + counts).astype(jnp.int32),
                        (padded - counts).astype(jnp.int32), n_blocks * bm, d)
    y = _expert_call(xs, block_e, n_used, w_gate, w_up, w_down)
    return _combine_call(x1, info, mod3, norm_post_ffn, y, tiled(COMBINE_TILE))


def kernel(x, c, w_ada, b_ada, norm_pre_mix, norm_post_mix, w_in, dn_conv_w, dn_a_log, dn_dt_bias, dn_norm_w,
           cf_pw1_b, cf_dw_w, cf_dw_b, cf_ln_w, cf_ln_b, w_out, norm_pre_ffn, norm_post_ffn,
           w_router_group, b_router_group, w_router_expert, b_router_expert, w_gate, w_up, w_down):
    depth = w_ada.shape[0]
    for l in range(depth):
        mod = _ada_call(c, w_ada[l], b_ada[l])
        x = _layer(x, mod, norm_pre_mix[l], norm_post_mix[l], w_in[l], dn_conv_w[l], dn_a_log[l],
                   dn_dt_bias[l], dn_norm_w[l], cf_pw1_b[l], cf_dw_w[l], cf_dw_b[l], cf_ln_w[l], cf_ln_b[l],
                   w_out[l], norm_pre_ffn[l], norm_post_ffn[l], w_router_group[l], b_router_group[l],
                   w_router_expert[l], b_router_expert[l], w_gate[l], w_up[l], w_down[l])
    return x
```

```python
import functools

import jax
import jax.numpy as jnp
from jax import lax
from jax.experimental import pallas as pl
from jax.experimental.pallas import tpu as pltpu

F32 = jnp.float32
BF16 = jnp.bfloat16
EPS = 1e-6

DN_HEADS = 4
HEAD_DIM = 128
DN_WIDTH = DN_HEADS * HEAD_DIM
DN_CONV = 4
DN_CHUNK = 64
CF_KERNEL = 31
N_GROUPS = 8
EXPERTS_PER_GROUP = 8
N_EXPERTS = N_GROUPS * EXPERTS_PER_GROUP
TOP_K = 2

LANES = 128
SUBLANES = 8
SEQ_TILE = 256
CONV_ROWS = 32
MIXER_PAIR = 2
MIXER_SKEW = 4
ROUTER_TILE = 512
EXPERT_BLOCK = 256
COMBINE_TILE = 512
COMBINE_ROWS = 32
WEIGHT_SLOTS = 3
ROW_SLOTS_IN = 3
ROW_SLOTS_OUT = 2
DISPATCH_TILE = 1024
VMEM_LIMIT = 56 * 1024 * 1024


def _dot(a, b):
    return jnp.dot(a, b, preferred_element_type=F32)


def _dot_nt(a, b):
    return lax.dot_general(a, b, (((1,), (1,)), ((), ())), preferred_element_type=F32)


def _dot_tn(a, b):
    return lax.dot_general(a, b, (((0,), (0,)), ((), ())), preferred_element_type=F32)


def _split3(x):
    hi = x.astype(BF16)
    r1 = x - hi.astype(F32)
    mid = r1.astype(BF16)
    lo = (r1 - mid.astype(F32)).astype(BF16)
    return hi, mid, lo


def _silu(x):
    return x * jax.nn.sigmoid(x)


def _softplus(x):
    return jnp.maximum(x, 0.0) + jnp.log1p(jnp.exp(-jnp.abs(x)))


def _store_token_major(ref, val, base=0):
    n, d = val.shape
    pitch = d // LANES
    for j in range(pitch):
        ref[pl.ds(base + j, n, stride=pitch), :] = val[:, j * LANES:(j + 1) * LANES]


def _load_token_major(ref, n, d, base=0):
    pitch = d // LANES
    return jnp.concatenate([ref[pl.ds(base + j, n, stride=pitch), :] for j in range(pitch)], axis=1)


def _ada_kernel(c_ref, w_ref, b_ref, o_ref):
    c = c_ref[...]
    ca = _silu(c)
    c_hi, c_lo, _ = _split3(ca)
    w_hi, w_lo, _ = _split3(w_ref[...])
    o_ref[...] = _dot(c_hi, w_hi) + (_dot(c_hi, w_lo) + _dot(c_lo, w_hi)) + b_ref[...]


def _ada_call(c, w, b):
    bsz, d = c.shape
    n = w.shape[1]
    tn = 512
    return pl.pallas_call(
        _ada_kernel,
        grid=(n // tn,),
        in_specs=[pl.BlockSpec((bsz, d), lambda i: (0, 0)),
                  pl.BlockSpec((d, tn), lambda i: (0, i)),
                  pl.BlockSpec((1, tn), lambda i: (0, i))],
        out_specs=pl.BlockSpec((bsz, tn), lambda i: (0, i)),
        out_shape=jax.ShapeDtypeStruct((bsz, n), F32),
        compiler_params=pltpu.CompilerParams(dimension_semantics=("arbitrary",),
                                             vmem_limit_bytes=VMEM_LIMIT),
        name="ada",
    )(c, w, b.reshape(1, n))


def _time_perm(ts, transpose):
    ri = lax.broadcasted_iota(jnp.int32, (ts, ts), 0)
    ci = lax.broadcasted_iota(jnp.int32, (ts, ts), 1)
    strided, natural = (ci, ri) if transpose else (ri, ci)
    return jnp.where(natural == (ts // SUBLANES) * (strided % SUBLANES) + strided // SUBLANES, 1.0, 0.0).astype(BF16)


def _fill_conv_window(ext_ref, prev_ref, cur, n_taps):
    ts = cur.shape[0]
    lead = (n_taps - 1) * SUBLANES
    tail = cur[ts - lead:, :]
    sub = lax.broadcasted_iota(jnp.int32, tail.shape, 0) % SUBLANES
    merged = jnp.where(sub == SUBLANES - 1, prev_ref[...], tail)
    for g in range(n_taps - 1):
        rows = slice(g * SUBLANES, (g + 1) * SUBLANES)
        ext_ref[rows, :] = pltpu.roll(merged[rows, :], 1, 0)
    ext_ref[lead:lead + ts, :] = cur
    prev_ref[...] = tail


def _conv_block(ext_ref, w_ref, n_taps, r0, row_blk, c0, col_blk):
    groups = row_blk // SUBLANES
    acc = [jnp.zeros((SUBLANES, col_blk), F32) for _ in range(groups)]
    for k in range(n_taps):
        w = w_ref[k * SUBLANES:(k + 1) * SUBLANES, c0:c0 + col_blk]
        for g in range(groups):
            lo = r0 + (k + g) * SUBLANES
            acc[g] = acc[g] + w * ext_ref[lo:lo + SUBLANES, c0:c0 + col_blk]
    return jnp.concatenate(acc, axis=0)


def _mixer_kernel(x_ref, mod_ref, npre_ref, npost_ref, win_ref,
                  convw_ref, alog_ref, dtb_ref, dnw_ref, pw1b_ref, dww_ref, dwb_ref, lnw_ref, lnb_ref,
                  woutf_ref, o_ref, qkv_ext, qkv_prev, qkv_p16, qkv_act, cf_ext, cf_prev, cf_p16, state, mixed,
                  consts, wqkv_ref, wz_ref, wba_ref, wcf_ref, wout_ref):
    ts = x_ref.shape[2]
    n_chunks = ts // DN_CHUNK

    @pl.when(pl.program_id(1) == 0)
    def _():
        qkv_prev[...] = jnp.zeros(qkv_prev.shape, F32)
        cf_prev[...] = jnp.zeros(cf_prev.shape, F32)
        state[...] = jnp.zeros(state.shape, F32)

    @pl.when((pl.program_id(0) == 0) & (pl.program_id(1) == 0))
    def _():
        ri = lax.broadcasted_iota(jnp.int32, (ts, ts), 0)
        ci = lax.broadcasted_iota(jnp.int32, (ts, ts), 1)
        consts[0] = _time_perm(ts, False)
        consts[1] = _time_perm(ts, True)
        consts[2] = jnp.where((ri // DN_CHUNK == ci // DN_CHUNK) & (ci <= ri), 1.0, 0.0).astype(BF16)
        n_qkv, n_z, n_ba = wqkv_ref.shape[1], wz_ref.shape[1], 2 * DN_HEADS
        for c0 in range(0, n_qkv, DN_WIDTH):
            wqkv_ref[:, c0:c0 + DN_WIDTH] = win_ref[c0:c0 + DN_WIDTH, :].T.astype(BF16)
        wz_ref[...] = win_ref[n_qkv:n_qkv + n_z, :].T.astype(BF16)
        lane = lax.broadcasted_iota(jnp.int32, wba_ref.shape, 1)
        wba_ref[...] = jnp.where(lane < n_ba, win_ref[n_qkv + n_z:n_qkv + n_z + LANES, :].T, 0.0).astype(BF16)
        cf0 = n_qkv + n_z + n_ba
        for c0 in range(0, wcf_ref.shape[1], DN_WIDTH):
            wcf_ref[:, c0:c0 + DN_WIDTH] = win_ref[cf0 + c0:cf0 + c0 + DN_WIDTH, :].T.astype(BF16)
        wout_ref[...] = woutf_ref[...].astype(BF16)

    r64 = lax.broadcasted_iota(jnp.int32, (DN_CHUNK, DN_CHUNK), 0)
    c64 = lax.broadcasted_iota(jnp.int32, (DN_CHUNK, DN_CHUNK), 1)
    causal = c64 <= r64
    strict = c64 < r64
    eye = jnp.where(c64 == r64, 1.0, 0.0).astype(F32)
    dnw = dnw_ref[...]
    to_natural = consts[1]
    tri = consts[2]
    cells = [(hd, ch) for hd in range(DN_HEADS) for ch in range(n_chunks)]
    rows = lambda ch: slice(ch * DN_CHUNK, (ch + 1) * DN_CHUNK)

    def tile_stages(sid, qkv_ext, qkv_prev, qkv_p16, qkv_act, cf_ext, cf_prev, cf_p16, state, mixed):
        x = x_ref[sid, 0]
        mod = mod_ref[sid, 0]
        shift1, scale1, gate1 = mod[0:1], mod[1:2], mod[2:3]
        h = x * lax.rsqrt(jnp.mean(x * x, axis=-1, keepdims=True) + EPS)
        h = h * npre_ref[...] * (1.0 + scale1) + shift1
        hb = h.astype(BF16)
        yield
        hb_st = _dot(consts[0], hb).astype(BF16)

        _fill_conv_window(qkv_ext, qkv_prev, _dot(hb_st, wqkv_ref[...]), DN_CONV)
        yield
        for r0 in range(0, ts, CONV_ROWS):
            for c0 in range(0, qkv_ext.shape[1], DN_WIDTH):
                blk = _conv_block(qkv_ext, convw_ref, DN_CONV, r0, CONV_ROWS, c0, DN_WIDTH)
                qkv_p16[r0:r0 + CONV_ROWS, c0:c0 + DN_WIDTH] = _silu(blk).astype(BF16)
        yield
        qkv_act[...] = _dot(to_natural, qkv_p16[...])

        cf_pre = _dot(hb_st, wcf_ref[...]) + pw1b_ref[...]
        cfw = cf_ext.shape[1]
        _fill_conv_window(cf_ext, cf_prev, cf_pre[:, :cfw] * jax.nn.sigmoid(cf_pre[:, cfw:]), CF_KERNEL)
        cf_pending = list(range(0, ts, CONV_ROWS))
        yield

        def cf_step(n=1):
            for _ in range(min(n, len(cf_pending))):
                r0 = cf_pending.pop(0)
                cf = _conv_block(cf_ext, dww_ref, CF_KERNEL, r0, CONV_ROWS, 0, cfw) + dwb_ref[...]
                mu = jnp.mean(cf, axis=-1, keepdims=True)
                xc = cf - mu
                var = jnp.mean(xc * xc, axis=-1, keepdims=True)
                cfn = xc * lax.rsqrt(var + EPS) * lnw_ref[...] + lnb_ref[...]
                cf_p16[r0:r0 + CONV_ROWS, :] = _silu(cfn).astype(BF16)

        z = _dot(hb, wz_ref[...])
        ba = _dot(hb, wba_ref[...])
        beta_all = jax.nn.sigmoid(ba)
        g_all = -jnp.exp(alog_ref[...]) * _softplus(ba + dtb_ref[...])

        g_hi, g_mid, g_lo = _split3(g_all)
        gcum = _dot(tri, g_hi) + _dot(tri, g_mid) + _dot(tri, g_lo)
        gcum_t = gcum.T
        exp_g = jnp.exp(gcum)
        yield

        heads = []
        for hd in range(DN_HEADS):
            lo = hd * HEAD_DIM
            qh = qkv_act[:, lo:lo + HEAD_DIM]
            kh = qkv_act[:, DN_WIDTH + lo:DN_WIDTH + lo + HEAD_DIM]
            vh = qkv_act[:, 2 * DN_WIDTH + lo:2 * DN_WIDTH + lo + HEAD_DIM]
            qn = qh * lax.rsqrt(jnp.sum(qh * qh, axis=-1, keepdims=True) + EPS) * (HEAD_DIM ** -0.5)
            kn = kh * lax.rsqrt(jnp.sum(kh * kh, axis=-1, keepdims=True) + EPS)
            beta_h = beta_all[:, hd:hd + 1]
            gc_h = gcum[:, DN_HEADS + hd:DN_HEADS + hd + 1]
            eg_h = exp_g[:, DN_HEADS + hd:DN_HEADS + hd + 1]
            k_beta = kn * beta_h
            heads.append(dict(qn=qn, kn=kn, k_beta=k_beta, v_beta=vh * beta_h, kbg=k_beta * eg_h, qg=qn * eg_h,
                              gc=gc_h, zg=_silu(z[:, lo:lo + HEAD_DIM])))
        yield

        decay, kq = {}, {}
        for hd, ch in cells:
            hv, sl = heads[hd], rows(ch)
            gc_row = gcum_t[DN_HEADS + hd:DN_HEADS + hd + 1, sl]
            decay[hd, ch] = jnp.where(causal, jnp.exp(hv["gc"][sl] - gc_row), 0.0)
            lhs = jnp.concatenate([hv["k_beta"][sl], hv["qn"][sl]], axis=0).astype(BF16)
            kq[hd, ch] = _dot_nt(lhs, hv["kn"][sl].astype(BF16))
        a = {c: jnp.where(strict, kq[c][:DN_CHUNK] * decay[c], 0.0) for c in cells}
        attn = {c: (kq[c][DN_CHUNK:] * decay[c]).astype(BF16) for c in cells}
        yield

        t_inv = {c: eye - a[c] for c in cells}
        pw = {c: a[c].astype(BF16) for c in cells}
        pw = {c: _dot(pw[c], pw[c]).astype(BF16) for c in cells}
        cf_step()
        yield
        for _ in range(4):
            both = {c: _dot(jnp.concatenate([t_inv[c].astype(BF16), pw[c]], axis=0), pw[c]) for c in cells}
            t_inv = {c: t_inv[c] + both[c][:DN_CHUNK] for c in cells}
            pw = {c: both[c][DN_CHUNK:].astype(BF16) for c in cells}
            cf_step()
            yield
        t_inv = {c: t_inv[c] + _dot(t_inv[c].astype(BF16), pw[c]) for c in cells}

        sol, aw, ks, glast = {}, {}, {}, {}
        for hd, ch in cells:
            hv, sl = heads[hd], rows(ch)
            rhs = jnp.concatenate([hv["v_beta"][sl], hv["kbg"][sl]], axis=1).astype(BF16)
            sol[hd, ch] = _dot(t_inv[hd, ch].astype(BF16), rhs).astype(BF16)
        yield
        for hd, ch in cells:
            hv, sl = heads[hd], rows(ch)
            gc_col = hv["gc"][sl]
            glast[hd, ch] = gc_col[DN_CHUNK - 1:DN_CHUNK]
            k_dec = (hv["kn"][sl] * jnp.exp(glast[hd, ch] - gc_col)).astype(BF16)
            aw[hd, ch] = _dot(attn[hd, ch], sol[hd, ch])
            ks[hd, ch] = _dot_tn(k_dec, sol[hd, ch])
        yield

        s_in = {}
        s_cur = [state[hd] for hd in range(DN_HEADS)]
        for ch in range(n_chunks):
            for hd in range(DN_HEADS):
                s_in[hd, ch] = s_cur[hd].astype(BF16)
                kd_u, kd_w = ks[hd, ch][:, :HEAD_DIM], ks[hd, ch][:, HEAD_DIM:]
                s_cur[hd] = s_cur[hd] * jnp.exp(glast[hd, ch]) + kd_u - _dot(kd_w.astype(BF16), s_in[hd, ch])
            cf_step()
            yield
        for hd in range(DN_HEADS):
            state[hd] = s_cur[hd]

        for hd, ch in cells:
            hv, sl = heads[hd], rows(ch)
            lo = hd * HEAD_DIM
            q_eff = (hv["qg"][sl] - aw[hd, ch][:, HEAD_DIM:]).astype(BF16)
            o = _dot(q_eff, s_in[hd, ch]) + aw[hd, ch][:, :HEAD_DIM]
            on = o * lax.rsqrt(jnp.mean(o * o, axis=-1, keepdims=True) + EPS) * dnw * hv["zg"][sl]
            mixed[sl, lo:lo + HEAD_DIM] = on.astype(BF16)
        yield

        cf_step(len(cf_pending))
        mixed[:, DN_WIDTH:DN_WIDTH + cfw] = _dot(to_natural, cf_p16[...]).astype(BF16)
        yield

        out = _dot(mixed[...], wout_ref[...])
        yield
        y = out * lax.rsqrt(jnp.mean(out * out, axis=-1, keepdims=True) + EPS) * npost_ref[...]
        o_ref[sid, 0] = x + gate1 * y

    scratch = (qkv_ext, qkv_prev, qkv_p16, qkv_act, cf_ext, cf_prev, cf_p16, state, mixed)
    streams = [tile_stages(sid, *(ref.at[sid] for ref in scratch)) for sid in range(x_ref.shape[0])]
    live = [True] * len(streams)

    def advance(i, n=1):
        for _ in range(n):
            if live[i]:
                try:
                    next(streams[i])
                except StopIteration:
                    live[i] = False

    advance(0, MIXER_SKEW)
    while any(live):
        for i in range(len(streams)):
            advance(i)


def _mixer_call(x, mod3, npre, npost, w_in, dn_conv_w, dn_a_log, dn_dt_bias, dn_norm_w,
                cf_pw1_b, cf_dw_w, cf_dw_b, cf_ln_w, cf_ln_b, w_out):
    bsz, seq, d = x.shape
    ts = SEQ_TILE
    cfw = cf_dw_w.shape[1]
    n_qkv = 3 * DN_WIDTH
    alog = jnp.pad(dn_a_log, (DN_HEADS, LANES - 2 * DN_HEADS)).reshape(1, LANES)
    dtb = jnp.pad(dn_dt_bias, (DN_HEADS, LANES - 2 * DN_HEADS)).reshape(1, LANES)
    dww = jnp.repeat(cf_dw_w, SUBLANES, axis=0)
    convw = jnp.repeat(dn_conv_w, SUBLANES, axis=0)

    def full(a):
        mode = dict(pipeline_mode=pl.Buffered(1)) if a.size >= d * d else {}
        return pl.BlockSpec(a.shape, lambda b, j: (0,) * a.ndim, **mode)

    row = lambda a: a.reshape(1, -1)
    nb = MIXER_PAIR
    x4 = x.reshape(nb, bsz // nb, seq, d)
    mod4 = mod3.reshape((nb, bsz // nb) + mod3.shape[1:])
    operands = [x4, mod4, row(npre), row(npost), w_in.T, convw, alog, dtb, row(dn_norm_w),
                row(cf_pw1_b), dww, row(cf_dw_b), row(cf_ln_w), row(cf_ln_b), w_out]
    in_specs = [pl.BlockSpec((nb, 1, ts, d), lambda b, j: (0, b, j, 0)),
                pl.BlockSpec((nb, 1) + mod3.shape[1:], lambda b, j: (0, b, 0, 0))]
    in_specs += [full(a) for a in operands[2:]]
    out = pl.pallas_call(
        _mixer_kernel,
        grid=(bsz // nb, seq // ts),
        in_specs=in_specs,
        out_specs=pl.BlockSpec((nb, 1, ts, d), lambda b, j: (0, b, j, 0)),
        out_shape=jax.ShapeDtypeStruct((nb, bsz // nb, seq, d), F32),
        scratch_shapes=[pltpu.VMEM((nb, (DN_CONV - 1) * SUBLANES + ts, n_qkv), F32),
                        pltpu.VMEM((nb, (DN_CONV - 1) * SUBLANES, n_qkv), F32),
                        pltpu.VMEM((nb, ts, n_qkv), BF16),
                        pltpu.VMEM((nb, ts, n_qkv), F32),
                        pltpu.VMEM((nb, (CF_KERNEL - 1) * SUBLANES + ts, cfw), F32),
                        pltpu.VMEM((nb, (CF_KERNEL - 1) * SUBLANES, cfw), F32),
                        pltpu.VMEM((nb, ts, cfw), BF16),
                        pltpu.VMEM((nb, DN_HEADS, HEAD_DIM, HEAD_DIM), F32),
                        pltpu.VMEM((nb, ts, DN_WIDTH + cfw), BF16),
                        pltpu.VMEM((3, ts, ts), BF16),
                        pltpu.VMEM((d, n_qkv), BF16), pltpu.VMEM((d, DN_WIDTH), BF16), pltpu.VMEM((d, LANES), BF16),
                        pltpu.VMEM((d, 2 * cfw), BF16), pltpu.VMEM(w_out.shape, BF16)],
        compiler_params=pltpu.CompilerParams(dimension_semantics=("arbitrary", "arbitrary"),
                                             vmem_limit_bytes=VMEM_LIMIT),
        name="mixer",
    )(*operands)
    return out.reshape(bsz, seq, d)


def _router_kernel(x_ref, mod_ref, nw_ref, wr_ref, br_ref, h_ref, info_ref, infot_ref, cnt_ref, carry, strict_ref):
    tt = x_ref.shape[1]

    @pl.when((pl.program_id(0) == 0) & (pl.program_id(1) == 0))
    def _():
        carry[...] = jnp.zeros(carry.shape, F32)
        rr = lax.broadcasted_iota(jnp.int32, (tt, tt), 0)
        cc = lax.broadcasted_iota(jnp.int32, (tt, tt), 1)
        strict_ref[...] = jnp.where(cc < rr, 1.0, 0.0).astype(BF16)

    x = x_ref[0]
    mod = mod_ref[0]
    shift2, scale2 = mod[3:4], mod[4:5]
    h = x * lax.rsqrt(jnp.mean(x * x, axis=-1, keepdims=True) + EPS)
    h = h * nw_ref[...] * (1.0 + scale2) + shift2
    _store_token_major(h_ref, h)

    h_hi, h_lo, _ = _split3(h)
    w_hi, w_lo, _ = _split3(wr_ref[...])
    logits = _dot(h_hi, w_hi) + (_dot(h_hi, w_lo) + _dot(h_lo, w_hi)) + br_ref[...]

    lane = lax.broadcasted_iota(jnp.int32, (tt, LANES), 1)
    neg = -jnp.inf
    is_grp = (lane >= N_EXPERTS) & (lane < N_EXPERTS + N_GROUPS)
    gl = jnp.where(is_grp, logits, neg)
    gmax = jnp.max(gl, axis=-1, keepdims=True)
    gsum = jnp.sum(jnp.where(is_grp, jnp.exp(gl - gmax), 0.0), axis=-1, keepdims=True)
    grp_p = 1.0 / gsum
    grp_lane = jnp.min(jnp.where(is_grp & (gl == gmax), lane, LANES), axis=-1, keepdims=True)
    grp_idx = grp_lane - N_EXPERTS

    in_grp = (lane < N_EXPERTS) & (lane // EXPERTS_PER_GROUP == grp_idx)
    el = jnp.where(in_grp, logits, neg)
    m1 = jnp.max(el, axis=-1, keepdims=True)
    e1 = jnp.min(jnp.where(in_grp & (el == m1), lane, LANES), axis=-1, keepdims=True)
    el2 = jnp.where(lane == e1, neg, el)
    m2 = jnp.max(el2, axis=-1, keepdims=True)
    e2 = jnp.min(jnp.where(in_grp & (lane != e1) & (el2 == m2), lane, LANES), axis=-1, keepdims=True)
    r = jnp.exp(m2 - m1)
    w1 = grp_p / (1.0 + r)
    w2 = grp_p * r / (1.0 + r)

    hit1 = lane == e1
    hit2 = lane == e2
    onehot = jnp.where(hit1 | hit2, 1.0, 0.0)
    prefix = _dot(strict_ref[...], onehot.astype(BF16)) + carry[0:1, :]
    rank1 = jnp.sum(jnp.where(hit1, prefix, 0.0), axis=-1, keepdims=True)
    rank2 = jnp.sum(jnp.where(hit2, prefix, 0.0), axis=-1, keepdims=True)
    total = carry[0:1, :] + jnp.sum(onehot, axis=0, keepdims=True)
    carry[...] = jnp.broadcast_to(total, carry.shape)
    cnt_ref[...] = jnp.broadcast_to(total, cnt_ref.shape)

    info = jnp.where(lane == 0, e1.astype(F32), 0.0)
    info = jnp.where(lane == 1, e2.astype(F32), info)
    info = jnp.where(lane == 2, w1, info)
    info = jnp.where(lane == 3, w2, info)
    info = jnp.where(lane == 4, rank1, info)
    info = jnp.where(lane == 5, rank2, info)
    info_ref[0] = info
    infot_ref[...] = info.T[0:SUBLANES, :]


def _router_call(x1, mod3, norm_w, w_router_group, b_router_group, w_router_expert, b_router_expert):
    bsz, seq, d = x1.shape
    tt = ROUTER_TILE
    nj = seq // tt
    pitch = d // LANES
    pad = LANES - N_EXPERTS - N_GROUPS
    wr = jnp.pad(jnp.concatenate([w_router_expert, w_router_group], axis=1), ((0, 0), (0, pad)))
    br = jnp.pad(jnp.concatenate([b_router_expert, b_router_group]), (0, pad)).reshape(1, LANES)
    return pl.pallas_call(
        _router_kernel,
        grid=(bsz, seq // tt),
        in_specs=[pl.BlockSpec((1, tt, d), lambda b, j: (b, j, 0)),
                  pl.BlockSpec((1,) + mod3.shape[1:], lambda b, j: (b, 0, 0)),
                  pl.BlockSpec((1, d), lambda b, j: (0, 0)),
                  pl.BlockSpec((d, LANES), lambda b, j: (0, 0)),
                  pl.BlockSpec((1, LANES), lambda b, j: (0, 0))],
        out_specs=[pl.BlockSpec((tt * pitch, LANES), lambda b, j: (b * nj + j, 0)),
                   pl.BlockSpec((1, tt, LANES), lambda b, j: (b, j, 0)),
                   pl.BlockSpec((SUBLANES, tt), lambda b, j: (0, b * nj + j)),
                   pl.BlockSpec((8, LANES), lambda b, j: (0, 0))],
        out_shape=[jax.ShapeDtypeStruct((bsz * seq * pitch, LANES), F32),
                   jax.ShapeDtypeStruct((bsz, seq, LANES), F32),
                   jax.ShapeDtypeStruct((SUBLANES, bsz * seq), F32),
                   jax.ShapeDtypeStruct((8, LANES), F32)],
        scratch_shapes=[pltpu.VMEM((8, LANES), F32), pltpu.VMEM((tt, tt), BF16)],
        compiler_params=pltpu.CompilerParams(dimension_semantics=("arbitrary", "arbitrary"),
                                             vmem_limit_bytes=VMEM_LIMIT),
        name="router",
    )(x1, mod3, norm_w.reshape(1, d), wr, br)


def _tile_rows(idx, pitch):
    if isinstance(idx, int):
        return pl.ds(idx * pitch, pitch)
    return pl.ds(pl.multiple_of(idx * pitch, pitch), pitch)


def _dispatch_kernel(nused_ref, padfrom_ref, padlen_ref, pos_ref, h_ref, xs_hbm, zbuf, zsem, sem, *, pitch, bm):
    step = pl.program_id(0)
    tt = h_ref.shape[0] // pitch

    def fill_copies(action):
        def per_expert(e, carry):
            cursor = padfrom_ref[e]
            n = padlen_ref[e]
            bit = bm // 2
            while bit >= 1:
                take = (n & bit) != 0

                @pl.when(take)
                def _(cursor=cursor, bit=bit):
                    action(pltpu.make_async_copy(zbuf.at[pl.ds(0, bit * pitch), :],
                                                 xs_hbm.at[pl.ds(pl.multiple_of(cursor * pitch, pitch), bit * pitch), :],
                                                 zsem.at[0]))
                cursor = cursor + jnp.where(take, bit, 0)
                bit //= 2
            return carry

        def per_block(b, carry):
            start = pl.multiple_of(b * bm * pitch, bm * pitch)
            action(pltpu.make_async_copy(zbuf, xs_hbm.at[pl.ds(start, bm * pitch), :], zsem.at[0]))
            return carry

        lax.fori_loop(0, N_EXPERTS, per_expert, 0)
        lax.fori_loop(nused_ref[0], xs_hbm.shape[0] // (bm * pitch), per_block, 0)

    @pl.when(step == 0)
    def _():
        zbuf[...] = jnp.zeros(zbuf.shape, F32)
        fill_copies(lambda cp: cp.start())

    for r in range(tt):
        for k in range(TOP_K):
            p = pos_ref[0, 0, k * tt + r]
            pltpu.make_async_copy(h_ref.at[_tile_rows(r, pitch), :], xs_hbm.at[_tile_rows(p, pitch), :],
                                  sem.at[0]).start(priority=k)
    for k in range(TOP_K):
        pltpu.make_async_copy(h_ref, xs_hbm.at[pl.ds(0, tt * pitch), :], sem.at[0]).wait()

    @pl.when(step == pl.num_programs(0) - 1)
    def _():
        fill_copies(lambda cp: cp.wait())


def _dispatch_call(h2, pos3, n_used, padfrom, padlen, n_rows, d):
    pitch = d // LANES
    bm = EXPERT_BLOCK
    n_tiles, _, two_tt = pos3.shape
    tt = two_tt // TOP_K
    grid_spec = pltpu.PrefetchScalarGridSpec(
        num_scalar_prefetch=3,
        grid=(n_tiles,),
        in_specs=[pl.BlockSpec((1, 1, two_tt), lambda s, *_: (s, 0, 0), memory_space=pltpu.SMEM),
                  pl.BlockSpec((tt * pitch, LANES), lambda s, *_: (s, 0))],
        out_specs=pl.BlockSpec(memory_space=pl.ANY),
        scratch_shapes=[pltpu.VMEM((bm * pitch, LANES), F32), pltpu.SemaphoreType.DMA((1,)),
                        pltpu.SemaphoreType.DMA((1,))],
    )
    return pl.pallas_call(
        functools.partial(_dispatch_kernel, pitch=pitch, bm=bm),
        grid_spec=grid_spec,
        out_shape=jax.ShapeDtypeStruct((n_rows * pitch, LANES), F32),
        compiler_params=pltpu.CompilerParams(dimension_semantics=("arbitrary",),
                                             vmem_limit_bytes=VMEM_LIMIT),
        name="dispatch",
    )(n_used, padfrom, padlen, pos3, h2)


def _expert_kernel(nused_ref, nruns_ref, first_ref, run_ref, rune_ref, x_hbm, wg_hbm, wu_hbm, wd_hbm, y_hbm,
                   wg_f32, wu_f32, wd_f32, wg_b, wu_b, wd_b, wsems, xbuf, ybuf, zbuf, xsems, ysems, zsem):
    i = pl.program_id(0)
    n_used = nused_ref[0]
    n_runs = nruns_ref[0]
    d = wg_b.shape[0]
    pitch = d // LANES
    bm = zbuf.shape[0] // pitch
    n_slots = wg_f32.shape[0]

    def weight_copies(e, slot):
        return [pltpu.make_async_copy(src.at[e], dst.at[slot], wsems.at[slot])
                for src, dst in ((wg_hbm, wg_f32), (wu_hbm, wu_f32), (wd_hbm, wd_f32))]

    def start_run(r):
        @pl.when(r < n_runs)
        def _():
            for prio, cp in zip((1, 1, 0), weight_copies(rune_ref[r], r % n_slots)):
                cp.start(priority=prio)

    @pl.when(i == 0)
    def _():
        for r in range(n_slots - 1):
            start_run(r)

    @pl.when((i < n_used) & (first_ref[i] == 1))
    def _():
        run = run_ref[i]
        slot = run % n_slots
        for cp in weight_copies(0, slot):
            cp.wait()
        start_run(run + n_slots - 1)

        wg_b[...] = wg_f32[slot].astype(BF16)
        wu_b[...] = wu_f32[slot].astype(BF16)
        wd_b[...] = wd_f32[slot].astype(BF16)

    blk_rows = bm * pitch
    n_blocks = pl.num_programs(0)
    x_slots, y_slots = xbuf.shape[0], ybuf.shape[0]

    def block(ref, b):
        return ref.at[pl.ds(pl.multiple_of(b * blk_rows, blk_rows), blk_rows), :]

    def x_copy(b):
        return pltpu.make_async_copy(block(x_hbm, b), xbuf.at[b % x_slots], xsems.at[b % x_slots])

    def y_copy(b):
        return pltpu.make_async_copy(ybuf.at[b % y_slots], block(y_hbm, b), ysems.at[b % y_slots])

    def zero_copy(b):
        return pltpu.make_async_copy(zbuf, block(y_hbm, b), zsem.at[0])

    @pl.when(i == 0)
    def _():
        for b in range(x_slots - 1):
            @pl.when(b < n_used)
            def _(b=b):
                x_copy(b).start()

    @pl.when(i < n_used)
    def _():
        x_copy(i).wait()

        @pl.when(i + x_slots - 1 < n_used)
        def _():
            x_copy(i + x_slots - 1).start()

        @pl.when(i >= y_slots)
        def _():
            y_copy(i - y_slots).wait()

        xb = _load_token_major(xbuf.at[i % x_slots], bm, d).astype(BF16)
        gate = _dot(xb, wg_b[...])
        up = _dot(xb, wu_b[...])
        hid = (_silu(gate) * up).astype(BF16)
        _store_token_major(ybuf.at[i % y_slots], _dot(hid, wd_b[...]))
        y_copy(i).start()

    @pl.when(i == n_used - 1)
    def _():
        for back in range(y_slots):
            @pl.when(i - back >= 0)
            def _(back=back):
                y_copy(i - back).wait()
        zbuf[...] = jnp.zeros(zbuf.shape, F32)
        lax.fori_loop(n_used, n_blocks, lambda b, c: (zero_copy(b).start(), c)[1], 0)

    @pl.when(i == n_blocks - 1)
    def _():
        lax.fori_loop(n_used, n_blocks, lambda b, c: (zero_copy(b).wait(), c)[1], 0)


def _expert_call(xs, block_e, n_used, w_gate, w_up, w_down):
    bm = EXPERT_BLOCK
    d, de = w_gate.shape[1], w_gate.shape[2]
    pitch = d // LANES
    n_blocks = xs.shape[0] // (bm * pitch)
    blk = jnp.arange(n_blocks, dtype=jnp.int32)
    used = blk < n_used[0]
    first = jnp.concatenate([jnp.ones((1,), jnp.int32), (block_e[1:] != block_e[:-1]).astype(jnp.int32)])
    first = jnp.where(used, first, 0)
    run = (jnp.cumsum(first) - 1).astype(jnp.int32)
    n_runs = jnp.sum(first).astype(jnp.int32).reshape(1)
    eids = jnp.arange(N_EXPERTS, dtype=jnp.int32)
    run_of_e = jnp.where((first[:, None] == 1) & (block_e[:, None] == eids[None, :]), run[:, None], -1).max(axis=0)
    run_e = jnp.sum(jnp.where(run_of_e[None, :] == eids[:, None], eids[None, :], 0), axis=1).astype(jnp.int32)
    grid_spec = pltpu.PrefetchScalarGridSpec(
        num_scalar_prefetch=5,
        grid=(n_blocks,),
        in_specs=[pl.BlockSpec(memory_space=pl.ANY)] * 4,
        out_specs=pl.BlockSpec(memory_space=pl.ANY),
        scratch_shapes=[pltpu.VMEM((WEIGHT_SLOTS, d, de), F32), pltpu.VMEM((WEIGHT_SLOTS, d, de), F32),
                        pltpu.VMEM((WEIGHT_SLOTS, de, d), F32),
                        pltpu.VMEM((d, de), BF16), pltpu.VMEM((d, de), BF16), pltpu.VMEM((de, d), BF16),
                        pltpu.SemaphoreType.DMA((WEIGHT_SLOTS,)),
                        pltpu.VMEM((ROW_SLOTS_IN, bm * pitch, LANES), F32),
                        pltpu.VMEM((ROW_SLOTS_OUT, bm * pitch, LANES), F32),
                        pltpu.VMEM((bm * pitch, LANES), F32),
                        pltpu.SemaphoreType.DMA((ROW_SLOTS_IN,)), pltpu.SemaphoreType.DMA((ROW_SLOTS_OUT,)),
                        pltpu.SemaphoreType.DMA((1,))],
    )
    return pl.pallas_call(
        _expert_kernel,
        grid_spec=grid_spec,
        out_shape=jax.ShapeDtypeStruct((n_blocks * bm * pitch, LANES), F32),
        compiler_params=pltpu.CompilerParams(dimension_semantics=("arbitrary",),
                                             vmem_limit_bytes=VMEM_LIMIT),
        name="experts",
    )(n_used, n_runs, first, run, run_e, xs, w_gate, w_up, w_down)


def _combine_kernel(pos_cur_ref, pos_nxt_ref, x_ref, info_ref, mod_ref, nw_ref, y_hbm, o_ref, ybuf_a, ybuf_b, sems):
    tt, d = x_ref.shape[1], x_ref.shape[2]
    pitch = d // LANES
    step = pl.program_id(0) * pl.num_programs(1) + pl.program_id(1)
    n_steps = pl.num_programs(0) * pl.num_programs(1)
    n_chunks = tt // COMBINE_ROWS

    def row_copy(pos_ref, buf, sem, k, r):
        p = pos_ref[0, 0, k * tt + r]
        return pltpu.make_async_copy(y_hbm.at[_tile_rows(p, pitch), :], buf.at[_tile_rows(k * tt + r, pitch), :], sem)

    def wait_tile(buf, sem):
        pltpu.make_async_copy(y_hbm.at[pl.ds(0, TOP_K * tt * pitch), :], buf, sem).wait()

    @pl.when(step == 0)
    def _():
        def body(r, carry):
            for k in range(TOP_K):
                row_copy(pos_cur_ref, ybuf_a, sems.at[0], k, r).start(priority=k)
            return carry
        lax.fori_loop(0, tt, body, 0, unroll=4)

    gate2 = mod_ref[0][5:6]

    def run_tile(cur, cur_sem, nxt, nxt_sem):
        wait_tile(cur, cur_sem)

        def chunk(c, carry):
            r0 = c * COMBINE_ROWS
            for r in range(COMBINE_ROWS):
                for k in range(TOP_K):
                    row_copy(pos_nxt_ref, nxt, nxt_sem, k, r0 + r).start(priority=k)
            rows = pl.ds(r0, COMBINE_ROWS)
            info = info_ref[0, rows, :]
            moe = (_load_token_major(cur, COMBINE_ROWS, d, r0 * pitch) * info[:, 2:3]
                   + _load_token_major(cur, COMBINE_ROWS, d, (tt + r0) * pitch) * info[:, 3:4])
            y = moe * lax.rsqrt(jnp.mean(moe * moe, axis=-1, keepdims=True) + EPS) * nw_ref[...]
            o_ref[0, rows, :] = x_ref[0, rows, :] + gate2 * y
            return carry
        for c in range(n_chunks):
            chunk(c, 0)

        @pl.when(step == n_steps - 1)
        def _():
            wait_tile(nxt, nxt_sem)

    @pl.when(step % 2 == 0)
    def _():
        run_tile(ybuf_a, sems.at[0], ybuf_b, sems.at[1])

    @pl.when(step % 2 == 1)
    def _():
        run_tile(ybuf_b, sems.at[1], ybuf_a, sems.at[0])


def _combine_call(x1, info, mod3, norm_w, y, pos3):
    bsz, seq, d = x1.shape
    tt = COMBINE_TILE
    nj = seq // tt
    n_tiles = bsz * nj
    pos_blk = lambda f: pl.BlockSpec((1, 1, TOP_K * tt), f, memory_space=pltpu.SMEM)
    return pl.pallas_call(
        _combine_kernel,
        grid=(bsz, nj),
        in_specs=[pos_blk(lambda b, j: (b * nj + j, 0, 0)),
                  pos_blk(lambda b, j: (jnp.minimum(b * nj + j + 1, n_tiles - 1), 0, 0)),
                  pl.BlockSpec((1, tt, d), lambda b, j: (b, j, 0)),
                  pl.BlockSpec((1, tt, LANES), lambda b, j: (b, j, 0)),
                  pl.BlockSpec((1,) + mod3.shape[1:], lambda b, j: (b, 0, 0)),
                  pl.BlockSpec((1, d), lambda b, j: (0, 0)),
                  pl.BlockSpec(memory_space=pl.ANY)],
        out_specs=pl.BlockSpec((1, tt, d), lambda b, j: (b, j, 0)),
        out_shape=jax.ShapeDtypeStruct((bsz, seq, d), F32),
        scratch_shapes=[pltpu.VMEM((TOP_K * tt * (d // LANES), LANES), F32),
                        pltpu.VMEM((TOP_K * tt * (d // LANES), LANES), F32), pltpu.SemaphoreType.DMA((2,))],
        compiler_params=pltpu.CompilerParams(dimension_semantics=("arbitrary", "arbitrary"),
                                             vmem_limit_bytes=VMEM_LIMIT),
        name="combine",
    )(pos3, pos3, x1, info, mod3, norm_w.reshape(1, d), y)


def _layer(x, mod, norm_pre_mix, norm_post_mix, w_in, dn_conv_w, dn_a_log, dn_dt_bias, dn_norm_w,
           cf_pw1_b, cf_dw_w, cf_dw_b, cf_ln_w, cf_ln_b, w_out, norm_pre_ffn, norm_post_ffn,
           w_router_group, b_router_group, w_router_expert, b_router_expert, w_gate, w_up, w_down):
    bsz, seq, d = x.shape
    t = bsz * seq
    mod3 = mod.reshape(bsz, -1, d)
    x1 = _mixer_call(x, mod3, norm_pre_mix, norm_post_mix, w_in, dn_conv_w, dn_a_log, dn_dt_bias, dn_norm_w,
                     cf_pw1_b, cf_dw_w, cf_dw_b, cf_ln_w, cf_ln_b, w_out)
    h2, info, info_t, cnt = _router_call(x1, mod3, norm_pre_ffn, w_router_group, b_router_group,
                                         w_router_expert, b_router_expert)

    bm = EXPERT_BLOCK
    expert_id = info_t[0:TOP_K].astype(jnp.int32)
    rank = info_t[4:4 + TOP_K].astype(jnp.int32)
    counts = cnt[0, :N_EXPERTS].astype(jnp.int32)
    padded = (counts + bm - 1) // bm * bm
    pend = jnp.cumsum(padded)
    pstart = pend - padded
    eids = jnp.arange(N_EXPERTS, dtype=jnp.int32)
    onehot = expert_id[:, None, :] == eids[None, :, None]
    pos = jnp.sum(jnp.where(onehot, pstart[None, :, None], 0), axis=1) + rank

    def tiled(tt):
        return pos.reshape(TOP_K, t // tt, tt).transpose(1, 0, 2).reshape(t // tt, 1, TOP_K * tt)
    n_blocks = -(-(t * TOP_K) // bm) + N_EXPERTS
    block_start = jnp.arange(n_blocks, dtype=jnp.int32) * bm
    block_e = jnp.minimum(jnp.sum(pend[None, :] <= block_start[:, None], axis=1), N_EXPERTS - 1).astype(jnp.int32)
    n_used = (pend[-1] // bm).astype(jnp.int32).reshape(1)

    xs = _dispatch_call(h2, tiled(DISPATCH_TILE), n_used, (pstart + counts).astype(jnp.int32),
                        (padded - counts).astype(jnp.int32), n_blocks * bm, d)
    y = _expert_call(xs, block_e, n_used, w_gate, w_up, w_down)
    return _combine_call(x1, info, mod3, norm_post_ffn, y, tiled(COMBINE_TILE))


def kernel(x, c, w_ada, b_ada, norm_pre_mix, norm_post_mix, w_in, dn_conv_w, dn_a_log, dn_dt_bias, dn_norm_w,
           cf_pw1_b, cf_dw_w, cf_dw_b, cf_ln_w, cf_ln_b, w_out, norm_pre_ffn, norm_post_ffn,
           w_router_group, b_router_group, w_router_expert, b_router_expert, w_gate, w_up, w_down):
    depth = w_ada.shape[0]
    for l in range(depth):
        mod = _ada_call(c, w_ada[l], b_ada[l])
        x = _layer(x, mod, norm_pre_mix[l], norm_post_mix[l], w_in[l], dn_conv_w[l], dn_a_log[l],
                   dn_dt_bias[l], dn_norm_w[l], cf_pw1_b[l], cf_dw_w[l], cf_dw_b[l], cf_ln_w[l], cf_ln_b[l],
                   w_out[l], norm_pre_ffn[l], norm_post_ffn[l], w_router_group[l], b_router_group[l],
                   w_router_expert[l], b_router_expert[l], w_gate[l], w_up[l], w_down[l])
    return x
```

```python
import functools

import jax
import jax.numpy as jnp
from jax import lax
from jax.experimental import pallas as pl
from jax.experimental.pallas import tpu as pltpu

F32 = jnp.float32
BF16 = jnp.bfloat16
EPS = 1e-6

DN_HEADS = 4
HEAD_DIM = 128
DN_WIDTH = DN_HEADS * HEAD_DIM
DN_CONV = 4
DN_CHUNK = 64
CF_KERNEL = 31
N_GROUPS = 8
EXPERTS_PER_GROUP = 8
N_EXPERTS = N_GROUPS * EXPERTS_PER_GROUP
TOP_K = 2

LANES = 128
SUBLANES = 8
SEQ_TILE = 256
CONV_ROWS = 32
MIXER_PAIR = 2
MIXER_SKEW = 4
ROUTER_TILE = 512
EXPERT_BLOCK = 256
COMBINE_TILE = 512
COMBINE_ROWS = 32
WEIGHT_SLOTS = 3
ROW_SLOTS_IN = 3
ROW_SLOTS_OUT = 2
DISPATCH_TILE = 1024
VMEM_LIMIT = 56 * 1024 * 1024


def _dot(a, b):
    return jnp.dot(a, b, preferred_element_type=F32)


def _dot_nt(a, b):
    return lax.dot_general(a, b, (((1,), (1,)), ((), ())), preferred_element_type=F32)


def _dot_tn(a, b):
    return lax.dot_general(a, b, (((0,), (0,)), ((), ())), preferred_element_type=F32)


def _split3(x):
    hi = x.astype(BF16)
    r1 = x - hi.astype(F32)
    mid = r1.astype(BF16)
    lo = (r1 - mid.astype(F32)).astype(BF16)
    return hi, mid, lo


def _silu(x):
    return x * jax.nn.sigmoid(x)


def _softplus(x):
    return jnp.maximum(x, 0.0) + jnp.log1p(jnp.exp(-jnp.abs(x)))


def _store_token_major(ref, val, base=0):
    n, d = val.shape
    pitch = d // LANES
    for j in range(pitch):
        ref[pl.ds(base + j, n, stride=pitch), :] = val[:, j * LANES:(j + 1) * LANES]


def _load_token_major(ref, n, d, base=0):
    pitch = d // LANES
    return jnp.concatenate([ref[pl.ds(base + j, n, stride=pitch), :] for j in range(pitch)], axis=1)


def _ada_kernel(c_ref, w_ref, b_ref, o_ref):
    c = c_ref[...]
    ca = _silu(c)
    c_hi, c_lo, _ = _split3(ca)
    w_hi, w_lo, _ = _split3(w_ref[...])
    o_ref[...] = _dot(c_hi, w_hi) + (_dot(c_hi, w_lo) + _dot(c_lo, w_hi)) + b_ref[...]


def _ada_call(c, w, b):
    bsz, d = c.shape
    n = w.shape[1]
    tn = 512
    return pl.pallas_call(
        _ada_kernel,
        grid=(n // tn,),
        in_specs=[pl.BlockSpec((bsz, d), lambda i: (0, 0)),
                  pl.BlockSpec((d, tn), lambda i: (0, i)),
                  pl.BlockSpec((1, tn), lambda i: (0, i))],
        out_specs=pl.BlockSpec((bsz, tn), lambda i: (0, i)),
        out_shape=jax.ShapeDtypeStruct((bsz, n), F32),
        compiler_params=pltpu.CompilerParams(dimension_semantics=("arbitrary",),
                                             vmem_limit_bytes=VMEM_LIMIT),
        name="ada",
    )(c, w, b.reshape(1, n))


def _time_perm(ts, transpose):
    ri = lax.broadcasted_iota(jnp.int32, (ts, ts), 0)
    ci = lax.broadcasted_iota(jnp.int32, (ts, ts), 1)
    strided, natural = (ci, ri) if transpose else (ri, ci)
    return jnp.where(natural == (ts // SUBLANES) * (strided % SUBLANES) + strided // SUBLANES, 1.0, 0.0).astype(BF16)


def _fill_conv_window(ext_ref, prev_ref, cur, n_taps):
    ts = cur.shape[0]
    lead = (n_taps - 1) * SUBLANES
    tail = cur[ts - lead:, :]
    sub = lax.broadcasted_iota(jnp.int32, tail.shape, 0) % SUBLANES
    merged = jnp.where(sub == SUBLANES - 1, prev_ref[...], tail)
    for g in range(n_taps - 1):
        rows = slice(g * SUBLANES, (g + 1) * SUBLANES)
        ext_ref[rows, :] = pltpu.roll(merged[rows, :], 1, 0)
    ext_ref[lead:lead + ts, :] = cur
    prev_ref[...] = tail


def _conv_block(ext_ref, w_ref, n_taps, r0, row_blk, c0, col_blk):
    groups = row_blk // SUBLANES
    acc = [jnp.zeros((SUBLANES, col_blk), F32) for _ in range(groups)]
    for k in range(n_taps):
        w = w_ref[k * SUBLANES:(k + 1) * SUBLANES, c0:c0 + col_blk]
        for g in range(groups):
            lo = r0 + (k + g) * SUBLANES
            acc[g] = acc[g] + w * ext_ref[lo:lo + SUBLANES, c0:c0 + col_blk]
    return jnp.concatenate(acc, axis=0)


def _mixer_kernel(x_ref, mod_ref, npre_ref, npost_ref, win_ref,
                  convw_ref, alog_ref, dtb_ref, dnw_ref, pw1b_ref, dww_ref, dwb_ref, lnw_ref, lnb_ref,
                  woutf_ref, o_ref, qkv_ext, qkv_prev, qkv_p16, qkv_act, cf_ext, cf_prev, cf_p16, state, mixed,
                  consts, wqkv_ref, wz_ref, wba_ref, wcf_ref, wout_ref):
    ts = x_ref.shape[2]
    n_chunks = ts // DN_CHUNK

    @pl.when(pl.program_id(1) == 0)
    def _():
        qkv_prev[...] = jnp.zeros(qkv_prev.shape, F32)
        cf_prev[...] = jnp.zeros(cf_prev.shape, F32)
        state[...] = jnp.zeros(state.shape, F32)

    @pl.when((pl.program_id(0) == 0) & (pl.program_id(1) == 0))
    def _():
        ri = lax.broadcasted_iota(jnp.int32, (ts, ts), 0)
        ci = lax.broadcasted_iota(jnp.int32, (ts, ts), 1)
        consts[0] = _time_perm(ts, False)
        consts[1] = _time_perm(ts, True)
        consts[2] = jnp.where((ri // DN_CHUNK == ci // DN_CHUNK) & (ci <= ri), 1.0, 0.0).astype(BF16)
        n_qkv, n_z, n_ba = wqkv_ref.shape[1], wz_ref.shape[1], 2 * DN_HEADS
        for c0 in range(0, n_qkv, DN_WIDTH):
            wqkv_ref[:, c0:c0 + DN_WIDTH] = win_ref[c0:c0 + DN_WIDTH, :].T.astype(BF16)
        wz_ref[...] = win_ref[n_qkv:n_qkv + n_z, :].T.astype(BF16)
        lane = lax.broadcasted_iota(jnp.int32, wba_ref.shape, 1)
        wba_ref[...] = jnp.where(lane < n_ba, win_ref[n_qkv + n_z:n_qkv + n_z + LANES, :].T, 0.0).astype(BF16)
        cf0 = n_qkv + n_z + n_ba
        for c0 in range(0, wcf_ref.shape[1], DN_WIDTH):
            wcf_ref[:, c0:c0 + DN_WIDTH] = win_ref[cf0 + c0:cf0 + c0 + DN_WIDTH, :].T.astype(BF16)
        wout_ref[...] = woutf_ref[...].astype(BF16)

    r64 = lax.broadcasted_iota(jnp.int32, (DN_CHUNK, DN_CHUNK), 0)
    c64 = lax.broadcasted_iota(jnp.int32, (DN_CHUNK, DN_CHUNK), 1)
    causal = c64 <= r64
    strict = c64 < r64
    eye = jnp.where(c64 == r64, 1.0, 0.0).astype(F32)
    dnw = dnw_ref[...]
    to_natural = consts[1]
    tri = consts[2]
    cells = [(hd, ch) for hd in range(DN_HEADS) for ch in range(n_chunks)]
    rows = lambda ch: slice(ch * DN_CHUNK, (ch + 1) * DN_CHUNK)

    def tile_stages(sid, qkv_ext, qkv_prev, qkv_p16, qkv_act, cf_ext, cf_prev, cf_p16, state, mixed):
        x = x_ref[sid, 0]
        mod = mod_ref[sid, 0]
        shift1, scale1, gate1 = mod[0:1], mod[1:2], mod[2:3]
        h = x * lax.rsqrt(jnp.mean(x * x, axis=-1, keepdims=True) + EPS)
        h = h * npre_ref[...] * (1.0 + scale1) + shift1
        hb = h.astype(BF16)
        yield
        hb_st = _dot(consts[0], hb).astype(BF16)

        _fill_conv_window(qkv_ext, qkv_prev, _dot(hb_st, wqkv_ref[...]), DN_CONV)
        yield
        for r0 in range(0, ts, CONV_ROWS):
            for c0 in range(0, qkv_ext.shape[1], DN_WIDTH):
                blk = _conv_block(qkv_ext, convw_ref, DN_CONV, r0, CONV_ROWS, c0, DN_WIDTH)
                qkv_p16[r0:r0 + CONV_ROWS, c0:c0 + DN_WIDTH] = _silu(blk).astype(BF16)
        yield
        qkv_act[...] = _dot(to_natural, qkv_p16[...])

        cf_pre = _dot(hb_st, wcf_ref[...]) + pw1b_ref[...]
        cfw = cf_ext.shape[1]
        _fill_conv_window(cf_ext, cf_prev, cf_pre[:, :cfw] * jax.nn.sigmoid(cf_pre[:, cfw:]), CF_KERNEL)
        cf_pending = list(range(0, ts, CONV_ROWS))
        yield

        def cf_step(n=1):
            for _ in range(min(n, len(cf_pending))):
                r0 = cf_pending.pop(0)
                cf = _conv_block(cf_ext, dww_ref, CF_KERNEL, r0, CONV_ROWS, 0, cfw) + dwb_ref[...]
                mu = jnp.mean(cf, axis=-1, keepdims=True)
                xc = cf - mu
                var = jnp.mean(xc * xc, axis=-1, keepdims=True)
                cfn = xc * lax.rsqrt(var + EPS) * lnw_ref[...] + lnb_ref[...]
                cf_p16[r0:r0 + CONV_ROWS, :] = _silu(cfn).astype(BF16)

        z = _dot(hb, wz_ref[...])
        ba = _dot(hb, wba_ref[...])
        beta_all = jax.nn.sigmoid(ba)
        g_all = -jnp.exp(alog_ref[...]) * _softplus(ba + dtb_ref[...])

        g_hi, g_mid, g_lo = _split3(g_all)
        gcum = _dot(tri, g_hi) + _dot(tri, g_mid) + _dot(tri, g_lo)
        gcum_t = gcum.T
        exp_g = jnp.exp(gcum)
        yield

        heads = []
        for hd in range(DN_HEADS):
            lo = hd * HEAD_DIM
            qh = qkv_act[:, lo:lo + HEAD_DIM]
            kh = qkv_act[:, DN_WIDTH + lo:DN_WIDTH + lo + HEAD_DIM]
            vh = qkv_act[:, 2 * DN_WIDTH + lo:2 * DN_WIDTH + lo + HEAD_DIM]
            qn = qh * lax.rsqrt(jnp.sum(qh * qh, axis=-1, keepdims=True) + EPS) * (HEAD_DIM ** -0.5)
            kn = kh * lax.rsqrt(jnp.sum(kh * kh, axis=-1, keepdims=True) + EPS)
            beta_h = beta_all[:, hd:hd + 1]
            gc_h = gcum[:, DN_HEADS + hd:DN_HEADS + hd + 1]
            eg_h = exp_g[:, DN_HEADS + hd:DN_HEADS + hd + 1]
            k_beta = kn * beta_h
            heads.append(dict(qn=qn, kn=kn, k_beta=k_beta, v_beta=vh * beta_h, kbg=k_beta * eg_h, qg=qn * eg_h,
                              gc=gc_h, zg=_silu(z[:, lo:lo + HEAD_DIM])))
        yield

        decay, kq = {}, {}
        for hd, ch in cells:
            hv, sl = heads[hd], rows(ch)
            gc_row = gcum_t[DN_HEADS + hd:DN_HEADS + hd + 1, sl]
            decay[hd, ch] = jnp.where(causal, jnp.exp(hv["gc"][sl] - gc_row), 0.0)
            lhs = jnp.concatenate([hv["k_beta"][sl], hv["qn"][sl]], axis=0).astype(BF16)
            kq[hd, ch] = _dot_nt(lhs, hv["kn"][sl].astype(BF16))
        a = {c: jnp.where(strict, kq[c][:DN_CHUNK] * decay[c], 0.0) for c in cells}
        attn = {c: (kq[c][DN_CHUNK:] * decay[c]).astype(BF16) for c in cells}
        yield

        t_inv = {c: eye - a[c] for c in cells}
        pw = {c: a[c].astype(BF16) for c in cells}
        pw = {c: _dot(pw[c], pw[c]).astype(BF16) for c in cells}
        cf_step()
        yield
        for _ in range(4):
            both = {c: _dot(jnp.concatenate([t_inv[c].astype(BF16), pw[c]], axis=0), pw[c]) for c in cells}
            t_inv = {c: t_inv[c] + both[c][:DN_CHUNK] for c in cells}
            pw = {c: both[c][DN_CHUNK:].astype(BF16) for c in cells}
            cf_step()
            yield
        t_inv = {c: t_inv[c] + _dot(t_inv[c].astype(BF16), pw[c]) for c in cells}

        sol, aw, ks, glast = {}, {}, {}, {}
        for hd, ch in cells:
            hv, sl = heads[hd], rows(ch)
            rhs = jnp.concatenate([hv["v_beta"][sl], hv["kbg"][sl]], axis=1).astype(BF16)
            sol[hd, ch] = _dot(t_inv[hd, ch].astype(BF16), rhs).astype(BF16)
        yield
        for hd, ch in cells:
            hv, sl = heads[hd], rows(ch)
            gc_col = hv["gc"][sl]
            glast[hd, ch] = gc_col[DN_CHUNK - 1:DN_CHUNK]
            k_dec = (hv["kn"][sl] * jnp.exp(glast[hd, ch] - gc_col)).astype(BF16)
            aw[hd, ch] = _dot(attn[hd, ch], sol[hd, ch])
            ks[hd, ch] = _dot_tn(k_dec, sol[hd, ch])
        yield

        s_in = {}
        s_cur = [state[hd] for hd in range(DN_HEADS)]
        for ch in range(n_chunks):
            for hd in range(DN_HEADS):
                s_in[hd, ch] = s_cur[hd].astype(BF16)
                kd_u, kd_w = ks[hd, ch][:, :HEAD_DIM], ks[hd, ch][:, HEAD_DIM:]
                s_cur[hd] = s_cur[hd] * jnp.exp(glast[hd, ch]) + kd_u - _dot(kd_w.astype(BF16), s_in[hd, ch])
            cf_step()
            yield
        for hd in range(DN_HEADS):
            state[hd] = s_cur[hd]

        for hd, ch in cells:
            hv, sl = heads[hd], rows(ch)
            lo = hd * HEAD_DIM
            q_eff = (hv["qg"][sl] - aw[hd, ch][:, HEAD_DIM:]).astype(BF16)
            o = _dot(q_eff, s_in[hd, ch]) + aw[hd, ch][:, :HEAD_DIM]
            on = o * lax.rsqrt(jnp.mean(o * o, axis=-1, keepdims=True) + EPS) * dnw * hv["zg"][sl]
            mixed[sl, lo:lo + HEAD_DIM] = on.astype(BF16)
        yield

        cf_step(len(cf_pending))
        mixed[:, DN_WIDTH:DN_WIDTH + cfw] = _dot(to_natural, cf_p16[...]).astype(BF16)
        yield

        out = _dot(mixed[...], wout_ref[...])
        yield
        y = out * lax.rsqrt(jnp.mean(out * out, axis=-1, keepdims=True) + EPS) * npost_ref[...]
        o_ref[sid, 0] = x + gate1 * y

    scratch = (qkv_ext, qkv_prev, qkv_p16, qkv_act, cf_ext, cf_prev, cf_p16, state, mixed)
    streams = [tile_stages(sid, *(ref.at[sid] for ref in scratch)) for sid in range(x_ref.shape[0])]
    live = [True] * len(streams)

    def advance(i, n=1):
        for _ in range(n):
            if live[i]:
                try:
                    next(streams[i])
                except StopIteration:
                    live[i] = False

    advance(0, MIXER_SKEW)
    while any(live):
        for i in range(len(streams)):
            advance(i)


def _mixer_call(x, mod3, npre, npost, w_in, dn_conv_w, dn_a_log, dn_dt_bias, dn_norm_w,
                cf_pw1_b, cf_dw_w, cf_dw_b, cf_ln_w, cf_ln_b, w_out):
    bsz, seq, d = x.shape
    ts = SEQ_TILE
    cfw = cf_dw_w.shape[1]
    n_qkv = 3 * DN_WIDTH
    alog = jnp.pad(dn_a_log, (DN_HEADS, LANES - 2 * DN_HEADS)).reshape(1, LANES)
    dtb = jnp.pad(dn_dt_bias, (DN_HEADS, LANES - 2 * DN_HEADS)).reshape(1, LANES)
    dww = jnp.repeat(cf_dw_w, SUBLANES, axis=0)
    convw = jnp.repeat(dn_conv_w, SUBLANES, axis=0)

    def full(a):
        mode = dict(pipeline_mode=pl.Buffered(1)) if a.size >= d * d else {}
        return pl.BlockSpec(a.shape, lambda b, j: (0,) * a.ndim, **mode)

    row = lambda a: a.reshape(1, -1)
    nb = MIXER_PAIR
    x4 = x.reshape(nb, bsz // nb, seq, d)
    mod4 = mod3.reshape((nb, bsz // nb) + mod3.shape[1:])
    operands = [x4, mod4, row(npre), row(npost), w_in.T, convw, alog, dtb, row(dn_norm_w),
                row(cf_pw1_b), dww, row(cf_dw_b), row(cf_ln_w), row(cf_ln_b), w_out]
    in_specs = [pl.BlockSpec((nb, 1, ts, d), lambda b, j: (0, b, j, 0)),
                pl.BlockSpec((nb, 1) + mod3.shape[1:], lambda b, j: (0, b, 0, 0))]
    in_specs += [full(a) for a in operands[2:]]
    out = pl.pallas_call(
        _mixer_kernel,
        grid=(bsz // nb, seq // ts),
        in_specs=in_specs,
        out_specs=pl.BlockSpec((nb, 1, ts, d), lambda b, j: (0, b, j, 0)),
        out_shape=jax.ShapeDtypeStruct((nb, bsz // nb, seq, d), F32),
        scratch_shapes=[pltpu.VMEM((nb, (DN_CONV - 1) * SUBLANES + ts, n_qkv), F32),
                        pltpu.VMEM((nb, (DN_CONV - 1) * SUBLANES, n_qkv), F32),
                        pltpu.VMEM((nb, ts, n_qkv), BF16),
                        pltpu.VMEM((nb, ts, n_qkv), F32),
                        pltpu.VMEM((nb, (CF_KERNEL - 1) * SUBLANES + ts, cfw), F32),
                        pltpu.VMEM((nb, (CF_KERNEL - 1) * SUBLANES, cfw), F32),
                        pltpu.VMEM((nb, ts, cfw), BF16),
                        pltpu.VMEM((nb, DN_HEADS, HEAD_DIM, HEAD_DIM), F32),
                        pltpu.VMEM((nb, ts, DN_WIDTH + cfw), BF16),
                        pltpu.VMEM((3, ts, ts), BF16),
                        pltpu.VMEM((d, n_qkv), BF16), pltpu.VMEM((d, DN_WIDTH), BF16), pltpu.VMEM((d, LANES), BF16),
                        pltpu.VMEM((d, 2 * cfw), BF16), pltpu.VMEM(w_out.shape, BF16)],
        compiler_params=pltpu.CompilerParams(dimension_semantics=("arbitrary", "arbitrary"),
                                             vmem_limit_bytes=VMEM_LIMIT),
        name="mixer",
    )(*operands)
    return out.reshape(bsz, seq, d)


def _router_kernel(x_ref, mod_ref, nw_ref, wr_ref, br_ref, h_ref, info_ref, infot_ref, cnt_ref, carry, strict_ref):
    tt = x_ref.shape[1]

    @pl.when((pl.program_id(0) == 0) & (pl.program_id(1) == 0))
    def _():
        carry[...] = jnp.zeros(carry.shape, F32)
        rr = lax.broadcasted_iota(jnp.int32, (tt, tt), 0)
        cc = lax.broadcasted_iota(jnp.int32, (tt, tt), 1)
        strict_ref[...] = jnp.where(cc < rr, 1.0, 0.0).astype(BF16)

    x = x_ref[0]
    mod = mod_ref[0]
    shift2, scale2 = mod[3:4], mod[4:5]
    h = x * lax.rsqrt(jnp.mean(x * x, axis=-1, keepdims=True) + EPS)
    h = h * nw_ref[...] * (1.0 + scale2) + shift2
    _store_token_major(h_ref, h)

    h_hi, h_lo, _ = _split3(h)
    w_hi, w_lo, _ = _split3(wr_ref[...])
    logits = _dot(h_hi, w_hi) + (_dot(h_hi, w_lo) + _dot(h_lo, w_hi)) + br_ref[...]

    lane = lax.broadcasted_iota(jnp.int32, (tt, LANES), 1)
    neg = -jnp.inf
    is_grp = (lane >= N_EXPERTS) & (lane < N_EXPERTS + N_GROUPS)
    gl = jnp.where(is_grp, logits, neg)
    gmax = jnp.max(gl, axis=-1, keepdims=True)
    gsum = jnp.sum(jnp.where(is_grp, jnp.exp(gl - gmax), 0.0), axis=-1, keepdims=True)
    grp_p = 1.0 / gsum
    grp_lane = jnp.min(jnp.where(is_grp & (gl == gmax), lane, LANES), axis=-1, keepdims=True)
    grp_idx = grp_lane - N_EXPERTS

    in_grp = (lane < N_EXPERTS) & (lane // EXPERTS_PER_GROUP == grp_idx)
    el = jnp.where(in_grp, logits, neg)
    m1 = jnp.max(el, axis=-1, keepdims=True)
    e1 = jnp.min(jnp.where(in_grp & (el == m1), lane, LANES), axis=-1, keepdims=True)
    el2 = jnp.where(lane == e1, neg, el)
    m2 = jnp.max(el2, axis=-1, keepdims=True)
    e2 = jnp.min(jnp.where(in_grp & (lane != e1) & (el2 == m2), lane, LANES), axis=-1, keepdims=True)
    r = jnp.exp(m2 - m1)
    w1 = grp_p / (1.0 + r)
    w2 = grp_p * r / (1.0 + r)

    hit1 = lane == e1
    hit2 = lane == e2
    onehot = jnp.where(hit1 | hit2, 1.0, 0.0)
    prefix = _dot(strict_ref[...], onehot.astype(BF16)) + carry[0:1, :]
    rank1 = jnp.sum(jnp.where(hit1, prefix, 0.0), axis=-1, keepdims=True)
    rank2 = jnp.sum(jnp.where(hit2, prefix, 0.0), axis=-1, keepdims=True)
    total = carry[0:1, :] + jnp.sum(onehot, axis=0, keepdims=True)
    carry[...] = jnp.broadcast_to(total, carry.shape)
    cnt_ref[...] = jnp.broadcast_to(total, cnt_ref.shape)

    info = jnp.where(lane == 0, e1.astype(F32), 0.0)
    info = jnp.where(lane == 1, e2.astype(F32), info)
    info = jnp.where(lane == 2, w1, info)
    info = jnp.where(lane == 3, w2, info)
    info = jnp.where(lane == 4, rank1, info)
    info = jnp.where(lane == 5, rank2, info)
    info_ref[0] = info
    infot_ref[...] = info.T[0:SUBLANES, :]


def _router_call(x1, mod3, norm_w, w_router_group, b_router_group, w_router_expert, b_router_expert):
    bsz, seq, d = x1.shape
    tt = ROUTER_TILE
    nj = seq // tt
    pitch = d // LANES
    pad = LANES - N_EXPERTS - N_GROUPS
    wr = jnp.pad(jnp.concatenate([w_router_expert, w_router_group], axis=1), ((0, 0), (0, pad)))
    br = jnp.pad(jnp.concatenate([b_router_expert, b_router_group]), (0, pad)).reshape(1, LANES)
    return pl.pallas_call(
        _router_kernel,
        grid=(bsz, seq // tt),
        in_specs=[pl.BlockSpec((1, tt, d), lambda b, j: (b, j, 0)),
                  pl.BlockSpec((1,) + mod3.shape[1:], lambda b, j: (b, 0, 0)),
                  pl.BlockSpec((1, d), lambda b, j: (0, 0)),
                  pl.BlockSpec((d, LANES), lambda b, j: (0, 0)),
                  pl.BlockSpec((1, LANES), lambda b, j: (0, 0))],
        out_specs=[pl.BlockSpec((tt * pitch, LANES), lambda b, j: (b * nj + j, 0)),
                   pl.BlockSpec((1, tt, LANES), lambda b, j: (b, j, 0)),
                   pl.BlockSpec((SUBLANES, tt), lambda b, j: (0, b * nj + j)),
                   pl.BlockSpec((8, LANES), lambda b, j: (0, 0))],
        out_shape=[jax.ShapeDtypeStruct((bsz * seq * pitch, LANES), F32),
                   jax.ShapeDtypeStruct((bsz, seq, LANES), F32),
                   jax.ShapeDtypeStruct((SUBLANES, bsz * seq), F32),
                   jax.ShapeDtypeStruct((8, LANES), F32)],
        scratch_shapes=[pltpu.VMEM((8, LANES), F32), pltpu.VMEM((tt, tt), BF16)],
        compiler_params=pltpu.CompilerParams(dimension_semantics=("arbitrary", "arbitrary"),
                                             vmem_limit_bytes=VMEM_LIMIT),
        name="router",
    )(x1, mod3, norm_w.reshape(1, d), wr, br)


def _tile_rows(idx, pitch):
    if isinstance(idx, int):
        return pl.ds(idx * pitch, pitch)
    return pl.ds(pl.multiple_of(idx * pitch, pitch), pitch)


def _dispatch_kernel(nused_ref, padfrom_ref, padlen_ref, pos_ref, h_ref, xs_hbm, zbuf, zsem, sem, *, pitch, bm):
    step = pl.program_id(0)
    tt = h_ref.shape[0] // pitch

    def fill_copies(action):
        def per_expert(e, carry):
            cursor = padfrom_ref[e]
            n = padlen_ref[e]
            bit = bm // 2
            while bit >= 1:
                take = (n & bit) != 0

                @pl.when(take)
                def _(cursor=cursor, bit=bit):
                    action(pltpu.make_async_copy(zbuf.at[pl.ds(0, bit * pitch), :],
                                                 xs_hbm.at[pl.ds(pl.multiple_of(cursor * pitch, pitch), bit * pitch), :],
                                                 zsem.at[0]))
                cursor = cursor + jnp.where(take, bit, 0)
                bit //= 2
            return carry

        def per_block(b, carry):
            start = pl.multiple_of(b * bm * pitch, bm * pitch)
            action(pltpu.make_async_copy(zbuf, xs_hbm.at[pl.ds(start, bm * pitch), :], zsem.at[0]))
            return carry

        lax.fori_loop(0, N_EXPERTS, per_expert, 0)
        lax.fori_loop(nused_ref[0], xs_hbm.shape[0] // (bm * pitch), per_block, 0)

    @pl.when(step == 0)
    def _():
        zbuf[...] = jnp.zeros(zbuf.shape, F32)
        fill_copies(lambda cp: cp.start())

    for r in range(tt):
        for k in range(TOP_K):
            p = pos_ref[0, 0, k * tt + r]
            pltpu.make_async_copy(h_ref.at[_tile_rows(r, pitch), :], xs_hbm.at[_tile_rows(p, pitch), :],
                                  sem.at[0]).start(priority=k)
    for k in range(TOP_K):
        pltpu.make_async_copy(h_ref, xs_hbm.at[pl.ds(0, tt * pitch), :], sem.at[0]).wait()

    @pl.when(step == pl.num_programs(0) - 1)
    def _():
        fill_copies(lambda cp: cp.wait())


def _dispatch_call(h2, pos3, n_used, padfrom, padlen, n_rows, d):
    pitch = d // LANES
    bm = EXPERT_BLOCK
    n_tiles, _, two_tt = pos3.shape
    tt = two_tt // TOP_K
    grid_spec = pltpu.PrefetchScalarGridSpec(
        num_scalar_prefetch=3,
        grid=(n_tiles,),
        in_specs=[pl.BlockSpec((1, 1, two_tt), lambda s, *_: (s, 0, 0), memory_space=pltpu.SMEM),
                  pl.BlockSpec((tt * pitch, LANES), lambda s, *_: (s, 0))],
        out_specs=pl.BlockSpec(memory_space=pl.ANY),
        scratch_shapes=[pltpu.VMEM((bm * pitch, LANES), F32), pltpu.SemaphoreType.DMA((1,)),
                        pltpu.SemaphoreType.DMA((1,))],
    )
    return pl.pallas_call(
        functools.partial(_dispatch_kernel, pitch=pitch, bm=bm),
        grid_spec=grid_spec,
        out_shape=jax.ShapeDtypeStruct((n_rows * pitch, LANES), F32),
        compiler_params=pltpu.CompilerParams(dimension_semantics=("arbitrary",),
                                             vmem_limit_bytes=VMEM_LIMIT),
        name="dispatch",
    )(n_used, padfrom, padlen, pos3, h2)


def _expert_kernel(nused_ref, nruns_ref, first_ref, run_ref, rune_ref, x_hbm, wg_hbm, wu_hbm, wd_hbm, y_hbm,
                   wg_f32, wu_f32, wd_f32, wg_b, wu_b, wd_b, wsems, xbuf, ybuf, xsems, ysems):
    i = pl.program_id(0)
    n_used = nused_ref[0]
    n_runs = nruns_ref[0]
    d = wg_b.shape[0]
    pitch = d // LANES
    bm = xbuf.shape[1] // pitch
    n_slots = wg_f32.shape[0]

    def weight_copies(e, slot):
        return [pltpu.make_async_copy(src.at[e], dst.at[slot], wsems.at[slot])
                for src, dst in ((wg_hbm, wg_f32), (wu_hbm, wu_f32), (wd_hbm, wd_f32))]

    def start_run(r):
        @pl.when(r < n_runs)
        def _():
            for prio, cp in zip((1, 1, 0), weight_copies(rune_ref[r], r % n_slots)):
                cp.start(priority=prio)

    @pl.when(i == 0)
    def _():
        for r in range(n_slots - 1):
            start_run(r)

    @pl.when((i < n_used) & (first_ref[i] == 1))
    def _():
        run = run_ref[i]
        slot = run % n_slots
        for cp in weight_copies(0, slot):
            cp.wait()
        start_run(run + n_slots - 1)

        wg_b[...] = wg_f32[slot].astype(BF16)
        wu_b[...] = wu_f32[slot].astype(BF16)
        wd_b[...] = wd_f32[slot].astype(BF16)

    blk_rows = bm * pitch
    x_slots, y_slots = xbuf.shape[0], ybuf.shape[0]

    def block(ref, b):
        return ref.at[pl.ds(pl.multiple_of(b * blk_rows, blk_rows), blk_rows), :]

    def x_copy(b):
        return pltpu.make_async_copy(block(x_hbm, b), xbuf.at[b % x_slots], xsems.at[b % x_slots])

    def y_copy(b):
        return pltpu.make_async_copy(ybuf.at[b % y_slots], block(y_hbm, b), ysems.at[b % y_slots])

    @pl.when(i == 0)
    def _():
        for b in range(x_slots - 1):
            @pl.when(b < n_used)
            def _(b=b):
                x_copy(b).start()

    @pl.when(i < n_used)
    def _():
        x_copy(i).wait()

        @pl.when(i + x_slots - 1 < n_used)
        def _():
            x_copy(i + x_slots - 1).start()

        @pl.when(i >= y_slots)
        def _():
            y_copy(i - y_slots).wait()

        xb = _load_token_major(xbuf.at[i % x_slots], bm, d).astype(BF16)
        gate = _dot(xb, wg_b[...])
        up = _dot(xb, wu_b[...])
        hid = (_silu(gate) * up).astype(BF16)
        _store_token_major(ybuf.at[i % y_slots], _dot(hid, wd_b[...]))
        y_copy(i).start()

    @pl.when(i == n_used - 1)
    def _():
        for back in range(y_slots):
            @pl.when(i - back >= 0)
            def _(back=back):
                y_copy(i - back).wait()


def _expert_call(xs, block_e, n_used, w_gate, w_up, w_down):
    bm = EXPERT_BLOCK
    d, de = w_gate.shape[1], w_gate.shape[2]
    pitch = d // LANES
    n_blocks = xs.shape[0] // (bm * pitch)
    blk = jnp.arange(n_blocks, dtype=jnp.int32)
    used = blk < n_used[0]
    first = jnp.concatenate([jnp.ones((1,), jnp.int32), (block_e[1:] != block_e[:-1]).astype(jnp.int32)])
    first = jnp.where(used, first, 0)
    run = (jnp.cumsum(first) - 1).astype(jnp.int32)
    n_runs = jnp.sum(first).astype(jnp.int32).reshape(1)
    eids = jnp.arange(N_EXPERTS, dtype=jnp.int32)
    run_of_e = jnp.where((first[:, None] == 1) & (block_e[:, None] == eids[None, :]), run[:, None], -1).max(axis=0)
    run_e = jnp.sum(jnp.where(run_of_e[None, :] == eids[:, None], eids[None, :], 0), axis=1).astype(jnp.int32)
    grid_spec = pltpu.PrefetchScalarGridSpec(
        num_scalar_prefetch=5,
        grid=(n_blocks,),
        in_specs=[pl.BlockSpec(memory_space=pl.ANY)] * 4,
        out_specs=pl.BlockSpec(memory_space=pl.ANY),
        scratch_shapes=[pltpu.VMEM((WEIGHT_SLOTS, d, de), F32), pltpu.VMEM((WEIGHT_SLOTS, d, de), F32),
                        pltpu.VMEM((WEIGHT_SLOTS, de, d), F32),
                        pltpu.VMEM((d, de), BF16), pltpu.VMEM((d, de), BF16), pltpu.VMEM((de, d), BF16),
                        pltpu.SemaphoreType.DMA((WEIGHT_SLOTS,)),
                        pltpu.VMEM((ROW_SLOTS_IN, bm * pitch, LANES), F32),
                        pltpu.VMEM((ROW_SLOTS_OUT, bm * pitch, LANES), F32),
                        pltpu.SemaphoreType.DMA((ROW_SLOTS_IN,)), pltpu.SemaphoreType.DMA((ROW_SLOTS_OUT,))],
    )
    return pl.pallas_call(
        _expert_kernel,
        grid_spec=grid_spec,
        out_shape=jax.ShapeDtypeStruct((n_blocks * bm * pitch, LANES), F32),
        input_output_aliases={5: 0},
        compiler_params=pltpu.CompilerParams(dimension_semantics=("arbitrary",),
                                             vmem_limit_bytes=VMEM_LIMIT),
        name="experts",
    )(n_used, n_runs, first, run, run_e, xs, w_gate, w_up, w_down)


def _combine_kernel(pos_cur_ref, pos_nxt_ref, x_ref, info_ref, mod_ref, nw_ref, y_hbm, o_ref, ybuf_a, ybuf_b, sems):
    tt, d = x_ref.shape[1], x_ref.shape[2]
    pitch = d // LANES
    step = pl.program_id(0) * pl.num_programs(1) + pl.program_id(1)
    n_steps = pl.num_programs(0) * pl.num_programs(1)
    n_chunks = tt // COMBINE_ROWS

    def row_copy(pos_ref, buf, sem, k, r):
        p = pos_ref[0, 0, k * tt + r]
        return pltpu.make_async_copy(y_hbm.at[_tile_rows(p, pitch), :], buf.at[_tile_rows(k * tt + r, pitch), :], sem)

    def wait_tile(buf, sem):
        pltpu.make_async_copy(y_hbm.at[pl.ds(0, TOP_K * tt * pitch), :], buf, sem).wait()

    @pl.when(step == 0)
    def _():
        def body(r, carry):
            for k in range(TOP_K):
                row_copy(pos_cur_ref, ybuf_a, sems.at[0], k, r).start(priority=k)
            return carry
        lax.fori_loop(0, tt, body, 0, unroll=4)

    gate2 = mod_ref[0][5:6]

    def run_tile(cur, cur_sem, nxt, nxt_sem):
        wait_tile(cur, cur_sem)

        def chunk(c, carry):
            r0 = c * COMBINE_ROWS
            for r in range(COMBINE_ROWS):
                for k in range(TOP_K):
                    row_copy(pos_nxt_ref, nxt, nxt_sem, k, r0 + r).start(priority=k)
            rows = pl.ds(r0, COMBINE_ROWS)
            info = info_ref[0, rows, :]
            moe = (_load_token_major(cur, COMBINE_ROWS, d, r0 * pitch) * info[:, 2:3]
                   + _load_token_major(cur, COMBINE_ROWS, d, (tt + r0) * pitch) * info[:, 3:4])
            y = moe * lax.rsqrt(jnp.mean(moe * moe, axis=-1, keepdims=True) + EPS) * nw_ref[...]
            o_ref[0, rows, :] = x_ref[0, rows, :] + gate2 * y
            return carry
        for c in range(n_chunks):
            chunk(c, 0)

        @pl.when(step == n_steps - 1)
        def _():
            wait_tile(nxt, nxt_sem)

    @pl.when(step % 2 == 0)
    def _():
        run_tile(ybuf_a, sems.at[0], ybuf_b, sems.at[1])

    @pl.when(step % 2 == 1)
    def _():
        run_tile(ybuf_b, sems.at[1], ybuf_a, sems.at[0])


def _combine_call(x1, info, mod3, norm_w, y, pos3):
    bsz, seq, d = x1.shape
    tt = COMBINE_TILE
    nj = seq // tt
    n_tiles = bsz * nj
    pos_blk = lambda f: pl.BlockSpec((1, 1, TOP_K * tt), f, memory_space=pltpu.SMEM)
    return pl.pallas_call(
        _combine_kernel,
        grid=(bsz, nj),
        in_specs=[pos_blk(lambda b, j: (b * nj + j, 0, 0)),
                  pos_blk(lambda b, j: (jnp.minimum(b * nj + j + 1, n_tiles - 1), 0, 0)),
                  pl.BlockSpec((1, tt, d), lambda b, j: (b, j, 0)),
                  pl.BlockSpec((1, tt, LANES), lambda b, j: (b, j, 0)),
                  pl.BlockSpec((1,) + mod3.shape[1:], lambda b, j: (b, 0, 0)),
                  pl.BlockSpec((1, d), lambda b, j: (0, 0)),
                  pl.BlockSpec(memory_space=pl.ANY)],
        out_specs=pl.BlockSpec((1, tt, d), lambda b, j: (b, j, 0)),
        out_shape=jax.ShapeDtypeStruct((bsz, seq, d), F32),
        scratch_shapes=[pltpu.VMEM((TOP_K * tt * (d // LANES), LANES), F32),
                        pltpu.VMEM((TOP_K * tt * (d // LANES), LANES), F32), pltpu.SemaphoreType.DMA((2,))],
        compiler_params=pltpu.CompilerParams(dimension_semantics=("arbitrary", "arbitrary"),
                                             vmem_limit_bytes=VMEM_LIMIT),
        name="combine",
    )(pos3, pos3, x1, info, mod3, norm_w.reshape(1, d), y)


def _layer(x, mod, norm_pre_mix, norm_post_mix, w_in, dn_conv_w, dn_a_log, dn_dt_bias, dn_norm_w,
           cf_pw1_b, cf_dw_w, cf_dw_b, cf_ln_w, cf_ln_b, w_out, norm_pre_ffn, norm_post_ffn,
           w_router_group, b_router_group, w_router_expert, b_router_expert, w_gate, w_up, w_down):
    bsz, seq, d = x.shape
    t = bsz * seq
    mod3 = mod.reshape(bsz, -1, d)
    x1 = _mixer_call(x, mod3, norm_pre_mix, norm_post_mix, w_in, dn_conv_w, dn_a_log, dn_dt_bias, dn_norm_w,
                     cf_pw1_b, cf_dw_w, cf_dw_b, cf_ln_w, cf_ln_b, w_out)
    h2, info, info_t, cnt = _router_call(x1, mod3, norm_pre_ffn, w_router_group, b_router_group,
                                         w_router_expert, b_router_expert)

    bm = EXPERT_BLOCK
    expert_id = info_t[0:TOP_K].astype(jnp.int32)
    rank = info_t[4:4 + TOP_K].astype(jnp.int32)
    counts = cnt[0, :N_EXPERTS].astype(jnp.int32)
    padded = (counts + bm - 1) // bm * bm
    pend = jnp.cumsum(padded)
    pstart = pend - padded
    eids = jnp.arange(N_EXPERTS, dtype=jnp.int32)
    onehot = expert_id[:, None, :] == eids[None, :, None]
    pos = jnp.sum(jnp.where(onehot, pstart[None, :, None], 0), axis=1) + rank

    def tiled(tt):
        return pos.reshape(TOP_K, t // tt, tt).transpose(1, 0, 2).reshape(t // tt, 1, TOP_K * tt)
    n_blocks = -(-(t * TOP_K) // bm) + N_EXPERTS
    block_start = jnp.arange(n_blocks, dtype=jnp.int32) * bm
    block_e = jnp.minimum(jnp.sum(pend[None, :] <= block_start[:, None], axis=1), N_EXPERTS - 1).astype(jnp.int32)
    n_used = (pend[-1] // bm).astype(jnp.int32).reshape(1)

    xs = _dispatch_call(h2, tiled(DISPATCH_TILE), n_used, (pstart + counts).astype(jnp.int32),
                        (padded - counts).astype(jnp.int32), n_blocks * bm, d)
    y = _expert_call(xs, block_e, n_used, w_gate, w_up, w_down)
    return _combine_call(x1, info, mod3, norm_post_ffn, y, tiled(COMBINE_TILE))


def kernel(x, c, w_ada, b_ada, norm_pre_mix, norm_post_mix, w_in, dn_conv_w, dn_a_log, dn_dt_bias, dn_norm_w,
           cf_pw1_b, cf_dw_w, cf_dw_b, cf_ln_w, cf_ln_b, w_out, norm_pre_ffn, norm_post_ffn,
           w_router_group, b_router_group, w_router_expert, b_router_expert, w_gate, w_up, w_down):
    depth = w_ada.shape[0]
    for l in range(depth):
        mod = _ada_call(c, w_ada[l], b_ada[l])
        x = _layer(x, mod, norm_pre_mix[l], norm_post_mix[l], w_in[l], dn_conv_w[l], dn_a_log[l],
                   dn_dt_bias[l], dn_norm_w[l], cf_pw1_b[l], cf_dw_w[l], cf_dw_b[l], cf_ln_w[l], cf_ln_b[l],
                   w_out[l], norm_pre_ffn[l], norm_post_ffn[l], w_router_group[l], b_router_group[l],
                   w_router_expert[l], b_router_expert[l], w_gate[l], w_up[l], w_down[l])
    return x
```

```python
import functools

import jax
import jax.numpy as jnp
from jax import lax
from jax.experimental import pallas as pl
from jax.experimental.pallas import tpu as pltpu

F32 = jnp.float32
BF16 = jnp.bfloat16
EPS = 1e-6

DN_HEADS = 4
HEAD_DIM = 128
DN_WIDTH = DN_HEADS * HEAD_DIM
DN_CONV = 4
DN_CHUNK = 64
CF_KERNEL = 31
N_GROUPS = 8
EXPERTS_PER_GROUP = 8
N_EXPERTS = N_GROUPS * EXPERTS_PER_GROUP
TOP_K = 2

LANES = 128
SUBLANES = 8
SEQ_TILE = 256
CONV_ROWS = 32
MIXER_PAIR = 2
MIXER_SKEW = 4
ROUTER_TILE = 512
EXPERT_BLOCK = 256
COMBINE_TILE = 512
COMBINE_ROWS = 32
WEIGHT_SLOTS = 3
ROW_SLOTS_IN = 3
ROW_SLOTS_OUT = 2
DISPATCH_TILE = 1024
DISPATCH_SLOTS = 3
VMEM_LIMIT = 56 * 1024 * 1024


def _dot(a, b):
    return jnp.dot(a, b, preferred_element_type=F32)


def _dot_nt(a, b):
    return lax.dot_general(a, b, (((1,), (1,)), ((), ())), preferred_element_type=F32)


def _dot_tn(a, b):
    return lax.dot_general(a, b, (((0,), (0,)), ((), ())), preferred_element_type=F32)


def _split3(x):
    hi = x.astype(BF16)
    r1 = x - hi.astype(F32)
    mid = r1.astype(BF16)
    lo = (r1 - mid.astype(F32)).astype(BF16)
    return hi, mid, lo


def _silu(x):
    return x * jax.nn.sigmoid(x)


def _softplus(x):
    return jnp.maximum(x, 0.0) + jnp.log1p(jnp.exp(-jnp.abs(x)))


def _store_token_major(ref, val, base=0):
    n, d = val.shape
    pitch = d // LANES
    for j in range(pitch):
        ref[pl.ds(base + j, n, stride=pitch), :] = val[:, j * LANES:(j + 1) * LANES]


def _load_token_major(ref, n, d, base=0):
    pitch = d // LANES
    return jnp.concatenate([ref[pl.ds(base + j, n, stride=pitch), :] for j in range(pitch)], axis=1)


def _ada_kernel(c_ref, w_ref, b_ref, o_ref):
    c = c_ref[...]
    ca = _silu(c)
    c_hi, c_lo, _ = _split3(ca)
    w_hi, w_lo, _ = _split3(w_ref[...])
    o_ref[...] = _dot(c_hi, w_hi) + (_dot(c_hi, w_lo) + _dot(c_lo, w_hi)) + b_ref[...]


def _ada_call(c, w, b):
    bsz, d = c.shape
    n = w.shape[1]
    tn = 512
    return pl.pallas_call(
        _ada_kernel,
        grid=(n // tn,),
        in_specs=[pl.BlockSpec((bsz, d), lambda i: (0, 0)),
                  pl.BlockSpec((d, tn), lambda i: (0, i)),
                  pl.BlockSpec((1, tn), lambda i: (0, i))],
        out_specs=pl.BlockSpec((bsz, tn), lambda i: (0, i)),
        out_shape=jax.ShapeDtypeStruct((bsz, n), F32),
        compiler_params=pltpu.CompilerParams(dimension_semantics=("arbitrary",),
                                             vmem_limit_bytes=VMEM_LIMIT),
        name="ada",
    )(c, w, b.reshape(1, n))


def _time_perm(ts, transpose):
    ri = lax.broadcasted_iota(jnp.int32, (ts, ts), 0)
    ci = lax.broadcasted_iota(jnp.int32, (ts, ts), 1)
    strided, natural = (ci, ri) if transpose else (ri, ci)
    return jnp.where(natural == (ts // SUBLANES) * (strided % SUBLANES) + strided // SUBLANES, 1.0, 0.0).astype(BF16)


def _fill_conv_window(ext_ref, prev_ref, cur, n_taps):
    ts = cur.shape[0]
    lead = (n_taps - 1) * SUBLANES
    tail = cur[ts - lead:, :]
    sub = lax.broadcasted_iota(jnp.int32, tail.shape, 0) % SUBLANES
    merged = jnp.where(sub == SUBLANES - 1, prev_ref[...], tail)
    for g in range(n_taps - 1):
        rows = slice(g * SUBLANES, (g + 1) * SUBLANES)
        ext_ref[rows, :] = pltpu.roll(merged[rows, :], 1, 0)
    ext_ref[lead:lead + ts, :] = cur
    prev_ref[...] = tail


def _conv_block(ext_ref, w_ref, n_taps, r0, row_blk, c0, col_blk):
    groups = row_blk // SUBLANES
    acc = [jnp.zeros((SUBLANES, col_blk), F32) for _ in range(groups)]
    for k in range(n_taps):
        w = w_ref[k * SUBLANES:(k + 1) * SUBLANES, c0:c0 + col_blk]
        for g in range(groups):
            lo = r0 + (k + g) * SUBLANES
            acc[g] = acc[g] + w * ext_ref[lo:lo + SUBLANES, c0:c0 + col_blk]
    return jnp.concatenate(acc, axis=0)


def _mixer_kernel(x_ref, mod_ref, npre_ref, npost_ref, win_ref,
                  convw_ref, alog_ref, dtb_ref, dnw_ref, pw1b_ref, dww_ref, dwb_ref, lnw_ref, lnb_ref,
                  woutf_ref, o_ref, qkv_ext, qkv_prev, qkv_p16, qkv_act, cf_ext, cf_prev, cf_p16, state, mixed,
                  consts, wqkv_ref, wz_ref, wba_ref, wcf_ref, wout_ref):
    ts = x_ref.shape[2]
    n_chunks = ts // DN_CHUNK

    @pl.when(pl.program_id(1) == 0)
    def _():
        qkv_prev[...] = jnp.zeros(qkv_prev.shape, F32)
        cf_prev[...] = jnp.zeros(cf_prev.shape, F32)
        state[...] = jnp.zeros(state.shape, F32)

    @pl.when((pl.program_id(0) == 0) & (pl.program_id(1) == 0))
    def _():
        ri = lax.broadcasted_iota(jnp.int32, (ts, ts), 0)
        ci = lax.broadcasted_iota(jnp.int32, (ts, ts), 1)
        consts[0] = _time_perm(ts, False)
        consts[1] = _time_perm(ts, True)
        consts[2] = jnp.where((ri // DN_CHUNK == ci // DN_CHUNK) & (ci <= ri), 1.0, 0.0).astype(BF16)
        n_qkv, n_z, n_ba = wqkv_ref.shape[1], wz_ref.shape[1], 2 * DN_HEADS
        for c0 in range(0, n_qkv, DN_WIDTH):
            wqkv_ref[:, c0:c0 + DN_WIDTH] = win_ref[c0:c0 + DN_WIDTH, :].T.astype(BF16)
        wz_ref[...] = win_ref[n_qkv:n_qkv + n_z, :].T.astype(BF16)
        lane = lax.broadcasted_iota(jnp.int32, wba_ref.shape, 1)
        wba_ref[...] = jnp.where(lane < n_ba, win_ref[n_qkv + n_z:n_qkv + n_z + LANES, :].T, 0.0).astype(BF16)
        cf0 = n_qkv + n_z + n_ba
        for c0 in range(0, wcf_ref.shape[1], DN_WIDTH):
            wcf_ref[:, c0:c0 + DN_WIDTH] = win_ref[cf0 + c0:cf0 + c0 + DN_WIDTH, :].T.astype(BF16)
        wout_ref[...] = woutf_ref[...].astype(BF16)

    r64 = lax.broadcasted_iota(jnp.int32, (DN_CHUNK, DN_CHUNK), 0)
    c64 = lax.broadcasted_iota(jnp.int32, (DN_CHUNK, DN_CHUNK), 1)
    causal = c64 <= r64
    strict = c64 < r64
    eye = jnp.where(c64 == r64, 1.0, 0.0).astype(F32)
    dnw = dnw_ref[...]
    to_natural = consts[1]
    tri = consts[2]
    cells = [(hd, ch) for hd in range(DN_HEADS) for ch in range(n_chunks)]
    rows = lambda ch: slice(ch * DN_CHUNK, (ch + 1) * DN_CHUNK)

    def tile_stages(sid, qkv_ext, qkv_prev, qkv_p16, qkv_act, cf_ext, cf_prev, cf_p16, state, mixed):
        x = x_ref[sid, 0]
        mod = mod_ref[sid, 0]
        shift1, scale1, gate1 = mod[0:1], mod[1:2], mod[2:3]
        h = x * lax.rsqrt(jnp.mean(x * x, axis=-1, keepdims=True) + EPS)
        h = h * npre_ref[...] * (1.0 + scale1) + shift1
        hb = h.astype(BF16)
        yield
        hb_st = _dot(consts[0], hb).astype(BF16)

        _fill_conv_window(qkv_ext, qkv_prev, _dot(hb_st, wqkv_ref[...]), DN_CONV)
        yield
        for r0 in range(0, ts, CONV_ROWS):
            for c0 in range(0, qkv_ext.shape[1], DN_WIDTH):
                blk = _conv_block(qkv_ext, convw_ref, DN_CONV, r0, CONV_ROWS, c0, DN_WIDTH)
                qkv_p16[r0:r0 + CONV_ROWS, c0:c0 + DN_WIDTH] = _silu(blk).astype(BF16)
        yield
        qkv_act[...] = _dot(to_natural, qkv_p16[...])

        cf_pre = _dot(hb_st, wcf_ref[...]) + pw1b_ref[...]
        cfw = cf_ext.shape[1]
        _fill_conv_window(cf_ext, cf_prev, cf_pre[:, :cfw] * jax.nn.sigmoid(cf_pre[:, cfw:]), CF_KERNEL)
        cf_pending = list(range(0, ts, CONV_ROWS))
        yield

        def cf_step(n=1):
            for _ in range(min(n, len(cf_pending))):
                r0 = cf_pending.pop(0)
                cf = _conv_block(cf_ext, dww_ref, CF_KERNEL, r0, CONV_ROWS, 0, cfw) + dwb_ref[...]
                mu = jnp.mean(cf, axis=-1, keepdims=True)
                xc = cf - mu
                var = jnp.mean(xc * xc, axis=-1, keepdims=True)
                cfn = xc * lax.rsqrt(var + EPS) * lnw_ref[...] + lnb_ref[...]
                cf_p16[r0:r0 + CONV_ROWS, :] = _silu(cfn).astype(BF16)

        z = _dot(hb, wz_ref[...])
        ba = _dot(hb, wba_ref[...])
        beta_all = jax.nn.sigmoid(ba)
        g_all = -jnp.exp(alog_ref[...]) * _softplus(ba + dtb_ref[...])

        g_hi, g_mid, g_lo = _split3(g_all)
        gcum = _dot(tri, g_hi) + _dot(tri, g_mid) + _dot(tri, g_lo)
        gcum_t = gcum.T
        exp_g = jnp.exp(gcum)
        yield

        heads = []
        for hd in range(DN_HEADS):
            lo = hd * HEAD_DIM
            qh = qkv_act[:, lo:lo + HEAD_DIM]
            kh = qkv_act[:, DN_WIDTH + lo:DN_WIDTH + lo + HEAD_DIM]
            vh = qkv_act[:, 2 * DN_WIDTH + lo:2 * DN_WIDTH + lo + HEAD_DIM]
            qn = qh * lax.rsqrt(jnp.sum(qh * qh, axis=-1, keepdims=True) + EPS) * (HEAD_DIM ** -0.5)
            kn = kh * lax.rsqrt(jnp.sum(kh * kh, axis=-1, keepdims=True) + EPS)
            beta_h = beta_all[:, hd:hd + 1]
            gc_h = gcum[:, DN_HEADS + hd:DN_HEADS + hd + 1]
            eg_h = exp_g[:, DN_HEADS + hd:DN_HEADS + hd + 1]
            k_beta = kn * beta_h
            heads.append(dict(qn=qn, kn=kn, k_beta=k_beta, v_beta=vh * beta_h, kbg=k_beta * eg_h, qg=qn * eg_h,
                              gc=gc_h, zg=_silu(z[:, lo:lo + HEAD_DIM])))
        yield

        decay, kq = {}, {}
        for hd, ch in cells:
            hv, sl = heads[hd], rows(ch)
            gc_row = gcum_t[DN_HEADS + hd:DN_HEADS + hd + 1, sl]
            decay[hd, ch] = jnp.where(causal, jnp.exp(hv["gc"][sl] - gc_row), 0.0)
            lhs = jnp.concatenate([hv["k_beta"][sl], hv["qn"][sl]], axis=0).astype(BF16)
            kq[hd, ch] = _dot_nt(lhs, hv["kn"][sl].astype(BF16))
        a = {c: jnp.where(strict, kq[c][:DN_CHUNK] * decay[c], 0.0) for c in cells}
        attn = {c: (kq[c][DN_CHUNK:] * decay[c]).astype(BF16) for c in cells}
        yield

        t_inv = {c: eye - a[c] for c in cells}
        pw = {c: a[c].astype(BF16) for c in cells}
        pw = {c: _dot(pw[c], pw[c]).astype(BF16) for c in cells}
        cf_step()
        yield
        for _ in range(4):
            both = {c: _dot(jnp.concatenate([t_inv[c].astype(BF16), pw[c]], axis=0), pw[c]) for c in cells}
            t_inv = {c: t_inv[c] + both[c][:DN_CHUNK] for c in cells}
            pw = {c: both[c][DN_CHUNK:].astype(BF16) for c in cells}
            cf_step()
            yield
        t_inv = {c: t_inv[c] + _dot(t_inv[c].astype(BF16), pw[c]) for c in cells}

        sol, aw, ks, glast = {}, {}, {}, {}
        for hd, ch in cells:
            hv, sl = heads[hd], rows(ch)
            rhs = jnp.concatenate([hv["v_beta"][sl], hv["kbg"][sl]], axis=1).astype(BF16)
            sol[hd, ch] = _dot(t_inv[hd, ch].astype(BF16), rhs).astype(BF16)
        yield
        for hd, ch in cells:
            hv, sl = heads[hd], rows(ch)
            gc_col = hv["gc"][sl]
            glast[hd, ch] = gc_col[DN_CHUNK - 1:DN_CHUNK]
            k_dec = (hv["kn"][sl] * jnp.exp(glast[hd, ch] - gc_col)).astype(BF16)
            aw[hd, ch] = _dot(attn[hd, ch], sol[hd, ch])
            ks[hd, ch] = _dot_tn(k_dec, sol[hd, ch])
        yield

        s_in = {}
        s_cur = [state[hd] for hd in range(DN_HEADS)]
        for ch in range(n_chunks):
            for hd in range(DN_HEADS):
                s_in[hd, ch] = s_cur[hd].astype(BF16)
                kd_u, kd_w = ks[hd, ch][:, :HEAD_DIM], ks[hd, ch][:, HEAD_DIM:]
                s_cur[hd] = s_cur[hd] * jnp.exp(glast[hd, ch]) + kd_u - _dot(kd_w.astype(BF16), s_in[hd, ch])
            cf_step()
            yield
        for hd in range(DN_HEADS):
            state[hd] = s_cur[hd]

        for hd, ch in cells:
            hv, sl = heads[hd], rows(ch)
            lo = hd * HEAD_DIM
            q_eff = (hv["qg"][sl] - aw[hd, ch][:, HEAD_DIM:]).astype(BF16)
            o = _dot(q_eff, s_in[hd, ch]) + aw[hd, ch][:, :HEAD_DIM]
            on = o * lax.rsqrt(jnp.mean(o * o, axis=-1, keepdims=True) + EPS) * dnw * hv["zg"][sl]
            mixed[sl, lo:lo + HEAD_DIM] = on.astype(BF16)
        yield

        cf_step(len(cf_pending))
        mixed[:, DN_WIDTH:DN_WIDTH + cfw] = _dot(to_natural, cf_p16[...]).astype(BF16)
        yield

        out = _dot(mixed[...], wout_ref[...])
        yield
        y = out * lax.rsqrt(jnp.mean(out * out, axis=-1, keepdims=True) + EPS) * npost_ref[...]
        o_ref[sid, 0] = x + gate1 * y

    scratch = (qkv_ext, qkv_prev, qkv_p16, qkv_act, cf_ext, cf_prev, cf_p16, state, mixed)
    streams = [tile_stages(sid, *(ref.at[sid] for ref in scratch)) for sid in range(x_ref.shape[0])]
    live = [True] * len(streams)

    def advance(i, n=1):
        for _ in range(n):
            if live[i]:
                try:
                    next(streams[i])
                except StopIteration:
                    live[i] = False

    advance(0, MIXER_SKEW)
    while any(live):
        for i in range(len(streams)):
            advance(i)


def _mixer_call(x, mod3, npre, npost, w_in, dn_conv_w, dn_a_log, dn_dt_bias, dn_norm_w,
                cf_pw1_b, cf_dw_w, cf_dw_b, cf_ln_w, cf_ln_b, w_out):
    bsz, seq, d = x.shape
    ts = SEQ_TILE
    cfw = cf_dw_w.shape[1]
    n_qkv = 3 * DN_WIDTH
    alog = jnp.pad(dn_a_log, (DN_HEADS, LANES - 2 * DN_HEADS)).reshape(1, LANES)
    dtb = jnp.pad(dn_dt_bias, (DN_HEADS, LANES - 2 * DN_HEADS)).reshape(1, LANES)
    dww = jnp.repeat(cf_dw_w, SUBLANES, axis=0)
    convw = jnp.repeat(dn_conv_w, SUBLANES, axis=0)

    def full(a):
        mode = dict(pipeline_mode=pl.Buffered(1)) if a.size >= d * d else {}
        return pl.BlockSpec(a.shape, lambda b, j: (0,) * a.ndim, **mode)

    row = lambda a: a.reshape(1, -1)
    nb = MIXER_PAIR
    x4 = x.reshape(nb, bsz // nb, seq, d)
    mod4 = mod3.reshape((nb, bsz // nb) + mod3.shape[1:])
    operands = [x4, mod4, row(npre), row(npost), w_in.T, convw, alog, dtb, row(dn_norm_w),
                row(cf_pw1_b), dww, row(cf_dw_b), row(cf_ln_w), row(cf_ln_b), w_out]
    in_specs = [pl.BlockSpec((nb, 1, ts, d), lambda b, j: (0, b, j, 0)),
                pl.BlockSpec((nb, 1) + mod3.shape[1:], lambda b, j: (0, b, 0, 0))]
    in_specs += [full(a) for a in operands[2:]]
    out = pl.pallas_call(
        _mixer_kernel,
        grid=(bsz // nb, seq // ts),
        in_specs=in_specs,
        out_specs=pl.BlockSpec((nb, 1, ts, d), lambda b, j: (0, b, j, 0)),
        out_shape=jax.ShapeDtypeStruct((nb, bsz // nb, seq, d), F32),
        scratch_shapes=[pltpu.VMEM((nb, (DN_CONV - 1) * SUBLANES + ts, n_qkv), F32),
                        pltpu.VMEM((nb, (DN_CONV - 1) * SUBLANES, n_qkv), F32),
                        pltpu.VMEM((nb, ts, n_qkv), BF16),
                        pltpu.VMEM((nb, ts, n_qkv), F32),
                        pltpu.VMEM((nb, (CF_KERNEL - 1) * SUBLANES + ts, cfw), F32),
                        pltpu.VMEM((nb, (CF_KERNEL - 1) * SUBLANES, cfw), F32),
                        pltpu.VMEM((nb, ts, cfw), BF16),
                        pltpu.VMEM((nb, DN_HEADS, HEAD_DIM, HEAD_DIM), F32),
                        pltpu.VMEM((nb, ts, DN_WIDTH + cfw), BF16),
                        pltpu.VMEM((3, ts, ts), BF16),
                        pltpu.VMEM((d, n_qkv), BF16), pltpu.VMEM((d, DN_WIDTH), BF16), pltpu.VMEM((d, LANES), BF16),
                        pltpu.VMEM((d, 2 * cfw), BF16), pltpu.VMEM(w_out.shape, BF16)],
        compiler_params=pltpu.CompilerParams(dimension_semantics=("arbitrary", "arbitrary"),
                                             vmem_limit_bytes=VMEM_LIMIT),
        name="mixer",
    )(*operands)
    return out.reshape(bsz, seq, d)


def _router_kernel(x_ref, mod_ref, nw_ref, wr_ref, br_ref, h_ref, info_ref, infot_ref, cnt_ref, carry, strict_ref):
    tt = x_ref.shape[1]

    @pl.when((pl.program_id(0) == 0) & (pl.program_id(1) == 0))
    def _():
        carry[...] = jnp.zeros(carry.shape, F32)
        rr = lax.broadcasted_iota(jnp.int32, (tt, tt), 0)
        cc = lax.broadcasted_iota(jnp.int32, (tt, tt), 1)
        strict_ref[...] = jnp.where(cc < rr, 1.0, 0.0).astype(BF16)

    x = x_ref[0]
    mod = mod_ref[0]
    shift2, scale2 = mod[3:4], mod[4:5]
    h = x * lax.rsqrt(jnp.mean(x * x, axis=-1, keepdims=True) + EPS)
    h = h * nw_ref[...] * (1.0 + scale2) + shift2
    _store_token_major(h_ref, h)

    h_hi, h_lo, _ = _split3(h)
    w_hi, w_lo, _ = _split3(wr_ref[...])
    logits = _dot(h_hi, w_hi) + (_dot(h_hi, w_lo) + _dot(h_lo, w_hi)) + br_ref[...]

    lane = lax.broadcasted_iota(jnp.int32, (tt, LANES), 1)
    neg = -jnp.inf
    is_grp = (lane >= N_EXPERTS) & (lane < N_EXPERTS + N_GROUPS)
    gl = jnp.where(is_grp, logits, neg)
    gmax = jnp.max(gl, axis=-1, keepdims=True)
    gsum = jnp.sum(jnp.where(is_grp, jnp.exp(gl - gmax), 0.0), axis=-1, keepdims=True)
    grp_p = 1.0 / gsum
    grp_lane = jnp.min(jnp.where(is_grp & (gl == gmax), lane, LANES), axis=-1, keepdims=True)
    grp_idx = grp_lane - N_EXPERTS

    in_grp = (lane < N_EXPERTS) & (lane // EXPERTS_PER_GROUP == grp_idx)
    el = jnp.where(in_grp, logits, neg)
    m1 = jnp.max(el, axis=-1, keepdims=True)
    e1 = jnp.min(jnp.where(in_grp & (el == m1), lane, LANES), axis=-1, keepdims=True)
    el2 = jnp.where(lane == e1, neg, el)
    m2 = jnp.max(el2, axis=-1, keepdims=True)
    e2 = jnp.min(jnp.where(in_grp & (lane != e1) & (el2 == m2), lane, LANES), axis=-1, keepdims=True)
    r = jnp.exp(m2 - m1)
    w1 = grp_p / (1.0 + r)
    w2 = grp_p * r / (1.0 + r)

    hit1 = lane == e1
    hit2 = lane == e2
    onehot = jnp.where(hit1 | hit2, 1.0, 0.0)
    prefix = _dot(strict_ref[...], onehot.astype(BF16)) + carry[0:1, :]
    rank1 = jnp.sum(jnp.where(hit1, prefix, 0.0), axis=-1, keepdims=True)
    rank2 = jnp.sum(jnp.where(hit2, prefix, 0.0), axis=-1, keepdims=True)
    total = carry[0:1, :] + jnp.sum(onehot, axis=0, keepdims=True)
    carry[...] = jnp.broadcast_to(total, carry.shape)
    cnt_ref[...] = jnp.broadcast_to(total, cnt_ref.shape)

    info = jnp.where(lane == 0, e1.astype(F32), 0.0)
    info = jnp.where(lane == 1, e2.astype(F32), info)
    info = jnp.where(lane == 2, w1, info)
    info = jnp.where(lane == 3, w2, info)
    info = jnp.where(lane == 4, rank1, info)
    info = jnp.where(lane == 5, rank2, info)
    info_ref[0] = info
    infot_ref[...] = info.T[0:SUBLANES, :]


def _router_call(x1, mod3, norm_w, w_router_group, b_router_group, w_router_expert, b_router_expert):
    bsz, seq, d = x1.shape
    tt = ROUTER_TILE
    nj = seq // tt
    pitch = d // LANES
    pad = LANES - N_EXPERTS - N_GROUPS
    wr = jnp.pad(jnp.concatenate([w_router_expert, w_router_group], axis=1), ((0, 0), (0, pad)))
    br = jnp.pad(jnp.concatenate([b_router_expert, b_router_group]), (0, pad)).reshape(1, LANES)
    return pl.pallas_call(
        _router_kernel,
        grid=(bsz, seq // tt),
        in_specs=[pl.BlockSpec((1, tt, d), lambda b, j: (b, j, 0)),
                  pl.BlockSpec((1,) + mod3.shape[1:], lambda b, j: (b, 0, 0)),
                  pl.BlockSpec((1, d), lambda b, j: (0, 0)),
                  pl.BlockSpec((d, LANES), lambda b, j: (0, 0)),
                  pl.BlockSpec((1, LANES), lambda b, j: (0, 0))],
        out_specs=[pl.BlockSpec((tt * pitch, LANES), lambda b, j: (b * nj + j, 0)),
                   pl.BlockSpec((1, tt, LANES), lambda b, j: (b, j, 0)),
                   pl.BlockSpec((SUBLANES, tt), lambda b, j: (0, b * nj + j)),
                   pl.BlockSpec((8, LANES), lambda b, j: (0, 0))],
        out_shape=[jax.ShapeDtypeStruct((bsz * seq * pitch, LANES), F32),
                   jax.ShapeDtypeStruct((bsz, seq, LANES), F32),
                   jax.ShapeDtypeStruct((SUBLANES, bsz * seq), F32),
                   jax.ShapeDtypeStruct((8, LANES), F32)],
        scratch_shapes=[pltpu.VMEM((8, LANES), F32), pltpu.VMEM((tt, tt), BF16)],
        compiler_params=pltpu.CompilerParams(dimension_semantics=("arbitrary", "arbitrary"),
                                             vmem_limit_bytes=VMEM_LIMIT),
        name="router",
    )(x1, mod3, norm_w.reshape(1, d), wr, br)


def _tile_rows(idx, pitch):
    if isinstance(idx, int):
        return pl.ds(idx * pitch, pitch)
    return pl.ds(pl.multiple_of(idx * pitch, pitch), pitch)


def _dispatch_kernel(nused_ref, padfrom_ref, padlen_ref, pos_ref, h_hbm, xs_hbm, zbuf, zsem, hbuf, hsems, ssems,
                     *, pitch, bm):
    step = pl.program_id(0)
    tt = hbuf.shape[1] // pitch

    def fill_copies(action):
        def per_expert(e, carry):
            cursor = padfrom_ref[e]
            n = padlen_ref[e]
            bit = bm // 2
            while bit >= 1:
                take = (n & bit) != 0

                @pl.when(take)
                def _(cursor=cursor, bit=bit):
                    action(pltpu.make_async_copy(zbuf.at[pl.ds(0, bit * pitch), :],
                                                 xs_hbm.at[pl.ds(pl.multiple_of(cursor * pitch, pitch), bit * pitch), :],
                                                 zsem.at[0]))
                cursor = cursor + jnp.where(take, bit, 0)
                bit //= 2
            return carry

        def per_block(b, carry):
            start = pl.multiple_of(b * bm * pitch, bm * pitch)
            action(pltpu.make_async_copy(zbuf, xs_hbm.at[pl.ds(start, bm * pitch), :], zsem.at[0]))
            return carry

        lax.fori_loop(0, N_EXPERTS, per_expert, 0)
        lax.fori_loop(nused_ref[0], xs_hbm.shape[0] // (bm * pitch), per_block, 0)

    @pl.when(step == 0)
    def _():
        zbuf[...] = jnp.zeros(zbuf.shape, F32)
        fill_copies(lambda cp: cp.start())

    n_steps = pl.num_programs(0)
    slots = hbuf.shape[0]
    tile_rows = tt * pitch

    def tile_in(s):
        src = h_hbm.at[pl.ds(pl.multiple_of(s * tile_rows, tile_rows), tile_rows), :]
        return pltpu.make_async_copy(src, hbuf.at[s % slots], hsems.at[s % slots])

    def scatter_wait(s):
        for k in range(TOP_K):
            pltpu.make_async_copy(hbuf.at[s % slots], xs_hbm.at[pl.ds(0, tile_rows), :], ssems.at[s % slots]).wait()

    @pl.when(step == 0)
    def _():
        for s in range(slots - 1):
            @pl.when(s < n_steps)
            def _(s=s):
                tile_in(s).start()

    tile_in(step).wait()
    src = hbuf.at[step % slots]
    for r in range(tt):
        for k in range(TOP_K):
            p = pos_ref[0, 0, k * tt + r]
            pltpu.make_async_copy(src.at[_tile_rows(r, pitch), :], xs_hbm.at[_tile_rows(p, pitch), :],
                                  ssems.at[step % slots]).start(priority=k)

    @pl.when(step >= 1)
    def _():
        scatter_wait(step - 1)

    @pl.when(step + slots - 1 < n_steps)
    def _():
        tile_in(step + slots - 1).start()

    @pl.when(step == n_steps - 1)
    def _():
        scatter_wait(step)
        fill_copies(lambda cp: cp.wait())


def _dispatch_call(h2, pos3, n_used, padfrom, padlen, n_rows, d):
    pitch = d // LANES
    bm = EXPERT_BLOCK
    n_tiles, _, two_tt = pos3.shape
    tt = two_tt // TOP_K
    grid_spec = pltpu.PrefetchScalarGridSpec(
        num_scalar_prefetch=3,
        grid=(n_tiles,),
        in_specs=[pl.BlockSpec((1, 1, two_tt), lambda s, *_: (s, 0, 0), memory_space=pltpu.SMEM),
                  pl.BlockSpec(memory_space=pl.ANY)],
        out_specs=pl.BlockSpec(memory_space=pl.ANY),
        scratch_shapes=[pltpu.VMEM((bm * pitch, LANES), F32), pltpu.SemaphoreType.DMA((1,)),
                        pltpu.VMEM((DISPATCH_SLOTS, tt * pitch, LANES), F32),
                        pltpu.SemaphoreType.DMA((DISPATCH_SLOTS,)), pltpu.SemaphoreType.DMA((DISPATCH_SLOTS,))],
    )
    return pl.pallas_call(
        functools.partial(_dispatch_kernel, pitch=pitch, bm=bm),
        grid_spec=grid_spec,
        out_shape=jax.ShapeDtypeStruct((n_rows * pitch, LANES), F32),
        compiler_params=pltpu.CompilerParams(dimension_semantics=("arbitrary",),
                                             vmem_limit_bytes=VMEM_LIMIT),
        name="dispatch",
    )(n_used, padfrom, padlen, pos3, h2)


def _expert_kernel(nused_ref, nruns_ref, first_ref, run_ref, rune_ref, x_hbm, wg_hbm, wu_hbm, wd_hbm, y_hbm,
                   wg_f32, wu_f32, wd_f32, wg_b, wu_b, wd_b, wsems, xbuf, ybuf, xsems, ysems):
    i = pl.program_id(0)
    n_used = nused_ref[0]
    n_runs = nruns_ref[0]
    d = wg_b.shape[0]
    pitch = d // LANES
    bm = xbuf.shape[1] // pitch
    n_slots = wg_f32.shape[0]

    def weight_copies(e, slot):
        return [pltpu.make_async_copy(src.at[e], dst.at[slot], wsems.at[slot])
                for src, dst in ((wg_hbm, wg_f32), (wu_hbm, wu_f32), (wd_hbm, wd_f32))]

    def start_run(r):
        @pl.when(r < n_runs)
        def _():
            for prio, cp in zip((1, 1, 0), weight_copies(rune_ref[r], r % n_slots)):
                cp.start(priority=prio)

    @pl.when(i == 0)
    def _():
        for r in range(n_slots - 1):
            start_run(r)

    @pl.when((i < n_used) & (first_ref[i] == 1))
    def _():
        run = run_ref[i]
        slot = run % n_slots
        for cp in weight_copies(0, slot):
            cp.wait()
        start_run(run + n_slots - 1)

        wg_b[...] = wg_f32[slot].astype(BF16)
        wu_b[...] = wu_f32[slot].astype(BF16)
        wd_b[...] = wd_f32[slot].astype(BF16)

    blk_rows = bm * pitch
    x_slots, y_slots = xbuf.shape[0], ybuf.shape[0]

    def block(ref, b):
        return ref.at[pl.ds(pl.multiple_of(b * blk_rows, blk_rows), blk_rows), :]

    def x_copy(b):
        return pltpu.make_async_copy(block(x_hbm, b), xbuf.at[b % x_slots], xsems.at[b % x_slots])

    def y_copy(b):
        return pltpu.make_async_copy(ybuf.at[b % y_slots], block(y_hbm, b), ysems.at[b % y_slots])

    @pl.when(i == 0)
    def _():
        for b in range(x_slots - 1):
            @pl.when(b < n_used)
            def _(b=b):
                x_copy(b).start()

    @pl.when(i < n_used)
    def _():
        x_copy(i).wait()

        @pl.when(i + x_slots - 1 < n_used)
        def _():
            x_copy(i + x_slots - 1).start()

        @pl.when(i >= y_slots)
        def _():
            y_copy(i - y_slots).wait()

        xb = _load_token_major(xbuf.at[i % x_slots], bm, d).astype(BF16)
        gate = _dot(xb, wg_b[...])
        up = _dot(xb, wu_b[...])
        hid = (_silu(gate) * up).astype(BF16)
        _store_token_major(ybuf.at[i % y_slots], _dot(hid, wd_b[...]))
        y_copy(i).start()

    @pl.when(i == n_used - 1)
    def _():
        for back in range(y_slots):
            @pl.when(i - back >= 0)
            def _(back=back):
                y_copy(i - back).wait()


def _expert_call(xs, block_e, n_used, w_gate, w_up, w_down):
    bm = EXPERT_BLOCK
    d, de = w_gate.shape[1], w_gate.shape[2]
    pitch = d // LANES
    n_blocks = xs.shape[0] // (bm * pitch)
    blk = jnp.arange(n_blocks, dtype=jnp.int32)
    used = blk < n_used[0]
    first = jnp.concatenate([jnp.ones((1,), jnp.int32), (block_e[1:] != block_e[:-1]).astype(jnp.int32)])
    first = jnp.where(used, first, 0)
    run = (jnp.cumsum(first) - 1).astype(jnp.int32)
    n_runs = jnp.sum(first).astype(jnp.int32).reshape(1)
    eids = jnp.arange(N_EXPERTS, dtype=jnp.int32)
    run_of_e = jnp.where((first[:, None] == 1) & (block_e[:, None] == eids[None, :]), run[:, None], -1).max(axis=0)
    run_e = jnp.sum(jnp.where(run_of_e[None, :] == eids[:, None], eids[None, :], 0), axis=1).astype(jnp.int32)
    grid_spec = pltpu.PrefetchScalarGridSpec(
        num_scalar_prefetch=5,
        grid=(n_blocks,),
        in_specs=[pl.BlockSpec(memory_space=pl.ANY)] * 4,
        out_specs=pl.BlockSpec(memory_space=pl.ANY),
        scratch_shapes=[pltpu.VMEM((WEIGHT_SLOTS, d, de), F32), pltpu.VMEM((WEIGHT_SLOTS, d, de), F32),
                        pltpu.VMEM((WEIGHT_SLOTS, de, d), F32),
                        pltpu.VMEM((d, de), BF16), pltpu.VMEM((d, de), BF16), pltpu.VMEM((de, d), BF16),
                        pltpu.SemaphoreType.DMA((WEIGHT_SLOTS,)),
                        pltpu.VMEM((ROW_SLOTS_IN, bm * pitch, LANES), F32),
                        pltpu.VMEM((ROW_SLOTS_OUT, bm * pitch, LANES), F32),
                        pltpu.SemaphoreType.DMA((ROW_SLOTS_IN,)), pltpu.SemaphoreType.DMA((ROW_SLOTS_OUT,))],
    )
    return pl.pallas_call(
        _expert_kernel,
        grid_spec=grid_spec,
        out_shape=jax.ShapeDtypeStruct((n_blocks * bm * pitch, LANES), F32),
        input_output_aliases={5: 0},
        compiler_params=pltpu.CompilerParams(dimension_semantics=("arbitrary",),
                                             vmem_limit_bytes=VMEM_LIMIT),
        name="experts",
    )(n_used, n_runs, first, run, run_e, xs, w_gate, w_up, w_down)


def _combine_kernel(pos_cur_ref, pos_nxt_ref, x_ref, info_ref, mod_ref, nw_ref, y_hbm, o_ref, ybuf_a, ybuf_b, sems):
    tt, d = x_ref.shape[1], x_ref.shape[2]
    pitch = d // LANES
    step = pl.program_id(0) * pl.num_programs(1) + pl.program_id(1)
    n_steps = pl.num_programs(0) * pl.num_programs(1)
    n_chunks = tt // COMBINE_ROWS

    def row_copy(pos_ref, buf, sem, k, r):
        p = pos_ref[0, 0, k * tt + r]
        return pltpu.make_async_copy(y_hbm.at[_tile_rows(p, pitch), :], buf.at[_tile_rows(k * tt + r, pitch), :], sem)

    def wait_tile(buf, sem):
        pltpu.make_async_copy(y_hbm.at[pl.ds(0, TOP_K * tt * pitch), :], buf, sem).wait()

    @pl.when(step == 0)
    def _():
        def body(r, carry):
            for k in range(TOP_K):
                row_copy(pos_cur_ref, ybuf_a, sems.at[0], k, r).start(priority=k)
            return carry
        lax.fori_loop(0, tt, body, 0, unroll=4)

    gate2 = mod_ref[0][5:6]

    def run_tile(cur, cur_sem, nxt, nxt_sem):
        wait_tile(cur, cur_sem)

        def chunk(c, carry):
            r0 = c * COMBINE_ROWS
            for r in range(COMBINE_ROWS):
                for k in range(TOP_K):
                    row_copy(pos_nxt_ref, nxt, nxt_sem, k, r0 + r).start(priority=k)
            rows = pl.ds(r0, COMBINE_ROWS)
            info = info_ref[0, rows, :]
            moe = (_load_token_major(cur, COMBINE_ROWS, d, r0 * pitch) * info[:, 2:3]
                   + _load_token_major(cur, COMBINE_ROWS, d, (tt + r0) * pitch) * info[:, 3:4])
            y = moe * lax.rsqrt(jnp.mean(moe * moe, axis=-1, keepdims=True) + EPS) * nw_ref[...]
            o_ref[0, rows, :] = x_ref[0, rows, :] + gate2 * y
            return carry
        for c in range(n_chunks):
            chunk(c, 0)

        @pl.when(step == n_steps - 1)
        def _():
            wait_tile(nxt, nxt_sem)

    @pl.when(step % 2 == 0)
    def _():
        run_tile(ybuf_a, sems.at[0], ybuf_b, sems.at[1])

    @pl.when(step % 2 == 1)
    def _():
        run_tile(ybuf_b, sems.at[1], ybuf_a, sems.at[0])


def _combine_call(x1, info, mod3, norm_w, y, pos3):
    bsz, seq, d = x1.shape
    tt = COMBINE_TILE
    nj = seq // tt
    n_tiles = bsz * nj
    pos_blk = lambda f: pl.BlockSpec((1, 1, TOP_K * tt), f, memory_space=pltpu.SMEM)
    return pl.pallas_call(
        _combine_kernel,
        grid=(bsz, nj),
        in_specs=[pos_blk(lambda b, j: (b * nj + j, 0, 0)),
                  pos_blk(lambda b, j: (jnp.minimum(b * nj + j + 1, n_tiles - 1), 0, 0)),
                  pl.BlockSpec((1, tt, d), lambda b, j: (b, j, 0)),
                  pl.BlockSpec((1, tt, LANES), lambda b, j: (b, j, 0)),
                  pl.BlockSpec((1,) + mod3.shape[1:], lambda b, j: (b, 0, 0)),
                  pl.BlockSpec((1, d), lambda b, j: (0, 0)),
                  pl.BlockSpec(memory_space=pl.ANY)],
        out_specs=pl.BlockSpec((1, tt, d), lambda b, j: (b, j, 0)),
        out_shape=jax.ShapeDtypeStruct((bsz, seq, d), F32),
        scratch_shapes=[pltpu.VMEM((TOP_K * tt * (d // LANES), LANES), F32),
                        pltpu.VMEM((TOP_K * tt * (d // LANES), LANES), F32), pltpu.SemaphoreType.DMA((2,))],
        compiler_params=pltpu.CompilerParams(dimension_semantics=("arbitrary", "arbitrary"),
                                             vmem_limit_bytes=VMEM_LIMIT),
        name="combine",
    )(pos3, pos3, x1, info, mod3, norm_w.reshape(1, d), y)


def _layer(x, mod, norm_pre_mix, norm_post_mix, w_in, dn_conv_w, dn_a_log, dn_dt_bias, dn_norm_w,
           cf_pw1_b, cf_dw_w, cf_dw_b, cf_ln_w, cf_ln_b, w_out, norm_pre_ffn, norm_post_ffn,
           w_router_group, b_router_group, w_router_expert, b_router_expert, w_gate, w_up, w_down):
    bsz, seq, d = x.shape
    t = bsz * seq
    mod3 = mod.reshape(bsz, -1, d)
    x1 = _mixer_call(x, mod3, norm_pre_mix, norm_post_mix, w_in, dn_conv_w, dn_a_log, dn_dt_bias, dn_norm_w,
                     cf_pw1_b, cf_dw_w, cf_dw_b, cf_ln_w, cf_ln_b, w_out)
    h2, info, info_t, cnt = _router_call(x1, mod3, norm_pre_ffn, w_router_group, b_router_group,
                                         w_router_expert, b_router_expert)

    bm = EXPERT_BLOCK
    expert_id = info_t[0:TOP_K].astype(jnp.int32)
    rank = info_t[4:4 + TOP_K].astype(jnp.int32)
    counts = cnt[0, :N_EXPERTS].astype(jnp.int32)
    padded = (counts + bm - 1) // bm * bm
    pend = jnp.cumsum(padded)
    pstart = pend - padded
    eids = jnp.arange(N_EXPERTS, dtype=jnp.int32)
    onehot = expert_id[:, None, :] == eids[None, :, None]
    pos = jnp.sum(jnp.where(onehot, pstart[None, :, None], 0), axis=1) + rank

    def tiled(tt):
        return pos.reshape(TOP_K, t // tt, tt).transpose(1, 0, 2).reshape(t // tt, 1, TOP_K * tt)
    n_blocks = -(-(t * TOP_K) // bm) + N_EXPERTS
    block_start = jnp.arange(n_blocks, dtype=jnp.int32) * bm
    block_e = jnp.minimum(jnp.sum(pend[None, :] <= block_start[:, None], axis=1), N_EXPERTS - 1).astype(jnp.int32)
    n_used = (pend[-1] // bm).astype(jnp.int32).reshape(1)

    xs = _dispatch_call(h2, tiled(DISPATCH_TILE), n_used, (pstart + counts).astype(jnp.int32),
                        (padded - counts).astype(jnp.int32), n_blocks * bm, d)
    y = _expert_call(xs, block_e, n_used, w_gate, w_up, w_down)
    return _combine_call(x1, info, mod3, norm_post_ffn, y, tiled(COMBINE_TILE))


def kernel(x, c, w_ada, b_ada, norm_pre_mix, norm_post_mix, w_in, dn_conv_w, dn_a_log, dn_dt_bias, dn_norm_w,
           cf_pw1_b, cf_dw_w, cf_dw_b, cf_ln_w, cf_ln_b, w_out, norm_pre_ffn, norm_post_ffn,
           w_router_group, b_router_group, w_router_expert, b_router_expert, w_gate, w_up, w_down):
    depth = w_ada.shape[0]
    for l in range(depth):
        mod = _ada_call(c, w_ada[l], b_ada[l])
        x = _layer(x, mod, norm_pre_mix[l], norm_post_mix[l], w_in[l], dn_conv_w[l], dn_a_log[l],
                   dn_dt_bias[l], dn_norm_w[l], cf_pw1_b[l], cf_dw_w[l], cf_dw_b[l], cf_ln_w[l], cf_ln_b[l],
                   w_out[l], norm_pre_ffn[l], norm_post_ffn[l], w_router_group[l], b_router_group[l],
                   w_router_expert[l], b_router_expert[l], w_gate[l], w_up[l], w_down[l])
    return x
```

```python
import functools

import jax
import jax.numpy as jnp
from jax import lax
from jax.experimental import pallas as pl
from jax.experimental.pallas import tpu as pltpu

F32 = jnp.float32
BF16 = jnp.bfloat16
EPS = 1e-6

DN_HEADS = 4
HEAD_DIM = 128
DN_WIDTH = DN_HEADS * HEAD_DIM
DN_CONV = 4
DN_CHUNK = 64
CF_KERNEL = 31
N_GROUPS = 8
EXPERTS_PER_GROUP = 8
N_EXPERTS = N_GROUPS * EXPERTS_PER_GROUP
TOP_K = 2

LANES = 128
SUBLANES = 8
SEQ_TILE = 256
CONV_ROWS = 32
MIXER_PAIR = 2
MIXER_SKEW = 4
ROUTER_TILE = 512
EXPERT_BLOCK = 256
COMBINE_TILE = 512
COMBINE_SLOTS = 3
COMBINE_ROWS = 32
WEIGHT_SLOTS = 3
ROW_SLOTS_IN = 3
ROW_SLOTS_OUT = 2
DISPATCH_TILE = 1024
DISPATCH_SLOTS = 3
VMEM_LIMIT = 56 * 1024 * 1024


def _dot(a, b):
    return jnp.dot(a, b, preferred_element_type=F32)


def _dot_nt(a, b):
    return lax.dot_general(a, b, (((1,), (1,)), ((), ())), preferred_element_type=F32)


def _dot_tn(a, b):
    return lax.dot_general(a, b, (((0,), (0,)), ((), ())), preferred_element_type=F32)


def _split3(x):
    hi = x.astype(BF16)
    r1 = x - hi.astype(F32)
    mid = r1.astype(BF16)
    lo = (r1 - mid.astype(F32)).astype(BF16)
    return hi, mid, lo


def _silu(x):
    return x * jax.nn.sigmoid(x)


def _softplus(x):
    return jnp.maximum(x, 0.0) + jnp.log1p(jnp.exp(-jnp.abs(x)))


def _store_token_major(ref, val, base=0):
    n, d = val.shape
    pitch = d // LANES
    for j in range(pitch):
        ref[pl.ds(base + j, n, stride=pitch), :] = val[:, j * LANES:(j + 1) * LANES]


def _load_token_major(ref, n, d, base=0):
    pitch = d // LANES
    return jnp.concatenate([ref[pl.ds(base + j, n, stride=pitch), :] for j in range(pitch)], axis=1)


def _ada_kernel(c_ref, w_ref, b_ref, o_ref):
    c = c_ref[...]
    ca = _silu(c)
    c_hi, c_lo, _ = _split3(ca)
    w_hi, w_lo, _ = _split3(w_ref[...])
    o_ref[...] = _dot(c_hi, w_hi) + (_dot(c_hi, w_lo) + _dot(c_lo, w_hi)) + b_ref[...]


def _ada_call(c, w, b):
    bsz, d = c.shape
    n = w.shape[1]
    tn = 512
    return pl.pallas_call(
        _ada_kernel,
        grid=(n // tn,),
        in_specs=[pl.BlockSpec((bsz, d), lambda i: (0, 0)),
                  pl.BlockSpec((d, tn), lambda i: (0, i)),
                  pl.BlockSpec((1, tn), lambda i: (0, i))],
        out_specs=pl.BlockSpec((bsz, tn), lambda i: (0, i)),
        out_shape=jax.ShapeDtypeStruct((bsz, n), F32),
        compiler_params=pltpu.CompilerParams(dimension_semantics=("arbitrary",),
                                             vmem_limit_bytes=VMEM_LIMIT),
        name="ada",
    )(c, w, b.reshape(1, n))


def _time_perm(ts, transpose):
    ri = lax.broadcasted_iota(jnp.int32, (ts, ts), 0)
    ci = lax.broadcasted_iota(jnp.int32, (ts, ts), 1)
    strided, natural = (ci, ri) if transpose else (ri, ci)
    return jnp.where(natural == (ts // SUBLANES) * (strided % SUBLANES) + strided // SUBLANES, 1.0, 0.0).astype(BF16)


def _fill_conv_window(ext_ref, prev_ref, cur, n_taps):
    ts = cur.shape[0]
    lead = (n_taps - 1) * SUBLANES
    tail = cur[ts - lead:, :]
    sub = lax.broadcasted_iota(jnp.int32, tail.shape, 0) % SUBLANES
    merged = jnp.where(sub == SUBLANES - 1, prev_ref[...], tail)
    for g in range(n_taps - 1):
        rows = slice(g * SUBLANES, (g + 1) * SUBLANES)
        ext_ref[rows, :] = pltpu.roll(merged[rows, :], 1, 0)
    ext_ref[lead:lead + ts, :] = cur
    prev_ref[...] = tail


def _conv_block(ext_ref, w_ref, n_taps, r0, row_blk, c0, col_blk):
    groups = row_blk // SUBLANES
    acc = [jnp.zeros((SUBLANES, col_blk), F32) for _ in range(groups)]
    for k in range(n_taps):
        w = w_ref[k * SUBLANES:(k + 1) * SUBLANES, c0:c0 + col_blk]
        for g in range(groups):
            lo = r0 + (k + g) * SUBLANES
            acc[g] = acc[g] + w * ext_ref[lo:lo + SUBLANES, c0:c0 + col_blk]
    return jnp.concatenate(acc, axis=0)


def _mixer_kernel(x_ref, mod_ref, npre_ref, npost_ref, win_ref,
                  convw_ref, alog_ref, dtb_ref, dnw_ref, pw1b_ref, dww_ref, dwb_ref, lnw_ref, lnb_ref,
                  woutf_ref, o_ref, qkv_ext, qkv_prev, qkv_p16, qkv_act, cf_ext, cf_prev, cf_p16, state, mixed,
                  consts, wqkv_ref, wz_ref, wba_ref, wcf_ref, wout_ref):
    ts = x_ref.shape[2]
    n_chunks = ts // DN_CHUNK

    @pl.when(pl.program_id(1) == 0)
    def _():
        qkv_prev[...] = jnp.zeros(qkv_prev.shape, F32)
        cf_prev[...] = jnp.zeros(cf_prev.shape, F32)
        state[...] = jnp.zeros(state.shape, F32)

    @pl.when((pl.program_id(0) == 0) & (pl.program_id(1) == 0))
    def _():
        ri = lax.broadcasted_iota(jnp.int32, (ts, ts), 0)
        ci = lax.broadcasted_iota(jnp.int32, (ts, ts), 1)
        consts[0] = _time_perm(ts, False)
        consts[1] = _time_perm(ts, True)
        consts[2] = jnp.where((ri // DN_CHUNK == ci // DN_CHUNK) & (ci <= ri), 1.0, 0.0).astype(BF16)
        n_qkv, n_z, n_ba = wqkv_ref.shape[1], wz_ref.shape[1], 2 * DN_HEADS
        for c0 in range(0, n_qkv, DN_WIDTH):
            wqkv_ref[:, c0:c0 + DN_WIDTH] = win_ref[c0:c0 + DN_WIDTH, :].T.astype(BF16)
        wz_ref[...] = win_ref[n_qkv:n_qkv + n_z, :].T.astype(BF16)
        lane = lax.broadcasted_iota(jnp.int32, wba_ref.shape, 1)
        wba_ref[...] = jnp.where(lane < n_ba, win_ref[n_qkv + n_z:n_qkv + n_z + LANES, :].T, 0.0).astype(BF16)
        cf0 = n_qkv + n_z + n_ba
        for c0 in range(0, wcf_ref.shape[1], DN_WIDTH):
            wcf_ref[:, c0:c0 + DN_WIDTH] = win_ref[cf0 + c0:cf0 + c0 + DN_WIDTH, :].T.astype(BF16)
        wout_ref[...] = woutf_ref[...].astype(BF16)

    r64 = lax.broadcasted_iota(jnp.int32, (DN_CHUNK, DN_CHUNK), 0)
    c64 = lax.broadcasted_iota(jnp.int32, (DN_CHUNK, DN_CHUNK), 1)
    causal = c64 <= r64
    strict = c64 < r64
    eye = jnp.where(c64 == r64, 1.0, 0.0).astype(F32)
    dnw = dnw_ref[...]
    to_natural = consts[1]
    tri = consts[2]
    cells = [(hd, ch) for hd in range(DN_HEADS) for ch in range(n_chunks)]
    rows = lambda ch: slice(ch * DN_CHUNK, (ch + 1) * DN_CHUNK)

    def tile_stages(sid, qkv_ext, qkv_prev, qkv_p16, qkv_act, cf_ext, cf_prev, cf_p16, state, mixed):
        x = x_ref[sid, 0]
        mod = mod_ref[sid, 0]
        shift1, scale1, gate1 = mod[0:1], mod[1:2], mod[2:3]
        h = x * lax.rsqrt(jnp.mean(x * x, axis=-1, keepdims=True) + EPS)
        h = h * npre_ref[...] * (1.0 + scale1) + shift1
        hb = h.astype(BF16)
        yield
        hb_st = _dot(consts[0], hb).astype(BF16)

        _fill_conv_window(qkv_ext, qkv_prev, _dot(hb_st, wqkv_ref[...]), DN_CONV)
        yield
        for r0 in range(0, ts, CONV_ROWS):
            for c0 in range(0, qkv_ext.shape[1], DN_WIDTH):
                blk = _conv_block(qkv_ext, convw_ref, DN_CONV, r0, CONV_ROWS, c0, DN_WIDTH)
                qkv_p16[r0:r0 + CONV_ROWS, c0:c0 + DN_WIDTH] = _silu(blk).astype(BF16)
        yield
        qkv_act[...] = _dot(to_natural, qkv_p16[...])

        cf_pre = _dot(hb_st, wcf_ref[...]) + pw1b_ref[...]
        cfw = cf_ext.shape[1]
        _fill_conv_window(cf_ext, cf_prev, cf_pre[:, :cfw] * jax.nn.sigmoid(cf_pre[:, cfw:]), CF_KERNEL)
        cf_pending = list(range(0, ts, CONV_ROWS))
        yield

        def cf_step(n=1):
            for _ in range(min(n, len(cf_pending))):
                r0 = cf_pending.pop(0)
                cf = _conv_block(cf_ext, dww_ref, CF_KERNEL, r0, CONV_ROWS, 0, cfw) + dwb_ref[...]
                mu = jnp.mean(cf, axis=-1, keepdims=True)
                xc = cf - mu
                var = jnp.mean(xc * xc, axis=-1, keepdims=True)
                cfn = xc * lax.rsqrt(var + EPS) * lnw_ref[...] + lnb_ref[...]
                cf_p16[r0:r0 + CONV_ROWS, :] = _silu(cfn).astype(BF16)

        z = _dot(hb, wz_ref[...])
        ba = _dot(hb, wba_ref[...])
        beta_all = jax.nn.sigmoid(ba)
        g_all = -jnp.exp(alog_ref[...]) * _softplus(ba + dtb_ref[...])

        g_hi, g_mid, g_lo = _split3(g_all)
        gcum = _dot(tri, g_hi) + _dot(tri, g_mid) + _dot(tri, g_lo)
        gcum_t = gcum.T
        exp_g = jnp.exp(gcum)
        yield

        heads = []
        for hd in range(DN_HEADS):
            lo = hd * HEAD_DIM
            qh = qkv_act[:, lo:lo + HEAD_DIM]
            kh = qkv_act[:, DN_WIDTH + lo:DN_WIDTH + lo + HEAD_DIM]
            vh = qkv_act[:, 2 * DN_WIDTH + lo:2 * DN_WIDTH + lo + HEAD_DIM]
            qn = qh * lax.rsqrt(jnp.sum(qh * qh, axis=-1, keepdims=True) + EPS) * (HEAD_DIM ** -0.5)
            kn = kh * lax.rsqrt(jnp.sum(kh * kh, axis=-1, keepdims=True) + EPS)
            beta_h = beta_all[:, hd:hd + 1]
            gc_h = gcum[:, DN_HEADS + hd:DN_HEADS + hd + 1]
            eg_h = exp_g[:, DN_HEADS + hd:DN_HEADS + hd + 1]
            k_beta = kn * beta_h
            heads.append(dict(qn=qn, kn=kn, k_beta=k_beta, v_beta=vh * beta_h, kbg=k_beta * eg_h, qg=qn * eg_h,
                              gc=gc_h, zg=_silu(z[:, lo:lo + HEAD_DIM])))
        yield

        decay, kq = {}, {}
        for hd, ch in cells:
            hv, sl = heads[hd], rows(ch)
            gc_row = gcum_t[DN_HEADS + hd:DN_HEADS + hd + 1, sl]
            decay[hd, ch] = jnp.where(causal, jnp.exp(hv["gc"][sl] - gc_row), 0.0)
            lhs = jnp.concatenate([hv["k_beta"][sl], hv["qn"][sl]], axis=0).astype(BF16)
            kq[hd, ch] = _dot_nt(lhs, hv["kn"][sl].astype(BF16))
        a = {c: jnp.where(strict, kq[c][:DN_CHUNK] * decay[c], 0.0) for c in cells}
        attn = {c: (kq[c][DN_CHUNK:] * decay[c]).astype(BF16) for c in cells}
        yield

        t_inv = {c: eye - a[c] for c in cells}
        pw = {c: a[c].astype(BF16) for c in cells}
        pw = {c: _dot(pw[c], pw[c]).astype(BF16) for c in cells}
        cf_step()
        yield
        for _ in range(4):
            both = {c: _dot(jnp.concatenate([t_inv[c].astype(BF16), pw[c]], axis=0), pw[c]) for c in cells}
            t_inv = {c: t_inv[c] + both[c][:DN_CHUNK] for c in cells}
            pw = {c: both[c][DN_CHUNK:].astype(BF16) for c in cells}
            cf_step()
            yield
        t_inv = {c: t_inv[c] + _dot(t_inv[c].astype(BF16), pw[c]) for c in cells}

        sol, aw, ks, glast = {}, {}, {}, {}
        for hd, ch in cells:
            hv, sl = heads[hd], rows(ch)
            rhs = jnp.concatenate([hv["v_beta"][sl], hv["kbg"][sl]], axis=1).astype(BF16)
            sol[hd, ch] = _dot(t_inv[hd, ch].astype(BF16), rhs).astype(BF16)
        yield
        for hd, ch in cells:
            hv, sl = heads[hd], rows(ch)
            gc_col = hv["gc"][sl]
            glast[hd, ch] = gc_col[DN_CHUNK - 1:DN_CHUNK]
            k_dec = (hv["kn"][sl] * jnp.exp(glast[hd, ch] - gc_col)).astype(BF16)
            aw[hd, ch] = _dot(attn[hd, ch], sol[hd, ch])
            ks[hd, ch] = _dot_tn(k_dec, sol[hd, ch])
        yield

        s_in = {}
        s_cur = [state[hd] for hd in range(DN_HEADS)]
        for ch in range(n_chunks):
            for hd in range(DN_HEADS):
                s_in[hd, ch] = s_cur[hd].astype(BF16)
                kd_u, kd_w = ks[hd, ch][:, :HEAD_DIM], ks[hd, ch][:, HEAD_DIM:]
                s_cur[hd] = s_cur[hd] * jnp.exp(glast[hd, ch]) + kd_u - _dot(kd_w.astype(BF16), s_in[hd, ch])
            cf_step()
            yield
        for hd in range(DN_HEADS):
            state[hd] = s_cur[hd]

        for hd, ch in cells:
            hv, sl = heads[hd], rows(ch)
            lo = hd * HEAD_DIM
            q_eff = (hv["qg"][sl] - aw[hd, ch][:, HEAD_DIM:]).astype(BF16)
            o = _dot(q_eff, s_in[hd, ch]) + aw[hd, ch][:, :HEAD_DIM]
            on = o * lax.rsqrt(jnp.mean(o * o, axis=-1, keepdims=True) + EPS) * dnw * hv["zg"][sl]
            mixed[sl, lo:lo + HEAD_DIM] = on.astype(BF16)
        yield

        cf_step(len(cf_pending))
        mixed[:, DN_WIDTH:DN_WIDTH + cfw] = _dot(to_natural, cf_p16[...]).astype(BF16)
        yield

        out = _dot(mixed[...], wout_ref[...])
        yield
        y = out * lax.rsqrt(jnp.mean(out * out, axis=-1, keepdims=True) + EPS) * npost_ref[...]
        o_ref[sid, 0] = x + gate1 * y

    scratch = (qkv_ext, qkv_prev, qkv_p16, qkv_act, cf_ext, cf_prev, cf_p16, state, mixed)
    streams = [tile_stages(sid, *(ref.at[sid] for ref in scratch)) for sid in range(x_ref.shape[0])]
    live = [True] * len(streams)

    def advance(i, n=1):
        for _ in range(n):
            if live[i]:
                try:
                    next(streams[i])
                except StopIteration:
                    live[i] = False

    advance(0, MIXER_SKEW)
    while any(live):
        for i in range(len(streams)):
            advance(i)


def _mixer_call(x, mod3, npre, npost, w_in, dn_conv_w, dn_a_log, dn_dt_bias, dn_norm_w,
                cf_pw1_b, cf_dw_w, cf_dw_b, cf_ln_w, cf_ln_b, w_out):
    bsz, seq, d = x.shape
    ts = SEQ_TILE
    cfw = cf_dw_w.shape[1]
    n_qkv = 3 * DN_WIDTH
    alog = jnp.pad(dn_a_log, (DN_HEADS, LANES - 2 * DN_HEADS)).reshape(1, LANES)
    dtb = jnp.pad(dn_dt_bias, (DN_HEADS, LANES - 2 * DN_HEADS)).reshape(1, LANES)
    dww = jnp.repeat(cf_dw_w, SUBLANES, axis=0)
    convw = jnp.repeat(dn_conv_w, SUBLANES, axis=0)

    def full(a):
        mode = dict(pipeline_mode=pl.Buffered(1)) if a.size >= d * d else {}
        return pl.BlockSpec(a.shape, lambda b, j: (0,) * a.ndim, **mode)

    row = lambda a: a.reshape(1, -1)
    nb = MIXER_PAIR
    x4 = x.reshape(nb, bsz // nb, seq, d)
    mod4 = mod3.reshape((nb, bsz // nb) + mod3.shape[1:])
    operands = [x4, mod4, row(npre), row(npost), w_in.T, convw, alog, dtb, row(dn_norm_w),
                row(cf_pw1_b), dww, row(cf_dw_b), row(cf_ln_w), row(cf_ln_b), w_out]
    in_specs = [pl.BlockSpec((nb, 1, ts, d), lambda b, j: (0, b, j, 0)),
                pl.BlockSpec((nb, 1) + mod3.shape[1:], lambda b, j: (0, b, 0, 0))]
    in_specs += [full(a) for a in operands[2:]]
    out = pl.pallas_call(
        _mixer_kernel,
        grid=(bsz // nb, seq // ts),
        in_specs=in_specs,
        out_specs=pl.BlockSpec((nb, 1, ts, d), lambda b, j: (0, b, j, 0)),
        out_shape=jax.ShapeDtypeStruct((nb, bsz // nb, seq, d), F32),
        scratch_shapes=[pltpu.VMEM((nb, (DN_CONV - 1) * SUBLANES + ts, n_qkv), F32),
                        pltpu.VMEM((nb, (DN_CONV - 1) * SUBLANES, n_qkv), F32),
                        pltpu.VMEM((nb, ts, n_qkv), BF16),
                        pltpu.VMEM((nb, ts, n_qkv), F32),
                        pltpu.VMEM((nb, (CF_KERNEL - 1) * SUBLANES + ts, cfw), F32),
                        pltpu.VMEM((nb, (CF_KERNEL - 1) * SUBLANES, cfw), F32),
                        pltpu.VMEM((nb, ts, cfw), BF16),
                        pltpu.VMEM((nb, DN_HEADS, HEAD_DIM, HEAD_DIM), F32),
                        pltpu.VMEM((nb, ts, DN_WIDTH + cfw), BF16),
                        pltpu.VMEM((3, ts, ts), BF16),
                        pltpu.VMEM((d, n_qkv), BF16), pltpu.VMEM((d, DN_WIDTH), BF16), pltpu.VMEM((d, LANES), BF16),
                        pltpu.VMEM((d, 2 * cfw), BF16), pltpu.VMEM(w_out.shape, BF16)],
        compiler_params=pltpu.CompilerParams(dimension_semantics=("arbitrary", "arbitrary"),
                                             vmem_limit_bytes=VMEM_LIMIT),
        name="mixer",
    )(*operands)
    return out.reshape(bsz, seq, d)


def _router_kernel(x_ref, mod_ref, nw_ref, wr_ref, br_ref, h_ref, info_ref, infot_ref, cnt_ref, carry, strict_ref):
    tt = x_ref.shape[1]

    @pl.when((pl.program_id(0) == 0) & (pl.program_id(1) == 0))
    def _():
        carry[...] = jnp.zeros(carry.shape, F32)
        rr = lax.broadcasted_iota(jnp.int32, (tt, tt), 0)
        cc = lax.broadcasted_iota(jnp.int32, (tt, tt), 1)
        strict_ref[...] = jnp.where(cc < rr, 1.0, 0.0).astype(BF16)

    x = x_ref[0]
    mod = mod_ref[0]
    shift2, scale2 = mod[3:4], mod[4:5]
    h = x * lax.rsqrt(jnp.mean(x * x, axis=-1, keepdims=True) + EPS)
    h = h * nw_ref[...] * (1.0 + scale2) + shift2
    _store_token_major(h_ref, h)

    h_hi, h_lo, _ = _split3(h)
    w_hi, w_lo, _ = _split3(wr_ref[...])
    logits = _dot(h_hi, w_hi) + (_dot(h_hi, w_lo) + _dot(h_lo, w_hi)) + br_ref[...]

    lane = lax.broadcasted_iota(jnp.int32, (tt, LANES), 1)
    neg = -jnp.inf
    is_grp = (lane >= N_EXPERTS) & (lane < N_EXPERTS + N_GROUPS)
    gl = jnp.where(is_grp, logits, neg)
    gmax = jnp.max(gl, axis=-1, keepdims=True)
    gsum = jnp.sum(jnp.where(is_grp, jnp.exp(gl - gmax), 0.0), axis=-1, keepdims=True)
    grp_p = 1.0 / gsum
    grp_lane = jnp.min(jnp.where(is_grp & (gl == gmax), lane, LANES), axis=-1, keepdims=True)
    grp_idx = grp_lane - N_EXPERTS

    in_grp = (lane < N_EXPERTS) & (lane // EXPERTS_PER_GROUP == grp_idx)
    el = jnp.where(in_grp, logits, neg)
    m1 = jnp.max(el, axis=-1, keepdims=True)
    e1 = jnp.min(jnp.where(in_grp & (el == m1), lane, LANES), axis=-1, keepdims=True)
    el2 = jnp.where(lane == e1, neg, el)
    m2 = jnp.max(el2, axis=-1, keepdims=True)
    e2 = jnp.min(jnp.where(in_grp & (lane != e1) & (el2 == m2), lane, LANES), axis=-1, keepdims=True)
    r = jnp.exp(m2 - m1)
    w1 = grp_p / (1.0 + r)
    w2 = grp_p * r / (1.0 + r)

    hit1 = lane == e1
    hit2 = lane == e2
    onehot = jnp.where(hit1 | hit2, 1.0, 0.0)
    prefix = _dot(strict_ref[...], onehot.astype(BF16)) + carry[0:1, :]
    rank1 = jnp.sum(jnp.where(hit1, prefix, 0.0), axis=-1, keepdims=True)
    rank2 = jnp.sum(jnp.where(hit2, prefix, 0.0), axis=-1, keepdims=True)
    total = carry[0:1, :] + jnp.sum(onehot, axis=0, keepdims=True)
    carry[...] = jnp.broadcast_to(total, carry.shape)
    cnt_ref[...] = jnp.broadcast_to(total, cnt_ref.shape)

    info = jnp.where(lane == 0, e1.astype(F32), 0.0)
    info = jnp.where(lane == 1, e2.astype(F32), info)
    info = jnp.where(lane == 2, w1, info)
    info = jnp.where(lane == 3, w2, info)
    info = jnp.where(lane == 4, rank1, info)
    info = jnp.where(lane == 5, rank2, info)
    info_ref[0] = info
    infot_ref[...] = info.T[0:SUBLANES, :]


def _router_call(x1, mod3, norm_w, w_router_group, b_router_group, w_router_expert, b_router_expert):
    bsz, seq, d = x1.shape
    tt = ROUTER_TILE
    nj = seq // tt
    pitch = d // LANES
    pad = LANES - N_EXPERTS - N_GROUPS
    wr = jnp.pad(jnp.concatenate([w_router_expert, w_router_group], axis=1), ((0, 0), (0, pad)))
    br = jnp.pad(jnp.concatenate([b_router_expert, b_router_group]), (0, pad)).reshape(1, LANES)
    return pl.pallas_call(
        _router_kernel,
        grid=(bsz, seq // tt),
        in_specs=[pl.BlockSpec((1, tt, d), lambda b, j: (b, j, 0)),
                  pl.BlockSpec((1,) + mod3.shape[1:], lambda b, j: (b, 0, 0)),
                  pl.BlockSpec((1, d), lambda b, j: (0, 0)),
                  pl.BlockSpec((d, LANES), lambda b, j: (0, 0)),
                  pl.BlockSpec((1, LANES), lambda b, j: (0, 0))],
        out_specs=[pl.BlockSpec((tt * pitch, LANES), lambda b, j: (b * nj + j, 0)),
                   pl.BlockSpec((1, tt, LANES), lambda b, j: (b, j, 0)),
                   pl.BlockSpec((SUBLANES, tt), lambda b, j: (0, b * nj + j)),
                   pl.BlockSpec((8, LANES), lambda b, j: (0, 0))],
        out_shape=[jax.ShapeDtypeStruct((bsz * seq * pitch, LANES), F32),
                   jax.ShapeDtypeStruct((bsz, seq, LANES), F32),
                   jax.ShapeDtypeStruct((SUBLANES, bsz * seq), F32),
                   jax.ShapeDtypeStruct((8, LANES), F32)],
        scratch_shapes=[pltpu.VMEM((8, LANES), F32), pltpu.VMEM((tt, tt), BF16)],
        compiler_params=pltpu.CompilerParams(dimension_semantics=("arbitrary", "arbitrary"),
                                             vmem_limit_bytes=VMEM_LIMIT),
        name="router",
    )(x1, mod3, norm_w.reshape(1, d), wr, br)


def _tile_rows(idx, pitch):
    if isinstance(idx, int):
        return pl.ds(idx * pitch, pitch)
    return pl.ds(pl.multiple_of(idx * pitch, pitch), pitch)


def _dispatch_kernel(nused_ref, padfrom_ref, padlen_ref, pos_ref, h_hbm, xs_hbm, zbuf, zsem, hbuf, hsems, ssems,
                     *, pitch, bm):
    step = pl.program_id(0)
    tt = hbuf.shape[1] // pitch

    def fill_copies(action):
        def per_expert(e, carry):
            cursor = padfrom_ref[e]
            n = padlen_ref[e]
            bit = bm // 2
            while bit >= 1:
                take = (n & bit) != 0

                @pl.when(take)
                def _(cursor=cursor, bit=bit):
                    action(pltpu.make_async_copy(zbuf.at[pl.ds(0, bit * pitch), :],
                                                 xs_hbm.at[pl.ds(pl.multiple_of(cursor * pitch, pitch), bit * pitch), :],
                                                 zsem.at[0]))
                cursor = cursor + jnp.where(take, bit, 0)
                bit //= 2
            return carry

        def per_block(b, carry):
            start = pl.multiple_of(b * bm * pitch, bm * pitch)
            action(pltpu.make_async_copy(zbuf, xs_hbm.at[pl.ds(start, bm * pitch), :], zsem.at[0]))
            return carry

        lax.fori_loop(0, N_EXPERTS, per_expert, 0)
        lax.fori_loop(nused_ref[0], xs_hbm.shape[0] // (bm * pitch), per_block, 0)

    @pl.when(step == 0)
    def _():
        zbuf[...] = jnp.zeros(zbuf.shape, F32)
        fill_copies(lambda cp: cp.start())

    n_steps = pl.num_programs(0)
    slots = hbuf.shape[0]
    tile_rows = tt * pitch

    def tile_in(s):
        src = h_hbm.at[pl.ds(pl.multiple_of(s * tile_rows, tile_rows), tile_rows), :]
        return pltpu.make_async_copy(src, hbuf.at[s % slots], hsems.at[s % slots])

    def scatter_wait(s):
        for k in range(TOP_K):
            pltpu.make_async_copy(hbuf.at[s % slots], xs_hbm.at[pl.ds(0, tile_rows), :], ssems.at[s % slots]).wait()

    @pl.when(step == 0)
    def _():
        for s in range(slots - 1):
            @pl.when(s < n_steps)
            def _(s=s):
                tile_in(s).start()

    tile_in(step).wait()
    src = hbuf.at[step % slots]
    for r in range(tt):
        for k in range(TOP_K):
            p = pos_ref[0, 0, k * tt + r]
            pltpu.make_async_copy(src.at[_tile_rows(r, pitch), :], xs_hbm.at[_tile_rows(p, pitch), :],
                                  ssems.at[step % slots]).start(priority=k)

    @pl.when(step >= 1)
    def _():
        scatter_wait(step - 1)

    @pl.when(step + slots - 1 < n_steps)
    def _():
        tile_in(step + slots - 1).start()

    @pl.when(step == n_steps - 1)
    def _():
        scatter_wait(step)
        fill_copies(lambda cp: cp.wait())


def _dispatch_call(h2, pos3, n_used, padfrom, padlen, n_rows, d):
    pitch = d // LANES
    bm = EXPERT_BLOCK
    n_tiles, _, two_tt = pos3.shape
    tt = two_tt // TOP_K
    grid_spec = pltpu.PrefetchScalarGridSpec(
        num_scalar_prefetch=3,
        grid=(n_tiles,),
        in_specs=[pl.BlockSpec((1, 1, two_tt), lambda s, *_: (s, 0, 0), memory_space=pltpu.SMEM),
                  pl.BlockSpec(memory_space=pl.ANY)],
        out_specs=pl.BlockSpec(memory_space=pl.ANY),
        scratch_shapes=[pltpu.VMEM((bm * pitch, LANES), F32), pltpu.SemaphoreType.DMA((1,)),
                        pltpu.VMEM((DISPATCH_SLOTS, tt * pitch, LANES), F32),
                        pltpu.SemaphoreType.DMA((DISPATCH_SLOTS,)), pltpu.SemaphoreType.DMA((DISPATCH_SLOTS,))],
    )
    return pl.pallas_call(
        functools.partial(_dispatch_kernel, pitch=pitch, bm=bm),
        grid_spec=grid_spec,
        out_shape=jax.ShapeDtypeStruct((n_rows * pitch, LANES), F32),
        compiler_params=pltpu.CompilerParams(dimension_semantics=("arbitrary",),
                                             vmem_limit_bytes=VMEM_LIMIT),
        name="dispatch",
    )(n_used, padfrom, padlen, pos3, h2)


def _expert_kernel(nused_ref, nruns_ref, first_ref, run_ref, rune_ref, x_hbm, wg_hbm, wu_hbm, wd_hbm, y_hbm,
                   wg_f32, wu_f32, wd_f32, wg_b, wu_b, wd_b, wsems, xbuf, ybuf, xsems, ysems):
    i = pl.program_id(0)
    n_used = nused_ref[0]
    n_runs = nruns_ref[0]
    d = wg_b.shape[0]
    pitch = d // LANES
    bm = xbuf.shape[1] // pitch
    n_slots = wg_f32.shape[0]

    def weight_copies(e, slot):
        return [pltpu.make_async_copy(src.at[e], dst.at[slot], wsems.at[slot])
                for src, dst in ((wg_hbm, wg_f32), (wu_hbm, wu_f32), (wd_hbm, wd_f32))]

    def start_run(r):
        @pl.when(r < n_runs)
        def _():
            for prio, cp in zip((1, 1, 0), weight_copies(rune_ref[r], r % n_slots)):
                cp.start(priority=prio)

    @pl.when(i == 0)
    def _():
        for r in range(n_slots - 1):
            start_run(r)

    @pl.when((i < n_used) & (first_ref[i] == 1))
    def _():
        run = run_ref[i]
        slot = run % n_slots
        for cp in weight_copies(0, slot):
            cp.wait()
        start_run(run + n_slots - 1)

        wg_b[...] = wg_f32[slot].astype(BF16)
        wu_b[...] = wu_f32[slot].astype(BF16)
        wd_b[...] = wd_f32[slot].astype(BF16)

    blk_rows = bm * pitch
    x_slots, y_slots = xbuf.shape[0], ybuf.shape[0]

    def block(ref, b):
        return ref.at[pl.ds(pl.multiple_of(b * blk_rows, blk_rows), blk_rows), :]

    def x_copy(b):
        return pltpu.make_async_copy(block(x_hbm, b), xbuf.at[b % x_slots], xsems.at[b % x_slots])

    def y_copy(b):
        return pltpu.make_async_copy(ybuf.at[b % y_slots], block(y_hbm, b), ysems.at[b % y_slots])

    @pl.when(i == 0)
    def _():
        for b in range(x_slots - 1):
            @pl.when(b < n_used)
            def _(b=b):
                x_copy(b).start()

    @pl.when(i < n_used)
    def _():
        x_copy(i).wait()

        @pl.when(i + x_slots - 1 < n_used)
        def _():
            x_copy(i + x_slots - 1).start()

        @pl.when(i >= y_slots)
        def _():
            y_copy(i - y_slots).wait()

        xb = _load_token_major(xbuf.at[i % x_slots], bm, d).astype(BF16)
        gate = _dot(xb, wg_b[...])
        up = _dot(xb, wu_b[...])
        hid = (_silu(gate) * up).astype(BF16)
        _store_token_major(ybuf.at[i % y_slots], _dot(hid, wd_b[...]))
        y_copy(i).start()

    @pl.when(i == n_used - 1)
    def _():
        for back in range(y_slots):
            @pl.when(i - back >= 0)
            def _(back=back):
                y_copy(i - back).wait()


def _expert_call(xs, block_e, n_used, w_gate, w_up, w_down):
    bm = EXPERT_BLOCK
    d, de = w_gate.shape[1], w_gate.shape[2]
    pitch = d // LANES
    n_blocks = xs.shape[0] // (bm * pitch)
    blk = jnp.arange(n_blocks, dtype=jnp.int32)
    used = blk < n_used[0]
    first = jnp.concatenate([jnp.ones((1,), jnp.int32), (block_e[1:] != block_e[:-1]).astype(jnp.int32)])
    first = jnp.where(used, first, 0)
    run = (jnp.cumsum(first) - 1).astype(jnp.int32)
    n_runs = jnp.sum(first).astype(jnp.int32).reshape(1)
    eids = jnp.arange(N_EXPERTS, dtype=jnp.int32)
    run_of_e = jnp.where((first[:, None] == 1) & (block_e[:, None] == eids[None, :]), run[:, None], -1).max(axis=0)
    run_e = jnp.sum(jnp.where(run_of_e[None, :] == eids[:, None], eids[None, :], 0), axis=1).astype(jnp.int32)
    grid_spec = pltpu.PrefetchScalarGridSpec(
        num_scalar_prefetch=5,
        grid=(n_blocks,),
        in_specs=[pl.BlockSpec(memory_space=pl.ANY)] * 4,
        out_specs=pl.BlockSpec(memory_space=pl.ANY),
        scratch_shapes=[pltpu.VMEM((WEIGHT_SLOTS, d, de), F32), pltpu.VMEM((WEIGHT_SLOTS, d, de), F32),
                        pltpu.VMEM((WEIGHT_SLOTS, de, d), F32),
                        pltpu.VMEM((d, de), BF16), pltpu.VMEM((d, de), BF16), pltpu.VMEM((de, d), BF16),
                        pltpu.SemaphoreType.DMA((WEIGHT_SLOTS,)),
                        pltpu.VMEM((ROW_SLOTS_IN, bm * pitch, LANES), F32),
                        pltpu.VMEM((ROW_SLOTS_OUT, bm * pitch, LANES), F32),
                        pltpu.SemaphoreType.DMA((ROW_SLOTS_IN,)), pltpu.SemaphoreType.DMA((ROW_SLOTS_OUT,))],
    )
    return pl.pallas_call(
        _expert_kernel,
        grid_spec=grid_spec,
        out_shape=jax.ShapeDtypeStruct((n_blocks * bm * pitch, LANES), F32),
        input_output_aliases={5: 0},
        compiler_params=pltpu.CompilerParams(dimension_semantics=("arbitrary",),
                                             vmem_limit_bytes=VMEM_LIMIT),
        name="experts",
    )(n_used, n_runs, first, run, run_e, xs, w_gate, w_up, w_down)


def _combine_kernel(pos_pre_ref, pos_ahead_ref, x_ref, info_ref, mod_ref, nw_ref, y_hbm, o_ref, *bufs_and_sems):
    bufs, sems = bufs_and_sems[:-1], bufs_and_sems[-1]
    n_bufs = len(bufs)
    ahead = n_bufs - 1
    tt, d = x_ref.shape[1], x_ref.shape[2]
    pitch = d // LANES
    step = pl.program_id(0) * pl.num_programs(1) + pl.program_id(1)
    n_steps = pl.num_programs(0) * pl.num_programs(1)
    n_chunks = tt // COMBINE_ROWS

    def row_copy(pos_ref, tile, slot, k, r):
        p = pos_ref[tile, 0, k * tt + r]
        return pltpu.make_async_copy(y_hbm.at[_tile_rows(p, pitch), :],
                                     bufs[slot].at[_tile_rows(k * tt + r, pitch), :], sems.at[slot])

    def wait_tile(slot):
        pltpu.make_async_copy(y_hbm.at[pl.ds(0, TOP_K * tt * pitch), :], bufs[slot], sems.at[slot]).wait()

    @pl.when(step == 0)
    def _():
        for t in range(ahead):
            def body(r, carry, t=t):
                for k in range(TOP_K):
                    row_copy(pos_pre_ref, t, t, k, r).start(priority=k)
                return carry
            lax.fori_loop(0, tt, body, 0, unroll=4)

    gate2 = mod_ref[0][5:6]

    def run_tile(cur, nxt):
        wait_tile(cur)
        for c in range(n_chunks):
            r0 = c * COMBINE_ROWS
            for r in range(COMBINE_ROWS):
                for k in range(TOP_K):
                    row_copy(pos_ahead_ref, 0, nxt, k, r0 + r).start(priority=k)
            rows = pl.ds(r0, COMBINE_ROWS)
            info = info_ref[0, rows, :]
            moe = (_load_token_major(bufs[cur], COMBINE_ROWS, d, r0 * pitch) * info[:, 2:3]
                   + _load_token_major(bufs[cur], COMBINE_ROWS, d, (tt + r0) * pitch) * info[:, 3:4])
            y = moe * lax.rsqrt(jnp.mean(moe * moe, axis=-1, keepdims=True) + EPS) * nw_ref[...]
            o_ref[0, rows, :] = x_ref[0, rows, :] + gate2 * y

        @pl.when(step == n_steps - 1)
        def _():
            for other in range(n_bufs):
                if other != cur:
                    wait_tile(other)

    for parity in range(n_bufs):
        @pl.when(step % n_bufs == parity)
        def _(parity=parity):
            run_tile(parity, (parity + ahead) % n_bufs)


def _combine_call(x1, info, mod3, norm_w, y, pos3):
    bsz, seq, d = x1.shape
    tt = COMBINE_TILE
    nj = seq // tt
    n_tiles = bsz * nj
    ahead = COMBINE_SLOTS - 1
    pos_blk = lambda n, f: pl.BlockSpec((n, 1, TOP_K * tt), f, memory_space=pltpu.SMEM)
    tile_buf = pltpu.VMEM((TOP_K * tt * (d // LANES), LANES), F32)
    return pl.pallas_call(
        _combine_kernel,
        grid=(bsz, nj),
        in_specs=[pos_blk(ahead, lambda b, j: (0, 0, 0)),
                  pos_blk(1, lambda b, j: (jnp.minimum(b * nj + j + ahead, n_tiles - 1), 0, 0)),
                  pl.BlockSpec((1, tt, d), lambda b, j: (b, j, 0)),
                  pl.BlockSpec((1, tt, LANES), lambda b, j: (b, j, 0)),
                  pl.BlockSpec((1,) + mod3.shape[1:], lambda b, j: (b, 0, 0)),
                  pl.BlockSpec((1, d), lambda b, j: (0, 0)),
                  pl.BlockSpec(memory_space=pl.ANY)],
        out_specs=pl.BlockSpec((1, tt, d), lambda b, j: (b, j, 0)),
        out_shape=jax.ShapeDtypeStruct((bsz, seq, d), F32),
        scratch_shapes=[tile_buf] * COMBINE_SLOTS + [pltpu.SemaphoreType.DMA((COMBINE_SLOTS,))],
        compiler_params=pltpu.CompilerParams(dimension_semantics=("arbitrary", "arbitrary"),
                                             vmem_limit_bytes=VMEM_LIMIT),
        name="combine",
    )(pos3, pos3, x1, info, mod3, norm_w.reshape(1, d), y)


def _layer(x, mod, norm_pre_mix, norm_post_mix, w_in, dn_conv_w, dn_a_log, dn_dt_bias, dn_norm_w,
           cf_pw1_b, cf_dw_w, cf_dw_b, cf_ln_w, cf_ln_b, w_out, norm_pre_ffn, norm_post_ffn,
           w_router_group, b_router_group, w_router_expert, b_router_expert, w_gate, w_up, w_down):
    bsz, seq, d = x.shape
    t = bsz * seq
    mod3 = mod.reshape(bsz, -1, d)
    x1 = _mixer_call(x, mod3, norm_pre_mix, norm_post_mix, w_in, dn_conv_w, dn_a_log, dn_dt_bias, dn_norm_w,
                     cf_pw1_b, cf_dw_w, cf_dw_b, cf_ln_w, cf_ln_b, w_out)
    h2, info, info_t, cnt = _router_call(x1, mod3, norm_pre_ffn, w_router_group, b_router_group,
                                         w_router_expert, b_router_expert)

    bm = EXPERT_BLOCK
    expert_id = info_t[0:TOP_K].astype(jnp.int32)
    rank = info_t[4:4 + TOP_K].astype(jnp.int32)
    counts = cnt[0, :N_EXPERTS].astype(jnp.int32)
    padded = (counts + bm - 1) // bm * bm
    pend = jnp.cumsum(padded)
    pstart = pend - padded
    eids = jnp.arange(N_EXPERTS, dtype=jnp.int32)
    onehot = expert_id[:, None, :] == eids[None, :, None]
    pos = jnp.sum(jnp.where(onehot, pstart[None, :, None], 0), axis=1) + rank

    def tiled(tt):
        return pos.reshape(TOP_K, t // tt, tt).transpose(1, 0, 2).reshape(t // tt, 1, TOP_K * tt)
    n_blocks = -(-(t * TOP_K) // bm) + N_EXPERTS
    block_start = jnp.arange(n_blocks, dtype=jnp.int32) * bm
    block_e = jnp.minimum(jnp.sum(pend[None, :] <= block_start[:, None], axis=1), N_EXPERTS - 1).astype(jnp.int32)
    n_used = (pend[-1] // bm).astype(jnp.int32).reshape(1)

    xs = _dispatch_call(h2, tiled(DISPATCH_TILE), n_used, (pstart + counts).astype(jnp.int32),
                        (padded - counts).astype(jnp.int32), n_blocks * bm, d)
    y = _expert_call(xs, block_e, n_used, w_gate, w_up, w_down)
    return _combine_call(x1, info, mod3, norm_post_ffn, y, tiled(COMBINE_TILE))


def kernel(x, c, w_ada, b_ada, norm_pre_mix, norm_post_mix, w_in, dn_conv_w, dn_a_log, dn_dt_bias, dn_norm_w,
           cf_pw1_b, cf_dw_w, cf_dw_b, cf_ln_w, cf_ln_b, w_out, norm_pre_ffn, norm_post_ffn,
           w_router_group, b_router_group, w_router_expert, b_router_expert, w_gate, w_up, w_down):
    depth = w_ada.shape[0]
    for l in range(depth):
        mod = _ada_call(c, w_ada[l], b_ada[l])
        x = _layer(x, mod, norm_pre_mix[l], norm_post_mix[l], w_in[l], dn_conv_w[l], dn_a_log[l],
                   dn_dt_bias[l], dn_norm_w[l], cf_pw1_b[l], cf_dw_w[l], cf_dw_b[l], cf_ln_w[l], cf_ln_b[l],
                   w_out[l], norm_pre_ffn[l], norm_post_ffn[l], w_router_group[l], b_router_group[l],
                   w_router_expert[l], b_router_expert[l], w_gate[l], w_up[l], w_down[l])
    return x
```

```python
import functools

import jax
import jax.numpy as jnp
from jax import lax
from jax.experimental import pallas as pl
from jax.experimental.pallas import tpu as pltpu

F32 = jnp.float32
BF16 = jnp.bfloat16
EPS = 1e-6

DN_HEADS = 4
HEAD_DIM = 128
DN_WIDTH = DN_HEADS * HEAD_DIM
DN_CONV = 4
DN_CHUNK = 64
CF_KERNEL = 31
N_GROUPS = 8
EXPERTS_PER_GROUP = 8
N_EXPERTS = N_GROUPS * EXPERTS_PER_GROUP
TOP_K = 2

LANES = 128
SUBLANES = 8
SEQ_TILE = 256
CONV_ROWS = 32
MIXER_PAIR = 2
MIXER_SKEW = 4
ROUTER_TILE = 512
EXPERT_BLOCK = 256
COMBINE_TILE = 512
COMBINE_SLOTS = 3
COMBINE_ROWS = 32
WEIGHT_SLOTS = 3
ROW_SLOTS_IN = 3
ROW_SLOTS_OUT = 2
DISPATCH_TILE = 1024
DISPATCH_SLOTS = 3
VMEM_LIMIT = 56 * 1024 * 1024


def _dot(a, b):
    return jnp.dot(a, b, preferred_element_type=F32)


def _dot_nt(a, b):
    return lax.dot_general(a, b, (((1,), (1,)), ((), ())), preferred_element_type=F32)


def _dot_tn(a, b):
    return lax.dot_general(a, b, (((0,), (0,)), ((), ())), preferred_element_type=F32)


def _split3(x):
    hi = x.astype(BF16)
    r1 = x - hi.astype(F32)
    mid = r1.astype(BF16)
    lo = (r1 - mid.astype(F32)).astype(BF16)
    return hi, mid, lo


def _silu(x):
    return x * jax.nn.sigmoid(x)


def _softplus(x):
    return jnp.maximum(x, 0.0) + jnp.log1p(jnp.exp(-jnp.abs(x)))


def _store_token_major(ref, val, base=0):
    n, d = val.shape
    pitch = d // LANES
    for j in range(pitch):
        ref[pl.ds(base + j, n, stride=pitch), :] = val[:, j * LANES:(j + 1) * LANES]


def _load_token_major(ref, n, d, base=0):
    pitch = d // LANES
    return jnp.concatenate([ref[pl.ds(base + j, n, stride=pitch), :] for j in range(pitch)], axis=1)


def _ada_kernel(c_ref, w_ref, b_ref, o_ref):
    c = c_ref[...]
    ca = _silu(c)
    c_hi, c_lo, _ = _split3(ca)
    w_hi, w_lo, _ = _split3(w_ref[...])
    o_ref[...] = _dot(c_hi, w_hi) + (_dot(c_hi, w_lo) + _dot(c_lo, w_hi)) + b_ref[...]


def _ada_call(c, w, b):
    bsz, d = c.shape
    n = w.shape[1]
    tn = 512
    return pl.pallas_call(
        _ada_kernel,
        grid=(n // tn,),
        in_specs=[pl.BlockSpec((bsz, d), lambda i: (0, 0)),
                  pl.BlockSpec((d, tn), lambda i: (0, i)),
                  pl.BlockSpec((1, tn), lambda i: (0, i))],
        out_specs=pl.BlockSpec((bsz, tn), lambda i: (0, i)),
        out_shape=jax.ShapeDtypeStruct((bsz, n), F32),
        compiler_params=pltpu.CompilerParams(dimension_semantics=("arbitrary",),
                                             vmem_limit_bytes=VMEM_LIMIT),
        name="ada",
    )(c, w, b.reshape(1, n))


def _time_perm(ts, transpose):
    ri = lax.broadcasted_iota(jnp.int32, (ts, ts), 0)
    ci = lax.broadcasted_iota(jnp.int32, (ts, ts), 1)
    strided, natural = (ci, ri) if transpose else (ri, ci)
    return jnp.where(natural == (ts // SUBLANES) * (strided % SUBLANES) + strided // SUBLANES, 1.0, 0.0).astype(BF16)


def _fill_conv_window(ext_ref, prev_ref, cur, n_taps):
    ts = cur.shape[0]
    lead = (n_taps - 1) * SUBLANES
    tail = cur[ts - lead:, :]
    sub = lax.broadcasted_iota(jnp.int32, tail.shape, 0) % SUBLANES
    merged = jnp.where(sub == SUBLANES - 1, prev_ref[...], tail)
    for g in range(n_taps - 1):
        rows = slice(g * SUBLANES, (g + 1) * SUBLANES)
        ext_ref[rows, :] = pltpu.roll(merged[rows, :], 1, 0)
    ext_ref[lead:lead + ts, :] = cur
    prev_ref[...] = tail


def _conv_block(ext_ref, w_ref, n_taps, r0, row_blk, c0, col_blk):
    groups = row_blk // SUBLANES
    acc = [jnp.zeros((SUBLANES, col_blk), F32) for _ in range(groups)]
    for k in range(n_taps):
        w = w_ref[k * SUBLANES:(k + 1) * SUBLANES, c0:c0 + col_blk]
        for g in range(groups):
            lo = r0 + (k + g) * SUBLANES
            acc[g] = acc[g] + w * ext_ref[lo:lo + SUBLANES, c0:c0 + col_blk]
    return jnp.concatenate(acc, axis=0)


def _mixer_kernel(x_ref, mod_ref, npre_ref, npost_ref, win_ref,
                  convw_ref, alog_ref, dtb_ref, dnw_ref, pw1b_ref, dww_ref, dwb_ref, lnw_ref, lnb_ref,
                  woutf_ref, o_ref, qkv_ext, qkv_prev, qkv_p16, qkv_act, cf_ext, cf_prev, cf_p16, state, mixed,
                  consts, wqkv_ref, wz_ref, wba_ref, wcf_ref, wout_ref):
    ts = x_ref.shape[2]
    n_chunks = ts // DN_CHUNK

    @pl.when(pl.program_id(1) == 0)
    def _():
        qkv_prev[...] = jnp.zeros(qkv_prev.shape, F32)
        cf_prev[...] = jnp.zeros(cf_prev.shape, F32)
        state[...] = jnp.zeros(state.shape, F32)

    @pl.when((pl.program_id(0) == 0) & (pl.program_id(1) == 0))
    def _():
        ri = lax.broadcasted_iota(jnp.int32, (ts, ts), 0)
        ci = lax.broadcasted_iota(jnp.int32, (ts, ts), 1)
        consts[0] = _time_perm(ts, False)
        consts[1] = _time_perm(ts, True)
        consts[2] = jnp.where((ri // DN_CHUNK == ci // DN_CHUNK) & (ci <= ri), 1.0, 0.0).astype(BF16)
        n_qkv, n_z, n_ba = wqkv_ref.shape[1], wz_ref.shape[1], 2 * DN_HEADS
        for c0 in range(0, n_qkv, DN_WIDTH):
            wqkv_ref[:, c0:c0 + DN_WIDTH] = win_ref[c0:c0 + DN_WIDTH, :].T.astype(BF16)
        wz_ref[...] = win_ref[n_qkv:n_qkv + n_z, :].T.astype(BF16)
        lane = lax.broadcasted_iota(jnp.int32, wba_ref.shape, 1)
        wba_ref[...] = jnp.where(lane < n_ba, win_ref[n_qkv + n_z:n_qkv + n_z + LANES, :].T, 0.0).astype(BF16)
        cf0 = n_qkv + n_z + n_ba
        for c0 in range(0, wcf_ref.shape[1], DN_WIDTH):
            wcf_ref[:, c0:c0 + DN_WIDTH] = win_ref[cf0 + c0:cf0 + c0 + DN_WIDTH, :].T.astype(BF16)
        wout_ref[...] = woutf_ref[...].astype(BF16)

    r64 = lax.broadcasted_iota(jnp.int32, (DN_CHUNK, DN_CHUNK), 0)
    c64 = lax.broadcasted_iota(jnp.int32, (DN_CHUNK, DN_CHUNK), 1)
    causal = c64 <= r64
    strict = c64 < r64
    eye = jnp.where(c64 == r64, 1.0, 0.0).astype(F32)
    dnw = dnw_ref[...]
    to_natural = consts[1]
    tri = consts[2]
    cells = [(hd, ch) for hd in range(DN_HEADS) for ch in range(n_chunks)]
    rows = lambda ch: slice(ch * DN_CHUNK, (ch + 1) * DN_CHUNK)

    def tile_stages(sid, qkv_ext, qkv_prev, qkv_p16, qkv_act, cf_ext, cf_prev, cf_p16, state, mixed):
        x = x_ref[sid, 0]
        mod = mod_ref[sid, 0]
        shift1, scale1, gate1 = mod[0:1], mod[1:2], mod[2:3]
        h = x * lax.rsqrt(jnp.mean(x * x, axis=-1, keepdims=True) + EPS)
        h = h * npre_ref[...] * (1.0 + scale1) + shift1
        hb = h.astype(BF16)
        yield
        hb_st = _dot(consts[0], hb).astype(BF16)

        _fill_conv_window(qkv_ext, qkv_prev, _dot(hb_st, wqkv_ref[...]), DN_CONV)
        yield
        for r0 in range(0, ts, CONV_ROWS):
            for c0 in range(0, qkv_ext.shape[1], DN_WIDTH):
                blk = _conv_block(qkv_ext, convw_ref, DN_CONV, r0, CONV_ROWS, c0, DN_WIDTH)
                qkv_p16[r0:r0 + CONV_ROWS, c0:c0 + DN_WIDTH] = _silu(blk).astype(BF16)
        yield
        qkv_act[...] = _dot(to_natural, qkv_p16[...])

        cf_pre = _dot(hb_st, wcf_ref[...]) + pw1b_ref[...]
        cfw = cf_ext.shape[1]
        _fill_conv_window(cf_ext, cf_prev, cf_pre[:, :cfw] * jax.nn.sigmoid(cf_pre[:, cfw:]), CF_KERNEL)
        cf_pending = list(range(0, ts, CONV_ROWS))
        yield

        def cf_step(n=1):
            for _ in range(min(n, len(cf_pending))):
                r0 = cf_pending.pop(0)
                cf = _conv_block(cf_ext, dww_ref, CF_KERNEL, r0, CONV_ROWS, 0, cfw) + dwb_ref[...]
                mu = jnp.mean(cf, axis=-1, keepdims=True)
                xc = cf - mu
                var = jnp.mean(xc * xc, axis=-1, keepdims=True)
                cfn = xc * lax.rsqrt(var + EPS) * lnw_ref[...] + lnb_ref[...]
                cf_p16[r0:r0 + CONV_ROWS, :] = _silu(cfn).astype(BF16)

        z = _dot(hb, wz_ref[...])
        ba = _dot(hb, wba_ref[...])
        beta_all = jax.nn.sigmoid(ba)
        g_all = -jnp.exp(alog_ref[...]) * _softplus(ba + dtb_ref[...])

        g_hi, g_mid, g_lo = _split3(g_all)
        gcum = _dot(tri, g_hi) + _dot(tri, g_mid) + _dot(tri, g_lo)
        gcum_t = gcum.T
        exp_g = jnp.exp(gcum)
        yield

        heads = []
        for hd in range(DN_HEADS):
            lo = hd * HEAD_DIM
            qh = qkv_act[:, lo:lo + HEAD_DIM]
            kh = qkv_act[:, DN_WIDTH + lo:DN_WIDTH + lo + HEAD_DIM]
            vh = qkv_act[:, 2 * DN_WIDTH + lo:2 * DN_WIDTH + lo + HEAD_DIM]
            qn = qh * lax.rsqrt(jnp.sum(qh * qh, axis=-1, keepdims=True) + EPS) * (HEAD_DIM ** -0.5)
            kn = kh * lax.rsqrt(jnp.sum(kh * kh, axis=-1, keepdims=True) + EPS)
            beta_h = beta_all[:, hd:hd + 1]
            gc_h = gcum[:, DN_HEADS + hd:DN_HEADS + hd + 1]
            eg_h = exp_g[:, DN_HEADS + hd:DN_HEADS + hd + 1]
            k_beta = kn * beta_h
            heads.append(dict(qn=qn, kn=kn, k_beta=k_beta, v_beta=vh * beta_h, kbg=k_beta * eg_h, qg=qn * eg_h,
                              gc=gc_h, zg=_silu(z[:, lo:lo + HEAD_DIM])))
        yield

        decay, kq = {}, {}
        for hd, ch in cells:
            hv, sl = heads[hd], rows(ch)
            gc_row = gcum_t[DN_HEADS + hd:DN_HEADS + hd + 1, sl]
            decay[hd, ch] = jnp.where(causal, jnp.exp(hv["gc"][sl] - gc_row), 0.0)
            lhs = jnp.concatenate([hv["k_beta"][sl], hv["qn"][sl]], axis=0).astype(BF16)
            kq[hd, ch] = _dot_nt(lhs, hv["kn"][sl].astype(BF16))
        a = {c: jnp.where(strict, kq[c][:DN_CHUNK] * decay[c], 0.0) for c in cells}
        attn = {c: (kq[c][DN_CHUNK:] * decay[c]).astype(BF16) for c in cells}
        yield

        t_inv = {c: eye - a[c] for c in cells}
        pw = {c: a[c].astype(BF16) for c in cells}
        pw = {c: _dot(pw[c], pw[c]).astype(BF16) for c in cells}
        cf_step()
        yield
        for _ in range(4):
            both = {c: _dot(jnp.concatenate([t_inv[c].astype(BF16), pw[c]], axis=0), pw[c]) for c in cells}
            t_inv = {c: t_inv[c] + both[c][:DN_CHUNK] for c in cells}
            pw = {c: both[c][DN_CHUNK:].astype(BF16) for c in cells}
            cf_step()
            yield
        t_inv = {c: t_inv[c] + _dot(t_inv[c].astype(BF16), pw[c]) for c in cells}

        sol, aw, ks, glast = {}, {}, {}, {}
        for hd, ch in cells:
            hv, sl = heads[hd], rows(ch)
            rhs = jnp.concatenate([hv["v_beta"][sl], hv["kbg"][sl]], axis=1).astype(BF16)
            sol[hd, ch] = _dot(t_inv[hd, ch].astype(BF16), rhs).astype(BF16)
        yield
        for hd, ch in cells:
            hv, sl = heads[hd], rows(ch)
            gc_col = hv["gc"][sl]
            glast[hd, ch] = gc_col[DN_CHUNK - 1:DN_CHUNK]
            k_dec = (hv["kn"][sl] * jnp.exp(glast[hd, ch] - gc_col)).astype(BF16)
            aw[hd, ch] = _dot(attn[hd, ch], sol[hd, ch])
            ks[hd, ch] = _dot_tn(k_dec, sol[hd, ch])
        yield

        s_in = {}
        s_cur = [state[hd] for hd in range(DN_HEADS)]
        for ch in range(n_chunks):
            for hd in range(DN_HEADS):
                s_in[hd, ch] = s_cur[hd].astype(BF16)
                kd_u, kd_w = ks[hd, ch][:, :HEAD_DIM], ks[hd, ch][:, HEAD_DIM:]
                s_cur[hd] = s_cur[hd] * jnp.exp(glast[hd, ch]) + kd_u - _dot(kd_w.astype(BF16), s_in[hd, ch])
            cf_step()
            yield
        for hd in range(DN_HEADS):
            state[hd] = s_cur[hd]

        for hd, ch in cells:
            hv, sl = heads[hd], rows(ch)
            lo = hd * HEAD_DIM
            q_eff = (hv["qg"][sl] - aw[hd, ch][:, HEAD_DIM:]).astype(BF16)
            o = _dot(q_eff, s_in[hd, ch]) + aw[hd, ch][:, :HEAD_DIM]
            on = o * lax.rsqrt(jnp.mean(o * o, axis=-1, keepdims=True) + EPS) * dnw * hv["zg"][sl]
            mixed[sl, lo:lo + HEAD_DIM] = on.astype(BF16)
        yield

        cf_step(len(cf_pending))
        mixed[:, DN_WIDTH:DN_WIDTH + cfw] = _dot(to_natural, cf_p16[...]).astype(BF16)
        yield

        out = _dot(mixed[...], wout_ref[...])
        yield
        y = out * lax.rsqrt(jnp.mean(out * out, axis=-1, keepdims=True) + EPS) * npost_ref[...]
        o_ref[sid, 0] = x + gate1 * y

    scratch = (qkv_ext, qkv_prev, qkv_p16, qkv_act, cf_ext, cf_prev, cf_p16, state, mixed)
    streams = [tile_stages(sid, *(ref.at[sid] for ref in scratch)) for sid in range(x_ref.shape[0])]
    live = [True] * len(streams)

    def advance(i, n=1):
        for _ in range(n):
            if live[i]:
                try:
                    next(streams[i])
                except StopIteration:
                    live[i] = False

    advance(0, MIXER_SKEW)
    while any(live):
        for i in range(len(streams)):
            advance(i)


def _mixer_call(x, mod3, npre, npost, w_in, dn_conv_w, dn_a_log, dn_dt_bias, dn_norm_w,
                cf_pw1_b, cf_dw_w, cf_dw_b, cf_ln_w, cf_ln_b, w_out):
    bsz, seq, d = x.shape
    ts = SEQ_TILE
    cfw = cf_dw_w.shape[1]
    n_qkv = 3 * DN_WIDTH
    alog = jnp.pad(dn_a_log, (DN_HEADS, LANES - 2 * DN_HEADS)).reshape(1, LANES)
    dtb = jnp.pad(dn_dt_bias, (DN_HEADS, LANES - 2 * DN_HEADS)).reshape(1, LANES)
    dww = jnp.repeat(cf_dw_w, SUBLANES, axis=0)
    convw = jnp.repeat(dn_conv_w, SUBLANES, axis=0)

    def full(a):
        mode = dict(pipeline_mode=pl.Buffered(1)) if a.size >= d * d else {}
        return pl.BlockSpec(a.shape, lambda b, j: (0,) * a.ndim, **mode)

    row = lambda a: a.reshape(1, -1)
    nb = MIXER_PAIR
    x4 = x.reshape(nb, bsz // nb, seq, d)
    mod4 = mod3.reshape((nb, bsz // nb) + mod3.shape[1:])
    operands = [x4, mod4, row(npre), row(npost), w_in.T, convw, alog, dtb, row(dn_norm_w),
                row(cf_pw1_b), dww, row(cf_dw_b), row(cf_ln_w), row(cf_ln_b), w_out]
    in_specs = [pl.BlockSpec((nb, 1, ts, d), lambda b, j: (0, b, j, 0)),
                pl.BlockSpec((nb, 1) + mod3.shape[1:], lambda b, j: (0, b, 0, 0))]
    in_specs += [full(a) for a in operands[2:]]
    out = pl.pallas_call(
        _mixer_kernel,
        grid=(bsz // nb, seq // ts),
        in_specs=in_specs,
        out_specs=pl.BlockSpec((nb, 1, ts, d), lambda b, j: (0, b, j, 0)),
        out_shape=jax.ShapeDtypeStruct((nb, bsz // nb, seq, d), F32),
        scratch_shapes=[pltpu.VMEM((nb, (DN_CONV - 1) * SUBLANES + ts, n_qkv), F32),
                        pltpu.VMEM((nb, (DN_CONV - 1) * SUBLANES, n_qkv), F32),
                        pltpu.VMEM((nb, ts, n_qkv), BF16),
                        pltpu.VMEM((nb, ts, n_qkv), F32),
                        pltpu.VMEM((nb, (CF_KERNEL - 1) * SUBLANES + ts, cfw), F32),
                        pltpu.VMEM((nb, (CF_KERNEL - 1) * SUBLANES, cfw), F32),
                        pltpu.VMEM((nb, ts, cfw), BF16),
                        pltpu.VMEM((nb, DN_HEADS, HEAD_DIM, HEAD_DIM), F32),
                        pltpu.VMEM((nb, ts, DN_WIDTH + cfw), BF16),
                        pltpu.VMEM((3, ts, ts), BF16),
                        pltpu.VMEM((d, n_qkv), BF16), pltpu.VMEM((d, DN_WIDTH), BF16), pltpu.VMEM((d, LANES), BF16),
                        pltpu.VMEM((d, 2 * cfw), BF16), pltpu.VMEM(w_out.shape, BF16)],
        compiler_params=pltpu.CompilerParams(dimension_semantics=("arbitrary", "arbitrary"),
                                             vmem_limit_bytes=VMEM_LIMIT),
        name="mixer",
    )(*operands)
    return out.reshape(bsz, seq, d)


def _router_kernel(x_ref, mod_ref, nw_ref, wr_ref, br_ref, h_ref, info_ref, infot_ref, cnt_ref, carry, strict_ref):
    tt = x_ref.shape[1]

    @pl.when((pl.program_id(0) == 0) & (pl.program_id(1) == 0))
    def _():
        carry[...] = jnp.zeros(carry.shape, F32)
        rr = lax.broadcasted_iota(jnp.int32, (tt, tt), 0)
        cc = lax.broadcasted_iota(jnp.int32, (tt, tt), 1)
        strict_ref[...] = jnp.where(cc < rr, 1.0, 0.0).astype(BF16)

    x = x_ref[0]
    mod = mod_ref[0]
    shift2, scale2 = mod[3:4], mod[4:5]
    h = x * lax.rsqrt(jnp.mean(x * x, axis=-1, keepdims=True) + EPS)
    h = h * nw_ref[...] * (1.0 + scale2) + shift2
    _store_token_major(h_ref, h)

    h_hi, h_lo, _ = _split3(h)
    w_hi, w_lo, _ = _split3(wr_ref[...])
    logits = _dot(h_hi, w_hi) + (_dot(h_hi, w_lo) + _dot(h_lo, w_hi)) + br_ref[...]

    lane = lax.broadcasted_iota(jnp.int32, (tt, LANES), 1)
    neg = -jnp.inf
    is_grp = (lane >= N_EXPERTS) & (lane < N_EXPERTS + N_GROUPS)
    gl = jnp.where(is_grp, logits, neg)
    gmax = jnp.max(gl, axis=-1, keepdims=True)
    gsum = jnp.sum(jnp.where(is_grp, jnp.exp(gl - gmax), 0.0), axis=-1, keepdims=True)
    grp_p = 1.0 / gsum
    grp_lane = jnp.min(jnp.where(is_grp & (gl == gmax), lane, LANES), axis=-1, keepdims=True)
    grp_idx = grp_lane - N_EXPERTS

    in_grp = (lane < N_EXPERTS) & (lane // EXPERTS_PER_GROUP == grp_idx)
    el = jnp.where(in_grp, logits, neg)
    m1 = jnp.max(el, axis=-1, keepdims=True)
    e1 = jnp.min(jnp.where(in_grp & (el == m1), lane, LANES), axis=-1, keepdims=True)
    el2 = jnp.where(lane == e1, neg, el)
    m2 = jnp.max(el2, axis=-1, keepdims=True)
    e2 = jnp.min(jnp.where(in_grp & (lane != e1) & (el2 == m2), lane, LANES), axis=-1, keepdims=True)
    r = jnp.exp(m2 - m1)
    w1 = grp_p / (1.0 + r)
    w2 = grp_p * r / (1.0 + r)

    hit1 = lane == e1
    hit2 = lane == e2
    onehot = jnp.where(hit1 | hit2, 1.0, 0.0)
    prefix = _dot(strict_ref[...], onehot.astype(BF16)) + carry[0:1, :]
    rank1 = jnp.sum(jnp.where(hit1, prefix, 0.0), axis=-1, keepdims=True)
    rank2 = jnp.sum(jnp.where(hit2, prefix, 0.0), axis=-1, keepdims=True)
    total = carry[0:1, :] + jnp.sum(onehot, axis=0, keepdims=True)
    carry[...] = jnp.broadcast_to(total, carry.shape)
    cnt_ref[...] = jnp.broadcast_to(total, cnt_ref.shape)

    info = jnp.where(lane == 0, e1.astype(F32), 0.0)
    info = jnp.where(lane == 1, e2.astype(F32), info)
    info = jnp.where(lane == 2, w1, info)
    info = jnp.where(lane == 3, w2, info)
    info = jnp.where(lane == 4, rank1, info)
    info = jnp.where(lane == 5, rank2, info)
    info_ref[0] = info
    infot_ref[...] = info.T[0:SUBLANES, :]


def _router_call(x1, mod3, norm_w, w_router_group, b_router_group, w_router_expert, b_router_expert):
    bsz, seq, d = x1.shape
    tt = ROUTER_TILE
    nj = seq // tt
    pitch = d // LANES
    pad = LANES - N_EXPERTS - N_GROUPS
    wr = jnp.pad(jnp.concatenate([w_router_expert, w_router_group], axis=1), ((0, 0), (0, pad)))
    br = jnp.pad(jnp.concatenate([b_router_expert, b_router_group]), (0, pad)).reshape(1, LANES)
    return pl.pallas_call(
        _router_kernel,
        grid=(bsz, seq // tt),
        in_specs=[pl.BlockSpec((1, tt, d), lambda b, j: (b, j, 0)),
                  pl.BlockSpec((1,) + mod3.shape[1:], lambda b, j: (b, 0, 0)),
                  pl.BlockSpec((1, d), lambda b, j: (0, 0)),
                  pl.BlockSpec((d, LANES), lambda b, j: (0, 0)),
                  pl.BlockSpec((1, LANES), lambda b, j: (0, 0))],
        out_specs=[pl.BlockSpec((tt * pitch, LANES), lambda b, j: (b * nj + j, 0)),
                   pl.BlockSpec((1, tt, LANES), lambda b, j: (b, j, 0)),
                   pl.BlockSpec((SUBLANES, tt), lambda b, j: (0, b * nj + j)),
                   pl.BlockSpec((8, LANES), lambda b, j: (0, 0))],
        out_shape=[jax.ShapeDtypeStruct((bsz * seq * pitch, LANES), F32),
                   jax.ShapeDtypeStruct((bsz, seq, LANES), F32),
                   jax.ShapeDtypeStruct((SUBLANES, bsz * seq), F32),
                   jax.ShapeDtypeStruct((8, LANES), F32)],
        scratch_shapes=[pltpu.VMEM((8, LANES), F32), pltpu.VMEM((tt, tt), BF16)],
        compiler_params=pltpu.CompilerParams(dimension_semantics=("arbitrary", "arbitrary"),
                                             vmem_limit_bytes=VMEM_LIMIT),
        name="router",
    )(x1, mod3, norm_w.reshape(1, d), wr, br)


def _tile_rows(idx, pitch):
    if isinstance(idx, int):
        return pl.ds(idx * pitch, pitch)
    return pl.ds(pl.multiple_of(idx * pitch, pitch), pitch)


def _dispatch_kernel(nused_ref, padfrom_ref, padlen_ref, pos_ref, h_hbm, xs_hbm, zbuf, zsem, hbuf, hsems, ssems,
                     *, pitch, bm):
    step = pl.program_id(0)
    tt = hbuf.shape[1] // pitch

    def fill_copies(action):
        def per_expert(e, carry):
            cursor = padfrom_ref[e]
            n = padlen_ref[e]
            bit = bm // 2
            while bit >= 1:
                take = (n & bit) != 0

                @pl.when(take)
                def _(cursor=cursor, bit=bit):
                    action(pltpu.make_async_copy(zbuf.at[pl.ds(0, bit * pitch), :],
                                                 xs_hbm.at[pl.ds(pl.multiple_of(cursor * pitch, pitch), bit * pitch), :],
                                                 zsem.at[0]))
                cursor = cursor + jnp.where(take, bit, 0)
                bit //= 2
            return carry

        def per_block(b, carry):
            start = pl.multiple_of(b * bm * pitch, bm * pitch)
            action(pltpu.make_async_copy(zbuf, xs_hbm.at[pl.ds(start, bm * pitch), :], zsem.at[0]))
            return carry

        lax.fori_loop(0, N_EXPERTS, per_expert, 0)
        lax.fori_loop(nused_ref[0], xs_hbm.shape[0] // (bm * pitch), per_block, 0)

    @pl.when(step == 0)
    def _():
        zbuf[...] = jnp.zeros(zbuf.shape, F32)
        fill_copies(lambda cp: cp.start())

    n_steps = pl.num_programs(0)
    slots = hbuf.shape[0]
    tile_rows = tt * pitch

    def tile_in(s):
        src = h_hbm.at[pl.ds(pl.multiple_of(s * tile_rows, tile_rows), tile_rows), :]
        return pltpu.make_async_copy(src, hbuf.at[s % slots], hsems.at[s % slots])

    def scatter_wait(s):
        for k in range(TOP_K):
            pltpu.make_async_copy(hbuf.at[s % slots], xs_hbm.at[pl.ds(0, tile_rows), :], ssems.at[s % slots]).wait()

    @pl.when(step == 0)
    def _():
        for s in range(slots - 1):
            @pl.when(s < n_steps)
            def _(s=s):
                tile_in(s).start()

    tile_in(step).wait()
    src = hbuf.at[step % slots]
    for r in range(tt):
        for k in range(TOP_K):
            p = pos_ref[0, 0, k * tt + r]
            pltpu.make_async_copy(src.at[_tile_rows(r, pitch), :], xs_hbm.at[_tile_rows(p, pitch), :],
                                  ssems.at[step % slots]).start(priority=k)

    @pl.when(step >= 1)
    def _():
        scatter_wait(step - 1)

    @pl.when(step + slots - 1 < n_steps)
    def _():
        tile_in(step + slots - 1).start()

    @pl.when(step == n_steps - 1)
    def _():
        scatter_wait(step)
        fill_copies(lambda cp: cp.wait())


def _dispatch_call(h2, pos3, n_used, padfrom, padlen, n_rows, d):
    pitch = d // LANES
    bm = EXPERT_BLOCK
    n_tiles, _, two_tt = pos3.shape
    tt = two_tt // TOP_K
    grid_spec = pltpu.PrefetchScalarGridSpec(
        num_scalar_prefetch=3,
        grid=(n_tiles,),
        in_specs=[pl.BlockSpec((1, 1, two_tt), lambda s, *_: (s, 0, 0), memory_space=pltpu.SMEM),
                  pl.BlockSpec(memory_space=pl.ANY)],
        out_specs=pl.BlockSpec(memory_space=pl.ANY),
        scratch_shapes=[pltpu.VMEM((bm * pitch, LANES), F32), pltpu.SemaphoreType.DMA((1,)),
                        pltpu.VMEM((DISPATCH_SLOTS, tt * pitch, LANES), F32),
                        pltpu.SemaphoreType.DMA((DISPATCH_SLOTS,)), pltpu.SemaphoreType.DMA((DISPATCH_SLOTS,))],
    )
    return pl.pallas_call(
        functools.partial(_dispatch_kernel, pitch=pitch, bm=bm),
        grid_spec=grid_spec,
        out_shape=jax.ShapeDtypeStruct((n_rows * pitch, LANES), F32),
        compiler_params=pltpu.CompilerParams(dimension_semantics=("arbitrary",),
                                             vmem_limit_bytes=VMEM_LIMIT),
        name="dispatch",
    )(n_used, padfrom, padlen, pos3, h2)


def _expert_kernel(nused_ref, nruns_ref, first_ref, run_ref, rune_ref, x_hbm, wg_hbm, wu_hbm, wd_hbm, y_hbm,
                   wg_f32, wu_f32, wd_f32, wg_b, wu_b, wd_b, wsems, xbuf, ybuf, xsems, ysems):
    i = pl.program_id(0)
    n_used = nused_ref[0]
    n_runs = nruns_ref[0]
    d = wg_b.shape[0]
    pitch = d // LANES
    bm = xbuf.shape[1] // pitch
    n_slots = wg_f32.shape[0]

    def weight_copies(e, slot):
        return [pltpu.make_async_copy(src.at[e], dst.at[slot], wsems.at[slot])
                for src, dst in ((wg_hbm, wg_f32), (wu_hbm, wu_f32), (wd_hbm, wd_f32))]

    def start_run(r):
        @pl.when(r < n_runs)
        def _():
            for cp in weight_copies(rune_ref[r], r % n_slots):
                cp.start(priority=1)

    @pl.when(i == 0)
    def _():
        for r in range(n_slots - 1):
            start_run(r)

    @pl.when((i < n_used) & (first_ref[i] == 1))
    def _():
        run = run_ref[i]
        slot = run % n_slots
        for cp in weight_copies(0, slot):
            cp.wait()
        start_run(run + n_slots - 1)

        wg_b[...] = wg_f32[slot].astype(BF16)
        wu_b[...] = wu_f32[slot].astype(BF16)
        wd_b[...] = wd_f32[slot].astype(BF16)

    blk_rows = bm * pitch
    x_slots, y_slots = xbuf.shape[0], ybuf.shape[0]

    def block(ref, b):
        return ref.at[pl.ds(pl.multiple_of(b * blk_rows, blk_rows), blk_rows), :]

    def x_copy(b):
        return pltpu.make_async_copy(block(x_hbm, b), xbuf.at[b % x_slots], xsems.at[b % x_slots])

    def y_copy(b):
        return pltpu.make_async_copy(ybuf.at[b % y_slots], block(y_hbm, b), ysems.at[b % y_slots])

    @pl.when(i == 0)
    def _():
        for b in range(x_slots - 1):
            @pl.when(b < n_used)
            def _(b=b):
                x_copy(b).start()

    @pl.when(i < n_used)
    def _():
        x_copy(i).wait()

        @pl.when(i + x_slots - 1 < n_used)
        def _():
            x_copy(i + x_slots - 1).start()

        @pl.when(i >= y_slots)
        def _():
            y_copy(i - y_slots).wait()

        xb = _load_token_major(xbuf.at[i % x_slots], bm, d).astype(BF16)
        gate = _dot(xb, wg_b[...])
        up = _dot(xb, wu_b[...])
        hid = (_silu(gate) * up).astype(BF16)
        _store_token_major(ybuf.at[i % y_slots], _dot(hid, wd_b[...]))
        y_copy(i).start()

    @pl.when(i == n_used - 1)
    def _():
        for back in range(y_slots):
            @pl.when(i - back >= 0)
            def _(back=back):
                y_copy(i - back).wait()


def _expert_call(xs, block_e, n_used, w_gate, w_up, w_down):
    bm = EXPERT_BLOCK
    d, de = w_gate.shape[1], w_gate.shape[2]
    pitch = d // LANES
    n_blocks = xs.shape[0] // (bm * pitch)
    blk = jnp.arange(n_blocks, dtype=jnp.int32)
    used = blk < n_used[0]
    first = jnp.concatenate([jnp.ones((1,), jnp.int32), (block_e[1:] != block_e[:-1]).astype(jnp.int32)])
    first = jnp.where(used, first, 0)
    run = (jnp.cumsum(first) - 1).astype(jnp.int32)
    n_runs = jnp.sum(first).astype(jnp.int32).reshape(1)
    eids = jnp.arange(N_EXPERTS, dtype=jnp.int32)
    run_of_e = jnp.where((first[:, None] == 1) & (block_e[:, None] == eids[None, :]), run[:, None], -1).max(axis=0)
    run_e = jnp.sum(jnp.where(run_of_e[None, :] == eids[:, None], eids[None, :], 0), axis=1).astype(jnp.int32)
    grid_spec = pltpu.PrefetchScalarGridSpec(
        num_scalar_prefetch=5,
        grid=(n_blocks,),
        in_specs=[pl.BlockSpec(memory_space=pl.ANY)] * 4,
        out_specs=pl.BlockSpec(memory_space=pl.ANY),
        scratch_shapes=[pltpu.VMEM((WEIGHT_SLOTS, d, de), F32), pltpu.VMEM((WEIGHT_SLOTS, d, de), F32),
                        pltpu.VMEM((WEIGHT_SLOTS, de, d), F32),
                        pltpu.VMEM((d, de), BF16), pltpu.VMEM((d, de), BF16), pltpu.VMEM((de, d), BF16),
                        pltpu.SemaphoreType.DMA((WEIGHT_SLOTS,)),
                        pltpu.VMEM((ROW_SLOTS_IN, bm * pitch, LANES), F32),
                        pltpu.VMEM((ROW_SLOTS_OUT, bm * pitch, LANES), F32),
                        pltpu.SemaphoreType.DMA((ROW_SLOTS_IN,)), pltpu.SemaphoreType.DMA((ROW_SLOTS_OUT,))],
    )
    return pl.pallas_call(
        _expert_kernel,
        grid_spec=grid_spec,
        out_shape=jax.ShapeDtypeStruct((n_blocks * bm * pitch, LANES), F32),
        input_output_aliases={5: 0},
        compiler_params=pltpu.CompilerParams(dimension_semantics=("arbitrary",),
                                             vmem_limit_bytes=VMEM_LIMIT),
        name="experts",
    )(n_used, n_runs, first, run, run_e, xs, w_gate, w_up, w_down)


def _combine_kernel(pos_pre_ref, pos_ahead_ref, x_ref, info_ref, mod_ref, nw_ref, y_hbm, o_ref, *bufs_and_sems):
    bufs, sems = bufs_and_sems[:-1], bufs_and_sems[-1]
    n_bufs = len(bufs)
    ahead = n_bufs - 1
    tt, d = x_ref.shape[1], x_ref.shape[2]
    pitch = d // LANES
    step = pl.program_id(0) * pl.num_programs(1) + pl.program_id(1)
    n_steps = pl.num_programs(0) * pl.num_programs(1)
    n_chunks = tt // COMBINE_ROWS

    def row_copy(pos_ref, tile, slot, k, r):
        p = pos_ref[tile, 0, k * tt + r]
        return pltpu.make_async_copy(y_hbm.at[_tile_rows(p, pitch), :],
                                     bufs[slot].at[_tile_rows(k * tt + r, pitch), :], sems.at[slot])

    def wait_tile(slot):
        pltpu.make_async_copy(y_hbm.at[pl.ds(0, TOP_K * tt * pitch), :], bufs[slot], sems.at[slot]).wait()

    @pl.when(step == 0)
    def _():
        for t in range(ahead):
            def body(r, carry, t=t):
                for k in range(TOP_K):
                    row_copy(pos_pre_ref, t, t, k, r).start(priority=k)
                return carry
            lax.fori_loop(0, tt, body, 0, unroll=4)

    gate2 = mod_ref[0][5:6]

    def run_tile(cur, nxt):
        wait_tile(cur)
        for c in range(n_chunks):
            r0 = c * COMBINE_ROWS
            for r in range(COMBINE_ROWS):
                for k in range(TOP_K):
                    row_copy(pos_ahead_ref, 0, nxt, k, r0 + r).start(priority=k)
            rows = pl.ds(r0, COMBINE_ROWS)
            info = info_ref[0, rows, :]
            moe = (_load_token_major(bufs[cur], COMBINE_ROWS, d, r0 * pitch) * info[:, 2:3]
                   + _load_token_major(bufs[cur], COMBINE_ROWS, d, (tt + r0) * pitch) * info[:, 3:4])
            y = moe * lax.rsqrt(jnp.mean(moe * moe, axis=-1, keepdims=True) + EPS) * nw_ref[...]
            o_ref[0, rows, :] = x_ref[0, rows, :] + gate2 * y

        @pl.when(step == n_steps - 1)
        def _():
            for other in range(n_bufs):
                if other != cur:
                    wait_tile(other)

    for parity in range(n_bufs):
        @pl.when(step % n_bufs == parity)
        def _(parity=parity):
            run_tile(parity, (parity + ahead) % n_bufs)


def _combine_call(x1, info, mod3, norm_w, y, pos3):
    bsz, seq, d = x1.shape
    tt = COMBINE_TILE
    nj = seq // tt
    n_tiles = bsz * nj
    ahead = COMBINE_SLOTS - 1
    pos_blk = lambda n, f: pl.BlockSpec((n, 1, TOP_K * tt), f, memory_space=pltpu.SMEM)
    tile_buf = pltpu.VMEM((TOP_K * tt * (d // LANES), LANES), F32)
    return pl.pallas_call(
        _combine_kernel,
        grid=(bsz, nj),
        in_specs=[pos_blk(ahead, lambda b, j: (0, 0, 0)),
                  pos_blk(1, lambda b, j: (jnp.minimum(b * nj + j + ahead, n_tiles - 1), 0, 0)),
                  pl.BlockSpec((1, tt, d), lambda b, j: (b, j, 0)),
                  pl.BlockSpec((1, tt, LANES), lambda b, j: (b, j, 0)),
                  pl.BlockSpec((1,) + mod3.shape[1:], lambda b, j: (b, 0, 0)),
                  pl.BlockSpec((1, d), lambda b, j: (0, 0)),
                  pl.BlockSpec(memory_space=pl.ANY)],
        out_specs=pl.BlockSpec((1, tt, d), lambda b, j: (b, j, 0)),
        out_shape=jax.ShapeDtypeStruct((bsz, seq, d), F32),
        scratch_shapes=[tile_buf] * COMBINE_SLOTS + [pltpu.SemaphoreType.DMA((COMBINE_SLOTS,))],
        compiler_params=pltpu.CompilerParams(dimension_semantics=("arbitrary", "arbitrary"),
                                             vmem_limit_bytes=VMEM_LIMIT),
        name="combine",
    )(pos3, pos3, x1, info, mod3, norm_w.reshape(1, d), y)


def _layer(x, mod, norm_pre_mix, norm_post_mix, w_in, dn_conv_w, dn_a_log, dn_dt_bias, dn_norm_w,
           cf_pw1_b, cf_dw_w, cf_dw_b, cf_ln_w, cf_ln_b, w_out, norm_pre_ffn, norm_post_ffn,
           w_router_group, b_router_group, w_router_expert, b_router_expert, w_gate, w_up, w_down):
    bsz, seq, d = x.shape
    t = bsz * seq
    mod3 = mod.reshape(bsz, -1, d)
    x1 = _mixer_call(x, mod3, norm_pre_mix, norm_post_mix, w_in, dn_conv_w, dn_a_log, dn_dt_bias, dn_norm_w,
                     cf_pw1_b, cf_dw_w, cf_dw_b, cf_ln_w, cf_ln_b, w_out)
    h2, info, info_t, cnt = _router_call(x1, mod3, norm_pre_ffn, w_router_group, b_router_group,
                                         w_router_expert, b_router_expert)

    bm = EXPERT_BLOCK
    expert_id = info_t[0:TOP_K].astype(jnp.int32)
    rank = info_t[4:4 + TOP_K].astype(jnp.int32)
    counts = cnt[0, :N_EXPERTS].astype(jnp.int32)
    padded = (counts + bm - 1) // bm * bm
    pend = jnp.cumsum(padded)
    pstart = pend - padded
    eids = jnp.arange(N_EXPERTS, dtype=jnp.int32)
    onehot = expert_id[:, None, :] == eids[None, :, None]
    pos = jnp.sum(jnp.where(onehot, pstart[None, :, None], 0), axis=1) + rank

    def tiled(tt):
        return pos.reshape(TOP_K, t // tt, tt).transpose(1, 0, 2).reshape(t // tt, 1, TOP_K * tt)
    n_blocks = -(-(t * TOP_K) // bm) + N_EXPERTS
    block_start = jnp.arange(n_blocks, dtype=jnp.int32) * bm
    block_e = jnp.minimum(jnp.sum(pend[None, :] <= block_start[:, None], axis=1), N_EXPERTS - 1).astype(jnp.int32)
    n_used = (pend[-1] // bm).astype(jnp.int32).reshape(1)

    xs = _dispatch_call(h2, tiled(DISPATCH_TILE), n_used, (pstart + counts).astype(jnp.int32),
                        (padded - counts).astype(jnp.int32), n_blocks * bm, d)
    y = _expert_call(xs, block_e, n_used, w_gate, w_up, w_down)
    return _combine_call(x1, info, mod3, norm_post_ffn, y, tiled(COMBINE_TILE))


def kernel(x, c, w_ada, b_ada, norm_pre_mix, norm_post_mix, w_in, dn_conv_w, dn_a_log, dn_dt_bias, dn_norm_w,
           cf_pw1_b, cf_dw_w, cf_dw_b, cf_ln_w, cf_ln_b, w_out, norm_pre_ffn, norm_post_ffn,
           w_router_group, b_router_group, w_router_expert, b_router_expert, w_gate, w_up, w_down):
    depth = w_ada.shape[0]
    for l in range(depth):
        mod = _ada_call(c, w_ada[l], b_ada[l])
        x = _layer(x, mod, norm_pre_mix[l], norm_post_mix[l], w_in[l], dn_conv_w[l], dn_a_log[l],
                   dn_dt_bias[l], dn_norm_w[l], cf_pw1_b[l], cf_dw_w[l], cf_dw_b[l], cf_ln_w[l], cf_ln_b[l],
                   w_out[l], norm_pre_ffn[l], norm_post_ffn[l], w_router_group[l], b_router_group[l],
                   w_router_expert[l], b_router_expert[l], w_gate[l], w_up[l], w_down[l])
    return x
```

```python
import functools

import jax
import jax.numpy as jnp
from jax import lax
from jax.experimental import pallas as pl
from jax.experimental.pallas import tpu as pltpu

F32 = jnp.float32
BF16 = jnp.bfloat16
EPS = 1e-6

DN_HEADS = 4
HEAD_DIM = 128
DN_WIDTH = DN_HEADS * HEAD_DIM
DN_CONV = 4
DN_CHUNK = 64
CF_KERNEL = 31
N_GROUPS = 8
EXPERTS_PER_GROUP = 8
N_EXPERTS = N_GROUPS * EXPERTS_PER_GROUP
TOP_K = 2

LANES = 128
SUBLANES = 8
SEQ_TILE = 256
CONV_ROWS = 32
MIXER_PAIR = 2
MIXER_SKEW = 4
ROUTER_TILE = 512
EXPERT_BLOCK = 256
COMBINE_TILE = 512
COMBINE_SLOTS = 3
COMBINE_ROWS = 32
WEIGHT_SLOTS = 3
ROW_SLOTS_IN = 3
ROW_SLOTS_OUT = 2
DISPATCH_TILE = 1024
DISPATCH_SLOTS = 3
VMEM_LIMIT = 56 * 1024 * 1024


def _dot(a, b):
    return jnp.dot(a, b, preferred_element_type=F32)


def _dot_nt(a, b):
    return lax.dot_general(a, b, (((1,), (1,)), ((), ())), preferred_element_type=F32)


def _dot_tn(a, b):
    return lax.dot_general(a, b, (((0,), (0,)), ((), ())), preferred_element_type=F32)


def _split3(x):
    hi = x.astype(BF16)
    r1 = x - hi.astype(F32)
    mid = r1.astype(BF16)
    lo = (r1 - mid.astype(F32)).astype(BF16)
    return hi, mid, lo


def _silu(x):
    return x * jax.nn.sigmoid(x)


def _softplus(x):
    return jnp.maximum(x, 0.0) + jnp.log1p(jnp.exp(-jnp.abs(x)))


def _store_token_major(ref, val, base=0):
    n, d = val.shape
    pitch = d // LANES
    for j in range(pitch):
        ref[pl.ds(base + j, n, stride=pitch), :] = val[:, j * LANES:(j + 1) * LANES]


def _load_token_major(ref, n, d, base=0):
    pitch = d // LANES
    return jnp.concatenate([ref[pl.ds(base + j, n, stride=pitch), :] for j in range(pitch)], axis=1)


def _ada_kernel(c_ref, w_ref, b_ref, o_ref):
    c = c_ref[...]
    ca = _silu(c)
    c_hi, c_lo, _ = _split3(ca)
    w_hi, w_lo, _ = _split3(w_ref[...])
    o_ref[...] = _dot(c_hi, w_hi) + (_dot(c_hi, w_lo) + _dot(c_lo, w_hi)) + b_ref[...]


def _ada_call(c, w, b):
    bsz, d = c.shape
    n = w.shape[1]
    tn = 512
    return pl.pallas_call(
        _ada_kernel,
        grid=(n // tn,),
        in_specs=[pl.BlockSpec((bsz, d), lambda i: (0, 0)),
                  pl.BlockSpec((d, tn), lambda i: (0, i)),
                  pl.BlockSpec((1, tn), lambda i: (0, i))],
        out_specs=pl.BlockSpec((bsz, tn), lambda i: (0, i)),
        out_shape=jax.ShapeDtypeStruct((bsz, n), F32),
        compiler_params=pltpu.CompilerParams(dimension_semantics=("arbitrary",),
                                             vmem_limit_bytes=VMEM_LIMIT),
        name="ada",
    )(c, w, b.reshape(1, n))


def _time_perm(ts, transpose):
    ri = lax.broadcasted_iota(jnp.int32, (ts, ts), 0)
    ci = lax.broadcasted_iota(jnp.int32, (ts, ts), 1)
    strided, natural = (ci, ri) if transpose else (ri, ci)
    return jnp.where(natural == (ts // SUBLANES) * (strided % SUBLANES) + strided // SUBLANES, 1.0, 0.0).astype(BF16)


def _fill_conv_window(ext_ref, prev_ref, cur, n_taps):
    ts = cur.shape[0]
    lead = (n_taps - 1) * SUBLANES
    tail = cur[ts - lead:, :]
    sub = lax.broadcasted_iota(jnp.int32, tail.shape, 0) % SUBLANES
    merged = jnp.where(sub == SUBLANES - 1, prev_ref[...], tail)
    for g in range(n_taps - 1):
        rows = slice(g * SUBLANES, (g + 1) * SUBLANES)
        ext_ref[rows, :] = pltpu.roll(merged[rows, :], 1, 0)
    ext_ref[lead:lead + ts, :] = cur
    prev_ref[...] = tail


def _conv_block(ext_ref, w_ref, n_taps, r0, row_blk, c0, col_blk):
    groups = row_blk // SUBLANES
    acc = [jnp.zeros((SUBLANES, col_blk), F32) for _ in range(groups)]
    for k in range(n_taps):
        w = w_ref[k * SUBLANES:(k + 1) * SUBLANES, c0:c0 + col_blk]
        for g in range(groups):
            lo = r0 + (k + g) * SUBLANES
            acc[g] = acc[g] + w * ext_ref[lo:lo + SUBLANES, c0:c0 + col_blk]
    return jnp.concatenate(acc, axis=0)


def _mixer_kernel(x_ref, mod_ref, npre_ref, npost_ref, win_ref,
                  convw_ref, alog_ref, dtb_ref, dnw_ref, pw1b_ref, dww_ref, dwb_ref, lnw_ref, lnb_ref,
                  woutf_ref, o_ref, qkv_ext, qkv_prev, qkv_p16, qkv_act, cf_ext, cf_prev, cf_p16, state, mixed,
                  consts, wqkv_ref, wz_ref, wba_ref, wcf_ref, wout_ref):
    ts = x_ref.shape[2]
    n_chunks = ts // DN_CHUNK

    @pl.when(pl.program_id(1) == 0)
    def _():
        qkv_prev[...] = jnp.zeros(qkv_prev.shape, F32)
        cf_prev[...] = jnp.zeros(cf_prev.shape, F32)
        state[...] = jnp.zeros(state.shape, F32)

    @pl.when((pl.program_id(0) == 0) & (pl.program_id(1) == 0))
    def _():
        ri = lax.broadcasted_iota(jnp.int32, (ts, ts), 0)
        ci = lax.broadcasted_iota(jnp.int32, (ts, ts), 1)
        consts[0] = _time_perm(ts, False)
        consts[1] = _time_perm(ts, True)
        consts[2] = jnp.where((ri // DN_CHUNK == ci // DN_CHUNK) & (ci <= ri), 1.0, 0.0).astype(BF16)
        n_qkv, n_z, n_ba = wqkv_ref.shape[1], wz_ref.shape[1], 2 * DN_HEADS
        for c0 in range(0, n_qkv, DN_WIDTH):
            wqkv_ref[:, c0:c0 + DN_WIDTH] = win_ref[c0:c0 + DN_WIDTH, :].T.astype(BF16)
        wz_ref[...] = win_ref[n_qkv:n_qkv + n_z, :].T.astype(BF16)
        lane = lax.broadcasted_iota(jnp.int32, wba_ref.shape, 1)
        wba_ref[...] = jnp.where(lane < n_ba, win_ref[n_qkv + n_z:n_qkv + n_z + LANES, :].T, 0.0).astype(BF16)
        cf0 = n_qkv + n_z + n_ba
        for c0 in range(0, wcf_ref.shape[1], DN_WIDTH):
            wcf_ref[:, c0:c0 + DN_WIDTH] = win_ref[cf0 + c0:cf0 + c0 + DN_WIDTH, :].T.astype(BF16)
        wout_ref[...] = woutf_ref[...].astype(BF16)

    r64 = lax.broadcasted_iota(jnp.int32, (DN_CHUNK, DN_CHUNK), 0)
    c64 = lax.broadcasted_iota(jnp.int32, (DN_CHUNK, DN_CHUNK), 1)
    causal = c64 <= r64
    strict = c64 < r64
    eye = jnp.where(c64 == r64, 1.0, 0.0).astype(F32)
    dnw = dnw_ref[...]
    to_natural = consts[1]
    tri = consts[2]
    cells = [(hd, ch) for hd in range(DN_HEADS) for ch in range(n_chunks)]
    rows = lambda ch: slice(ch * DN_CHUNK, (ch + 1) * DN_CHUNK)

    def tile_stages(sid, qkv_ext, qkv_prev, qkv_p16, qkv_act, cf_ext, cf_prev, cf_p16, state, mixed):
        x = x_ref[sid, 0]
        mod = mod_ref[sid, 0]
        shift1, scale1, gate1 = mod[0:1], mod[1:2], mod[2:3]
        h = x * lax.rsqrt(jnp.mean(x * x, axis=-1, keepdims=True) + EPS)
        h = h * npre_ref[...] * (1.0 + scale1) + shift1
        hb = h.astype(BF16)
        yield
        hb_st = _dot(consts[0], hb).astype(BF16)

        _fill_conv_window(qkv_ext, qkv_prev, _dot(hb_st, wqkv_ref[...]), DN_CONV)
        yield
        for r0 in range(0, ts, CONV_ROWS):
            for c0 in range(0, qkv_ext.shape[1], DN_WIDTH):
                blk = _conv_block(qkv_ext, convw_ref, DN_CONV, r0, CONV_ROWS, c0, DN_WIDTH)
                qkv_p16[r0:r0 + CONV_ROWS, c0:c0 + DN_WIDTH] = _silu(blk).astype(BF16)
        yield
        qkv_act[...] = _dot(to_natural, qkv_p16[...])

        cf_pre = _dot(hb_st, wcf_ref[...]) + pw1b_ref[...]
        cfw = cf_ext.shape[1]
        _fill_conv_window(cf_ext, cf_prev, cf_pre[:, :cfw] * jax.nn.sigmoid(cf_pre[:, cfw:]), CF_KERNEL)
        cf_pending = list(range(0, ts, CONV_ROWS))
        yield

        def cf_step(n=1):
            for _ in range(min(n, len(cf_pending))):
                r0 = cf_pending.pop(0)
                cf = _conv_block(cf_ext, dww_ref, CF_KERNEL, r0, CONV_ROWS, 0, cfw) + dwb_ref[...]
                mu = jnp.mean(cf, axis=-1, keepdims=True)
                xc = cf - mu
                var = jnp.mean(xc * xc, axis=-1, keepdims=True)
                cfn = xc * lax.rsqrt(var + EPS) * lnw_ref[...] + lnb_ref[...]
                cf_p16[r0:r0 + CONV_ROWS, :] = _silu(cfn).astype(BF16)

        z = _dot(hb, wz_ref[...])
        ba = _dot(hb, wba_ref[...])
        beta_all = jax.nn.sigmoid(ba)
        g_all = -jnp.exp(alog_ref[...]) * _softplus(ba + dtb_ref[...])

        g_hi, g_mid, g_lo = _split3(g_all)
        gcum = _dot(tri, g_hi) + _dot(tri, g_mid) + _dot(tri, g_lo)
        gcum_t = gcum.T
        exp_g = jnp.exp(gcum)
        yield

        heads = []
        for hd in range(DN_HEADS):
            lo = hd * HEAD_DIM
            qh = qkv_act[:, lo:lo + HEAD_DIM]
            kh = qkv_act[:, DN_WIDTH + lo:DN_WIDTH + lo + HEAD_DIM]
            vh = qkv_act[:, 2 * DN_WIDTH + lo:2 * DN_WIDTH + lo + HEAD_DIM]
            qn = qh * lax.rsqrt(jnp.sum(qh * qh, axis=-1, keepdims=True) + EPS) * (HEAD_DIM ** -0.5)
            kn = kh * lax.rsqrt(jnp.sum(kh * kh, axis=-1, keepdims=True) + EPS)
            beta_h = beta_all[:, hd:hd + 1]
            gc_h = gcum[:, DN_HEADS + hd:DN_HEADS + hd + 1]
            eg_h = exp_g[:, DN_HEADS + hd:DN_HEADS + hd + 1]
            k_beta = kn * beta_h
            heads.append(dict(qn=qn, kn=kn, k_beta=k_beta, v_beta=vh * beta_h, kbg=k_beta * eg_h, qg=qn * eg_h,
                              gc=gc_h, zg=_silu(z[:, lo:lo + HEAD_DIM])))
        yield

        decay, kq = {}, {}
        for hd, ch in cells:
            hv, sl = heads[hd], rows(ch)
            gc_row = gcum_t[DN_HEADS + hd:DN_HEADS + hd + 1, sl]
            decay[hd, ch] = jnp.where(causal, jnp.exp(hv["gc"][sl] - gc_row), 0.0)
            lhs = jnp.concatenate([hv["k_beta"][sl], hv["qn"][sl]], axis=0).astype(BF16)
            kq[hd, ch] = _dot_nt(lhs, hv["kn"][sl].astype(BF16))
        a = {c: jnp.where(strict, kq[c][:DN_CHUNK] * decay[c], 0.0) for c in cells}
        attn = {c: (kq[c][DN_CHUNK:] * decay[c]).astype(BF16) for c in cells}
        yield

        t_inv = {c: eye - a[c] for c in cells}
        pw = {c: a[c].astype(BF16) for c in cells}
        pw = {c: _dot(pw[c], pw[c]).astype(BF16) for c in cells}
        cf_step()
        yield
        for _ in range(4):
            both = {c: _dot(jnp.concatenate([t_inv[c].astype(BF16), pw[c]], axis=0), pw[c]) for c in cells}
            t_inv = {c: t_inv[c] + both[c][:DN_CHUNK] for c in cells}
            pw = {c: both[c][DN_CHUNK:].astype(BF16) for c in cells}
            cf_step()
            yield
        t_inv = {c: t_inv[c] + _dot(t_inv[c].astype(BF16), pw[c]) for c in cells}

        sol, aw, ks, glast = {}, {}, {}, {}
        for hd, ch in cells:
            hv, sl = heads[hd], rows(ch)
            rhs = jnp.concatenate([hv["v_beta"][sl], hv["kbg"][sl]], axis=1).astype(BF16)
            sol[hd, ch] = _dot(t_inv[hd, ch].astype(BF16), rhs).astype(BF16)
        yield
        for hd, ch in cells:
            hv, sl = heads[hd], rows(ch)
            gc_col = hv["gc"][sl]
            glast[hd, ch] = gc_col[DN_CHUNK - 1:DN_CHUNK]
            k_dec = (hv["kn"][sl] * jnp.exp(glast[hd, ch] - gc_col)).astype(BF16)
            aw[hd, ch] = _dot(attn[hd, ch], sol[hd, ch])
            ks[hd, ch] = _dot_tn(k_dec, sol[hd, ch])
        yield

        s_in = {}
        s_cur = [state[hd] for hd in range(DN_HEADS)]
        for ch in range(n_chunks):
            for hd in range(DN_HEADS):
                s_in[hd, ch] = s_cur[hd].astype(BF16)
                kd_u, kd_w = ks[hd, ch][:, :HEAD_DIM], ks[hd, ch][:, HEAD_DIM:]
                s_cur[hd] = s_cur[hd] * jnp.exp(glast[hd, ch]) + kd_u - _dot(kd_w.astype(BF16), s_in[hd, ch])
            cf_step()
            yield
        for hd in range(DN_HEADS):
            state[hd] = s_cur[hd]

        for hd, ch in cells:
            hv, sl = heads[hd], rows(ch)
            lo = hd * HEAD_DIM
            q_eff = (hv["qg"][sl] - aw[hd, ch][:, HEAD_DIM:]).astype(BF16)
            o = _dot(q_eff, s_in[hd, ch]) + aw[hd, ch][:, :HEAD_DIM]
            on = o * lax.rsqrt(jnp.mean(o * o, axis=-1, keepdims=True) + EPS) * dnw * hv["zg"][sl]
            mixed[sl, lo:lo + HEAD_DIM] = on.astype(BF16)
        yield

        cf_step(len(cf_pending))
        mixed[:, DN_WIDTH:DN_WIDTH + cfw] = _dot(to_natural, cf_p16[...]).astype(BF16)
        yield

        out = _dot(mixed[...], wout_ref[...])
        yield
        y = out * lax.rsqrt(jnp.mean(out * out, axis=-1, keepdims=True) + EPS) * npost_ref[...]
        o_ref[sid, 0] = x + gate1 * y

    scratch = (qkv_ext, qkv_prev, qkv_p16, qkv_act, cf_ext, cf_prev, cf_p16, state, mixed)
    streams = [tile_stages(sid, *(ref.at[sid] for ref in scratch)) for sid in range(x_ref.shape[0])]
    live = [True] * len(streams)

    def advance(i, n=1):
        for _ in range(n):
            if live[i]:
                try:
                    next(streams[i])
                except StopIteration:
                    live[i] = False

    advance(0, MIXER_SKEW)
    while any(live):
        for i in range(len(streams)):
            advance(i)


def _mixer_call(x, mod3, npre, npost, w_in, dn_conv_w, dn_a_log, dn_dt_bias, dn_norm_w,
                cf_pw1_b, cf_dw_w, cf_dw_b, cf_ln_w, cf_ln_b, w_out):
    bsz, seq, d = x.shape
    ts = SEQ_TILE
    cfw = cf_dw_w.shape[1]
    n_qkv = 3 * DN_WIDTH
    alog = jnp.pad(dn_a_log, (DN_HEADS, LANES - 2 * DN_HEADS)).reshape(1, LANES)
    dtb = jnp.pad(dn_dt_bias, (DN_HEADS, LANES - 2 * DN_HEADS)).reshape(1, LANES)
    dww = jnp.repeat(cf_dw_w, SUBLANES, axis=0)
    convw = jnp.repeat(dn_conv_w, SUBLANES, axis=0)

    def full(a):
        mode = dict(pipeline_mode=pl.Buffered(1)) if a.size >= d * d else {}
        return pl.BlockSpec(a.shape, lambda b, j: (0,) * a.ndim, **mode)

    row = lambda a: a.reshape(1, -1)
    nb = MIXER_PAIR
    x4 = x.reshape(nb, bsz // nb, seq, d)
    mod4 = mod3.reshape((nb, bsz // nb) + mod3.shape[1:])
    operands = [x4, mod4, row(npre), row(npost), w_in.T, convw, alog, dtb, row(dn_norm_w),
                row(cf_pw1_b), dww, row(cf_dw_b), row(cf_ln_w), row(cf_ln_b), w_out]
    in_specs = [pl.BlockSpec((nb, 1, ts, d), lambda b, j: (0, b, j, 0)),
                pl.BlockSpec((nb, 1) + mod3.shape[1:], lambda b, j: (0, b, 0, 0))]
    in_specs += [full(a) for a in operands[2:]]
    out = pl.pallas_call(
        _mixer_kernel,
        grid=(bsz // nb, seq // ts),
        in_specs=in_specs,
        out_specs=pl.BlockSpec((nb, 1, ts, d), lambda b, j: (0, b, j, 0)),
        out_shape=jax.ShapeDtypeStruct((nb, bsz // nb, seq, d), F32),
        scratch_shapes=[pltpu.VMEM((nb, (DN_CONV - 1) * SUBLANES + ts, n_qkv), F32),
                        pltpu.VMEM((nb, (DN_CONV - 1) * SUBLANES, n_qkv), F32),
                        pltpu.VMEM((nb, ts, n_qkv), BF16),
                        pltpu.VMEM((nb, ts, n_qkv), F32),
                        pltpu.VMEM((nb, (CF_KERNEL - 1) * SUBLANES + ts, cfw), F32),
                        pltpu.VMEM((nb, (CF_KERNEL - 1) * SUBLANES, cfw), F32),
                        pltpu.VMEM((nb, ts, cfw), BF16),
                        pltpu.VMEM((nb, DN_HEADS, HEAD_DIM, HEAD_DIM), F32),
                        pltpu.VMEM((nb, ts, DN_WIDTH + cfw), BF16),
                        pltpu.VMEM((3, ts, ts), BF16),
                        pltpu.VMEM((d, n_qkv), BF16), pltpu.VMEM((d, DN_WIDTH), BF16), pltpu.VMEM((d, LANES), BF16),
                        pltpu.VMEM((d, 2 * cfw), BF16), pltpu.VMEM(w_out.shape, BF16)],
        compiler_params=pltpu.CompilerParams(dimension_semantics=("arbitrary", "arbitrary"),
                                             vmem_limit_bytes=VMEM_LIMIT),
        name="mixer",
    )(*operands)
    return out.reshape(bsz, seq, d)


def _router_kernel(x_ref, mod_ref, nw_ref, wr_ref, br_ref, h_ref, info_ref, infot_ref, cnt_ref, carry, strict_ref):
    tt = x_ref.shape[1]

    @pl.when((pl.program_id(0) == 0) & (pl.program_id(1) == 0))
    def _():
        carry[...] = jnp.zeros(carry.shape, F32)
        rr = lax.broadcasted_iota(jnp.int32, (tt, tt), 0)
        cc = lax.broadcasted_iota(jnp.int32, (tt, tt), 1)
        strict_ref[...] = jnp.where(cc < rr, 1.0, 0.0).astype(BF16)

    x = x_ref[0]
    mod = mod_ref[0]
    shift2, scale2 = mod[3:4], mod[4:5]
    h = x * lax.rsqrt(jnp.mean(x * x, axis=-1, keepdims=True) + EPS)
    h = h * nw_ref[...] * (1.0 + scale2) + shift2
    _store_token_major(h_ref, h)

    h_hi, h_lo, _ = _split3(h)
    w_hi, w_lo, _ = _split3(wr_ref[...])
    logits = _dot(h_hi, w_hi) + (_dot(h_hi, w_lo) + _dot(h_lo, w_hi)) + br_ref[...]

    lane = lax.broadcasted_iota(jnp.int32, (tt, LANES), 1)
    neg = -jnp.inf
    is_grp = (lane >= N_EXPERTS) & (lane < N_EXPERTS + N_GROUPS)
    gl = jnp.where(is_grp, logits, neg)
    gmax = jnp.max(gl, axis=-1, keepdims=True)
    gsum = jnp.sum(jnp.where(is_grp, jnp.exp(gl - gmax), 0.0), axis=-1, keepdims=True)
    grp_p = 1.0 / gsum
    lane_f = lane.astype(F32)
    no_lane = float(LANES)
    grp_lane = jnp.min(jnp.where(is_grp & (gl == gmax), lane_f, no_lane), axis=-1, keepdims=True)
    grp_idx = grp_lane.astype(jnp.int32) - N_EXPERTS

    in_grp = (lane < N_EXPERTS) & (lane // EXPERTS_PER_GROUP == grp_idx)
    el = jnp.where(in_grp, logits, neg)
    m1 = jnp.max(el, axis=-1, keepdims=True)
    e1 = jnp.min(jnp.where(in_grp & (el == m1), lane_f, no_lane), axis=-1, keepdims=True)
    el2 = jnp.where(lane_f == e1, neg, el)
    m2 = jnp.max(el2, axis=-1, keepdims=True)
    e2 = jnp.min(jnp.where(in_grp & (lane_f != e1) & (el2 == m2), lane_f, no_lane), axis=-1, keepdims=True)
    r = jnp.exp(m2 - m1)
    w1 = grp_p / (1.0 + r)
    w2 = grp_p * r / (1.0 + r)

    hit1 = lane_f == e1
    hit2 = lane_f == e2
    onehot = jnp.where(hit1 | hit2, 1.0, 0.0)
    prefix = _dot(strict_ref[...], onehot.astype(BF16)) + carry[0:1, :]
    rank1 = jnp.sum(jnp.where(hit1, prefix, 0.0), axis=-1, keepdims=True)
    rank2 = jnp.sum(jnp.where(hit2, prefix, 0.0), axis=-1, keepdims=True)
    total = carry[0:1, :] + jnp.sum(onehot, axis=0, keepdims=True)
    carry[...] = jnp.broadcast_to(total, carry.shape)
    cnt_ref[...] = jnp.broadcast_to(total, cnt_ref.shape)

    info = jnp.where(lane == 0, e1, 0.0)
    info = jnp.where(lane == 1, e2, info)
    info = jnp.where(lane == 2, w1, info)
    info = jnp.where(lane == 3, w2, info)
    info = jnp.where(lane == 4, rank1, info)
    info = jnp.where(lane == 5, rank2, info)
    info_ref[0] = info
    infot_ref[...] = info.T[0:SUBLANES, :]


def _router_call(x1, mod3, norm_w, w_router_group, b_router_group, w_router_expert, b_router_expert):
    bsz, seq, d = x1.shape
    tt = ROUTER_TILE
    nj = seq // tt
    pitch = d // LANES
    pad = LANES - N_EXPERTS - N_GROUPS
    wr = jnp.pad(jnp.concatenate([w_router_expert, w_router_group], axis=1), ((0, 0), (0, pad)))
    br = jnp.pad(jnp.concatenate([b_router_expert, b_router_group]), (0, pad)).reshape(1, LANES)
    return pl.pallas_call(
        _router_kernel,
        grid=(bsz, seq // tt),
        in_specs=[pl.BlockSpec((1, tt, d), lambda b, j: (b, j, 0)),
                  pl.BlockSpec((1,) + mod3.shape[1:], lambda b, j: (b, 0, 0)),
                  pl.BlockSpec((1, d), lambda b, j: (0, 0)),
                  pl.BlockSpec((d, LANES), lambda b, j: (0, 0)),
                  pl.BlockSpec((1, LANES), lambda b, j: (0, 0))],
        out_specs=[pl.BlockSpec((tt * pitch, LANES), lambda b, j: (b * nj + j, 0)),
                   pl.BlockSpec((1, tt, LANES), lambda b, j: (b, j, 0)),
                   pl.BlockSpec((SUBLANES, tt), lambda b, j: (0, b * nj + j)),
                   pl.BlockSpec((8, LANES), lambda b, j: (0, 0))],
        out_shape=[jax.ShapeDtypeStruct((bsz * seq * pitch, LANES), F32),
                   jax.ShapeDtypeStruct((bsz, seq, LANES), F32),
                   jax.ShapeDtypeStruct((SUBLANES, bsz * seq), F32),
                   jax.ShapeDtypeStruct((8, LANES), F32)],
        scratch_shapes=[pltpu.VMEM((8, LANES), F32), pltpu.VMEM((tt, tt), BF16)],
        compiler_params=pltpu.CompilerParams(dimension_semantics=("arbitrary", "arbitrary"),
                                             vmem_limit_bytes=VMEM_LIMIT),
        name="router",
    )(x1, mod3, norm_w.reshape(1, d), wr, br)


def _tile_rows(idx, pitch):
    if isinstance(idx, int):
        return pl.ds(idx * pitch, pitch)
    return pl.ds(pl.multiple_of(idx * pitch, pitch), pitch)


def _dispatch_kernel(nused_ref, padfrom_ref, padlen_ref, pos_ref, h_hbm, xs_hbm, zbuf, zsem, hbuf, hsems, ssems,
                     *, pitch, bm):
    step = pl.program_id(0)
    tt = hbuf.shape[1] // pitch

    def fill_copies(action):
        def per_expert(e, carry):
            cursor = padfrom_ref[e]
            n = padlen_ref[e]
            bit = bm // 2
            while bit >= 1:
                take = (n & bit) != 0

                @pl.when(take)
                def _(cursor=cursor, bit=bit):
                    action(pltpu.make_async_copy(zbuf.at[pl.ds(0, bit * pitch), :],
                                                 xs_hbm.at[pl.ds(pl.multiple_of(cursor * pitch, pitch), bit * pitch), :],
                                                 zsem.at[0]))
                cursor = cursor + jnp.where(take, bit, 0)
                bit //= 2
            return carry

        def per_block(b, carry):
            start = pl.multiple_of(b * bm * pitch, bm * pitch)
            action(pltpu.make_async_copy(zbuf, xs_hbm.at[pl.ds(start, bm * pitch), :], zsem.at[0]))
            return carry

        lax.fori_loop(0, N_EXPERTS, per_expert, 0)
        lax.fori_loop(nused_ref[0], xs_hbm.shape[0] // (bm * pitch), per_block, 0)

    @pl.when(step == 0)
    def _():
        zbuf[...] = jnp.zeros(zbuf.shape, F32)
        fill_copies(lambda cp: cp.start())

    n_steps = pl.num_programs(0)
    slots = hbuf.shape[0]
    tile_rows = tt * pitch

    def tile_in(s):
        src = h_hbm.at[pl.ds(pl.multiple_of(s * tile_rows, tile_rows), tile_rows), :]
        return pltpu.make_async_copy(src, hbuf.at[s % slots], hsems.at[s % slots])

    def scatter_wait(s):
        for k in range(TOP_K):
            pltpu.make_async_copy(hbuf.at[s % slots], xs_hbm.at[pl.ds(0, tile_rows), :], ssems.at[s % slots]).wait()

    @pl.when(step == 0)
    def _():
        for s in range(slots - 1):
            @pl.when(s < n_steps)
            def _(s=s):
                tile_in(s).start()

    tile_in(step).wait()
    src = hbuf.at[step % slots]
    for r in range(tt):
        for k in range(TOP_K):
            p = pos_ref[0, 0, k * tt + r]
            pltpu.make_async_copy(src.at[_tile_rows(r, pitch), :], xs_hbm.at[_tile_rows(p, pitch), :],
                                  ssems.at[step % slots]).start(priority=k)

    @pl.when(step >= 1)
    def _():
        scatter_wait(step - 1)

    @pl.when(step + slots - 1 < n_steps)
    def _():
        tile_in(step + slots - 1).start()

    @pl.when(step == n_steps - 1)
    def _():
        scatter_wait(step)
        fill_copies(lambda cp: cp.wait())


def _dispatch_call(h2, pos3, n_used, padfrom, padlen, n_rows, d):
    pitch = d // LANES
    bm = EXPERT_BLOCK
    n_tiles, _, two_tt = pos3.shape
    tt = two_tt // TOP_K
    grid_spec = pltpu.PrefetchScalarGridSpec(
        num_scalar_prefetch=3,
        grid=(n_tiles,),
        in_specs=[pl.BlockSpec((1, 1, two_tt), lambda s, *_: (s, 0, 0), memory_space=pltpu.SMEM),
                  pl.BlockSpec(memory_space=pl.ANY)],
        out_specs=pl.BlockSpec(memory_space=pl.ANY),
        scratch_shapes=[pltpu.VMEM((bm * pitch, LANES), F32), pltpu.SemaphoreType.DMA((1,)),
                        pltpu.VMEM((DISPATCH_SLOTS, tt * pitch, LANES), F32),
                        pltpu.SemaphoreType.DMA((DISPATCH_SLOTS,)), pltpu.SemaphoreType.DMA((DISPATCH_SLOTS,))],
    )
    return pl.pallas_call(
        functools.partial(_dispatch_kernel, pitch=pitch, bm=bm),
        grid_spec=grid_spec,
        out_shape=jax.ShapeDtypeStruct((n_rows * pitch, LANES), F32),
        compiler_params=pltpu.CompilerParams(dimension_semantics=("arbitrary",),
                                             vmem_limit_bytes=VMEM_LIMIT),
        name="dispatch",
    )(n_used, padfrom, padlen, pos3, h2)


def _expert_kernel(nused_ref, nruns_ref, first_ref, run_ref, rune_ref, x_hbm, wg_hbm, wu_hbm, wd_hbm, y_hbm,
                   wg_f32, wu_f32, wd_f32, wg_b, wu_b, wd_b, wsems, xbuf, ybuf, xsems, ysems):
    i = pl.program_id(0)
    n_used = nused_ref[0]
    n_runs = nruns_ref[0]
    d = wg_b.shape[0]
    pitch = d // LANES
    bm = xbuf.shape[1] // pitch
    n_slots = wg_f32.shape[0]

    def weight_copies(e, slot):
        return [pltpu.make_async_copy(src.at[e], dst.at[slot], wsems.at[slot])
                for src, dst in ((wg_hbm, wg_f32), (wu_hbm, wu_f32), (wd_hbm, wd_f32))]

    def start_run(r):
        @pl.when(r < n_runs)
        def _():
            for cp in weight_copies(rune_ref[r], r % n_slots):
                cp.start(priority=1)

    @pl.when(i == 0)
    def _():
        for r in range(n_slots - 1):
            start_run(r)

    @pl.when((i < n_used) & (first_ref[i] == 1))
    def _():
        run = run_ref[i]
        slot = run % n_slots
        for cp in weight_copies(0, slot):
            cp.wait()
        start_run(run + n_slots - 1)

        wg_b[...] = wg_f32[slot].astype(BF16)
        wu_b[...] = wu_f32[slot].astype(BF16)
        wd_b[...] = wd_f32[slot].astype(BF16)

    blk_rows = bm * pitch
    x_slots, y_slots = xbuf.shape[0], ybuf.shape[0]

    def block(ref, b):
        return ref.at[pl.ds(pl.multiple_of(b * blk_rows, blk_rows), blk_rows), :]

    def x_copy(b):
        return pltpu.make_async_copy(block(x_hbm, b), xbuf.at[b % x_slots], xsems.at[b % x_slots])

    def y_copy(b):
        return pltpu.make_async_copy(ybuf.at[b % y_slots], block(y_hbm, b), ysems.at[b % y_slots])

    @pl.when(i == 0)
    def _():
        for b in range(x_slots - 1):
            @pl.when(b < n_used)
            def _(b=b):
                x_copy(b).start()

    @pl.when(i < n_used)
    def _():
        x_copy(i).wait()

        @pl.when(i + x_slots - 1 < n_used)
        def _():
            x_copy(i + x_slots - 1).start()

        @pl.when(i >= y_slots)
        def _():
            y_copy(i - y_slots).wait()

        xb = _load_token_major(xbuf.at[i % x_slots], bm, d).astype(BF16)
        gate = _dot(xb, wg_b[...])
        up = _dot(xb, wu_b[...])
        hid = (_silu(gate) * up).astype(BF16)
        _store_token_major(ybuf.at[i % y_slots], _dot(hid, wd_b[...]))
        y_copy(i).start()

    @pl.when(i == n_used - 1)
    def _():
        for back in range(y_slots):
            @pl.when(i - back >= 0)
            def _(back=back):
                y_copy(i - back).wait()


def _expert_call(xs, block_e, n_used, w_gate, w_up, w_down):
    bm = EXPERT_BLOCK
    d, de = w_gate.shape[1], w_gate.shape[2]
    pitch = d // LANES
    n_blocks = xs.shape[0] // (bm * pitch)
    blk = jnp.arange(n_blocks, dtype=jnp.int32)
    used = blk < n_used[0]
    first = jnp.concatenate([jnp.ones((1,), jnp.int32), (block_e[1:] != block_e[:-1]).astype(jnp.int32)])
    first = jnp.where(used, first, 0)
    run = (jnp.cumsum(first) - 1).astype(jnp.int32)
    n_runs = jnp.sum(first).astype(jnp.int32).reshape(1)
    eids = jnp.arange(N_EXPERTS, dtype=jnp.int32)
    run_of_e = jnp.where((first[:, None] == 1) & (block_e[:, None] == eids[None, :]), run[:, None], -1).max(axis=0)
    run_e = jnp.sum(jnp.where(run_of_e[None, :] == eids[:, None], eids[None, :], 0), axis=1).astype(jnp.int32)
    grid_spec = pltpu.PrefetchScalarGridSpec(
        num_scalar_prefetch=5,
        grid=(n_blocks,),
        in_specs=[pl.BlockSpec(memory_space=pl.ANY)] * 4,
        out_specs=pl.BlockSpec(memory_space=pl.ANY),
        scratch_shapes=[pltpu.VMEM((WEIGHT_SLOTS, d, de), F32), pltpu.VMEM((WEIGHT_SLOTS, d, de), F32),
                        pltpu.VMEM((WEIGHT_SLOTS, de, d), F32),
                        pltpu.VMEM((d, de), BF16), pltpu.VMEM((d, de), BF16), pltpu.VMEM((de, d), BF16),
                        pltpu.SemaphoreType.DMA((WEIGHT_SLOTS,)),
                        pltpu.VMEM((ROW_SLOTS_IN, bm * pitch, LANES), F32),
                        pltpu.VMEM((ROW_SLOTS_OUT, bm * pitch, LANES), F32),
                        pltpu.SemaphoreType.DMA((ROW_SLOTS_IN,)), pltpu.SemaphoreType.DMA((ROW_SLOTS_OUT,))],
    )
    return pl.pallas_call(
        _expert_kernel,
        grid_spec=grid_spec,
        out_shape=jax.ShapeDtypeStruct((n_blocks * bm * pitch, LANES), F32),
        input_output_aliases={5: 0},
        compiler_params=pltpu.CompilerParams(dimension_semantics=("arbitrary",),
                                             vmem_limit_bytes=VMEM_LIMIT),
        name="experts",
    )(n_used, n_runs, first, run, run_e, xs, w_gate, w_up, w_down)


def _combine_kernel(pos_pre_ref, pos_ahead_ref, x_ref, info_ref, mod_ref, nw_ref, y_hbm, o_ref, *bufs_and_sems):
    bufs, sems = bufs_and_sems[:-1], bufs_and_sems[-1]
    n_bufs = len(bufs)
    ahead = n_bufs - 1
    tt, d = x_ref.shape[1], x_ref.shape[2]
    pitch = d // LANES
    step = pl.program_id(0) * pl.num_programs(1) + pl.program_id(1)
    n_steps = pl.num_programs(0) * pl.num_programs(1)
    n_chunks = tt // COMBINE_ROWS

    def row_copy(pos_ref, tile, slot, k, r):
        p = pos_ref[tile, 0, k * tt + r]
        return pltpu.make_async_copy(y_hbm.at[_tile_rows(p, pitch), :],
                                     bufs[slot].at[_tile_rows(k * tt + r, pitch), :], sems.at[slot])

    def wait_tile(slot):
        pltpu.make_async_copy(y_hbm.at[pl.ds(0, TOP_K * tt * pitch), :], bufs[slot], sems.at[slot]).wait()

    @pl.when(step == 0)
    def _():
        for t in range(ahead):
            def body(r, carry, t=t):
                for k in range(TOP_K):
                    row_copy(pos_pre_ref, t, t, k, r).start(priority=k)
                return carry
            lax.fori_loop(0, tt, body, 0, unroll=4)

    gate2 = mod_ref[0][5:6]

    def run_tile(cur, nxt):
        wait_tile(cur)
        for c in range(n_chunks):
            r0 = c * COMBINE_ROWS
            for r in range(COMBINE_ROWS):
                for k in range(TOP_K):
                    row_copy(pos_ahead_ref, 0, nxt, k, r0 + r).start(priority=k)
            rows = pl.ds(r0, COMBINE_ROWS)
            info = info_ref[0, rows, :]
            moe = (_load_token_major(bufs[cur], COMBINE_ROWS, d, r0 * pitch) * info[:, 2:3]
                   + _load_token_major(bufs[cur], COMBINE_ROWS, d, (tt + r0) * pitch) * info[:, 3:4])
            y = moe * lax.rsqrt(jnp.mean(moe * moe, axis=-1, keepdims=True) + EPS) * nw_ref[...]
            o_ref[0, rows, :] = x_ref[0, rows, :] + gate2 * y

        @pl.when(step == n_steps - 1)
        def _():
            for other in range(n_bufs):
                if other != cur:
                    wait_tile(other)

    for parity in range(n_bufs):
        @pl.when(step % n_bufs == parity)
        def _(parity=parity):
            run_tile(parity, (parity + ahead) % n_bufs)


def _combine_call(x1, info, mod3, norm_w, y, pos3):
    bsz, seq, d = x1.shape
    tt = COMBINE_TILE
    nj = seq // tt
    n_tiles = bsz * nj
    ahead = COMBINE_SLOTS - 1
    pos_blk = lambda n, f: pl.BlockSpec((n, 1, TOP_K * tt), f, memory_space=pltpu.SMEM)
    tile_buf = pltpu.VMEM((TOP_K * tt * (d // LANES), LANES), F32)
    return pl.pallas_call(
        _combine_kernel,
        grid=(bsz, nj),
        in_specs=[pos_blk(ahead, lambda b, j: (0, 0, 0)),
                  pos_blk(1, lambda b, j: (jnp.minimum(b * nj + j + ahead, n_tiles - 1), 0, 0)),
                  pl.BlockSpec((1, tt, d), lambda b, j: (b, j, 0)),
                  pl.BlockSpec((1, tt, LANES), lambda b, j: (b, j, 0)),
                  pl.BlockSpec((1,) + mod3.shape[1:], lambda b, j: (b, 0, 0)),
                  pl.BlockSpec((1, d), lambda b, j: (0, 0)),
                  pl.BlockSpec(memory_space=pl.ANY)],
        out_specs=pl.BlockSpec((1, tt, d), lambda b, j: (b, j, 0)),
        out_shape=jax.ShapeDtypeStruct((bsz, seq, d), F32),
        scratch_shapes=[tile_buf] * COMBINE_SLOTS + [pltpu.SemaphoreType.DMA((COMBINE_SLOTS,))],
        compiler_params=pltpu.CompilerParams(dimension_semantics=("arbitrary", "arbitrary"),
                                             vmem_limit_bytes=VMEM_LIMIT),
        name="combine",
    )(pos3, pos3, x1, info, mod3, norm_w.reshape(1, d), y)


def _layer(x, mod, norm_pre_mix, norm_post_mix, w_in, dn_conv_w, dn_a_log, dn_dt_bias, dn_norm_w,
           cf_pw1_b, cf_dw_w, cf_dw_b, cf_ln_w, cf_ln_b, w_out, norm_pre_ffn, norm_post_ffn,
           w_router_group, b_router_group, w_router_expert, b_router_expert, w_gate, w_up, w_down):
    bsz, seq, d = x.shape
    t = bsz * seq
    mod3 = mod.reshape(bsz, -1, d)
    x1 = _mixer_call(x, mod3, norm_pre_mix, norm_post_mix, w_in, dn_conv_w, dn_a_log, dn_dt_bias, dn_norm_w,
                     cf_pw1_b, cf_dw_w, cf_dw_b, cf_ln_w, cf_ln_b, w_out)
    h2, info, info_t, cnt = _router_call(x1, mod3, norm_pre_ffn, w_router_group, b_router_group,
                                         w_router_expert, b_router_expert)

    bm = EXPERT_BLOCK
    expert_id = info_t[0:TOP_K].astype(jnp.int32)
    rank = info_t[4:4 + TOP_K].astype(jnp.int32)
    counts = cnt[0, :N_EXPERTS].astype(jnp.int32)
    padded = (counts + bm - 1) // bm * bm
    pend = jnp.cumsum(padded)
    pstart = pend - padded
    eids = jnp.arange(N_EXPERTS, dtype=jnp.int32)
    onehot = expert_id[:, None, :] == eids[None, :, None]
    pos = jnp.sum(jnp.where(onehot, pstart[None, :, None], 0), axis=1) + rank

    def tiled(tt):
        return pos.reshape(TOP_K, t // tt, tt).transpose(1, 0, 2).reshape(t // tt, 1, TOP_K * tt)
    n_blocks = -(-(t * TOP_K) // bm) + N_EXPERTS
    block_start = jnp.arange(n_blocks, dtype=jnp.int32) * bm
    block_e = jnp.minimum(jnp.sum(pend[None, :] <= block_start[:, None], axis=1), N_EXPERTS - 1).astype(jnp.int32)
    n_used = (pend[-1] // bm).astype(jnp.int32).reshape(1)

    xs = _dispatch_call(h2, tiled(DISPATCH_TILE), n_used, (pstart + counts).astype(jnp.int32),
                        (padded - counts).astype(jnp.int32), n_blocks * bm, d)
    y = _expert_call(xs, block_e, n_used, w_gate, w_up, w_down)
    return _combine_call(x1, info, mod3, norm_post_ffn, y, tiled(COMBINE_TILE))


def kernel(x, c, w_ada, b_ada, norm_pre_mix, norm_post_mix, w_in, dn_conv_w, dn_a_log, dn_dt_bias, dn_norm_w,
           cf_pw1_b, cf_dw_w, cf_dw_b, cf_ln_w, cf_ln_b, w_out, norm_pre_ffn, norm_post_ffn,
           w_router_group, b_router_group, w_router_expert, b_router_expert, w_gate, w_up, w_down):
    depth = w_ada.shape[0]
    for l in range(depth):
        mod = _ada_call(c, w_ada[l], b_ada[l])
        x = _layer(x, mod, norm_pre_mix[l], norm_post_mix[l], w_in[l], dn_conv_w[l], dn_a_log[l],
                   dn_dt_bias[l], dn_norm_w[l], cf_pw1_b[l], cf_dw_w[l], cf_dw_b[l], cf_ln_w[l], cf_ln_b[l],
                   w_out[l], norm_pre_ffn[l], norm_post_ffn[l], w_router_group[l], b_router_group[l],
                   w_router_expert[l], b_router_expert[l], w_gate[l], w_up[l], w_down[l])
    return x
```
